```python
import math
import jax, jax.numpy as jnp
from jax import lax
import numpy as np

D_MODEL = 1024
BATCH = 4
SEQ = 8192
DEPTH = 2

D_MIX = D_MODEL
ATT_HEADS = 8
ATT_HD = 64
D_ATT = ATT_HEADS * ATT_HD
D_CONV = 256
CONV_GROUPS = 4
CONV_K = 31
D_RNN = 256
RNN_BLOCKS = 4
RNN_BD = D_RNN // RNN_BLOCKS
RNN_CONV_K = 4
RG_C = 8.0
Q_BLOCK = 128
EPS = 1e-6

Q0, Q1 = 0, D_ATT
K0, K1 = Q1, Q1 + D_ATT
V0, V1 = K1, K1 + D_ATT
F0, F1 = V1, V1 + ATT_HEADS
C0, C1 = F1, F1 + 2 * D_CONV
R0, R1 = C1, C1 + D_RNN
G0, G1 = R1, R1 + D_RNN
N_IN = G1

PEER_HEADS = 8
N_KEYS = 128
N_EXPERTS = N_KEYS * N_KEYS
D_KEY = 256
D_HALF = D_KEY // 2
TOPK = 16
TOK_CHUNK = 128

kernel_name = "hymba_fox_conformer_rglru_peer"


def rms_norm(x, g):
    xf = x.astype(jnp.float32)
    y = xf * lax.rsqrt(jnp.mean(xf * xf, axis=-1, keepdims=True) + EPS)
    return (y * g.astype(jnp.float32)).astype(x.dtype)


def layer_norm(x, g, b):
    xf = x.astype(jnp.float32)
    mu = jnp.mean(xf, axis=-1, keepdims=True)
    var = jnp.mean(jnp.square(xf - mu), axis=-1, keepdims=True)
    y = (xf - mu) * lax.rsqrt(var + EPS)
    return (y * g.astype(jnp.float32) + b.astype(jnp.float32)).astype(x.dtype)


def causal_dwconv(x, w, b):
    k = w.shape[0]
    y = lax.conv_general_dilated(
        x, w[:, None, :], window_strides=(1,), padding=[(k - 1, 0)],
        dimension_numbers=("NWC", "WIO", "NWC"), feature_group_count=x.shape[-1])
    return y + b


def forgetting_attention(q, k, v, log_f):
    b, s, h, hd = q.shape
    nb = s // Q_BLOCK
    scale = 1.0 / math.sqrt(hd)
    cum = jnp.cumsum(log_f.astype(jnp.float32), axis=1).transpose(0, 2, 1)
    q = q.transpose(0, 2, 1, 3)
    k = k.transpose(0, 2, 1, 3)
    v = v.transpose(0, 2, 1, 3)
    qb = q.reshape(b, h, nb, Q_BLOCK, hd).transpose(2, 0, 1, 3, 4)
    cb = cum.reshape(b, h, nb, Q_BLOCK).transpose(2, 0, 1, 3)
    kpos = jnp.arange(s)

    def one_block(args):
        qi, ci, blk = args
        qpos = blk * Q_BLOCK + jnp.arange(Q_BLOCK)
        logits = jnp.einsum("bhqd,bhkd->bhqk", qi, k).astype(jnp.float32) * scale
        logits = logits + ci[..., :, None] - cum[:, :, None, :]
        logits = jnp.where(kpos[None, :] <= qpos[:, None], logits, -jnp.inf)
        p = jax.nn.softmax(logits, axis=-1)
        return jnp.einsum("bhqk,bhkd->bhqd", p.astype(v.dtype), v)

    o = lax.map(one_block, (qb, cb, jnp.arange(nb)))
    return o.transpose(1, 0, 3, 2, 4).reshape(b, s, h * hd)


def conformer_conv(u, w_dw, b_dw, ln_g, ln_b):
    a, gate = u[..., :D_CONV], u[..., D_CONV:]
    y = a * jax.nn.sigmoid(gate)
    y = causal_dwconv(y, w_dw, b_dw)
    y = layer_norm(y, ln_g, ln_b)
    return jax.nn.silu(y)


def _lin_combine(c1, c2):
    a1, b1 = c1
    a2, b2 = c2
    return a1 * a2, a2 * b1 + b2


def rglru_block(xr, gate_in, conv_w, conv_b, w_r, b_r, w_i, b_i, lam):
    xr = causal_dwconv(xr, conv_w, conv_b)
    b, s, c = xr.shape
    xh = xr.reshape(b, s, RNN_BLOCKS, RNN_BD)
    r = jax.nn.sigmoid(jnp.einsum("bshi,hij->bshj", xh, w_r).reshape(b, s, c) + b_r)
    i = jax.nn.sigmoid(jnp.einsum("bshi,hij->bshj", xh, w_i).reshape(b, s, c) + b_i)
    log_a = -RG_C * r.astype(jnp.float32) * jax.nn.softplus(-lam.astype(jnp.float32))
    a = jnp.exp(log_a)
    mult = jnp.sqrt(-jnp.expm1(2.0 * log_a))
    bterm = mult * (i * xr).astype(jnp.float32)
    _, hs = lax.associative_scan(_lin_combine, (a, bterm), axis=1)
    return hs.astype(xr.dtype) * jax.nn.gelu(gate_in)


def peer(xn, wq, k1, k2, u_tab, v_tab):
    b, s, d = xn.shape
    t = b * s
    xt = xn.reshape(t, d)
    q = (xt @ wq).reshape(t, PEER_HEADS, D_KEY)
    q1, q2 = q[..., :D_HALF], q[..., D_HALF:]
    s1 = jnp.einsum("thd,hnd->thn", q1, k1).astype(jnp.float32)
    s2 = jnp.einsum("thd,hnd->thn", q2, k2).astype(jnp.float32)
    v1, i1 = lax.top_k(s1, TOPK)
    v2, i2 = lax.top_k(s2, TOPK)
    cand = (v1[..., :, None] + v2[..., None, :]).reshape(t, PEER_HEADS, TOPK * TOPK)
    sv, ci = lax.top_k(cand, TOPK)
    e1 = jnp.take_along_axis(i1, ci // TOPK, axis=-1)
    e2 = jnp.take_along_axis(i2, ci % TOPK, axis=-1)
    nc = t // TOK_CHUNK
    experts = (e1 * N_KEYS + e2).reshape(nc, TOK_CHUNK, PEER_HEADS * TOPK)
    gates = jax.nn.softmax(sv, axis=-1).astype(xn.dtype).reshape(nc, TOK_CHUNK, PEER_HEADS * TOPK)
    xc = xt.reshape(nc, TOK_CHUNK, d)

    def chunk(args):
        xi, ei, gi = args
        hid = jax.nn.gelu(jnp.einsum("td,tkd->tk", xi, u_tab[ei]))
        return jnp.einsum("tk,tkd->td", gi * hid, v_tab[ei])

    out = lax.map(chunk, (xc, experts, gates))
    return out.reshape(b, s, d)


def setup_inputs(seed: int = 0) -> dict:
    key = jax.random.key(seed)
    ks = jax.random.split(key, 26)
    L, D = DEPTH, D_MODEL
    nrm = jax.random.normal
    u_lr = jax.random.uniform(ks[14], (L, D_RNN), minval=0.9, maxval=0.999)
    p_lr = u_lr ** (1.0 / RG_C)
    lam = jnp.log(p_lr) - jnp.log1p(-p_lr)
    return {
        "x": nrm(ks[0], (BATCH, SEQ, D), jnp.float32),
        "norm1_g": 1.0 + 0.01 * nrm(ks[1], (L, D)),
        "w_in": nrm(ks[2], (L, D, N_IN)) * D ** -0.5,
        "b_forget": jax.random.uniform(ks[3], (L, ATT_HEADS), minval=1.0, maxval=5.0),
        "conv_dw_w": nrm(ks[4], (L, CONV_K, D_CONV)) * CONV_K ** -0.5,
        "conv_dw_b": 0.01 * nrm(ks[5], (L, D_CONV)),
        "conv_ln_g": 1.0 + 0.01 * nrm(ks[6], (L, D_CONV)),
        "conv_ln_b": 0.01 * nrm(ks[7], (L, D_CONV)),
        "rg_conv_w": nrm(ks[8], (L, RNN_CONV_K, D_RNN)) * RNN_CONV_K ** -0.5,
        "rg_conv_b": 0.01 * nrm(ks[9], (L, D_RNN)),
        "rg_w_r": nrm(ks[10], (L, RNN_BLOCKS, RNN_BD, RNN_BD)) * RNN_BD ** -0.5,
        "rg_b_r": 0.01 * nrm(ks[11], (L, D_RNN)),
        "rg_w_i": nrm(ks[12], (L, RNN_BLOCKS, RNN_BD, RNN_BD)) * RNN_BD ** -0.5,
        "rg_b_i": 0.01 * nrm(ks[13], (L, D_RNN)),
        "rg_lambda": lam,
        "w_out": nrm(ks[15], (L, D_MIX, D)) * D_MIX ** -0.5,
        "norm2_g": 1.0 + 0.01 * nrm(ks[16], (L, D)),
        "peer_wq": nrm(ks[17], (L, D, PEER_HEADS * D_KEY)) * D ** -0.5,
        "peer_k1": nrm(ks[18], (L, PEER_HEADS, N_KEYS, D_HALF)) * D_HALF ** -0.5,
        "peer_k2": nrm(ks[19], (L, PEER_HEADS, N_KEYS, D_HALF)) * D_HALF ** -0.5,
        "peer_u": nrm(ks[20], (L, N_EXPERTS, D)) * D ** -0.5,
        "peer_v": nrm(ks[21], (L, N_EXPERTS, D)) * 0.5 * PEER_HEADS ** -0.5,
        "final_g": 1.0 + 0.01 * nrm(ks[22], (D,)),
    }


def reference(x, norm1_g, w_in, b_forget, conv_dw_w, conv_dw_b, conv_ln_g, conv_ln_b,
              rg_conv_w, rg_conv_b, rg_w_r, rg_b_r, rg_w_i, rg_b_i, rg_lambda, w_out,
              norm2_g, peer_wq, peer_k1, peer_k2, peer_u, peer_v, final_g):
    b, s, _ = x.shape
    for l in range(DEPTH):
        h = rms_norm(x, norm1_g[l])
        proj = h @ w_in[l]
        q = proj[..., Q0:Q1].reshape(b, s, ATT_HEADS, ATT_HD)
        k = proj[..., K0:K1].reshape(b, s, ATT_HEADS, ATT_HD)
        v = proj[..., V0:V1].reshape(b, s, ATT_HEADS, ATT_HD)
        log_f = jax.nn.log_sigmoid((proj[..., F0:F1] + b_forget[l]).astype(jnp.float32))
        y_att = forgetting_attention(q, k, v, log_f)
        y_conv = conformer_conv(proj[..., C0:C1], conv_dw_w[l], conv_dw_b[l],
                                conv_ln_g[l], conv_ln_b[l])
        y_rnn = rglru_block(proj[..., R0:R1], proj[..., G0:G1], rg_conv_w[l], rg_conv_b[l],
                            rg_w_r[l], rg_b_r[l], rg_w_i[l], rg_b_i[l], rg_lambda[l])
        mixed = jnp.concatenate([y_att, y_conv.astype(x.dtype), y_rnn.astype(x.dtype)], axis=-1)
        x = x + mixed @ w_out[l]
        h2 = rms_norm(x, norm2_g[l])
        x = x + peer(h2, peer_wq[l], peer_k1[l], peer_k2[l], peer_u[l], peer_v[l])
    return rms_norm(x, final_g)
```

```python
import functools
import math
import jax, jax.numpy as jnp
from jax import lax
import numpy as np
from jax.experimental import pallas as pl
from jax.experimental.pallas import tpu as pltpu, tpu_sc as plsc


D_MODEL = 1024
BATCH = 4
SEQ = 8192
DEPTH = 2

ATT_HEADS = 8
ATT_HD = 64
D_ATT = ATT_HEADS * ATT_HD
D_CONV = 256
CONV_K = 31
D_RNN = 256
RNN_BLOCKS = 4
RNN_BD = D_RNN // RNN_BLOCKS
RNN_CONV_K = 4
RG_C = 8.0
Q_BLOCK = 128
EPS = 1e-6

Q0, Q1 = 0, D_ATT
K0, K1 = Q1, Q1 + D_ATT
V0, V1 = K1, K1 + D_ATT
F0, F1 = V1, V1 + ATT_HEADS
C0, C1 = F1, F1 + 2 * D_CONV
R0, R1 = C1, C1 + D_RNN
G0, G1 = R1, R1 + D_RNN
N_IN = G1

PEER_HEADS = 8
N_KEYS = 128
N_EXPERTS = N_KEYS * N_KEYS
D_KEY = 256
D_HALF = D_KEY // 2
TOPK = 16
TOK_CHUNK = 128

NC, NS, L = 2, 16, 16
NW = NC * NS
NJ = D_MODEL // L
KSEL = PEER_HEADS * TOPK
R = 16
NCH = KSEL // R
G = 8
GC = 0.7978845608028654


def _perm(x, idx):
    return jnp.take_along_axis(x, idx, axis=0, mode="promise_in_bounds")


def peer_sc(x, idx, gates, u_tab, v_tab):
    T = x.shape[0]
    tpw = T // NW
    ngroups = tpw // G
    idx3 = idx.reshape(T * NCH, R)
    g3 = gates.reshape(T * NCH, R)
    mesh = plsc.VectorSubcoreMesh(core_axis_name="c", subcore_axis_name="s")

    @functools.partial(
        pl.kernel, mesh=mesh,
        out_type=jax.ShapeDtypeStruct((T, D_MODEL), jnp.float32),
        scratch_types=[
            pltpu.VMEM((G, D_MODEL), jnp.float32),
            pltpu.VMEM((G, D_MODEL), jnp.float32),
            pltpu.VMEM((G * NCH, R), jnp.int32),
            pltpu.VMEM((G * NCH, R), jnp.float32),
            pltpu.VMEM((R, D_MODEL), jnp.float32),
            pltpu.VMEM((R, D_MODEL), jnp.float32),
            pltpu.VMEM((R, D_MODEL), jnp.float32),
            pltpu.VMEM((R, D_MODEL), jnp.float32),
            pltpu.SemaphoreType.DMA,
            pltpu.SemaphoreType.DMA,
            pltpu.SemaphoreType.DMA,
            pltpu.SemaphoreType.DMA,
        ],
        compiler_params=pltpu.CompilerParams(needs_layout_passes=False),
    )
    def k(x_hbm, idx_hbm, g_hbm, u_hbm, v_hbm, out_hbm,
          x_v, out_v, idx_v, g_v, ub0, ub1, vb0, vb1, su0, su1, sv0, sv1):
        wid = lax.axis_index("s") * NC + lax.axis_index("c")
        ubs, vbs, sus, svs = (ub0, ub1), (vb0, vb1), (su0, su1), (sv0, sv1)
        iota = lax.iota(jnp.int32, L)

        def gather_copies(c, b):
            return (pltpu.make_async_copy(u_hbm.at[idx_v.at[c]], ubs[b], sus[b]),
                    pltpu.make_async_copy(v_hbm.at[idx_v.at[c]], vbs[b], svs[b]))

        def issue(c, b):
            for cp in gather_copies(c, b):
                cp.start()

        def wait(c, b):
            for cp in gather_copies(c, b):
                cp.wait()

        def compute(c, b):
            ub, vb = ubs[b], vbs[b]
            t = c // NCH

            def ubody(j, accs):
                xj = x_v[t, pl.ds(j * L, L)]
                return tuple(accs[kk] + xj * ub[kk, pl.ds(j * L, L)] for kk in range(R))

            accs = lax.fori_loop(0, NJ, ubody, tuple(jnp.zeros((L,), jnp.float32) for _ in range(R)))
            vecs = list(accs)
            dist = L // 2
            while dist >= 1:
                pidx = jnp.bitwise_xor(iota, dist)
                low = jnp.bitwise_and(iota, dist) == 0
                nxt = []
                for kk in range(dist):
                    a = vecs[kk]
                    bvec = vecs[kk + dist]
                    a = a + _perm(a, pidx)
                    bvec = bvec + _perm(bvec, pidx)
                    nxt.append(jnp.where(low, a, bvec))
                vecs = nxt
                dist //= 2
            hid = vecs[0]
            z = GC * (hid + 0.044715 * hid * hid * hid)
            gel = hid / (1.0 + jnp.exp(-2.0 * z))
            w = g_v[c, :] * gel
            wbs = [_perm(w, jnp.full((L,), kk, jnp.int32)) for kk in range(R)]

            def vbody(j, carry):
                acc = out_v[t, pl.ds(j * L, L)]
                for kk in range(R):
                    acc = acc + wbs[kk] * vb[kk, pl.ds(j * L, L)]
                out_v[t, pl.ds(j * L, L)] = acc
                return carry

            lax.fori_loop(0, NJ, vbody, 0)

        def group(g, carry):
            tok0 = wid * tpw + g * G
            pltpu.sync_copy(x_hbm.at[pl.ds(tok0, G)], x_v)
            pltpu.sync_copy(idx_hbm.at[pl.ds(tok0 * NCH, G * NCH)], idx_v)
            pltpu.sync_copy(g_hbm.at[pl.ds(tok0 * NCH, G * NCH)], g_v)

            def zbody(i, cz):
                out_v[i // NJ, pl.ds((i % NJ) * L, L)] = jnp.zeros((L,), jnp.float32)
                return cz

            lax.fori_loop(0, G * NJ, zbody, 0)
            issue(0, 0)

            def cbody(cc, c2):
                c = cc * 2
                issue(c + 1, 1)
                wait(c, 0)
                compute(c, 0)

                @pl.when(c + 2 < G * NCH)
                def _():
                    issue(c + 2, 0)

                wait(c + 1, 1)
                compute(c + 1, 1)
                return c2

            lax.fori_loop(0, G * NCH // 2, cbody, 0)
            pltpu.sync_copy(out_v, out_hbm.at[pl.ds(tok0, G)])
            return carry

        lax.fori_loop(0, ngroups, group, 0)

    return k(x, idx3, g3, u_tab, v_tab)


def rms_norm(x, g):
    xf = x.astype(jnp.float32)
    y = xf * lax.rsqrt(jnp.mean(xf * xf, axis=-1, keepdims=True) + EPS)
    return (y * g.astype(jnp.float32)).astype(x.dtype)


def layer_norm(x, g, b):
    xf = x.astype(jnp.float32)
    mu = jnp.mean(xf, axis=-1, keepdims=True)
    var = jnp.mean(jnp.square(xf - mu), axis=-1, keepdims=True)
    y = (xf - mu) * lax.rsqrt(var + EPS)
    return (y * g.astype(jnp.float32) + b.astype(jnp.float32)).astype(x.dtype)


def causal_dwconv(x, w, b):
    k = w.shape[0]
    y = lax.conv_general_dilated(
        x, w[:, None, :], window_strides=(1,), padding=[(k - 1, 0)],
        dimension_numbers=("NWC", "WIO", "NWC"), feature_group_count=x.shape[-1])
    return y + b


def forgetting_attention(q, k, v, log_f):
    b, s, h, hd = q.shape
    nb = s // Q_BLOCK
    scale = 1.0 / math.sqrt(hd)
    cum = jnp.cumsum(log_f.astype(jnp.float32), axis=1).transpose(0, 2, 1)
    q = q.transpose(0, 2, 1, 3)
    k = k.transpose(0, 2, 1, 3)
    v = v.transpose(0, 2, 1, 3)
    qb = q.reshape(b, h, nb, Q_BLOCK, hd).transpose(2, 0, 1, 3, 4)
    cb = cum.reshape(b, h, nb, Q_BLOCK).transpose(2, 0, 1, 3)
    kpos = jnp.arange(s)

    def one_block(args):
        qi, ci, blk = args
        qpos = blk * Q_BLOCK + jnp.arange(Q_BLOCK)
        logits = jnp.einsum("bhqd,bhkd->bhqk", qi, k).astype(jnp.float32) * scale
        logits = logits + ci[..., :, None] - cum[:, :, None, :]
        logits = jnp.where(kpos[None, :] <= qpos[:, None], logits, -jnp.inf)
        p = jax.nn.softmax(logits, axis=-1)
        return jnp.einsum("bhqk,bhkd->bhqd", p.astype(v.dtype), v)

    o = lax.map(one_block, (qb, cb, jnp.arange(nb)))
    return o.transpose(1, 0, 3, 2, 4).reshape(b, s, h * hd)


def conformer_conv(u, w_dw, b_dw, ln_g, ln_b):
    a, gate = u[..., :D_CONV], u[..., D_CONV:]
    y = a * jax.nn.sigmoid(gate)
    y = causal_dwconv(y, w_dw, b_dw)
    y = layer_norm(y, ln_g, ln_b)
    return jax.nn.silu(y)


def _lin_combine(c1, c2):
    a1, b1 = c1
    a2, b2 = c2
    return a1 * a2, a2 * b1 + b2


def rglru_block(xr, gate_in, conv_w, conv_b, w_r, b_r, w_i, b_i, lam):
    xr = causal_dwconv(xr, conv_w, conv_b)
    b, s, c = xr.shape
    xh = xr.reshape(b, s, RNN_BLOCKS, RNN_BD)
    r = jax.nn.sigmoid(jnp.einsum("bshi,hij->bshj", xh, w_r).reshape(b, s, c) + b_r)
    i = jax.nn.sigmoid(jnp.einsum("bshi,hij->bshj", xh, w_i).reshape(b, s, c) + b_i)
    log_a = -RG_C * r.astype(jnp.float32) * jax.nn.softplus(-lam.astype(jnp.float32))
    a = jnp.exp(log_a)
    mult = jnp.sqrt(-jnp.expm1(2.0 * log_a))
    bterm = mult * (i * xr).astype(jnp.float32)
    _, hs = lax.associative_scan(_lin_combine, (a, bterm), axis=1)
    return hs.astype(xr.dtype) * jax.nn.gelu(gate_in)


def peer(xn, wq, k1, k2, u_tab, v_tab):
    b, s, d = xn.shape
    t = b * s
    xt = xn.reshape(t, d)
    q = (xt @ wq).reshape(t, PEER_HEADS, D_KEY)
    q1, q2 = q[..., :D_HALF], q[..., D_HALF:]
    s1 = jnp.einsum("thd,hnd->thn", q1, k1).astype(jnp.float32)
    s2 = jnp.einsum("thd,hnd->thn", q2, k2).astype(jnp.float32)
    v1, i1 = lax.top_k(s1, TOPK)
    v2, i2 = lax.top_k(s2, TOPK)
    cand = (v1[..., :, None] + v2[..., None, :]).reshape(t, PEER_HEADS, TOPK * TOPK)
    sv, ci = lax.top_k(cand, TOPK)
    e1 = jnp.take_along_axis(i1, ci // TOPK, axis=-1)
    e2 = jnp.take_along_axis(i2, ci % TOPK, axis=-1)
    experts = (e1 * N_KEYS + e2).reshape(t, KSEL).astype(jnp.int32)
    gates = jax.nn.softmax(sv, axis=-1).astype(xn.dtype).reshape(t, KSEL)
    return peer_sc(xt, experts, gates, u_tab, v_tab).reshape(b, s, d)


def _fn_body(x_ref, g_ref, o_ref):
    xf = x_ref[...]
    o_ref[...] = xf * lax.rsqrt(jnp.mean(xf * xf, axis=-1, keepdims=True) + EPS) * g_ref[...]


def _final_norm(x, g):
    b, s, d = x.shape
    xt = x.reshape(b * s, d)
    o = pl.pallas_call(_fn_body, out_shape=jax.ShapeDtypeStruct(xt.shape, xt.dtype), grid=(b * s // 1024,),
        in_specs=[pl.BlockSpec((1024, d), lambda i: (i, 0)), pl.BlockSpec((1, d), lambda i: (0, 0))],
        out_specs=pl.BlockSpec((1024, d), lambda i: (i, 0)))(xt, g.reshape(1, d))
    return o.reshape(b, s, d)


def kernel(x, norm1_g, w_in, b_forget, conv_dw_w, conv_dw_b, conv_ln_g, conv_ln_b,
           rg_conv_w, rg_conv_b, rg_w_r, rg_b_r, rg_w_i, rg_b_i, rg_lambda, w_out,
           norm2_g, peer_wq, peer_k1, peer_k2, peer_u, peer_v, final_g):
    b, s, _ = x.shape
    for l in range(DEPTH):
        h = rms_norm(x, norm1_g[l])
        proj = h @ w_in[l]
        q = proj[..., Q0:Q1].reshape(b, s, ATT_HEADS, ATT_HD)
        k = proj[..., K0:K1].reshape(b, s, ATT_HEADS, ATT_HD)
        v = proj[..., V0:V1].reshape(b, s, ATT_HEADS, ATT_HD)
        log_f = jax.nn.log_sigmoid((proj[..., F0:F1] + b_forget[l]).astype(jnp.float32))
        y_att = forgetting_attention(q, k, v, log_f)
        y_conv = conformer_conv(proj[..., C0:C1], conv_dw_w[l], conv_dw_b[l],
                                conv_ln_g[l], conv_ln_b[l])
        y_rnn = rglru_block(proj[..., R0:R1], proj[..., G0:G1], rg_conv_w[l], rg_conv_b[l],
                            rg_w_r[l], rg_b_r[l], rg_w_i[l], rg_b_i[l], rg_lambda[l])
        mixed = jnp.concatenate([y_att, y_conv.astype(x.dtype), y_rnn.astype(x.dtype)], axis=-1)
        x = x + mixed @ w_out[l]
        h2 = rms_norm(x, norm2_g[l])
        x = x + peer(h2, peer_wq[l], peer_k1[l], peer_k2[l], peer_u[l], peer_v[l])
    return _final_norm(x, final_g)
```

```python
import functools
import math

import jax
import jax.numpy as jnp
from jax import lax
from jax.experimental import pallas as pl
from jax.experimental.pallas import tpu as pltpu
from jax.experimental.pallas import tpu_sc as plsc

BF = jnp.bfloat16
F32 = jnp.float32
I32 = jnp.int32

D_MODEL = 1024
DEPTH = 2
ATT_HEADS = 8
ATT_HD = 64
D_ATT = ATT_HEADS * ATT_HD
D_CONV = 256
CONV_K = 31
D_RNN = 256
RNN_BLOCKS = 4
RNN_CONV_K = 4
RG_C = 8.0
EPS = 1e-6
N_REST = 2 * D_CONV + 2 * D_RNN
PEER_HEADS = 8
N_KEYS = 128
D_HALF = 128
TOPK = 16
KSEL = PEER_HEADS * TOPK
GC = 0.7978845608028654
NEG = float("-inf")

LANES = 128
VMEM_LIMIT = 48 * 1024 * 1024


def _cp(sem):
    return pltpu.CompilerParams(dimension_semantics=sem, vmem_limit_bytes=VMEM_LIMIT)


def _split3(x):
    hi = x.astype(BF)
    r = x - hi.astype(F32)
    mid = r.astype(BF)
    lo = (r - mid.astype(F32)).astype(BF)
    return hi, mid, lo


def _nt(a, b):
    return lax.dot_general(a, b, (((1,), (1,)), ((), ())), preferred_element_type=F32)


def _dot(a, b):
    return jnp.dot(a, b, preferred_element_type=F32)


def _sigmoid(x):
    return 1.0 / (1.0 + jnp.exp(-x))


def _gelu(x):
    return 0.5 * x * (1.0 + jnp.tanh(GC * (x + 0.044715 * x * x * x)))


def _inproj_body(x_ref, g_ref, wqkv_ref, wft_ref, wrest_ref, bf_ref, tri_ref,
                 qkv_ref, rest_ref, cum_ref, carry_ref, *, blocks_per_seq, tm):
    i = pl.program_id(0)
    x = x_ref[...]
    h = x * lax.rsqrt(jnp.mean(x * x, axis=-1, keepdims=True) + EPS) * g_ref[...]
    hb = h.astype(BF)
    qkv = _dot(hb, wqkv_ref[...])
    col = lax.broadcasted_iota(I32, (1, 3 * D_ATT), 1)
    qkv = jnp.where(col < D_ATT, qkv * (1.0 / math.sqrt(ATT_HD)), qkv)
    qkv_ref[...] = qkv.astype(BF)
    rest_ref[...] = _dot(hb, wrest_ref[...])
    ft = _nt(wft_ref[...], hb) + bf_ref[...]
    lf = jnp.minimum(ft, 0.0) - jnp.log(1.0 + jnp.exp(-jnp.abs(ft)))
    hi, mid, lo = _split3(lf)
    tri = tri_ref[...]
    cs = _dot(hi, tri) + _dot(mid, tri) + _dot(lo, tri)

    @pl.when(i % blocks_per_seq == 0)
    def _():
        carry_ref[...] = jnp.zeros_like(carry_ref)

    cum = cs + carry_ref[...]
    cum_ref[...] = cum
    carry_ref[...] = cum[:, tm - 1:tm]


def in_proj(x, g, wqkv, wft, wrest, bfg, seq, tm=512):
    T = x.shape[0]
    tri = (lax.broadcasted_iota(I32, (tm, tm), 0) <= lax.broadcasted_iota(I32, (tm, tm), 1)).astype(BF)
    body = functools.partial(_inproj_body, blocks_per_seq=seq // tm, tm=tm)
    return pl.pallas_call(
        body, grid=(T // tm,),
        in_specs=[pl.BlockSpec((tm, D_MODEL), lambda i: (i, 0)),
                  pl.BlockSpec((1, D_MODEL), lambda i: (0, 0)),
                  pl.BlockSpec((D_MODEL, 3 * D_ATT), lambda i: (0, 0)),
                  pl.BlockSpec((16, D_MODEL), lambda i: (0, 0)),
                  pl.BlockSpec((D_MODEL, N_REST), lambda i: (0, 0)),
                  pl.BlockSpec((16, 1), lambda i: (0, 0)),
                  pl.BlockSpec((tm, tm), lambda i: (0, 0))],
        out_specs=[pl.BlockSpec((tm, 3 * D_ATT), lambda i: (i, 0)),
                   pl.BlockSpec((tm, N_REST), lambda i: (i, 0)),
                   pl.BlockSpec((16, tm), lambda i: (0, i))],
        out_shape=[jax.ShapeDtypeStruct((T, 3 * D_ATT), BF),
                   jax.ShapeDtypeStruct((T, N_REST), F32),
                   jax.ShapeDtypeStruct((16, T), F32)],
        scratch_shapes=[pltpu.VMEM((16, 1), F32)],
        compiler_params=_cp(("arbitrary",)), name="in_proj",
    )(x, g, wqkv, wft, wrest, bfg, tri)


def _attn_body(q_ref, k_ref, v_ref, ck_ref, o_ref, m_ref, l_ref, acc_ref, *, tq, tk):
    pr = pl.program_id(1)
    qi = pl.program_id(2)
    ki = pl.program_id(3)

    @pl.when(ki == 0)
    def _():
        m_ref[...] = jnp.full_like(m_ref, NEG)
        l_ref[...] = jnp.zeros_like(l_ref)
        acc_ref[...] = jnp.zeros_like(acc_ref)

    lane = lax.broadcasted_iota(I32, (1, LANES), 1)
    first = lane < ATT_HD

    def step(masked):
        q = q_ref[...]
        k = k_ref[...]
        v = v_ref[...]
        if masked:
            keep = (lax.broadcasted_iota(I32, (tq, tk), 1) <= lax.broadcasted_iota(I32, (tq, tk), 0))
        alphas, pvs = [], []
        for hh in range(2):
            sel = first if hh == 0 else jnp.logical_not(first)
            qm = jnp.where(sel, q, jnp.zeros_like(q))
            s = _nt(qm, k) - ck_ref[pl.ds(2 * pr + hh, 1), :]
            if masked:
                s = jnp.where(keep, s, NEG)
            m_prev = m_ref[hh]
            m_new = jnp.maximum(m_prev, jnp.max(s, axis=1, keepdims=True))
            alpha = jnp.exp(m_prev - m_new)
            p = jnp.exp(s - m_new)
            l_ref[hh] = alpha * l_ref[hh] + jnp.sum(p, axis=1, keepdims=True)
            m_ref[hh] = m_new
            pvs.append(_dot(p.astype(BF), v))
            alphas.append(alpha)
        acc_ref[...] = jnp.where(first, alphas[0], alphas[1]) * acc_ref[...] + jnp.where(first, pvs[0], pvs[1])

    @pl.when(ki < qi)
    def _():
        step(False)

    @pl.when(ki == qi)
    def _():
        step(True)
        o_ref[...] = (acc_ref[...] / jnp.where(first, l_ref[0], l_ref[1])).astype(o_ref.dtype)


def attention(qkv, cum, batch, seq, tq=512):
    T = qkv.shape[0]
    tk = tq
    nq = seq // tq
    body = functools.partial(_attn_body, tq=tq, tk=tk)
    npair = ATT_HEADS // 2
    kblk = lambda b, qi, ki: b * nq + jnp.minimum(ki, qi)
    return pl.pallas_call(
        body, grid=(batch, npair, nq, nq),
        in_specs=[pl.BlockSpec((tq, LANES), lambda b, p, qi, ki: (b * nq + qi, p)),
                  pl.BlockSpec((tk, LANES), lambda b, p, qi, ki: (kblk(b, qi, ki), npair + p)),
                  pl.BlockSpec((tk, LANES), lambda b, p, qi, ki: (kblk(b, qi, ki), 2 * npair + p)),
                  pl.BlockSpec((16, tk), lambda b, p, qi, ki: (0, kblk(b, qi, ki)))],
        out_specs=pl.BlockSpec((tq, LANES), lambda b, p, qi, ki: (b * nq + qi, p)),
        out_shape=jax.ShapeDtypeStruct((T, D_ATT), BF),
        scratch_shapes=[pltpu.VMEM((2, tq, 1), F32), pltpu.VMEM((2, tq, 1), F32), pltpu.VMEM((tq, LANES), F32)],
        compiler_params=_cp(("parallel", "parallel", "parallel", "arbitrary")), name="fox_attention",
    )(qkv, qkv, qkv, cum)


CONV_HALO = 32
RG_HALO = 8


def _mix_body(rest_ref, cw_ref, cb_ref, lg_ref, lb_ref, rw_ref, rb_ref, wr_ref, br_ref, wi_ref, bi_ref, lam_ref,
              o_ref, ybuf, xbuf, hc, *, ts):
    si = pl.program_id(1)

    @pl.when(si == 0)
    def _():
        ybuf[0:CONV_HALO, :] = jnp.zeros((CONV_HALO, D_CONV), F32)
        xbuf[0:RG_HALO, :] = jnp.zeros((RG_HALO, D_RNN), F32)
        hc[...] = jnp.zeros_like(hc)

    y = rest_ref[:, 0:D_CONV] * _sigmoid(rest_ref[:, D_CONV:2 * D_CONV])
    ybuf[CONV_HALO:CONV_HALO + ts, :] = y
    acc = jnp.zeros((ts, D_CONV), F32)
    for k in range(CONV_K):
        acc = acc + cw_ref[k:k + 1, :] * ybuf[pl.ds(CONV_HALO - (CONV_K - 1) + k, ts), :]
    yc = acc + cb_ref[...]
    mu = jnp.mean(yc, axis=-1, keepdims=True)
    var = jnp.mean(jnp.square(yc - mu), axis=-1, keepdims=True)
    yn = (yc - mu) * lax.rsqrt(var + EPS) * lg_ref[...] + lb_ref[...]
    o_ref[:, 0:D_CONV] = (yn * _sigmoid(yn)).astype(o_ref.dtype)
    ybuf[0:CONV_HALO, :] = ybuf[ts:ts + CONV_HALO, :]

    xbuf[RG_HALO:RG_HALO + ts, :] = rest_ref[:, 2 * D_CONV:2 * D_CONV + D_RNN]
    xc = jnp.zeros((ts, D_RNN), F32)
    for k in range(RNN_CONV_K):
        xc = xc + rw_ref[k:k + 1, :] * xbuf[pl.ds(RG_HALO - (RNN_CONV_K - 1) + k, ts), :]
    xc = xc + rb_ref[...]
    xbuf[0:RG_HALO, :] = xbuf[ts:ts + RG_HALO, :]
    xcb = xc.astype(BF)
    r = _sigmoid(_dot(xcb, wr_ref[...]) + br_ref[...])
    gi = _sigmoid(_dot(xcb, wi_ref[...]) + bi_ref[...])
    nl = -lam_ref[...]
    sp = jnp.maximum(nl, 0.0) + jnp.log(1.0 + jnp.exp(-jnp.abs(nl)))
    log_a = -RG_C * r * sp
    a = jnp.exp(log_a)
    bt = jnp.sqrt(1.0 - jnp.exp(2.0 * log_a)) * (gi * xc)
    row = lax.broadcasted_iota(I32, (ts, 1), 0)
    sh = 1
    while sh < ts:
        live = row >= sh
        a_s = jnp.where(live, pltpu.roll(a, sh, 0), 1.0)
        b_s = jnp.where(live, pltpu.roll(bt, sh, 0), 0.0)
        bt = bt + a * b_s
        a = a * a_s
        sh *= 2
    h = bt + a * hc[...]
    hc[...] = h[ts - 1:ts, :]
    gate_in = rest_ref[:, 2 * D_CONV + D_RNN:2 * D_CONV + 2 * D_RNN]
    o_ref[:, D_CONV:D_CONV + D_RNN] = (h * _gelu(gate_in)).astype(o_ref.dtype)


def mixers(rest, p, batch, seq, ts=512):
    T = rest.shape[0]
    ns = seq // ts
    body = functools.partial(_mix_body, ts=ts)
    vec = lambda: pl.BlockSpec((1, D_CONV), lambda b, s: (0, 0))
    return pl.pallas_call(
        body, grid=(batch, ns),
        in_specs=[pl.BlockSpec((ts, N_REST), lambda b, s: (b * ns + s, 0)),
                  pl.BlockSpec((32, D_CONV), lambda b, s: (0, 0)), vec(), vec(), vec(),
                  pl.BlockSpec((8, D_RNN), lambda b, s: (0, 0)), vec(),
                  pl.BlockSpec((D_RNN, D_RNN), lambda b, s: (0, 0)), vec(),
                  pl.BlockSpec((D_RNN, D_RNN), lambda b, s: (0, 0)), vec(), vec()],
        out_specs=pl.BlockSpec((ts, D_CONV + D_RNN), lambda b, s: (b * ns + s, 0)),
        out_shape=jax.ShapeDtypeStruct((T, D_CONV + D_RNN), BF),
        scratch_shapes=[pltpu.VMEM((ts + CONV_HALO, D_CONV), F32), pltpu.VMEM((ts + RG_HALO, D_RNN), F32),
                        pltpu.VMEM((1, D_RNN), F32)],
        compiler_params=_cp(("arbitrary", "arbitrary")), name="conv_rglru",
    )(rest, p["cw"], p["cb"], p["lg"], p["lb"], p["rw"], p["rb"], p["wr"], p["br"], p["wi"], p["bi"], p["lam"])


def _outproj_body(x_ref, ya_ref, yc_ref, woa_ref, wob_ref, g2_ref, wq_ref, keys_ref, x1_ref, h2_ref, st_ref):
    x1 = x_ref[...] + _dot(ya_ref[...], woa_ref[...]) + _dot(yc_ref[...], wob_ref[...])
    x1_ref[...] = x1
    h2 = x1 * lax.rsqrt(jnp.mean(x1 * x1, axis=-1, keepdims=True) + EPS) * g2_ref[...]
    h2_ref[...] = h2
    q = _dot(h2.astype(BF), wq_ref[...]).astype(BF)
    for g in range(2 * PEER_HEADS):
        st_ref[g * N_KEYS:(g + 1) * N_KEYS, :] = _nt(keys_ref[g], q[:, g * D_HALF:(g + 1) * D_HALF])


def out_proj(x, ya, yc, p, g2, tm=256):
    T = x.shape[0]
    ng = 2 * PEER_HEADS
    return pl.pallas_call(
        _outproj_body, grid=(T // tm,),
        in_specs=[pl.BlockSpec((tm, D_MODEL), lambda i: (i, 0)),
                  pl.BlockSpec((tm, D_ATT), lambda i: (i, 0)),
                  pl.BlockSpec((tm, D_CONV + D_RNN), lambda i: (i, 0)),
                  pl.BlockSpec((D_ATT, D_MODEL), lambda i: (0, 0)),
                  pl.BlockSpec((D_CONV + D_RNN, D_MODEL), lambda i: (0, 0)),
                  pl.BlockSpec((1, D_MODEL), lambda i: (0, 0)),
                  pl.BlockSpec((D_MODEL, ng * D_HALF), lambda i: (0, 0)),
                  pl.BlockSpec((ng, N_KEYS, D_HALF), lambda i: (0, 0, 0))],
        out_specs=[pl.BlockSpec((tm, D_MODEL), lambda i: (i, 0)),
                   pl.BlockSpec((tm, D_MODEL), lambda i: (i, 0)),
                   pl.BlockSpec((ng * N_KEYS, tm), lambda i: (0, i))],
        out_shape=[jax.ShapeDtypeStruct((T, D_MODEL), F32),
                   jax.ShapeDtypeStruct((T, D_MODEL), F32),
                   jax.ShapeDtypeStruct((ng * N_KEYS, T), F32)],
        compiler_params=_cp(("parallel",)), name="out_proj_peer_scores",
    )(x, ya, yc, p["woa"], p["wob"], g2, p["wq"], p["keys"])


BIG_ID = 1 << 20


def _take_rounds(slabs, ids, nrounds):
    vals, picks = [], []
    for _ in range(nrounds):
        m = None
        for s in slabs:
            ms = jnp.max(s, axis=0, keepdims=True)
            m = ms if m is None else jnp.maximum(m, ms)
        pick = None
        for s, idc in zip(slabs, ids):
            ps = jnp.min(jnp.where(s == m, idc, BIG_ID), axis=0, keepdims=True)
            pick = ps if pick is None else jnp.minimum(pick, ps)
        slabs = [jnp.where(idc == pick, NEG, s) for s, idc in zip(slabs, ids)]
        vals.append(m)
        picks.append(pick)
    return vals, picks


def _route_body(st_ref, e_ref, g_ref, v_scr, i_scr, sv_scr, ci_scr, et_scr, gt_scr):
    ng = 2 * PEER_HEADS
    key_id = lax.broadcasted_iota(I32, (N_KEYS, LANES), 0)

    def stage1(g, carry):
        s = st_ref[pl.ds(pl.multiple_of(g * N_KEYS, N_KEYS), N_KEYS), :]
        vals, picks = _take_rounds([s], [key_id], TOPK)
        for r in range(TOPK):
            v_scr[g, r:r + 1, :] = vals[r]
            i_scr[g, r:r + 1, :] = picks[r]
        return carry

    lax.fori_loop(0, ng, stage1, 0)

    j16 = lax.broadcasted_iota(I32, (TOPK, LANES), 0)
    j8 = lax.broadcasted_iota(I32, (8, LANES), 0)

    def stage2(h, carry):
        v1 = v_scr[2 * h]
        v2 = v_scr[2 * h + 1]
        i1 = i_scr[2 * h]
        i2 = i_scr[2 * h + 1]
        slabs = [v1[0:1, :] + v2]
        ids = [j16]
        for i in range(1, TOPK):
            nj = TOPK // (i + 1)
            slabs.append(jnp.where(j8 < nj, v1[i:i + 1, :] + v2[0:8, :], NEG))
            ids.append(j8 + i * TOPK)
        vals, picks = _take_rounds(slabs, ids, TOPK)
        for r in range(TOPK):
            sv_scr[r:r + 1, :] = vals[r]
            ci_scr[r:r + 1, :] = picks[r]
        sv = sv_scr[...]
        ci = ci_scr[...]
        ci_hi = lax.shift_right_logical(ci, 4)
        ci_lo = jnp.bitwise_and(ci, TOPK - 1)
        e1 = jnp.zeros((TOPK, LANES), I32)
        e2 = jnp.zeros((TOPK, LANES), I32)
        for i in range(TOPK):
            e1 = jnp.where(ci_hi == i, i1[i:i + 1, :], e1)
            e2 = jnp.where(ci_lo == i, i2[i:i + 1, :], e2)
        p = jnp.exp(sv - sv[0:1, :])
        gates = p / jnp.sum(p, axis=0, keepdims=True)
        et_scr[pl.ds(pl.multiple_of(h * TOPK, TOPK), TOPK), :] = e1 * N_KEYS + e2
        gt_scr[pl.ds(pl.multiple_of(h * TOPK, TOPK), TOPK), :] = gates
        return carry

    lax.fori_loop(0, PEER_HEADS, stage2, 0)
    e_ref[...] = et_scr[...].T
    g_ref[...] = gt_scr[...].T


def route(st):
    T = st.shape[1]
    ng = 2 * PEER_HEADS
    return pl.pallas_call(
        _route_body, grid=(T // LANES,),
        in_specs=[pl.BlockSpec((ng * N_KEYS, LANES), lambda i: (0, i))],
        out_specs=[pl.BlockSpec((LANES, KSEL), lambda i: (i, 0)),
                   pl.BlockSpec((LANES, KSEL), lambda i: (i, 0))],
        out_shape=[jax.ShapeDtypeStruct((T, KSEL), I32),
                   jax.ShapeDtypeStruct((T, KSEL), F32)],
        scratch_shapes=[pltpu.VMEM((ng, TOPK, LANES), F32), pltpu.VMEM((ng, TOPK, LANES), I32),
                        pltpu.VMEM((TOPK, LANES), F32), pltpu.VMEM((TOPK, LANES), I32),
                        pltpu.VMEM((KSEL, LANES), I32), pltpu.VMEM((KSEL, LANES), F32)],
        compiler_params=_cp(("parallel",)), name="peer_route",
    )(st)


NC, NS, L = 2, 16, 16
NW = NC * NS
NJ = D_MODEL // L
R = TOPK
NCH = KSEL // R
G = 8


def _perm(x, idx):
    return jnp.take_along_axis(x, idx, axis=0, mode="promise_in_bounds")


def peer_sc(x, resid, idx, gates, u_tab, v_tab):
    T = x.shape[0]
    tpw = T // NW
    ngroups = tpw // G
    idx3 = idx.reshape(T * NCH, R)
    g3 = gates.reshape(T * NCH, R)
    mesh = plsc.VectorSubcoreMesh(core_axis_name="c", subcore_axis_name="s")

    @functools.partial(
        pl.kernel, mesh=mesh,
        out_type=jax.ShapeDtypeStruct((T, D_MODEL), F32),
        scratch_types=[
            pltpu.VMEM((G, D_MODEL), F32),
            pltpu.VMEM((G, D_MODEL), F32),
            pltpu.VMEM((G * NCH, R), I32),
            pltpu.VMEM((G * NCH, R), F32),
            pltpu.VMEM((R, D_MODEL), F32),
            pltpu.VMEM((R, D_MODEL), F32),
            pltpu.VMEM((R, D_MODEL), F32),
            pltpu.VMEM((R, D_MODEL), F32),
            pltpu.SemaphoreType.DMA,
            pltpu.SemaphoreType.DMA,
            pltpu.SemaphoreType.DMA,
            pltpu.SemaphoreType.DMA,
        ],
        compiler_params=pltpu.CompilerParams(needs_layout_passes=False),
        name="peer_experts_sc",
    )
    def k(x_hbm, r_hbm, idx_hbm, g_hbm, u_hbm, v_hbm, out_hbm,
          x_v, out_v, idx_v, g_v, ub0, ub1, vb0, vb1, su0, su1, sv0, sv1):
        wid = lax.axis_index("s") * NC + lax.axis_index("c")
        ubs, vbs, sus, svs = (ub0, ub1), (vb0, vb1), (su0, su1), (sv0, sv1)
        iota = lax.iota(I32, L)

        def gather_copies(c, b):
            return (pltpu.make_async_copy(u_hbm.at[idx_v.at[c]], ubs[b], sus[b]),
                    pltpu.make_async_copy(v_hbm.at[idx_v.at[c]], vbs[b], svs[b]))

        def issue(c, b):
            for cp in gather_copies(c, b):
                cp.start()

        def wait(c, b):
            for cp in gather_copies(c, b):
                cp.wait()

        def compute(c, b):
            ub, vb = ubs[b], vbs[b]
            t = c // NCH

            def ubody(j, accs):
                xj = x_v[t, pl.ds(j * L, L)]
                return tuple(accs[kk] + xj * ub[kk, pl.ds(j * L, L)] for kk in range(R))

            accs = lax.fori_loop(0, NJ, ubody, tuple(jnp.zeros((L,), F32) for _ in range(R)))
            vecs = list(accs)
            dist = L // 2
            while dist >= 1:
                pidx = jnp.bitwise_xor(iota, dist)
                low = jnp.bitwise_and(iota, dist) == 0
                nxt = []
                for kk in range(dist):
                    a = vecs[kk]
                    bvec = vecs[kk + dist]
                    a = a + _perm(a, pidx)
                    bvec = bvec + _perm(bvec, pidx)
                    nxt.append(jnp.where(low, a, bvec))
                vecs = nxt
                dist //= 2
            hid = vecs[0]
            z = GC * (hid + 0.044715 * hid * hid * hid)
            gel = hid / (1.0 + jnp.exp(-2.0 * z))
            w = g_v[c, :] * gel
            wbs = [_perm(w, jnp.full((L,), kk, I32)) for kk in range(R)]

            def vbody(j, carry):
                acc = out_v[t, pl.ds(j * L, L)]
                for kk in range(R):
                    acc = acc + wbs[kk] * vb[kk, pl.ds(j * L, L)]
                out_v[t, pl.ds(j * L, L)] = acc
                return carry

            lax.fori_loop(0, NJ, vbody, 0)

        def group(g, carry):
            tok0 = wid * tpw + g * G
            pltpu.sync_copy(x_hbm.at[pl.ds(tok0, G)], x_v)
            pltpu.sync_copy(r_hbm.at[pl.ds(tok0, G)], out_v)
            pltpu.sync_copy(idx_hbm.at[pl.ds(tok0 * NCH, G * NCH)], idx_v)
            pltpu.sync_copy(g_hbm.at[pl.ds(tok0 * NCH, G * NCH)], g_v)
            issue(0, 0)

            def cbody(cc, c2):
                c = cc * 2
                issue(c + 1, 1)
                wait(c, 0)
                compute(c, 0)

                @pl.when(c + 2 < G * NCH)
                def _():
                    issue(c + 2, 0)

                wait(c + 1, 1)
                compute(c + 1, 1)
                return c2

            lax.fori_loop(0, G * NCH // 2, cbody, 0)
            pltpu.sync_copy(out_v, out_hbm.at[pl.ds(tok0, G)])
            return carry

        lax.fori_loop(0, ngroups, group, 0)

    return k(x, resid, idx3, g3, u_tab, v_tab)


def _fn_body(x_ref, g_ref, o_ref):
    xf = x_ref[...]
    o_ref[...] = xf * lax.rsqrt(jnp.mean(xf * xf, axis=-1, keepdims=True) + EPS) * g_ref[...]


def final_norm(x, g, tm=1024):
    T, d = x.shape
    return pl.pallas_call(
        _fn_body, grid=(T // tm,),
        in_specs=[pl.BlockSpec((tm, d), lambda i: (i, 0)), pl.BlockSpec((1, d), lambda i: (0, 0))],
        out_specs=pl.BlockSpec((tm, d), lambda i: (i, 0)),
        out_shape=jax.ShapeDtypeStruct((T, d), F32),
        compiler_params=_cp(("parallel",)), name="final_norm",
    )(x, g)


def _prep_layer(w_in, b_forget, conv_dw_w, conv_dw_b, conv_ln_g, conv_ln_b, rg_conv_w, rg_conv_b,
                rg_w_r, rg_b_r, rg_w_i, rg_b_i, rg_lambda, w_out, peer_wq, peer_k1, peer_k2):
    f0 = 3 * D_ATT
    wft = jnp.zeros((16, D_MODEL), BF).at[0:ATT_HEADS].set(w_in[:, f0:f0 + ATT_HEADS].T.astype(BF))
    bfg = jnp.zeros((16, 1), F32).at[0:ATT_HEADS, 0].set(b_forget)
    cw = jnp.zeros((32, D_CONV), F32).at[0:CONV_K].set(conv_dw_w)
    rw = jnp.zeros((8, D_RNN), F32).at[0:RNN_CONV_K].set(rg_conv_w)
    bd = lambda w: jax.scipy.linalg.block_diag(*[w[i] for i in range(RNN_BLOCKS)]).astype(BF)
    row = lambda v: v.reshape(1, -1).astype(F32)
    keys = jnp.stack([peer_k1, peer_k2], axis=1).reshape(2 * PEER_HEADS, N_KEYS, D_HALF).astype(BF)
    return dict(wqkv=w_in[:, 0:f0].astype(BF), wft=wft, wrest=w_in[:, f0 + ATT_HEADS:].astype(BF), bfg=bfg,
                cw=cw, cb=row(conv_dw_b), lg=row(conv_ln_g), lb=row(conv_ln_b),
                rw=rw, rb=row(rg_conv_b), wr=bd(rg_w_r), br=row(rg_b_r), wi=bd(rg_w_i), bi=row(rg_b_i),
                lam=row(rg_lambda), woa=w_out[0:D_ATT].astype(BF), wob=w_out[D_ATT:].astype(BF),
                wq=peer_wq.astype(BF), keys=keys)


def kernel(x, norm1_g, w_in, b_forget, conv_dw_w, conv_dw_b, conv_ln_g, conv_ln_b,
           rg_conv_w, rg_conv_b, rg_w_r, rg_b_r, rg_w_i, rg_b_i, rg_lambda, w_out,
           norm2_g, peer_wq, peer_k1, peer_k2, peer_u, peer_v, final_g):
    b, s, d = x.shape
    xt = x.reshape(b * s, d)
    for l in range(DEPTH):
        p = _prep_layer(w_in[l], b_forget[l], conv_dw_w[l], conv_dw_b[l], conv_ln_g[l], conv_ln_b[l],
                        rg_conv_w[l], rg_conv_b[l], rg_w_r[l], rg_b_r[l], rg_w_i[l], rg_b_i[l], rg_lambda[l],
                        w_out[l], peer_wq[l], peer_k1[l], peer_k2[l])
        qkv, rest, cum = in_proj(xt, norm1_g[l].reshape(1, d), p["wqkv"], p["wft"], p["wrest"], p["bfg"], s)
        y_att = attention(qkv, cum, b, s)
        y_cr = mixers(rest, p, b, s)
        x1, h2, st = out_proj(xt, y_att, y_cr, p, norm2_g[l].reshape(1, d))
        experts, gates = route(st)
        xt = peer_sc(h2, x1, experts, gates, peer_u[l], peer_v[l])
    return final_norm(xt, final_g.reshape(1, d)).reshape(b, s, d)
```

```python
import functools
import math

import jax
import jax.numpy as jnp
from jax import lax
from jax.experimental import pallas as pl
from jax.experimental.pallas import tpu as pltpu
from jax.experimental.pallas import tpu_sc as plsc

BF = jnp.bfloat16
F32 = jnp.float32
I32 = jnp.int32

D_MODEL = 1024
DEPTH = 2
ATT_HEADS = 8
ATT_HD = 64
D_ATT = ATT_HEADS * ATT_HD
D_CONV = 256
CONV_K = 31
D_RNN = 256
RNN_BLOCKS = 4
RNN_CONV_K = 4
RG_C = 8.0
EPS = 1e-6
N_REST = 2 * D_CONV + 2 * D_RNN
PEER_HEADS = 8
N_KEYS = 128
D_HALF = 128
TOPK = 16
KSEL = PEER_HEADS * TOPK
GC = 0.7978845608028654
NEG = float("-inf")

N_SLICES = 2
LANES = 128
VMEM_LIMIT = 48 * 1024 * 1024


def _cp(sem):
    return pltpu.CompilerParams(dimension_semantics=sem, vmem_limit_bytes=VMEM_LIMIT)


def _split3(x):
    hi = x.astype(BF)
    r = x - hi.astype(F32)
    mid = r.astype(BF)
    lo = (r - mid.astype(F32)).astype(BF)
    return hi, mid, lo


def _nt(a, b):
    return lax.dot_general(a, b, (((1,), (1,)), ((), ())), preferred_element_type=F32)


def _dot(a, b):
    return jnp.dot(a, b, preferred_element_type=F32)


def _sigmoid(x):
    return 1.0 / (1.0 + jnp.exp(-x))


def _gelu(x):
    return 0.5 * x * (1.0 + jnp.tanh(GC * (x + 0.044715 * x * x * x)))


def _inproj_body(x_ref, g_ref, wqkv_ref, wft_ref, wrest_ref, bf_ref, tri_ref,
                 qkv_ref, rest_ref, cum_ref, carry_ref, *, blocks_per_seq, tm):
    i = pl.program_id(0)
    x = x_ref[...]
    h = x * lax.rsqrt(jnp.mean(x * x, axis=-1, keepdims=True) + EPS) * g_ref[...]
    hb = h.astype(BF)
    qkv = _dot(hb, wqkv_ref[...])
    col = lax.broadcasted_iota(I32, (1, 3 * D_ATT), 1)
    qkv = jnp.where(col < D_ATT, qkv * (1.0 / math.sqrt(ATT_HD)), qkv)
    qkv_ref[...] = qkv.astype(BF)
    rest_ref[...] = _dot(hb, wrest_ref[...])
    ft = _nt(wft_ref[...], hb) + bf_ref[...]
    lf = jnp.minimum(ft, 0.0) - jnp.log(1.0 + jnp.exp(-jnp.abs(ft)))
    hi, mid, lo = _split3(lf)
    tri = tri_ref[...]
    cs = _dot(hi, tri) + _dot(mid, tri) + _dot(lo, tri)

    @pl.when(i % blocks_per_seq == 0)
    def _():
        carry_ref[...] = jnp.zeros_like(carry_ref)

    cum = cs + carry_ref[...]
    cum_ref[...] = cum
    carry_ref[...] = cum[:, tm - 1:tm]


def in_proj(x, g, wqkv, wft, wrest, bfg, seq, tm=512):
    T = x.shape[0]
    tri = (lax.broadcasted_iota(I32, (tm, tm), 0) <= lax.broadcasted_iota(I32, (tm, tm), 1)).astype(BF)
    body = functools.partial(_inproj_body, blocks_per_seq=seq // tm, tm=tm)
    return pl.pallas_call(
        body, grid=(T // tm,),
        in_specs=[pl.BlockSpec((tm, D_MODEL), lambda i: (i, 0)),
                  pl.BlockSpec((1, D_MODEL), lambda i: (0, 0)),
                  pl.BlockSpec((D_MODEL, 3 * D_ATT), lambda i: (0, 0)),
                  pl.BlockSpec((16, D_MODEL), lambda i: (0, 0)),
                  pl.BlockSpec((D_MODEL, N_REST), lambda i: (0, 0)),
                  pl.BlockSpec((16, 1), lambda i: (0, 0)),
                  pl.BlockSpec((tm, tm), lambda i: (0, 0))],
        out_specs=[pl.BlockSpec((tm, 3 * D_ATT), lambda i: (i, 0)),
                   pl.BlockSpec((tm, N_REST), lambda i: (i, 0)),
                   pl.BlockSpec((16, tm), lambda i: (0, i))],
        out_shape=[jax.ShapeDtypeStruct((T, 3 * D_ATT), BF),
                   jax.ShapeDtypeStruct((T, N_REST), F32),
                   jax.ShapeDtypeStruct((16, T), F32)],
        scratch_shapes=[pltpu.VMEM((16, 1), F32)],
        compiler_params=_cp(("arbitrary",)), name="in_proj",
    )(x, g, wqkv, wft, wrest, bfg, tri)


def _attn_body(q_ref, k_ref, v_ref, ck_ref, o_ref, m_ref, l_ref, acc_ref, *, tq, tk):
    pr = pl.program_id(1)
    qi = pl.program_id(2)
    ki = pl.program_id(3)

    @pl.when(ki == 0)
    def _():
        m_ref[...] = jnp.full_like(m_ref, NEG)
        l_ref[...] = jnp.zeros_like(l_ref)
        acc_ref[...] = jnp.zeros_like(acc_ref)

    lane = lax.broadcasted_iota(I32, (1, LANES), 1)
    first = lane < ATT_HD

    def step(masked):
        q = q_ref[...]
        k = k_ref[...]
        v = v_ref[...]
        if masked:
            keep = (lax.broadcasted_iota(I32, (tq, tk), 1) <= lax.broadcasted_iota(I32, (tq, tk), 0))
        alphas, pvs = [], []
        for hh in range(2):
            sel = first if hh == 0 else jnp.logical_not(first)
            qm = jnp.where(sel, q, jnp.zeros_like(q))
            s = _nt(qm, k) - ck_ref[pl.ds(2 * pr + hh, 1), :]
            if masked:
                s = jnp.where(keep, s, NEG)
            m_prev = m_ref[hh]
            m_new = jnp.maximum(m_prev, jnp.max(s, axis=1, keepdims=True))
            alpha = jnp.exp(m_prev - m_new)
            p = jnp.exp(s - m_new)
            l_ref[hh] = alpha * l_ref[hh] + jnp.sum(p, axis=1, keepdims=True)
            m_ref[hh] = m_new
            pvs.append(_dot(p.astype(BF), v))
            alphas.append(alpha)
        acc_ref[...] = jnp.where(first, alphas[0], alphas[1]) * acc_ref[...] + jnp.where(first, pvs[0], pvs[1])

    @pl.when(ki < qi)
    def _():
        step(False)

    @pl.when(ki == qi)
    def _():
        step(True)
        o_ref[...] = (acc_ref[...] / jnp.where(first, l_ref[0], l_ref[1])).astype(o_ref.dtype)


def attention(qkv, cum, batch, seq, tq=512):
    T = qkv.shape[0]
    tk = tq
    nq = seq // tq
    body = functools.partial(_attn_body, tq=tq, tk=tk)
    npair = ATT_HEADS // 2
    kblk = lambda b, qi, ki: b * nq + jnp.minimum(ki, qi)
    return pl.pallas_call(
        body, grid=(batch, npair, nq, nq),
        in_specs=[pl.BlockSpec((tq, LANES), lambda b, p, qi, ki: (b * nq + qi, p)),
                  pl.BlockSpec((tk, LANES), lambda b, p, qi, ki: (kblk(b, qi, ki), npair + p)),
                  pl.BlockSpec((tk, LANES), lambda b, p, qi, ki: (kblk(b, qi, ki), 2 * npair + p)),
                  pl.BlockSpec((16, tk), lambda b, p, qi, ki: (0, kblk(b, qi, ki)))],
        out_specs=pl.BlockSpec((tq, LANES), lambda b, p, qi, ki: (b * nq + qi, p)),
        out_shape=jax.ShapeDtypeStruct((T, D_ATT), BF),
        scratch_shapes=[pltpu.VMEM((2, tq, 1), F32), pltpu.VMEM((2, tq, 1), F32), pltpu.VMEM((tq, LANES), F32)],
        compiler_params=_cp(("parallel", "parallel", "parallel", "arbitrary")), name="fox_attention",
    )(qkv, qkv, qkv, cum)


CONV_HALO = 32
RG_HALO = 8


def _mix_body(rest_ref, cw_ref, cb_ref, lg_ref, lb_ref, rw_ref, rb_ref, wr_ref, br_ref, wi_ref, bi_ref, lam_ref,
              o_ref, ybuf, xbuf, hc, *, ts):
    si = pl.program_id(1)

    @pl.when(si == 0)
    def _():
        ybuf[0:CONV_HALO, :] = jnp.zeros((CONV_HALO, D_CONV), F32)
        xbuf[0:RG_HALO, :] = jnp.zeros((RG_HALO, D_RNN), F32)
        hc[...] = jnp.zeros_like(hc)

    y = rest_ref[:, 0:D_CONV] * _sigmoid(rest_ref[:, D_CONV:2 * D_CONV])
    ybuf[CONV_HALO:CONV_HALO + ts, :] = y
    acc = jnp.zeros((ts, D_CONV), F32)
    for k in range(CONV_K):
        acc = acc + cw_ref[k:k + 1, :] * ybuf[pl.ds(CONV_HALO - (CONV_K - 1) + k, ts), :]
    yc = acc + cb_ref[...]
    mu = jnp.mean(yc, axis=-1, keepdims=True)
    var = jnp.mean(jnp.square(yc - mu), axis=-1, keepdims=True)
    yn = (yc - mu) * lax.rsqrt(var + EPS) * lg_ref[...] + lb_ref[...]
    o_ref[:, 0:D_CONV] = (yn * _sigmoid(yn)).astype(o_ref.dtype)
    ybuf[0:CONV_HALO, :] = ybuf[ts:ts + CONV_HALO, :]

    xbuf[RG_HALO:RG_HALO + ts, :] = rest_ref[:, 2 * D_CONV:2 * D_CONV + D_RNN]
    xc = jnp.zeros((ts, D_RNN), F32)
    for k in range(RNN_CONV_K):
        xc = xc + rw_ref[k:k + 1, :] * xbuf[pl.ds(RG_HALO - (RNN_CONV_K - 1) + k, ts), :]
    xc = xc + rb_ref[...]
    xbuf[0:RG_HALO, :] = xbuf[ts:ts + RG_HALO, :]
    xcb = xc.astype(BF)
    r = _sigmoid(_dot(xcb, wr_ref[...]) + br_ref[...])
    gi = _sigmoid(_dot(xcb, wi_ref[...]) + bi_ref[...])
    nl = -lam_ref[...]
    sp = jnp.maximum(nl, 0.0) + jnp.log(1.0 + jnp.exp(-jnp.abs(nl)))
    log_a = -RG_C * r * sp
    a = jnp.exp(log_a)
    bt = jnp.sqrt(1.0 - jnp.exp(2.0 * log_a)) * (gi * xc)
    row = lax.broadcasted_iota(I32, (ts, 1), 0)
    sh = 1
    while sh < ts:
        live = row >= sh
        a_s = jnp.where(live, pltpu.roll(a, sh, 0), 1.0)
        b_s = jnp.where(live, pltpu.roll(bt, sh, 0), 0.0)
        bt = bt + a * b_s
        a = a * a_s
        sh *= 2
    h = bt + a * hc[...]
    hc[...] = h[ts - 1:ts, :]
    gate_in = rest_ref[:, 2 * D_CONV + D_RNN:2 * D_CONV + 2 * D_RNN]
    o_ref[:, D_CONV:D_CONV + D_RNN] = (h * _gelu(gate_in)).astype(o_ref.dtype)


def mixers(rest, p, batch, seq, ts=512):
    T = rest.shape[0]
    ns = seq // ts
    body = functools.partial(_mix_body, ts=ts)
    vec = lambda: pl.BlockSpec((1, D_CONV), lambda b, s: (0, 0))
    return pl.pallas_call(
        body, grid=(batch, ns),
        in_specs=[pl.BlockSpec((ts, N_REST), lambda b, s: (b * ns + s, 0)),
                  pl.BlockSpec((32, D_CONV), lambda b, s: (0, 0)), vec(), vec(), vec(),
                  pl.BlockSpec((8, D_RNN), lambda b, s: (0, 0)), vec(),
                  pl.BlockSpec((D_RNN, D_RNN), lambda b, s: (0, 0)), vec(),
                  pl.BlockSpec((D_RNN, D_RNN), lambda b, s: (0, 0)), vec(), vec()],
        out_specs=pl.BlockSpec((ts, D_CONV + D_RNN), lambda b, s: (b * ns + s, 0)),
        out_shape=jax.ShapeDtypeStruct((T, D_CONV + D_RNN), BF),
        scratch_shapes=[pltpu.VMEM((ts + CONV_HALO, D_CONV), F32), pltpu.VMEM((ts + RG_HALO, D_RNN), F32),
                        pltpu.VMEM((1, D_RNN), F32)],
        compiler_params=_cp(("arbitrary", "arbitrary")), name="conv_rglru",
    )(rest, p["cw"], p["cb"], p["lg"], p["lb"], p["rw"], p["rb"], p["wr"], p["br"], p["wi"], p["bi"], p["lam"])


def _outproj_body(x_ref, ya_ref, yc_ref, woa_ref, wob_ref, g2_ref, wq_ref, keys_ref, x1_ref, h2_ref, st_ref):
    x1 = x_ref[...] + _dot(ya_ref[...], woa_ref[...]) + _dot(yc_ref[...], wob_ref[...])
    x1_ref[...] = x1
    h2 = x1 * lax.rsqrt(jnp.mean(x1 * x1, axis=-1, keepdims=True) + EPS) * g2_ref[...]
    h2_ref[...] = h2
    q = _dot(h2.astype(BF), wq_ref[...]).astype(BF)
    for g in range(2 * PEER_HEADS):
        st_ref[g * N_KEYS:(g + 1) * N_KEYS, :] = _nt(keys_ref[g], q[:, g * D_HALF:(g + 1) * D_HALF])


def out_proj(x, ya, yc, p, g2, tm=256):
    T = x.shape[0]
    ng = 2 * PEER_HEADS
    return pl.pallas_call(
        _outproj_body, grid=(T // tm,),
        in_specs=[pl.BlockSpec((tm, D_MODEL), lambda i: (i, 0)),
                  pl.BlockSpec((tm, D_ATT), lambda i: (i, 0)),
                  pl.BlockSpec((tm, D_CONV + D_RNN), lambda i: (i, 0)),
                  pl.BlockSpec((D_ATT, D_MODEL), lambda i: (0, 0)),
                  pl.BlockSpec((D_CONV + D_RNN, D_MODEL), lambda i: (0, 0)),
                  pl.BlockSpec((1, D_MODEL), lambda i: (0, 0)),
                  pl.BlockSpec((D_MODEL, ng * D_HALF), lambda i: (0, 0)),
                  pl.BlockSpec((ng, N_KEYS, D_HALF), lambda i: (0, 0, 0))],
        out_specs=[pl.BlockSpec((tm, D_MODEL), lambda i: (i, 0)),
                   pl.BlockSpec((tm, D_MODEL), lambda i: (i, 0)),
                   pl.BlockSpec((ng * N_KEYS, tm), lambda i: (0, i))],
        out_shape=[jax.ShapeDtypeStruct((T, D_MODEL), F32),
                   jax.ShapeDtypeStruct((T, D_MODEL), F32),
                   jax.ShapeDtypeStruct((ng * N_KEYS, T), F32)],
        compiler_params=_cp(("parallel",)), name="out_proj_peer_scores",
    )(x, ya, yc, p["woa"], p["wob"], g2, p["wq"], p["keys"])


BIG_ID = 1 << 20


def _take_rounds(slabs, ids, nrounds):
    vals, picks = [], []
    for _ in range(nrounds):
        m = None
        for s in slabs:
            ms = jnp.max(s, axis=0, keepdims=True)
            m = ms if m is None else jnp.maximum(m, ms)
        pick = None
        for s, idc in zip(slabs, ids):
            ps = jnp.min(jnp.where(s == m, idc, BIG_ID), axis=0, keepdims=True)
            pick = ps if pick is None else jnp.minimum(pick, ps)
        slabs = [jnp.where(idc == pick, NEG, s) for s, idc in zip(slabs, ids)]
        vals.append(m)
        picks.append(pick)
    return vals, picks


def _route_body(st_ref, e_ref, g_ref, v_scr, i_scr, sv_scr, ci_scr, et_scr, gt_scr):
    ng = 2 * PEER_HEADS
    key_id = lax.broadcasted_iota(I32, (N_KEYS, LANES), 0)

    def stage1(g, carry):
        s = st_ref[pl.ds(pl.multiple_of(g * N_KEYS, N_KEYS), N_KEYS), :]
        vals, picks = _take_rounds([s], [key_id], TOPK)
        for r in range(TOPK):
            v_scr[g, r:r + 1, :] = vals[r]
            i_scr[g, r:r + 1, :] = picks[r]
        return carry

    lax.fori_loop(0, ng, stage1, 0)

    j16 = lax.broadcasted_iota(I32, (TOPK, LANES), 0)
    j8 = lax.broadcasted_iota(I32, (8, LANES), 0)

    def stage2(h, carry):
        v1 = v_scr[2 * h]
        v2 = v_scr[2 * h + 1]
        i1 = i_scr[2 * h]
        i2 = i_scr[2 * h + 1]
        slabs = [v1[0:1, :] + v2]
        ids = [j16]
        for i in range(1, TOPK):
            nj = TOPK // (i + 1)
            slabs.append(jnp.where(j8 < nj, v1[i:i + 1, :] + v2[0:8, :], NEG))
            ids.append(j8 + i * TOPK)
        vals, picks = _take_rounds(slabs, ids, TOPK)
        for r in range(TOPK):
            sv_scr[r:r + 1, :] = vals[r]
            ci_scr[r:r + 1, :] = picks[r]
        sv = sv_scr[...]
        ci = ci_scr[...]
        ci_hi = lax.shift_right_logical(ci, 4)
        ci_lo = jnp.bitwise_and(ci, TOPK - 1)
        e1 = jnp.zeros((TOPK, LANES), I32)
        e2 = jnp.zeros((TOPK, LANES), I32)
        for i in range(TOPK):
            e1 = jnp.where(ci_hi == i, i1[i:i + 1, :], e1)
            e2 = jnp.where(ci_lo == i, i2[i:i + 1, :], e2)
        p = jnp.exp(sv - sv[0:1, :])
        gates = p / jnp.sum(p, axis=0, keepdims=True)
        et_scr[pl.ds(pl.multiple_of(h * TOPK, TOPK), TOPK), :] = e1 * N_KEYS + e2
        gt_scr[pl.ds(pl.multiple_of(h * TOPK, TOPK), TOPK), :] = gates
        return carry

    lax.fori_loop(0, PEER_HEADS, stage2, 0)
    e_ref[...] = et_scr[...].T
    g_ref[...] = gt_scr[...].T


def route(st):
    T = st.shape[1]
    ng = 2 * PEER_HEADS
    return pl.pallas_call(
        _route_body, grid=(T // LANES,),
        in_specs=[pl.BlockSpec((ng * N_KEYS, LANES), lambda i: (0, i))],
        out_specs=[pl.BlockSpec((LANES, KSEL), lambda i: (i, 0)),
                   pl.BlockSpec((LANES, KSEL), lambda i: (i, 0))],
        out_shape=[jax.ShapeDtypeStruct((T, KSEL), I32),
                   jax.ShapeDtypeStruct((T, KSEL), F32)],
        scratch_shapes=[pltpu.VMEM((ng, TOPK, LANES), F32), pltpu.VMEM((ng, TOPK, LANES), I32),
                        pltpu.VMEM((TOPK, LANES), F32), pltpu.VMEM((TOPK, LANES), I32),
                        pltpu.VMEM((KSEL, LANES), I32), pltpu.VMEM((KSEL, LANES), F32)],
        compiler_params=_cp(("parallel",)), name="peer_route",
    )(st)


NC, NS, L = 2, 16, 16
NW = NC * NS
NJ = D_MODEL // L
R = TOPK
NCH = KSEL // R
G = 8


def _perm(x, idx):
    return jnp.take_along_axis(x, idx, axis=0, mode="promise_in_bounds")


def peer_sc(x, resid, idx, gates, u_tab, v_tab):
    T = x.shape[0]
    tpw = T // NW
    ngroups = tpw // G
    idx3 = idx.reshape(T * NCH, R)
    g3 = gates.reshape(T * NCH, R)
    mesh = plsc.VectorSubcoreMesh(core_axis_name="c", subcore_axis_name="s")

    @functools.partial(
        pl.kernel, mesh=mesh,
        out_type=jax.ShapeDtypeStruct((T, D_MODEL), F32),
        scratch_types=[
            pltpu.VMEM((G, D_MODEL), F32),
            pltpu.VMEM((G, D_MODEL), F32),
            pltpu.VMEM((G * NCH, R), I32),
            pltpu.VMEM((G * NCH, R), F32),
            pltpu.VMEM((R, D_MODEL), F32),
            pltpu.VMEM((R, D_MODEL), F32),
            pltpu.VMEM((R, D_MODEL), F32),
            pltpu.VMEM((R, D_MODEL), F32),
            pltpu.SemaphoreType.DMA,
            pltpu.SemaphoreType.DMA,
            pltpu.SemaphoreType.DMA,
            pltpu.SemaphoreType.DMA,
        ],
        compiler_params=pltpu.CompilerParams(needs_layout_passes=False),
        name="peer_experts_sc",
    )
    def k(x_hbm, r_hbm, idx_hbm, g_hbm, u_hbm, v_hbm, out_hbm,
          x_v, out_v, idx_v, g_v, ub0, ub1, vb0, vb1, su0, su1, sv0, sv1):
        wid = lax.axis_index("s") * NC + lax.axis_index("c")
        ubs, vbs, sus, svs = (ub0, ub1), (vb0, vb1), (su0, su1), (sv0, sv1)
        iota = lax.iota(I32, L)

        def gather_copies(c, b):
            return (pltpu.make_async_copy(u_hbm.at[idx_v.at[c]], ubs[b], sus[b]),
                    pltpu.make_async_copy(v_hbm.at[idx_v.at[c]], vbs[b], svs[b]))

        def issue(c, b):
            for cp in gather_copies(c, b):
                cp.start()

        def wait(c, b):
            for cp in gather_copies(c, b):
                cp.wait()

        def compute(c, b):
            ub, vb = ubs[b], vbs[b]
            t = c // NCH

            def ubody(j, accs):
                xj = x_v[t, pl.ds(j * L, L)]
                return tuple(accs[kk] + xj * ub[kk, pl.ds(j * L, L)] for kk in range(R))

            accs = lax.fori_loop(0, NJ, ubody, tuple(jnp.zeros((L,), F32) for _ in range(R)))
            vecs = list(accs)
            dist = L // 2
            while dist >= 1:
                pidx = jnp.bitwise_xor(iota, dist)
                low = jnp.bitwise_and(iota, dist) == 0
                nxt = []
                for kk in range(dist):
                    a = vecs[kk]
                    bvec = vecs[kk + dist]
                    a = a + _perm(a, pidx)
                    bvec = bvec + _perm(bvec, pidx)
                    nxt.append(jnp.where(low, a, bvec))
                vecs = nxt
                dist //= 2
            hid = vecs[0]
            z = GC * (hid + 0.044715 * hid * hid * hid)
            gel = hid / (1.0 + jnp.exp(-2.0 * z))
            w = g_v[c, :] * gel
            wbs = [_perm(w, jnp.full((L,), kk, I32)) for kk in range(R)]

            def vbody(j, carry):
                acc = out_v[t, pl.ds(j * L, L)]
                for kk in range(R):
                    acc = acc + wbs[kk] * vb[kk, pl.ds(j * L, L)]
                out_v[t, pl.ds(j * L, L)] = acc
                return carry

            lax.fori_loop(0, NJ, vbody, 0)

        def group(g, carry):
            tok0 = wid * tpw + g * G
            pltpu.sync_copy(x_hbm.at[pl.ds(tok0, G)], x_v)
            pltpu.sync_copy(r_hbm.at[pl.ds(tok0, G)], out_v)
            pltpu.sync_copy(idx_hbm.at[pl.ds(tok0 * NCH, G * NCH)], idx_v)
            pltpu.sync_copy(g_hbm.at[pl.ds(tok0 * NCH, G * NCH)], g_v)
            issue(0, 0)

            def cbody(cc, c2):
                c = cc * 2
                issue(c + 1, 1)
                wait(c, 0)
                compute(c, 0)

                @pl.when(c + 2 < G * NCH)
                def _():
                    issue(c + 2, 0)

                wait(c + 1, 1)
                compute(c + 1, 1)
                return c2

            lax.fori_loop(0, G * NCH // 2, cbody, 0)
            pltpu.sync_copy(out_v, out_hbm.at[pl.ds(tok0, G)])
            return carry

        lax.fori_loop(0, ngroups, group, 0)

    return k(x, resid, idx3, g3, u_tab, v_tab)


def _fn_body(x_ref, g_ref, o_ref):
    xf = x_ref[...]
    o_ref[...] = xf * lax.rsqrt(jnp.mean(xf * xf, axis=-1, keepdims=True) + EPS) * g_ref[...]


def final_norm(x, g, tm=1024):
    T, d = x.shape
    return pl.pallas_call(
        _fn_body, grid=(T // tm,),
        in_specs=[pl.BlockSpec((tm, d), lambda i: (i, 0)), pl.BlockSpec((1, d), lambda i: (0, 0))],
        out_specs=pl.BlockSpec((tm, d), lambda i: (i, 0)),
        out_shape=jax.ShapeDtypeStruct((T, d), F32),
        compiler_params=_cp(("parallel",)), name="final_norm",
    )(x, g)


def _prep_layer(w_in, b_forget, conv_dw_w, conv_dw_b, conv_ln_g, conv_ln_b, rg_conv_w, rg_conv_b,
                rg_w_r, rg_b_r, rg_w_i, rg_b_i, rg_lambda, w_out, peer_wq, peer_k1, peer_k2):
    f0 = 3 * D_ATT
    wft = jnp.zeros((16, D_MODEL), BF).at[0:ATT_HEADS].set(w_in[:, f0:f0 + ATT_HEADS].T.astype(BF))
    bfg = jnp.zeros((16, 1), F32).at[0:ATT_HEADS, 0].set(b_forget)
    cw = jnp.zeros((32, D_CONV), F32).at[0:CONV_K].set(conv_dw_w)
    rw = jnp.zeros((8, D_RNN), F32).at[0:RNN_CONV_K].set(rg_conv_w)
    bd = lambda w: jax.scipy.linalg.block_diag(*[w[i] for i in range(RNN_BLOCKS)]).astype(BF)
    row = lambda v: v.reshape(1, -1).astype(F32)
    keys = jnp.stack([peer_k1, peer_k2], axis=1).reshape(2 * PEER_HEADS, N_KEYS, D_HALF).astype(BF)
    return dict(wqkv=w_in[:, 0:f0].astype(BF), wft=wft, wrest=w_in[:, f0 + ATT_HEADS:].astype(BF), bfg=bfg,
                cw=cw, cb=row(conv_dw_b), lg=row(conv_ln_g), lb=row(conv_ln_b),
                rw=rw, rb=row(rg_conv_b), wr=bd(rg_w_r), br=row(rg_b_r), wi=bd(rg_w_i), bi=row(rg_b_i),
                lam=row(rg_lambda), woa=w_out[0:D_ATT].astype(BF), wob=w_out[D_ATT:].astype(BF),
                wq=peer_wq.astype(BF), keys=keys)


def kernel(x, norm1_g, w_in, b_forget, conv_dw_w, conv_dw_b, conv_ln_g, conv_ln_b,
           rg_conv_w, rg_conv_b, rg_w_r, rg_b_r, rg_w_i, rg_b_i, rg_lambda, w_out,
           norm2_g, peer_wq, peer_k1, peer_k2, peer_u, peer_v, final_g):
    b, s, d = x.shape
    params = [_prep_layer(w_in[l], b_forget[l], conv_dw_w[l], conv_dw_b[l], conv_ln_g[l], conv_ln_b[l],
                          rg_conv_w[l], rg_conv_b[l], rg_w_r[l], rg_b_r[l], rg_w_i[l], rg_b_i[l], rg_lambda[l],
                          w_out[l], peer_wq[l], peer_k1[l], peer_k2[l]) for l in range(DEPTH)]
    bs = b // N_SLICES
    outs = []
    for i in range(N_SLICES):
        xt = x[i * bs:(i + 1) * bs].reshape(bs * s, d)
        for l in range(DEPTH):
            p = params[l]
            qkv, rest, cum = in_proj(xt, norm1_g[l].reshape(1, d), p["wqkv"], p["wft"], p["wrest"], p["bfg"], s)
            y_att = attention(qkv, cum, bs, s)
            y_cr = mixers(rest, p, bs, s)
            x1, h2, st = out_proj(xt, y_att, y_cr, p, norm2_g[l].reshape(1, d))
            experts, gates = route(st)
            xt = peer_sc(h2, x1, experts, gates, peer_u[l], peer_v[l])
        outs.append(final_norm(xt, final_g.reshape(1, d)).reshape(bs, s, d))
    return jnp.concatenate(outs, axis=0)
```

```python
import functools
import math

import jax
import jax.numpy as jnp
from jax import lax
from jax.experimental import pallas as pl
from jax.experimental.pallas import tpu as pltpu
from jax.experimental.pallas import tpu_sc as plsc

BF = jnp.bfloat16
F32 = jnp.float32
I32 = jnp.int32

D_MODEL = 1024
DEPTH = 2
ATT_HEADS = 8
ATT_HD = 64
D_ATT = ATT_HEADS * ATT_HD
D_CONV = 256
CONV_K = 31
D_RNN = 256
RNN_BLOCKS = 4
RNN_CONV_K = 4
RG_C = 8.0
EPS = 1e-6
N_REST = 2 * D_CONV + 2 * D_RNN
PEER_HEADS = 8
N_KEYS = 128
D_HALF = 128
TOPK = 16
KSEL = PEER_HEADS * TOPK
GC = 0.7978845608028654
NEG = float("-inf")

N_SLICES = 4
LANES = 128
VMEM_LIMIT = 48 * 1024 * 1024


def _cp(sem):
    return pltpu.CompilerParams(dimension_semantics=sem, vmem_limit_bytes=VMEM_LIMIT)


def _split3(x):
    hi = x.astype(BF)
    r = x - hi.astype(F32)
    mid = r.astype(BF)
    lo = (r - mid.astype(F32)).astype(BF)
    return hi, mid, lo


def _nt(a, b):
    return lax.dot_general(a, b, (((1,), (1,)), ((), ())), preferred_element_type=F32)


def _dot(a, b):
    return jnp.dot(a, b, preferred_element_type=F32)


def _sigmoid(x):
    return 1.0 / (1.0 + jnp.exp(-x))


def _gelu(x):
    return 0.5 * x * (1.0 + jnp.tanh(GC * (x + 0.044715 * x * x * x)))


def _inproj_body(x_ref, g_ref, wqkv_ref, wft_ref, wrest_ref, bf_ref, tri_ref,
                 qkv_ref, rest_ref, cum_ref, carry_ref, *, blocks_per_seq, tm):
    i = pl.program_id(0)
    x = x_ref[...]
    h = x * lax.rsqrt(jnp.mean(x * x, axis=-1, keepdims=True) + EPS) * g_ref[...]
    hb = h.astype(BF)
    qkv = _dot(hb, wqkv_ref[...])
    col = lax.broadcasted_iota(I32, (1, 3 * D_ATT), 1)
    qkv = jnp.where(col < D_ATT, qkv * (1.0 / math.sqrt(ATT_HD)), qkv)
    qkv_ref[...] = qkv.astype(BF)
    rest_ref[...] = _dot(hb, wrest_ref[...])
    ft = _nt(wft_ref[...], hb) + bf_ref[...]
    lf = jnp.minimum(ft, 0.0) - jnp.log(1.0 + jnp.exp(-jnp.abs(ft)))
    hi, mid, lo = _split3(lf)
    tri = tri_ref[...]
    cs = _dot(hi, tri) + _dot(mid, tri) + _dot(lo, tri)

    @pl.when(i % blocks_per_seq == 0)
    def _():
        carry_ref[...] = jnp.zeros_like(carry_ref)

    cum = cs + carry_ref[...]
    cum_ref[...] = cum
    carry_ref[...] = cum[:, tm - 1:tm]


def in_proj(x, g, wqkv, wft, wrest, bfg, seq, tm=512):
    T = x.shape[0]
    tri = (lax.broadcasted_iota(I32, (tm, tm), 0) <= lax.broadcasted_iota(I32, (tm, tm), 1)).astype(BF)
    body = functools.partial(_inproj_body, blocks_per_seq=seq // tm, tm=tm)
    return pl.pallas_call(
        body, grid=(T // tm,),
        in_specs=[pl.BlockSpec((tm, D_MODEL), lambda i: (i, 0)),
                  pl.BlockSpec((1, D_MODEL), lambda i: (0, 0)),
                  pl.BlockSpec((D_MODEL, 3 * D_ATT), lambda i: (0, 0)),
                  pl.BlockSpec((16, D_MODEL), lambda i: (0, 0)),
                  pl.BlockSpec((D_MODEL, N_REST), lambda i: (0, 0)),
                  pl.BlockSpec((16, 1), lambda i: (0, 0)),
                  pl.BlockSpec((tm, tm), lambda i: (0, 0))],
        out_specs=[pl.BlockSpec((tm, 3 * D_ATT), lambda i: (i, 0)),
                   pl.BlockSpec((tm, N_REST), lambda i: (i, 0)),
                   pl.BlockSpec((16, tm), lambda i: (0, i))],
        out_shape=[jax.ShapeDtypeStruct((T, 3 * D_ATT), BF),
                   jax.ShapeDtypeStruct((T, N_REST), F32),
                   jax.ShapeDtypeStruct((16, T), F32)],
        scratch_shapes=[pltpu.VMEM((16, 1), F32)],
        compiler_params=_cp(("arbitrary",)), name="in_proj",
    )(x, g, wqkv, wft, wrest, bfg, tri)


def _attn_body(q_ref, k_ref, v_ref, ck_ref, o_ref, m_ref, l_ref, acc_ref, *, tq, tk):
    pr = pl.program_id(1)
    qi = pl.program_id(2)
    ki = pl.program_id(3)

    @pl.when(ki == 0)
    def _():
        m_ref[...] = jnp.full_like(m_ref, NEG)
        l_ref[...] = jnp.zeros_like(l_ref)
        acc_ref[...] = jnp.zeros_like(acc_ref)

    lane = lax.broadcasted_iota(I32, (1, LANES), 1)
    first = lane < ATT_HD

    def step(masked):
        q = q_ref[...]
        k = k_ref[...]
        v = v_ref[...]
        if masked:
            keep = (lax.broadcasted_iota(I32, (tq, tk), 1) <= lax.broadcasted_iota(I32, (tq, tk), 0))
        alphas, pvs = [], []
        for hh in range(2):
            sel = first if hh == 0 else jnp.logical_not(first)
            qm = jnp.where(sel, q, jnp.zeros_like(q))
            s = _nt(qm, k) - ck_ref[pl.ds(2 * pr + hh, 1), :]
            if masked:
                s = jnp.where(keep, s, NEG)
            m_prev = m_ref[hh]
            m_new = jnp.maximum(m_prev, jnp.max(s, axis=1, keepdims=True))
            alpha = jnp.exp(m_prev - m_new)
            p = jnp.exp(s - m_new)
            l_ref[hh] = alpha * l_ref[hh] + jnp.sum(p, axis=1, keepdims=True)
            m_ref[hh] = m_new
            pvs.append(_dot(p.astype(BF), v))
            alphas.append(alpha)
        acc_ref[...] = jnp.where(first, alphas[0], alphas[1]) * acc_ref[...] + jnp.where(first, pvs[0], pvs[1])

    @pl.when(ki < qi)
    def _():
        step(False)

    @pl.when(ki == qi)
    def _():
        step(True)
        o_ref[...] = (acc_ref[...] / jnp.where(first, l_ref[0], l_ref[1])).astype(o_ref.dtype)


def attention(qkv, cum, batch, seq, tq=512):
    T = qkv.shape[0]
    tk = tq
    nq = seq // tq
    body = functools.partial(_attn_body, tq=tq, tk=tk)
    npair = ATT_HEADS // 2
    kblk = lambda b, qi, ki: b * nq + jnp.minimum(ki, qi)
    return pl.pallas_call(
        body, grid=(batch, npair, nq, nq),
        in_specs=[pl.BlockSpec((tq, LANES), lambda b, p, qi, ki: (b * nq + qi, p)),
                  pl.BlockSpec((tk, LANES), lambda b, p, qi, ki: (kblk(b, qi, ki), npair + p)),
                  pl.BlockSpec((tk, LANES), lambda b, p, qi, ki: (kblk(b, qi, ki), 2 * npair + p)),
                  pl.BlockSpec((16, tk), lambda b, p, qi, ki: (0, kblk(b, qi, ki)))],
        out_specs=pl.BlockSpec((tq, LANES), lambda b, p, qi, ki: (b * nq + qi, p)),
        out_shape=jax.ShapeDtypeStruct((T, D_ATT), BF),
        scratch_shapes=[pltpu.VMEM((2, tq, 1), F32), pltpu.VMEM((2, tq, 1), F32), pltpu.VMEM((tq, LANES), F32)],
        compiler_params=_cp(("parallel", "parallel", "parallel", "arbitrary")), name="fox_attention",
    )(qkv, qkv, qkv, cum)


CONV_HALO = 32
RG_HALO = 8


def _mix_body(rest_ref, cw_ref, cb_ref, lg_ref, lb_ref, rw_ref, rb_ref, wr_ref, br_ref, wi_ref, bi_ref, lam_ref,
              o_ref, ybuf, xbuf, hc, *, ts):
    si = pl.program_id(1)

    @pl.when(si == 0)
    def _():
        ybuf[0:CONV_HALO, :] = jnp.zeros((CONV_HALO, D_CONV), F32)
        xbuf[0:RG_HALO, :] = jnp.zeros((RG_HALO, D_RNN), F32)
        hc[...] = jnp.zeros_like(hc)

    y = rest_ref[:, 0:D_CONV] * _sigmoid(rest_ref[:, D_CONV:2 * D_CONV])
    ybuf[CONV_HALO:CONV_HALO + ts, :] = y
    acc = jnp.zeros((ts, D_CONV), F32)
    for k in range(CONV_K):
        acc = acc + cw_ref[k:k + 1, :] * ybuf[pl.ds(CONV_HALO - (CONV_K - 1) + k, ts), :]
    yc = acc + cb_ref[...]
    mu = jnp.mean(yc, axis=-1, keepdims=True)
    var = jnp.mean(jnp.square(yc - mu), axis=-1, keepdims=True)
    yn = (yc - mu) * lax.rsqrt(var + EPS) * lg_ref[...] + lb_ref[...]
    o_ref[:, 0:D_CONV] = (yn * _sigmoid(yn)).astype(o_ref.dtype)
    ybuf[0:CONV_HALO, :] = ybuf[ts:ts + CONV_HALO, :]

    xbuf[RG_HALO:RG_HALO + ts, :] = rest_ref[:, 2 * D_CONV:2 * D_CONV + D_RNN]
    xc = jnp.zeros((ts, D_RNN), F32)
    for k in range(RNN_CONV_K):
        xc = xc + rw_ref[k:k + 1, :] * xbuf[pl.ds(RG_HALO - (RNN_CONV_K - 1) + k, ts), :]
    xc = xc + rb_ref[...]
    xbuf[0:RG_HALO, :] = xbuf[ts:ts + RG_HALO, :]
    xcb = xc.astype(BF)
    r = _sigmoid(_dot(xcb, wr_ref[...]) + br_ref[...])
    gi = _sigmoid(_dot(xcb, wi_ref[...]) + bi_ref[...])
    nl = -lam_ref[...]
    sp = jnp.maximum(nl, 0.0) + jnp.log(1.0 + jnp.exp(-jnp.abs(nl)))
    log_a = -RG_C * r * sp
    a = jnp.exp(log_a)
    bt = jnp.sqrt(1.0 - jnp.exp(2.0 * log_a)) * (gi * xc)
    row = lax.broadcasted_iota(I32, (ts, 1), 0)
    sh = 1
    while sh < ts:
        live = row >= sh
        a_s = jnp.where(live, pltpu.roll(a, sh, 0), 1.0)
        b_s = jnp.where(live, pltpu.roll(bt, sh, 0), 0.0)
        bt = bt + a * b_s
        a = a * a_s
        sh *= 2
    h = bt + a * hc[...]
    hc[...] = h[ts - 1:ts, :]
    gate_in = rest_ref[:, 2 * D_CONV + D_RNN:2 * D_CONV + 2 * D_RNN]
    o_ref[:, D_CONV:D_CONV + D_RNN] = (h * _gelu(gate_in)).astype(o_ref.dtype)


def mixers(rest, p, batch, seq, ts=512):
    T = rest.shape[0]
    ns = seq // ts
    body = functools.partial(_mix_body, ts=ts)
    vec = lambda: pl.BlockSpec((1, D_CONV), lambda b, s: (0, 0))
    return pl.pallas_call(
        body, grid=(batch, ns),
        in_specs=[pl.BlockSpec((ts, N_REST), lambda b, s: (b * ns + s, 0)),
                  pl.BlockSpec((32, D_CONV), lambda b, s: (0, 0)), vec(), vec(), vec(),
                  pl.BlockSpec((8, D_RNN), lambda b, s: (0, 0)), vec(),
                  pl.BlockSpec((D_RNN, D_RNN), lambda b, s: (0, 0)), vec(),
                  pl.BlockSpec((D_RNN, D_RNN), lambda b, s: (0, 0)), vec(), vec()],
        out_specs=pl.BlockSpec((ts, D_CONV + D_RNN), lambda b, s: (b * ns + s, 0)),
        out_shape=jax.ShapeDtypeStruct((T, D_CONV + D_RNN), BF),
        scratch_shapes=[pltpu.VMEM((ts + CONV_HALO, D_CONV), F32), pltpu.VMEM((ts + RG_HALO, D_RNN), F32),
                        pltpu.VMEM((1, D_RNN), F32)],
        compiler_params=_cp(("arbitrary", "arbitrary")), name="conv_rglru",
    )(rest, p["cw"], p["cb"], p["lg"], p["lb"], p["rw"], p["rb"], p["wr"], p["br"], p["wi"], p["bi"], p["lam"])


def _outproj_body(x_ref, ya_ref, yc_ref, woa_ref, wob_ref, g2_ref, wq_ref, keys_ref, x1_ref, h2_ref, st_ref):
    x1 = x_ref[...] + _dot(ya_ref[...], woa_ref[...]) + _dot(yc_ref[...], wob_ref[...])
    x1_ref[...] = x1
    h2 = x1 * lax.rsqrt(jnp.mean(x1 * x1, axis=-1, keepdims=True) + EPS) * g2_ref[...]
    h2_ref[...] = h2
    q = _dot(h2.astype(BF), wq_ref[...]).astype(BF)
    for g in range(2 * PEER_HEADS):
        st_ref[g * N_KEYS:(g + 1) * N_KEYS, :] = _nt(keys_ref[g], q[:, g * D_HALF:(g + 1) * D_HALF])


def out_proj(x, ya, yc, p, g2, tm=256):
    T = x.shape[0]
    ng = 2 * PEER_HEADS
    return pl.pallas_call(
        _outproj_body, grid=(T // tm,),
        in_specs=[pl.BlockSpec((tm, D_MODEL), lambda i: (i, 0)),
                  pl.BlockSpec((tm, D_ATT), lambda i: (i, 0)),
                  pl.BlockSpec((tm, D_CONV + D_RNN), lambda i: (i, 0)),
                  pl.BlockSpec((D_ATT, D_MODEL), lambda i: (0, 0)),
                  pl.BlockSpec((D_CONV + D_RNN, D_MODEL), lambda i: (0, 0)),
                  pl.BlockSpec((1, D_MODEL), lambda i: (0, 0)),
                  pl.BlockSpec((D_MODEL, ng * D_HALF), lambda i: (0, 0)),
                  pl.BlockSpec((ng, N_KEYS, D_HALF), lambda i: (0, 0, 0))],
        out_specs=[pl.BlockSpec((tm, D_MODEL), lambda i: (i, 0)),
                   pl.BlockSpec((tm, D_MODEL), lambda i: (i, 0)),
                   pl.BlockSpec((ng * N_KEYS, tm), lambda i: (0, i))],
        out_shape=[jax.ShapeDtypeStruct((T, D_MODEL), F32),
                   jax.ShapeDtypeStruct((T, D_MODEL), F32),
                   jax.ShapeDtypeStruct((ng * N_KEYS, T), F32)],
        compiler_params=_cp(("parallel",)), name="out_proj_peer_scores",
    )(x, ya, yc, p["woa"], p["wob"], g2, p["wq"], p["keys"])


BIG_ID = 1 << 20


def _take_rounds(slabs, ids, nrounds):
    vals, picks = [], []
    for _ in range(nrounds):
        m = None
        for s in slabs:
            ms = jnp.max(s, axis=0, keepdims=True)
            m = ms if m is None else jnp.maximum(m, ms)
        pick = None
        for s, idc in zip(slabs, ids):
            ps = jnp.min(jnp.where(s == m, idc, BIG_ID), axis=0, keepdims=True)
            pick = ps if pick is None else jnp.minimum(pick, ps)
        slabs = [jnp.where(idc == pick, NEG, s) for s, idc in zip(slabs, ids)]
        vals.append(m)
        picks.append(pick)
    return vals, picks


def _route_body(st_ref, e_ref, g_ref, v_scr, i_scr, sv_scr, ci_scr, et_scr, gt_scr):
    ng = 2 * PEER_HEADS
    key_id = lax.broadcasted_iota(I32, (N_KEYS, LANES), 0)

    def stage1(g, carry):
        s = st_ref[pl.ds(pl.multiple_of(g * N_KEYS, N_KEYS), N_KEYS), :]
        vals, picks = _take_rounds([s], [key_id], TOPK)
        for r in range(TOPK):
            v_scr[g, r:r + 1, :] = vals[r]
            i_scr[g, r:r + 1, :] = picks[r]
        return carry

    lax.fori_loop(0, ng, stage1, 0)

    j16 = lax.broadcasted_iota(I32, (TOPK, LANES), 0)
    j8 = lax.broadcasted_iota(I32, (8, LANES), 0)

    def stage2(h, carry):
        v1 = v_scr[2 * h]
        v2 = v_scr[2 * h + 1]
        i1 = i_scr[2 * h]
        i2 = i_scr[2 * h + 1]
        slabs = [v1[0:1, :] + v2]
        ids = [j16]
        for i in range(1, TOPK):
            nj = TOPK // (i + 1)
            slabs.append(jnp.where(j8 < nj, v1[i:i + 1, :] + v2[0:8, :], NEG))
            ids.append(j8 + i * TOPK)
        vals, picks = _take_rounds(slabs, ids, TOPK)
        for r in range(TOPK):
            sv_scr[r:r + 1, :] = vals[r]
            ci_scr[r:r + 1, :] = picks[r]
        sv = sv_scr[...]
        ci = ci_scr[...]
        ci_hi = lax.shift_right_logical(ci, 4)
        ci_lo = jnp.bitwise_and(ci, TOPK - 1)
        e1 = jnp.zeros((TOPK, LANES), I32)
        e2 = jnp.zeros((TOPK, LANES), I32)
        for i in range(TOPK):
            e1 = jnp.where(ci_hi == i, i1[i:i + 1, :], e1)
            e2 = jnp.where(ci_lo == i, i2[i:i + 1, :], e2)
        p = jnp.exp(sv - sv[0:1, :])
        gates = p / jnp.sum(p, axis=0, keepdims=True)
        et_scr[pl.ds(pl.multiple_of(h * TOPK, TOPK), TOPK), :] = e1 * N_KEYS + e2
        gt_scr[pl.ds(pl.multiple_of(h * TOPK, TOPK), TOPK), :] = gates
        return carry

    lax.fori_loop(0, PEER_HEADS, stage2, 0)
    e_ref[...] = et_scr[...].T
    g_ref[...] = gt_scr[...].T


def route(st):
    T = st.shape[1]
    ng = 2 * PEER_HEADS
    return pl.pallas_call(
        _route_body, grid=(T // LANES,),
        in_specs=[pl.BlockSpec((ng * N_KEYS, LANES), lambda i: (0, i))],
        out_specs=[pl.BlockSpec((LANES, KSEL), lambda i: (i, 0)),
                   pl.BlockSpec((LANES, KSEL), lambda i: (i, 0))],
        out_shape=[jax.ShapeDtypeStruct((T, KSEL), I32),
                   jax.ShapeDtypeStruct((T, KSEL), F32)],
        scratch_shapes=[pltpu.VMEM((ng, TOPK, LANES), F32), pltpu.VMEM((ng, TOPK, LANES), I32),
                        pltpu.VMEM((TOPK, LANES), F32), pltpu.VMEM((TOPK, LANES), I32),
                        pltpu.VMEM((KSEL, LANES), I32), pltpu.VMEM((KSEL, LANES), F32)],
        compiler_params=_cp(("parallel",)), name="peer_route",
    )(st)


NC, NS, L = 2, 16, 16
NW = NC * NS
NJ = D_MODEL // L
R = TOPK
NCH = KSEL // R
G = 8


def _perm(x, idx):
    return jnp.take_along_axis(x, idx, axis=0, mode="promise_in_bounds")


def peer_sc(x, resid, idx, gates, u_tab, v_tab):
    T = x.shape[0]
    tpw = T // NW
    ngroups = tpw // G
    idx3 = idx.reshape(T * NCH, R)
    g3 = gates.reshape(T * NCH, R)
    mesh = plsc.VectorSubcoreMesh(core_axis_name="c", subcore_axis_name="s")

    @functools.partial(
        pl.kernel, mesh=mesh,
        out_type=jax.ShapeDtypeStruct((T, D_MODEL), F32),
        scratch_types=[
            pltpu.VMEM((G, D_MODEL), F32),
            pltpu.VMEM((G, D_MODEL), F32),
            pltpu.VMEM((G * NCH, R), I32),
            pltpu.VMEM((G * NCH, R), F32),
            pltpu.VMEM((R, D_MODEL), F32),
            pltpu.VMEM((R, D_MODEL), F32),
            pltpu.VMEM((R, D_MODEL), F32),
            pltpu.VMEM((R, D_MODEL), F32),
            pltpu.SemaphoreType.DMA,
            pltpu.SemaphoreType.DMA,
            pltpu.SemaphoreType.DMA,
            pltpu.SemaphoreType.DMA,
        ],
        compiler_params=pltpu.CompilerParams(needs_layout_passes=False),
        name="peer_experts_sc",
    )
    def k(x_hbm, r_hbm, idx_hbm, g_hbm, u_hbm, v_hbm, out_hbm,
          x_v, out_v, idx_v, g_v, ub0, ub1, vb0, vb1, su0, su1, sv0, sv1):
        wid = lax.axis_index("s") * NC + lax.axis_index("c")
        ubs, vbs, sus, svs = (ub0, ub1), (vb0, vb1), (su0, su1), (sv0, sv1)
        iota = lax.iota(I32, L)

        def gather_copies(c, b):
            return (pltpu.make_async_copy(u_hbm.at[idx_v.at[c]], ubs[b], sus[b]),
                    pltpu.make_async_copy(v_hbm.at[idx_v.at[c]], vbs[b], svs[b]))

        def issue(c, b):
            for cp in gather_copies(c, b):
                cp.start()

        def wait(c, b):
            for cp in gather_copies(c, b):
                cp.wait()

        def compute(c, b):
            ub, vb = ubs[b], vbs[b]
            t = c // NCH

            def ubody(j, accs):
                xj = x_v[t, pl.ds(j * L, L)]
                return tuple(accs[kk] + xj * ub[kk, pl.ds(j * L, L)] for kk in range(R))

            accs = lax.fori_loop(0, NJ, ubody, tuple(jnp.zeros((L,), F32) for _ in range(R)))
            vecs = list(accs)
            dist = L // 2
            while dist >= 1:
                pidx = jnp.bitwise_xor(iota, dist)
                low = jnp.bitwise_and(iota, dist) == 0
                nxt = []
                for kk in range(dist):
                    a = vecs[kk]
                    bvec = vecs[kk + dist]
                    a = a + _perm(a, pidx)
                    bvec = bvec + _perm(bvec, pidx)
                    nxt.append(jnp.where(low, a, bvec))
                vecs = nxt
                dist //= 2
            hid = vecs[0]
            z = GC * (hid + 0.044715 * hid * hid * hid)
            gel = hid / (1.0 + jnp.exp(-2.0 * z))
            w = g_v[c, :] * gel
            wbs = [_perm(w, jnp.full((L,), kk, I32)) for kk in range(R)]

            @plsc.parallel_loop(0, NJ, unroll=2)
            def _(j):
                prods = [wbs[kk] * vb[kk, pl.ds(j * L, L)] for kk in range(R)]
                while len(prods) > 1:
                    prods = [prods[i] + prods[i + 1] for i in range(0, len(prods), 2)]
                out_v[t, pl.ds(j * L, L)] = out_v[t, pl.ds(j * L, L)] + prods[0]

        def group(g, carry):
            tok0 = wid * tpw + g * G
            pltpu.sync_copy(x_hbm.at[pl.ds(tok0, G)], x_v)
            pltpu.sync_copy(r_hbm.at[pl.ds(tok0, G)], out_v)
            pltpu.sync_copy(idx_hbm.at[pl.ds(tok0 * NCH, G * NCH)], idx_v)
            pltpu.sync_copy(g_hbm.at[pl.ds(tok0 * NCH, G * NCH)], g_v)
            issue(0, 0)

            def cbody(cc, c2):
                c = cc * 2
                issue(c + 1, 1)
                wait(c, 0)
                compute(c, 0)

                @pl.when(c + 2 < G * NCH)
                def _():
                    issue(c + 2, 0)

                wait(c + 1, 1)
                compute(c + 1, 1)
                return c2

            lax.fori_loop(0, G * NCH // 2, cbody, 0)
            pltpu.sync_copy(out_v, out_hbm.at[pl.ds(tok0, G)])
            return carry

        lax.fori_loop(0, ngroups, group, 0)

    return k(x, resid, idx3, g3, u_tab, v_tab)


def _fn_body(x_ref, g_ref, o_ref):
    xf = x_ref[...]
    o_ref[...] = xf * lax.rsqrt(jnp.mean(xf * xf, axis=-1, keepdims=True) + EPS) * g_ref[...]


def final_norm(x, g, tm=1024):
    T, d = x.shape
    return pl.pallas_call(
        _fn_body, grid=(T // tm,),
        in_specs=[pl.BlockSpec((tm, d), lambda i: (i, 0)), pl.BlockSpec((1, d), lambda i: (0, 0))],
        out_specs=pl.BlockSpec((tm, d), lambda i: (i, 0)),
        out_shape=jax.ShapeDtypeStruct((T, d), F32),
        compiler_params=_cp(("parallel",)), name="final_norm",
    )(x, g)


def _prep_layer(w_in, b_forget, conv_dw_w, conv_dw_b, conv_ln_g, conv_ln_b, rg_conv_w, rg_conv_b,
                rg_w_r, rg_b_r, rg_w_i, rg_b_i, rg_lambda, w_out, peer_wq, peer_k1, peer_k2):
    f0 = 3 * D_ATT
    wft = jnp.zeros((16, D_MODEL), BF).at[0:ATT_HEADS].set(w_in[:, f0:f0 + ATT_HEADS].T.astype(BF))
    bfg = jnp.zeros((16, 1), F32).at[0:ATT_HEADS, 0].set(b_forget)
    cw = jnp.zeros((32, D_CONV), F32).at[0:CONV_K].set(conv_dw_w)
    rw = jnp.zeros((8, D_RNN), F32).at[0:RNN_CONV_K].set(rg_conv_w)
    bd = lambda w: jax.scipy.linalg.block_diag(*[w[i] for i in range(RNN_BLOCKS)]).astype(BF)
    row = lambda v: v.reshape(1, -1).astype(F32)
    keys = jnp.stack([peer_k1, peer_k2], axis=1).reshape(2 * PEER_HEADS, N_KEYS, D_HALF).astype(BF)
    return dict(wqkv=w_in[:, 0:f0].astype(BF), wft=wft, wrest=w_in[:, f0 + ATT_HEADS:].astype(BF), bfg=bfg,
                cw=cw, cb=row(conv_dw_b), lg=row(conv_ln_g), lb=row(conv_ln_b),
                rw=rw, rb=row(rg_conv_b), wr=bd(rg_w_r), br=row(rg_b_r), wi=bd(rg_w_i), bi=row(rg_b_i),
                lam=row(rg_lambda), woa=w_out[0:D_ATT].astype(BF), wob=w_out[D_ATT:].astype(BF),
                wq=peer_wq.astype(BF), keys=keys)


def kernel(x, norm1_g, w_in, b_forget, conv_dw_w, conv_dw_b, conv_ln_g, conv_ln_b,
           rg_conv_w, rg_conv_b, rg_w_r, rg_b_r, rg_w_i, rg_b_i, rg_lambda, w_out,
           norm2_g, peer_wq, peer_k1, peer_k2, peer_u, peer_v, final_g):
    b, s, d = x.shape
    params = [_prep_layer(w_in[l], b_forget[l], conv_dw_w[l], conv_dw_b[l], conv_ln_g[l], conv_ln_b[l],
                          rg_conv_w[l], rg_conv_b[l], rg_w_r[l], rg_b_r[l], rg_w_i[l], rg_b_i[l], rg_lambda[l],
                          w_out[l], peer_wq[l], peer_k1[l], peer_k2[l]) for l in range(DEPTH)]
    bs = b // N_SLICES
    outs = []
    for i in range(N_SLICES):
        xt = x[i * bs:(i + 1) * bs].reshape(bs * s, d)
        for l in range(DEPTH):
            p = params[l]
            qkv, rest, cum = in_proj(xt, norm1_g[l].reshape(1, d), p["wqkv"], p["wft"], p["wrest"], p["bfg"], s)
            y_att = attention(qkv, cum, bs, s)
            y_cr = mixers(rest, p, bs, s)
            x1, h2, st = out_proj(xt, y_att, y_cr, p, norm2_g[l].reshape(1, d))
            experts, gates = route(st)
            xt = peer_sc(h2, x1, experts, gates, peer_u[l], peer_v[l])
        outs.append(final_norm(xt, final_g.reshape(1, d)).reshape(bs, s, d))
    return jnp.concatenate(outs, axis=0)
```

```python
import functools
import math

import jax
import jax.numpy as jnp
from jax import lax
from jax.experimental import pallas as pl
from jax.experimental.pallas import tpu as pltpu
from jax.experimental.pallas import tpu_sc as plsc

BF = jnp.bfloat16
F32 = jnp.float32
I32 = jnp.int32

D_MODEL = 1024
DEPTH = 2
ATT_HEADS = 8
ATT_HD = 64
D_ATT = ATT_HEADS * ATT_HD
D_CONV = 256
CONV_K = 31
D_RNN = 256
RNN_BLOCKS = 4
RNN_CONV_K = 4
RG_C = 8.0
EPS = 1e-6
N_REST = 2 * D_CONV + 2 * D_RNN
PEER_HEADS = 8
N_KEYS = 128
D_HALF = 128
TOPK = 16
KSEL = PEER_HEADS * TOPK
GC = 0.7978845608028654
NEG = float("-inf")

N_SLICES = 4
LANES = 128
VMEM_LIMIT = 48 * 1024 * 1024


def _cp(sem):
    return pltpu.CompilerParams(dimension_semantics=sem, vmem_limit_bytes=VMEM_LIMIT)


def _split3(x):
    hi = x.astype(BF)
    r = x - hi.astype(F32)
    mid = r.astype(BF)
    lo = (r - mid.astype(F32)).astype(BF)
    return hi, mid, lo


def _nt(a, b):
    return lax.dot_general(a, b, (((1,), (1,)), ((), ())), preferred_element_type=F32)


def _dot(a, b):
    return jnp.dot(a, b, preferred_element_type=F32)


def _sigmoid(x):
    return 1.0 / (1.0 + jnp.exp(-x))


def _gelu(x):
    return 0.5 * x * (1.0 + jnp.tanh(GC * (x + 0.044715 * x * x * x)))


def _inproj_body(x_ref, g_ref, wqkv_ref, wft_ref, wrest_ref, bf_ref, tri_ref,
                 qkv_ref, rest_ref, cum_ref, carry_ref, *, blocks_per_seq, tm):
    i = pl.program_id(0)
    x = x_ref[...]
    h = x * lax.rsqrt(jnp.mean(x * x, axis=-1, keepdims=True) + EPS) * g_ref[...]
    hb = h.astype(BF)
    qkv = _dot(hb, wqkv_ref[...])
    col = lax.broadcasted_iota(I32, (1, 3 * D_ATT), 1)
    qkv = jnp.where(col < D_ATT, qkv * (1.0 / math.sqrt(ATT_HD)), qkv)
    qkv_ref[...] = qkv.astype(BF)
    rest_ref[...] = _dot(hb, wrest_ref[...])
    ft = _nt(wft_ref[...], hb) + bf_ref[...]
    lf = jnp.minimum(ft, 0.0) - jnp.log(1.0 + jnp.exp(-jnp.abs(ft)))
    hi, mid, lo = _split3(lf)
    tri = tri_ref[...]
    cs = _dot(hi, tri) + _dot(mid, tri) + _dot(lo, tri)

    @pl.when(i % blocks_per_seq == 0)
    def _():
        carry_ref[...] = jnp.zeros_like(carry_ref)

    cum = cs + carry_ref[...]
    cum_ref[...] = cum
    carry_ref[...] = cum[:, tm - 1:tm]


def in_proj(x, g, wqkv, wft, wrest, bfg, seq, tm=512):
    T = x.shape[0]
    tri = (lax.broadcasted_iota(I32, (tm, tm), 0) <= lax.broadcasted_iota(I32, (tm, tm), 1)).astype(BF)
    body = functools.partial(_inproj_body, blocks_per_seq=seq // tm, tm=tm)
    return pl.pallas_call(
        body, grid=(T // tm,),
        in_specs=[pl.BlockSpec((tm, D_MODEL), lambda i: (i, 0)),
                  pl.BlockSpec((1, D_MODEL), lambda i: (0, 0)),
                  pl.BlockSpec((D_MODEL, 3 * D_ATT), lambda i: (0, 0)),
                  pl.BlockSpec((16, D_MODEL), lambda i: (0, 0)),
                  pl.BlockSpec((D_MODEL, N_REST), lambda i: (0, 0)),
                  pl.BlockSpec((16, 1), lambda i: (0, 0)),
                  pl.BlockSpec((tm, tm), lambda i: (0, 0))],
        out_specs=[pl.BlockSpec((tm, 3 * D_ATT), lambda i: (i, 0)),
                   pl.BlockSpec((tm, N_REST), lambda i: (i, 0)),
                   pl.BlockSpec((16, tm), lambda i: (0, i))],
        out_shape=[jax.ShapeDtypeStruct((T, 3 * D_ATT), BF),
                   jax.ShapeDtypeStruct((T, N_REST), F32),
                   jax.ShapeDtypeStruct((16, T), F32)],
        scratch_shapes=[pltpu.VMEM((16, 1), F32)],
        compiler_params=_cp(("arbitrary",)), name="in_proj",
    )(x, g, wqkv, wft, wrest, bfg, tri)


def _attn_body(q_ref, k_ref, v_ref, ck_ref, o_ref, m_ref, l_ref, acc_ref, *, tq, tk):
    pr = pl.program_id(1)
    qi = pl.program_id(2)
    ki = pl.program_id(3)

    @pl.when(ki == 0)
    def _():
        m_ref[...] = jnp.full_like(m_ref, NEG)
        l_ref[...] = jnp.zeros_like(l_ref)
        acc_ref[...] = jnp.zeros_like(acc_ref)

    lane = lax.broadcasted_iota(I32, (1, LANES), 1)
    first = lane < ATT_HD

    def step(masked):
        q = q_ref[...]
        k = k_ref[...]
        v = v_ref[...]
        if masked:
            keep = (lax.broadcasted_iota(I32, (tq, tk), 1) <= lax.broadcasted_iota(I32, (tq, tk), 0))
        alphas, pvs = [], []
        for hh in range(2):
            sel = first if hh == 0 else jnp.logical_not(first)
            qm = jnp.where(sel, q, jnp.zeros_like(q))
            s = _nt(qm, k) - ck_ref[pl.ds(2 * pr + hh, 1), :]
            if masked:
                s = jnp.where(keep, s, NEG)
            m_prev = m_ref[hh]
            m_new = jnp.maximum(m_prev, jnp.max(s, axis=1, keepdims=True))
            alpha = jnp.exp(m_prev - m_new)
            p = jnp.exp(s - m_new)
            l_ref[hh] = alpha * l_ref[hh] + jnp.sum(p, axis=1, keepdims=True)
            m_ref[hh] = m_new
            pvs.append(_dot(p.astype(BF), v))
            alphas.append(alpha)
        acc_ref[...] = jnp.where(first, alphas[0], alphas[1]) * acc_ref[...] + jnp.where(first, pvs[0], pvs[1])

    @pl.when(ki < qi)
    def _():
        step(False)

    @pl.when(ki == qi)
    def _():
        step(True)
        o_ref[...] = (acc_ref[...] / jnp.where(first, l_ref[0], l_ref[1])).astype(o_ref.dtype)


def attention(qkv, cum, batch, seq, tq=512):
    T = qkv.shape[0]
    tk = tq
    nq = seq // tq
    body = functools.partial(_attn_body, tq=tq, tk=tk)
    npair = ATT_HEADS // 2
    kblk = lambda b, qi, ki: b * nq + jnp.minimum(ki, qi)
    return pl.pallas_call(
        body, grid=(batch, npair, nq, nq),
        in_specs=[pl.BlockSpec((tq, LANES), lambda b, p, qi, ki: (b * nq + qi, p)),
                  pl.BlockSpec((tk, LANES), lambda b, p, qi, ki: (kblk(b, qi, ki), npair + p)),
                  pl.BlockSpec((tk, LANES), lambda b, p, qi, ki: (kblk(b, qi, ki), 2 * npair + p)),
                  pl.BlockSpec((16, tk), lambda b, p, qi, ki: (0, kblk(b, qi, ki)))],
        out_specs=pl.BlockSpec((tq, LANES), lambda b, p, qi, ki: (b * nq + qi, p)),
        out_shape=jax.ShapeDtypeStruct((T, D_ATT), BF),
        scratch_shapes=[pltpu.VMEM((2, tq, 1), F32), pltpu.VMEM((2, tq, 1), F32), pltpu.VMEM((tq, LANES), F32)],
        compiler_params=_cp(("parallel", "parallel", "parallel", "arbitrary")), name="fox_attention",
    )(qkv, qkv, qkv, cum)


CONV_HALO = 32
RG_HALO = 8


def _mix_body(rest_ref, cw_ref, cb_ref, lg_ref, lb_ref, rw_ref, rb_ref, wr_ref, br_ref, wi_ref, bi_ref, lam_ref,
              o_ref, ybuf, xbuf, hc, *, ts):
    si = pl.program_id(1)

    @pl.when(si == 0)
    def _():
        ybuf[0:CONV_HALO, :] = jnp.zeros((CONV_HALO, D_CONV), F32)
        xbuf[0:RG_HALO, :] = jnp.zeros((RG_HALO, D_RNN), F32)
        hc[...] = jnp.zeros_like(hc)

    y = rest_ref[:, 0:D_CONV] * _sigmoid(rest_ref[:, D_CONV:2 * D_CONV])
    ybuf[CONV_HALO:CONV_HALO + ts, :] = y
    acc = jnp.zeros((ts, D_CONV), F32)
    for k in range(CONV_K):
        acc = acc + cw_ref[k:k + 1, :] * ybuf[pl.ds(CONV_HALO - (CONV_K - 1) + k, ts), :]
    yc = acc + cb_ref[...]
    mu = jnp.mean(yc, axis=-1, keepdims=True)
    var = jnp.mean(jnp.square(yc - mu), axis=-1, keepdims=True)
    yn = (yc - mu) * lax.rsqrt(var + EPS) * lg_ref[...] + lb_ref[...]
    o_ref[:, 0:D_CONV] = (yn * _sigmoid(yn)).astype(o_ref.dtype)
    ybuf[0:CONV_HALO, :] = ybuf[ts:ts + CONV_HALO, :]

    xbuf[RG_HALO:RG_HALO + ts, :] = rest_ref[:, 2 * D_CONV:2 * D_CONV + D_RNN]
    xc = jnp.zeros((ts, D_RNN), F32)
    for k in range(RNN_CONV_K):
        xc = xc + rw_ref[k:k + 1, :] * xbuf[pl.ds(RG_HALO - (RNN_CONV_K - 1) + k, ts), :]
    xc = xc + rb_ref[...]
    xbuf[0:RG_HALO, :] = xbuf[ts:ts + RG_HALO, :]
    xcb = xc.astype(BF)
    r = _sigmoid(_dot(xcb, wr_ref[...]) + br_ref[...])
    gi = _sigmoid(_dot(xcb, wi_ref[...]) + bi_ref[...])
    nl = -lam_ref[...]
    sp = jnp.maximum(nl, 0.0) + jnp.log(1.0 + jnp.exp(-jnp.abs(nl)))
    log_a = -RG_C * r * sp
    a = jnp.exp(log_a)
    bt = jnp.sqrt(1.0 - jnp.exp(2.0 * log_a)) * (gi * xc)
    row = lax.broadcasted_iota(I32, (ts, 1), 0)
    sh = 1
    while sh < ts:
        live = row >= sh
        a_s = jnp.where(live, pltpu.roll(a, sh, 0), 1.0)
        b_s = jnp.where(live, pltpu.roll(bt, sh, 0), 0.0)
        bt = bt + a * b_s
        a = a * a_s
        sh *= 2
    h = bt + a * hc[...]
    hc[...] = h[ts - 1:ts, :]
    gate_in = rest_ref[:, 2 * D_CONV + D_RNN:2 * D_CONV + 2 * D_RNN]
    o_ref[:, D_CONV:D_CONV + D_RNN] = (h * _gelu(gate_in)).astype(o_ref.dtype)


def mixers(rest, p, batch, seq, ts=512):
    T = rest.shape[0]
    ns = seq // ts
    body = functools.partial(_mix_body, ts=ts)
    vec = lambda: pl.BlockSpec((1, D_CONV), lambda b, s: (0, 0))
    return pl.pallas_call(
        body, grid=(batch, ns),
        in_specs=[pl.BlockSpec((ts, N_REST), lambda b, s: (b * ns + s, 0)),
                  pl.BlockSpec((32, D_CONV), lambda b, s: (0, 0)), vec(), vec(), vec(),
                  pl.BlockSpec((8, D_RNN), lambda b, s: (0, 0)), vec(),
                  pl.BlockSpec((D_RNN, D_RNN), lambda b, s: (0, 0)), vec(),
                  pl.BlockSpec((D_RNN, D_RNN), lambda b, s: (0, 0)), vec(), vec()],
        out_specs=pl.BlockSpec((ts, D_CONV + D_RNN), lambda b, s: (b * ns + s, 0)),
        out_shape=jax.ShapeDtypeStruct((T, D_CONV + D_RNN), BF),
        scratch_shapes=[pltpu.VMEM((ts + CONV_HALO, D_CONV), F32), pltpu.VMEM((ts + RG_HALO, D_RNN), F32),
                        pltpu.VMEM((1, D_RNN), F32)],
        compiler_params=_cp(("arbitrary", "arbitrary")), name="conv_rglru",
    )(rest, p["cw"], p["cb"], p["lg"], p["lb"], p["rw"], p["rb"], p["wr"], p["br"], p["wi"], p["bi"], p["lam"])


def _outproj_body(x_ref, ya_ref, yc_ref, woa_ref, wob_ref, g2_ref, wq_ref, keys_ref, x1_ref, h2_ref, st_ref):
    x1 = x_ref[...] + _dot(ya_ref[...], woa_ref[...]) + _dot(yc_ref[...], wob_ref[...])
    x1_ref[...] = x1
    h2 = x1 * lax.rsqrt(jnp.mean(x1 * x1, axis=-1, keepdims=True) + EPS) * g2_ref[...]
    h2_ref[...] = h2
    q = _dot(h2.astype(BF), wq_ref[...]).astype(BF)
    for g in range(2 * PEER_HEADS):
        st_ref[g * N_KEYS:(g + 1) * N_KEYS, :] = _nt(keys_ref[g], q[:, g * D_HALF:(g + 1) * D_HALF])


def out_proj(x, ya, yc, p, g2, tm=256):
    T = x.shape[0]
    ng = 2 * PEER_HEADS
    return pl.pallas_call(
        _outproj_body, grid=(T // tm,),
        in_specs=[pl.BlockSpec((tm, D_MODEL), lambda i: (i, 0)),
                  pl.BlockSpec((tm, D_ATT), lambda i: (i, 0)),
                  pl.BlockSpec((tm, D_CONV + D_RNN), lambda i: (i, 0)),
                  pl.BlockSpec((D_ATT, D_MODEL), lambda i: (0, 0)),
                  pl.BlockSpec((D_CONV + D_RNN, D_MODEL), lambda i: (0, 0)),
                  pl.BlockSpec((1, D_MODEL), lambda i: (0, 0)),
                  pl.BlockSpec((D_MODEL, ng * D_HALF), lambda i: (0, 0)),
                  pl.BlockSpec((ng, N_KEYS, D_HALF), lambda i: (0, 0, 0))],
        out_specs=[pl.BlockSpec((tm, D_MODEL), lambda i: (i, 0)),
                   pl.BlockSpec((tm, D_MODEL), lambda i: (i, 0)),
                   pl.BlockSpec((ng * N_KEYS, tm), lambda i: (0, i))],
        out_shape=[jax.ShapeDtypeStruct((T, D_MODEL), F32),
                   jax.ShapeDtypeStruct((T, D_MODEL), F32),
                   jax.ShapeDtypeStruct((ng * N_KEYS, T), F32)],
        compiler_params=_cp(("parallel",)), name="out_proj_peer_scores",
    )(x, ya, yc, p["woa"], p["wob"], g2, p["wq"], p["keys"])


BIG_ID = 1 << 20


def _take_rounds(slabs, ids, nrounds):
    vals, picks = [], []
    for _ in range(nrounds):
        m = None
        for s in slabs:
            ms = jnp.max(s, axis=0, keepdims=True)
            m = ms if m is None else jnp.maximum(m, ms)
        pick = None
        for s, idc in zip(slabs, ids):
            ps = jnp.min(jnp.where(s == m, idc, BIG_ID), axis=0, keepdims=True)
            pick = ps if pick is None else jnp.minimum(pick, ps)
        slabs = [jnp.where(idc == pick, NEG, s) for s, idc in zip(slabs, ids)]
        vals.append(m)
        picks.append(pick)
    return vals, picks


def _route_body(st_ref, e_ref, g_ref, v_scr, i_scr, sv_scr, ci_scr, et_scr, gt_scr):
    ng = 2 * PEER_HEADS
    key_id = lax.broadcasted_iota(I32, (N_KEYS, LANES), 0)

    def stage1(g, carry):
        s = st_ref[pl.ds(pl.multiple_of(g * N_KEYS, N_KEYS), N_KEYS), :]
        vals, picks = _take_rounds([s], [key_id], TOPK)
        for r in range(TOPK):
            v_scr[g, r:r + 1, :] = vals[r]
            i_scr[g, r:r + 1, :] = picks[r]
        return carry

    lax.fori_loop(0, ng, stage1, 0)

    j16 = lax.broadcasted_iota(I32, (TOPK, LANES), 0)
    j8 = lax.broadcasted_iota(I32, (8, LANES), 0)

    def stage2(h, carry):
        v1 = v_scr[2 * h]
        v2 = v_scr[2 * h + 1]
        i1 = i_scr[2 * h]
        i2 = i_scr[2 * h + 1]
        slabs = [v1[0:1, :] + v2]
        ids = [j16]
        for i in range(1, TOPK):
            nj = TOPK // (i + 1)
            slabs.append(jnp.where(j8 < nj, v1[i:i + 1, :] + v2[0:8, :], NEG))
            ids.append(j8 + i * TOPK)
        vals, picks = _take_rounds(slabs, ids, TOPK)
        for r in range(TOPK):
            sv_scr[r:r + 1, :] = vals[r]
            ci_scr[r:r + 1, :] = picks[r]
        sv = sv_scr[...]
        ci = ci_scr[...]
        ci_hi = lax.shift_right_logical(ci, 4)
        ci_lo = jnp.bitwise_and(ci, TOPK - 1)
        e1 = jnp.zeros((TOPK, LANES), I32)
        e2 = jnp.zeros((TOPK, LANES), I32)
        for i in range(TOPK):
            e1 = jnp.where(ci_hi == i, i1[i:i + 1, :], e1)
            e2 = jnp.where(ci_lo == i, i2[i:i + 1, :], e2)
        p = jnp.exp(sv - sv[0:1, :])
        gates = p / jnp.sum(p, axis=0, keepdims=True)
        et_scr[pl.ds(pl.multiple_of(h * TOPK, TOPK), TOPK), :] = e1 * N_KEYS + e2
        gt_scr[pl.ds(pl.multiple_of(h * TOPK, TOPK), TOPK), :] = gates
        return carry

    lax.fori_loop(0, PEER_HEADS, stage2, 0)
    e_ref[...] = et_scr[...].T
    g_ref[...] = gt_scr[...].T


def route(st):
    T = st.shape[1]
    ng = 2 * PEER_HEADS
    return pl.pallas_call(
        _route_body, grid=(T // LANES,),
        in_specs=[pl.BlockSpec((ng * N_KEYS, LANES), lambda i: (0, i))],
        out_specs=[pl.BlockSpec((LANES, KSEL), lambda i: (i, 0)),
                   pl.BlockSpec((LANES, KSEL), lambda i: (i, 0))],
        out_shape=[jax.ShapeDtypeStruct((T, KSEL), I32),
                   jax.ShapeDtypeStruct((T, KSEL), F32)],
        scratch_shapes=[pltpu.VMEM((ng, TOPK, LANES), F32), pltpu.VMEM((ng, TOPK, LANES), I32),
                        pltpu.VMEM((TOPK, LANES), F32), pltpu.VMEM((TOPK, LANES), I32),
                        pltpu.VMEM((KSEL, LANES), I32), pltpu.VMEM((KSEL, LANES), F32)],
        compiler_params=_cp(("parallel",)), name="peer_route",
    )(st)


NC, NS, L = 2, 16, 16
NW = NC * NS
NJ = D_MODEL // L
R = TOPK
NCH = KSEL // R
G = 16
DW = D_MODEL // 2


def _perm(x, idx):
    return jnp.take_along_axis(x, idx, axis=0, mode="promise_in_bounds")


def pack_table(tab):
    e, d = tab.shape
    tb = lax.bitcast_convert_type(tab.astype(BF), jnp.uint16).astype(jnp.uint32).reshape(e, d // (2 * L), 2, L)
    words = tb[:, :, 0, :] | (tb[:, :, 1, :] << 16)
    return lax.bitcast_convert_type(words.reshape(e, d // 2), I32)


def _halves(w):
    lo = lax.bitcast_convert_type(lax.shift_left(w, 16), F32)
    hi = lax.bitcast_convert_type(jnp.bitwise_and(w, jnp.int32(-65536)), F32)
    return lo, hi


def peer_sc(x, resid, idx, gates, u_tab, v_tab):
    T = x.shape[0]
    tpw = T // NW
    ngroups = tpw // G
    idx3 = idx.reshape(T * NCH, R)
    g3 = gates.reshape(T * NCH, R)
    mesh = plsc.VectorSubcoreMesh(core_axis_name="c", subcore_axis_name="s")

    @functools.partial(
        pl.kernel, mesh=mesh,
        out_type=jax.ShapeDtypeStruct((T, D_MODEL), F32),
        scratch_types=[
            pltpu.VMEM((G, D_MODEL), F32),
            pltpu.VMEM((G, D_MODEL), F32),
            pltpu.VMEM((G * NCH, R), I32),
            pltpu.VMEM((G * NCH, R), F32),
            pltpu.VMEM((R, DW), I32),
            pltpu.VMEM((R, DW), I32),
            pltpu.VMEM((R, DW), I32),
            pltpu.VMEM((R, DW), I32),
            pltpu.SemaphoreType.DMA,
            pltpu.SemaphoreType.DMA,
            pltpu.SemaphoreType.DMA,
            pltpu.SemaphoreType.DMA,
        ],
        compiler_params=pltpu.CompilerParams(needs_layout_passes=False),
        name="peer_experts_sc",
    )
    def k(x_hbm, r_hbm, idx_hbm, g_hbm, u_hbm, v_hbm, out_hbm,
          x_v, out_v, idx_v, g_v, ub0, ub1, vb0, vb1, su0, su1, sv0, sv1):
        wid = lax.axis_index("s") * NC + lax.axis_index("c")
        ubs, vbs, sus, svs = (ub0, ub1), (vb0, vb1), (su0, su1), (sv0, sv1)
        iota = lax.iota(I32, L)

        def gather_copies(c, b):
            return (pltpu.make_async_copy(u_hbm.at[idx_v.at[c]], ubs[b], sus[b]),
                    pltpu.make_async_copy(v_hbm.at[idx_v.at[c]], vbs[b], svs[b]))

        def issue(c, b):
            for cp in gather_copies(c, b):
                cp.start()

        def wait(c, b):
            for cp in gather_copies(c, b):
                cp.wait()

        def compute(c, b):
            ub, vb = ubs[b], vbs[b]
            t = c // NCH

            def ubody(m, accs):
                xa = x_v[t, pl.ds(m * 2 * L, L)]
                xb = x_v[t, pl.ds(m * 2 * L + L, L)]
                out = []
                for kk in range(R):
                    lo, hi = _halves(ub[kk, pl.ds(m * L, L)])
                    out.append(accs[kk] + (xa * lo + xb * hi))
                return tuple(out)

            accs = lax.fori_loop(0, NJ // 2, ubody, tuple(jnp.zeros((L,), F32) for _ in range(R)))
            vecs = list(accs)
            dist = L // 2
            while dist >= 1:
                pidx = jnp.bitwise_xor(iota, dist)
                low = jnp.bitwise_and(iota, dist) == 0
                nxt = []
                for kk in range(dist):
                    a = vecs[kk]
                    bvec = vecs[kk + dist]
                    a = a + _perm(a, pidx)
                    bvec = bvec + _perm(bvec, pidx)
                    nxt.append(jnp.where(low, a, bvec))
                vecs = nxt
                dist //= 2
            hid = vecs[0]
            z = GC * (hid + 0.044715 * hid * hid * hid)
            gel = hid / (1.0 + jnp.exp(-2.0 * z))
            w = g_v[c, :] * gel
            wbs = [_perm(w, jnp.full((L,), kk, I32)) for kk in range(R)]

            @plsc.parallel_loop(0, NJ // 2)
            def _(m):
                pa, pb = [], []
                for kk in range(R):
                    lo, hi = _halves(vb[kk, pl.ds(m * L, L)])
                    pa.append(wbs[kk] * lo)
                    pb.append(wbs[kk] * hi)
                while len(pa) > 1:
                    pa = [pa[i] + pa[i + 1] for i in range(0, len(pa), 2)]
                    pb = [pb[i] + pb[i + 1] for i in range(0, len(pb), 2)]
                out_v[t, pl.ds(m * 2 * L, L)] = out_v[t, pl.ds(m * 2 * L, L)] + pa[0]
                out_v[t, pl.ds(m * 2 * L + L, L)] = out_v[t, pl.ds(m * 2 * L + L, L)] + pb[0]

        def group(g, carry):
            tok0 = wid * tpw + g * G
            pltpu.sync_copy(x_hbm.at[pl.ds(tok0, G)], x_v)
            pltpu.sync_copy(r_hbm.at[pl.ds(tok0, G)], out_v)
            pltpu.sync_copy(idx_hbm.at[pl.ds(tok0 * NCH, G * NCH)], idx_v)
            pltpu.sync_copy(g_hbm.at[pl.ds(tok0 * NCH, G * NCH)], g_v)
            issue(0, 0)

            def cbody(cc, c2):
                c = cc * 2
                issue(c + 1, 1)
                wait(c, 0)
                compute(c, 0)

                @pl.when(c + 2 < G * NCH)
                def _():
                    issue(c + 2, 0)

                wait(c + 1, 1)
                compute(c + 1, 1)
                return c2

            lax.fori_loop(0, G * NCH // 2, cbody, 0)
            pltpu.sync_copy(out_v, out_hbm.at[pl.ds(tok0, G)])
            return carry

        lax.fori_loop(0, ngroups, group, 0)

    return k(x, resid, idx3, g3, u_tab, v_tab)


def _fn_body(x_ref, g_ref, o_ref):
    xf = x_ref[...]
    o_ref[...] = xf * lax.rsqrt(jnp.mean(xf * xf, axis=-1, keepdims=True) + EPS) * g_ref[...]


def final_norm(x, g, tm=1024):
    T, d = x.shape
    return pl.pallas_call(
        _fn_body, grid=(T // tm,),
        in_specs=[pl.BlockSpec((tm, d), lambda i: (i, 0)), pl.BlockSpec((1, d), lambda i: (0, 0))],
        out_specs=pl.BlockSpec((tm, d), lambda i: (i, 0)),
        out_shape=jax.ShapeDtypeStruct((T, d), F32),
        compiler_params=_cp(("parallel",)), name="final_norm",
    )(x, g)


def _prep_layer(w_in, b_forget, conv_dw_w, conv_dw_b, conv_ln_g, conv_ln_b, rg_conv_w, rg_conv_b,
                rg_w_r, rg_b_r, rg_w_i, rg_b_i, rg_lambda, w_out, peer_wq, peer_k1, peer_k2):
    f0 = 3 * D_ATT
    wft = jnp.zeros((16, D_MODEL), BF).at[0:ATT_HEADS].set(w_in[:, f0:f0 + ATT_HEADS].T.astype(BF))
    bfg = jnp.zeros((16, 1), F32).at[0:ATT_HEADS, 0].set(b_forget)
    cw = jnp.zeros((32, D_CONV), F32).at[0:CONV_K].set(conv_dw_w)
    rw = jnp.zeros((8, D_RNN), F32).at[0:RNN_CONV_K].set(rg_conv_w)
    bd = lambda w: jax.scipy.linalg.block_diag(*[w[i] for i in range(RNN_BLOCKS)]).astype(BF)
    row = lambda v: v.reshape(1, -1).astype(F32)
    keys = jnp.stack([peer_k1, peer_k2], axis=1).reshape(2 * PEER_HEADS, N_KEYS, D_HALF).astype(BF)
    return dict(wqkv=w_in[:, 0:f0].astype(BF), wft=wft, wrest=w_in[:, f0 + ATT_HEADS:].astype(BF), bfg=bfg,
                cw=cw, cb=row(conv_dw_b), lg=row(conv_ln_g), lb=row(conv_ln_b),
                rw=rw, rb=row(rg_conv_b), wr=bd(rg_w_r), br=row(rg_b_r), wi=bd(rg_w_i), bi=row(rg_b_i),
                lam=row(rg_lambda), woa=w_out[0:D_ATT].astype(BF), wob=w_out[D_ATT:].astype(BF),
                wq=peer_wq.astype(BF), keys=keys)


def kernel(x, norm1_g, w_in, b_forget, conv_dw_w, conv_dw_b, conv_ln_g, conv_ln_b,
           rg_conv_w, rg_conv_b, rg_w_r, rg_b_r, rg_w_i, rg_b_i, rg_lambda, w_out,
           norm2_g, peer_wq, peer_k1, peer_k2, peer_u, peer_v, final_g):
    b, s, d = x.shape
    params = [_prep_layer(w_in[l], b_forget[l], conv_dw_w[l], conv_dw_b[l], conv_ln_g[l], conv_ln_b[l],
                          rg_conv_w[l], rg_conv_b[l], rg_w_r[l], rg_b_r[l], rg_w_i[l], rg_b_i[l], rg_lambda[l],
                          w_out[l], peer_wq[l], peer_k1[l], peer_k2[l]) for l in range(DEPTH)]
    tabs = [(pack_table(peer_u[l]), pack_table(peer_v[l])) for l in range(DEPTH)]
    bs = b // N_SLICES
    outs = []
    for i in range(N_SLICES):
        xt = x[i * bs:(i + 1) * bs].reshape(bs * s, d)
        for l in range(DEPTH):
            p = params[l]
            qkv, rest, cum = in_proj(xt, norm1_g[l].reshape(1, d), p["wqkv"], p["wft"], p["wrest"], p["bfg"], s)
            y_att = attention(qkv, cum, bs, s)
            y_cr = mixers(rest, p, bs, s)
            x1, h2, st = out_proj(xt, y_att, y_cr, p, norm2_g[l].reshape(1, d))
            experts, gates = route(st)
            xt = peer_sc(h2, x1, experts, gates, tabs[l][0], tabs[l][1])
        outs.append(final_norm(xt, final_g.reshape(1, d)).reshape(bs, s, d))
    return jnp.concatenate(outs, axis=0)
```

```python
import functools
import math

import jax
import jax.numpy as jnp
from jax import lax
from jax.experimental import pallas as pl
from jax.experimental.pallas import tpu as pltpu
from jax.experimental.pallas import tpu_sc as plsc

BF = jnp.bfloat16
F32 = jnp.float32
I32 = jnp.int32

D_MODEL = 1024
DEPTH = 2
ATT_HEADS = 8
ATT_HD = 64
D_ATT = ATT_HEADS * ATT_HD
D_CONV = 256
CONV_K = 31
D_RNN = 256
RNN_BLOCKS = 4
RNN_CONV_K = 4
RG_C = 8.0
EPS = 1e-6
N_REST = 2 * D_CONV + 2 * D_RNN
PEER_HEADS = 8
N_KEYS = 128
D_HALF = 128
TOPK = 16
KSEL = PEER_HEADS * TOPK
GC = 0.7978845608028654
NEG = float("-inf")

N_SLICES = 4
LANES = 128
VMEM_LIMIT = 48 * 1024 * 1024


def _cp(sem):
    return pltpu.CompilerParams(dimension_semantics=sem, vmem_limit_bytes=VMEM_LIMIT)


def _split3(x):
    hi = x.astype(BF)
    r = x - hi.astype(F32)
    mid = r.astype(BF)
    lo = (r - mid.astype(F32)).astype(BF)
    return hi, mid, lo


def _nt(a, b):
    return lax.dot_general(a, b, (((1,), (1,)), ((), ())), preferred_element_type=F32)


def _dot(a, b):
    return jnp.dot(a, b, preferred_element_type=F32)


def _sigmoid(x):
    return 1.0 / (1.0 + jnp.exp(-x))


def _gelu(x):
    return 0.5 * x * (1.0 + jnp.tanh(GC * (x + 0.044715 * x * x * x)))


LOG2E = 1.4426950408889634
NSPLIT = 3
FPAD = 16


def _inproj_body(x_ref, g_ref, wqk_ref, wvt_ref, wf_ref, wrest_ref, bf_ref, tri_ref, place_ref,
                 qkb_ref, vt_ref, rest_ref, carry_ref, *, blocks_per_seq, tm):
    i = pl.program_id(0)
    x = x_ref[...]
    h = x * lax.rsqrt(jnp.mean(x * x, axis=-1, keepdims=True) + EPS) * g_ref[...]
    hb = h.astype(BF)
    qk = _dot(hb, wqk_ref[...])
    col = lax.broadcasted_iota(I32, (1, 2 * D_ATT), 1)
    qk = jnp.where(col < D_ATT, qk * (LOG2E / math.sqrt(ATT_HD)), qk)
    qkb_ref[:, 0:2 * D_ATT] = qk.astype(BF)
    vt_ref[...] = _nt(wvt_ref[...], hb).astype(BF)
    rest_ref[...] = _dot(hb, wrest_ref[...])
    ft = _dot(hb, wf_ref[...]) + bf_ref[...]
    lf = jnp.minimum(ft, 0.0) - jnp.log(1.0 + jnp.exp(-jnp.abs(ft)))
    hi, mid, lo = _split3(lf)
    tri = tri_ref[...]
    cs = _dot(tri, hi) + _dot(tri, mid) + _dot(tri, lo)

    @pl.when(i % blocks_per_seq == 0)
    def _():
        carry_ref[...] = jnp.zeros_like(carry_ref)

    cum = cs + carry_ref[...]
    carry_ref[...] = cum[tm - 1:tm, :]
    pieces = _split3(cum * (-LOG2E))
    kb = _dot(pieces[0], place_ref[0]) + _dot(pieces[1], place_ref[1]) + _dot(pieces[2], place_ref[2])
    qkb_ref[:, 2 * D_ATT:3 * D_ATT] = kb.astype(BF)


def bias_lane(hh, j):
    return (ATT_HD if hh == 0 else 0) + j


def in_proj(x, g, p, seq, tm=512):
    T = x.shape[0]
    tri = (lax.broadcasted_iota(I32, (tm, tm), 0) >= lax.broadcasted_iota(I32, (tm, tm), 1)).astype(BF)
    shp = (NSPLIT, FPAD, D_ATT)
    hd = lax.broadcasted_iota(I32, shp, 1)
    target = (hd // 2) * LANES + jnp.where(hd % 2 == 0, ATT_HD, 0) + lax.broadcasted_iota(I32, shp, 0)
    place = ((lax.broadcasted_iota(I32, shp, 2) == target) & (hd < ATT_HEADS)).astype(BF)
    body = functools.partial(_inproj_body, blocks_per_seq=seq // tm, tm=tm)
    return pl.pallas_call(
        body, grid=(T // tm,),
        in_specs=[pl.BlockSpec((tm, D_MODEL), lambda i: (i, 0)),
                  pl.BlockSpec((1, D_MODEL), lambda i: (0, 0)),
                  pl.BlockSpec((D_MODEL, 2 * D_ATT), lambda i: (0, 0)),
                  pl.BlockSpec((D_ATT, D_MODEL), lambda i: (0, 0)),
                  pl.BlockSpec((D_MODEL, FPAD), lambda i: (0, 0)),
                  pl.BlockSpec((D_MODEL, N_REST), lambda i: (0, 0)),
                  pl.BlockSpec((1, FPAD), lambda i: (0, 0)),
                  pl.BlockSpec((tm, tm), lambda i: (0, 0)),
                  pl.BlockSpec(shp, lambda i: (0, 0, 0))],
        out_specs=[pl.BlockSpec((tm, 3 * D_ATT), lambda i: (i, 0)),
                   pl.BlockSpec((D_ATT, tm), lambda i: (0, i)),
                   pl.BlockSpec((tm, N_REST), lambda i: (i, 0))],
        out_shape=[jax.ShapeDtypeStruct((T, 3 * D_ATT), BF),
                   jax.ShapeDtypeStruct((D_ATT, T), BF),
                   jax.ShapeDtypeStruct((T, N_REST), F32)],
        scratch_shapes=[pltpu.VMEM((1, FPAD), F32)],
        compiler_params=_cp(("arbitrary",)), name="in_proj",
    )(x, g, p["wqk"], p["wvt"], p["wf"], p["wrest"], p["bfg"], tri, place)


def _attn_body(q_ref, k_ref, kb_ref, vt_ref, o_ref, m_ref, acc_ref, *, tq, tk):
    qi = pl.program_id(2)
    ki = pl.program_id(3)

    @pl.when(ki == 0)
    def _():
        m_ref[...] = jnp.full_like(m_ref, NEG)
        acc_ref[...] = jnp.zeros_like(acc_ref)

    lane = lax.broadcasted_iota(I32, (1, LANES), 1)
    first = lane < ATT_HD
    vrow = lax.broadcasted_iota(I32, (LANES, 1), 0) < ATT_HD

    def step(masked):
        q = q_ref[...]
        k = k_ref[...]
        kb = kb_ref[...]
        vt = vt_ref[...]
        if masked:
            keep = (lax.broadcasted_iota(I32, (tk, tq), 0) <= lax.broadcasted_iota(I32, (tk, tq), 1))
        for hh in range(2):
            own = first if hh == 0 else jnp.logical_not(first)
            ones = (lane >= bias_lane(hh, 0)) & (lane < bias_lane(hh, NSPLIT))
            qa = jnp.where(own, q, jnp.where(ones, 1.0, 0.0).astype(BF))
            ka = jnp.where(own, k, kb)
            st = _nt(ka, qa)
            if masked:
                st = jnp.where(keep, st, NEG)
            m_prev = m_ref[hh]
            m_new = jnp.maximum(m_prev, jnp.max(st, axis=0, keepdims=True))
            alpha = jnp.exp2(m_prev - m_new)
            p = jnp.exp2(st - m_new).astype(BF)
            m_ref[hh] = m_new
            vown = vrow if hh == 0 else jnp.logical_not(vrow)
            va = jnp.where(vown, vt, jnp.ones_like(vt))
            acc_ref[hh] = alpha * acc_ref[hh] + _dot(va, p)

    @pl.when(ki < qi)
    def _():
        step(False)

    @pl.when(ki == qi)
    def _():
        step(True)
        a0 = acc_ref[0]
        a1 = acc_ref[1]
        ot = jnp.where(vrow, a0 / a0[ATT_HD:ATT_HD + 1, :], a1 / a1[0:1, :])
        o_ref[...] = ot.T.astype(o_ref.dtype)


def attention(qkb, vt, batch, seq, tq=512):
    T = qkb.shape[0]
    tk = tq
    nq = seq // tq
    npair = ATT_HEADS // 2
    body = functools.partial(_attn_body, tq=tq, tk=tk)
    kblk = lambda b, qi, ki: b * nq + jnp.minimum(ki, qi)
    return pl.pallas_call(
        body, grid=(batch, npair, nq, nq),
        in_specs=[pl.BlockSpec((tq, LANES), lambda b, p, qi, ki: (b * nq + qi, p)),
                  pl.BlockSpec((tk, LANES), lambda b, p, qi, ki: (kblk(b, qi, ki), npair + p)),
                  pl.BlockSpec((tk, LANES), lambda b, p, qi, ki: (kblk(b, qi, ki), 2 * npair + p)),
                  pl.BlockSpec((LANES, tk), lambda b, p, qi, ki: (p, kblk(b, qi, ki)))],
        out_specs=pl.BlockSpec((tq, LANES), lambda b, p, qi, ki: (b * nq + qi, p)),
        out_shape=jax.ShapeDtypeStruct((T, D_ATT), BF),
        scratch_shapes=[pltpu.VMEM((2, 1, tq), F32), pltpu.VMEM((2, LANES, tq), F32)],
        compiler_params=_cp(("parallel", "parallel", "parallel", "arbitrary")), name="fox_attention",
    )(qkb, qkb, qkb, vt)


CONV_HALO = 32
RG_HALO = 8


def _mix_body(rest_ref, cw_ref, cb_ref, lg_ref, lb_ref, rw_ref, rb_ref, wr_ref, br_ref, wi_ref, bi_ref, lam_ref,
              o_ref, ybuf, xbuf, hc, *, ts):
    si = pl.program_id(1)

    @pl.when(si == 0)
    def _():
        ybuf[0:CONV_HALO, :] = jnp.zeros((CONV_HALO, D_CONV), F32)
        xbuf[0:RG_HALO, :] = jnp.zeros((RG_HALO, D_RNN), F32)
        hc[...] = jnp.zeros_like(hc)

    y = rest_ref[:, 0:D_CONV] * _sigmoid(rest_ref[:, D_CONV:2 * D_CONV])
    ybuf[CONV_HALO:CONV_HALO + ts, :] = y
    acc = jnp.zeros((ts, D_CONV), F32)
    for k in range(CONV_K):
        acc = acc + cw_ref[k:k + 1, :] * ybuf[pl.ds(CONV_HALO - (CONV_K - 1) + k, ts), :]
    yc = acc + cb_ref[...]
    mu = jnp.mean(yc, axis=-1, keepdims=True)
    var = jnp.mean(jnp.square(yc - mu), axis=-1, keepdims=True)
    yn = (yc - mu) * lax.rsqrt(var + EPS) * lg_ref[...] + lb_ref[...]
    o_ref[:, 0:D_CONV] = (yn * _sigmoid(yn)).astype(o_ref.dtype)
    ybuf[0:CONV_HALO, :] = ybuf[ts:ts + CONV_HALO, :]

    xbuf[RG_HALO:RG_HALO + ts, :] = rest_ref[:, 2 * D_CONV:2 * D_CONV + D_RNN]
    xc = jnp.zeros((ts, D_RNN), F32)
    for k in range(RNN_CONV_K):
        xc = xc + rw_ref[k:k + 1, :] * xbuf[pl.ds(RG_HALO - (RNN_CONV_K - 1) + k, ts), :]
    xc = xc + rb_ref[...]
    xbuf[0:RG_HALO, :] = xbuf[ts:ts + RG_HALO, :]
    xcb = xc.astype(BF)
    r = _sigmoid(_dot(xcb, wr_ref[...]) + br_ref[...])
    gi = _sigmoid(_dot(xcb, wi_ref[...]) + bi_ref[...])
    nl = -lam_ref[...]
    sp = jnp.maximum(nl, 0.0) + jnp.log(1.0 + jnp.exp(-jnp.abs(nl)))
    log_a = -RG_C * r * sp
    a = jnp.exp(log_a)
    bt = jnp.sqrt(1.0 - jnp.exp(2.0 * log_a)) * (gi * xc)
    row = lax.broadcasted_iota(I32, (ts, 1), 0)
    sh = 1
    while sh < ts:
        live = row >= sh
        a_s = jnp.where(live, pltpu.roll(a, sh, 0), 1.0)
        b_s = jnp.where(live, pltpu.roll(bt, sh, 0), 0.0)
        bt = bt + a * b_s
        a = a * a_s
        sh *= 2
    h = bt + a * hc[...]
    hc[...] = h[ts - 1:ts, :]
    gate_in = rest_ref[:, 2 * D_CONV + D_RNN:2 * D_CONV + 2 * D_RNN]
    o_ref[:, D_CONV:D_CONV + D_RNN] = (h * _gelu(gate_in)).astype(o_ref.dtype)


def mixers(rest, p, batch, seq, ts=512):
    T = rest.shape[0]
    ns = seq // ts
    body = functools.partial(_mix_body, ts=ts)
    vec = lambda: pl.BlockSpec((1, D_CONV), lambda b, s: (0, 0))
    return pl.pallas_call(
        body, grid=(batch, ns),
        in_specs=[pl.BlockSpec((ts, N_REST), lambda b, s: (b * ns + s, 0)),
                  pl.BlockSpec((32, D_CONV), lambda b, s: (0, 0)), vec(), vec(), vec(),
                  pl.BlockSpec((8, D_RNN), lambda b, s: (0, 0)), vec(),
                  pl.BlockSpec((D_RNN, D_RNN), lambda b, s: (0, 0)), vec(),
                  pl.BlockSpec((D_RNN, D_RNN), lambda b, s: (0, 0)), vec(), vec()],
        out_specs=pl.BlockSpec((ts, D_CONV + D_RNN), lambda b, s: (b * ns + s, 0)),
        out_shape=jax.ShapeDtypeStruct((T, D_CONV + D_RNN), BF),
        scratch_shapes=[pltpu.VMEM((ts + CONV_HALO, D_CONV), F32), pltpu.VMEM((ts + RG_HALO, D_RNN), F32),
                        pltpu.VMEM((1, D_RNN), F32)],
        compiler_params=_cp(("arbitrary", "arbitrary")), name="conv_rglru",
    )(rest, p["cw"], p["cb"], p["lg"], p["lb"], p["rw"], p["rb"], p["wr"], p["br"], p["wi"], p["bi"], p["lam"])


def _outproj_body(x_ref, ya_ref, yc_ref, woa_ref, wob_ref, g2_ref, wq_ref, keys_ref, x1_ref, h2_ref, st_ref):
    x1 = x_ref[...] + _dot(ya_ref[...], woa_ref[...]) + _dot(yc_ref[...], wob_ref[...])
    x1_ref[...] = x1
    h2 = x1 * lax.rsqrt(jnp.mean(x1 * x1, axis=-1, keepdims=True) + EPS) * g2_ref[...]
    h2_ref[...] = h2
    q = _dot(h2.astype(BF), wq_ref[...]).astype(BF)
    for g in range(2 * PEER_HEADS):
        st_ref[g * N_KEYS:(g + 1) * N_KEYS, :] = _nt(keys_ref[g], q[:, g * D_HALF:(g + 1) * D_HALF])


def out_proj(x, ya, yc, p, g2, tm=256):
    T = x.shape[0]
    ng = 2 * PEER_HEADS
    return pl.pallas_call(
        _outproj_body, grid=(T // tm,),
        in_specs=[pl.BlockSpec((tm, D_MODEL), lambda i: (i, 0)),
                  pl.BlockSpec((tm, D_ATT), lambda i: (i, 0)),
                  pl.BlockSpec((tm, D_CONV + D_RNN), lambda i: (i, 0)),
                  pl.BlockSpec((D_ATT, D_MODEL), lambda i: (0, 0)),
                  pl.BlockSpec((D_CONV + D_RNN, D_MODEL), lambda i: (0, 0)),
                  pl.BlockSpec((1, D_MODEL), lambda i: (0, 0)),
                  pl.BlockSpec((D_MODEL, ng * D_HALF), lambda i: (0, 0)),
                  pl.BlockSpec((ng, N_KEYS, D_HALF), lambda i: (0, 0, 0))],
        out_specs=[pl.BlockSpec((tm, D_MODEL), lambda i: (i, 0)),
                   pl.BlockSpec((tm, D_MODEL), lambda i: (i, 0)),
                   pl.BlockSpec((ng * N_KEYS, tm), lambda i: (0, i))],
        out_shape=[jax.ShapeDtypeStruct((T, D_MODEL), F32),
                   jax.ShapeDtypeStruct((T, D_MODEL), F32),
                   jax.ShapeDtypeStruct((ng * N_KEYS, T), F32)],
        compiler_params=_cp(("parallel",)), name="out_proj_peer_scores",
    )(x, ya, yc, p["woa"], p["wob"], g2, p["wq"], p["keys"])


BIG_ID = 1 << 20


def _take_rounds(slabs, ids, nrounds):
    vals, picks = [], []
    for _ in range(nrounds):
        m = None
        for s in slabs:
            ms = jnp.max(s, axis=0, keepdims=True)
            m = ms if m is None else jnp.maximum(m, ms)
        pick = None
        for s, idc in zip(slabs, ids):
            ps = jnp.min(jnp.where(s == m, idc, BIG_ID), axis=0, keepdims=True)
            pick = ps if pick is None else jnp.minimum(pick, ps)
        slabs = [jnp.where(idc == pick, NEG, s) for s, idc in zip(slabs, ids)]
        vals.append(m)
        picks.append(pick)
    return vals, picks


def _route_body(st_ref, e_ref, g_ref, v_scr, i_scr, sv_scr, ci_scr, et_scr, gt_scr):
    ng = 2 * PEER_HEADS
    key_id = lax.broadcasted_iota(I32, (N_KEYS, LANES), 0)

    def stage1(g, carry):
        s = st_ref[pl.ds(pl.multiple_of(g * N_KEYS, N_KEYS), N_KEYS), :]
        vals, picks = _take_rounds([s], [key_id], TOPK)
        for r in range(TOPK):
            v_scr[g, r:r + 1, :] = vals[r]
            i_scr[g, r:r + 1, :] = picks[r]
        return carry

    lax.fori_loop(0, ng, stage1, 0)

    j16 = lax.broadcasted_iota(I32, (TOPK, LANES), 0)
    j8 = lax.broadcasted_iota(I32, (8, LANES), 0)

    def stage2(h, carry):
        v1 = v_scr[2 * h]
        v2 = v_scr[2 * h + 1]
        i1 = i_scr[2 * h]
        i2 = i_scr[2 * h + 1]
        slabs = [v1[0:1, :] + v2]
        ids = [j16]
        for i in range(1, TOPK):
            nj = TOPK // (i + 1)
            slabs.append(jnp.where(j8 < nj, v1[i:i + 1, :] + v2[0:8, :], NEG))
            ids.append(j8 + i * TOPK)
        vals, picks = _take_rounds(slabs, ids, TOPK)
        for r in range(TOPK):
            sv_scr[r:r + 1, :] = vals[r]
            ci_scr[r:r + 1, :] = picks[r]
        sv = sv_scr[...]
        ci = ci_scr[...]
        ci_hi = lax.shift_right_logical(ci, 4)
        ci_lo = jnp.bitwise_and(ci, TOPK - 1)
        e1 = jnp.zeros((TOPK, LANES), I32)
        e2 = jnp.zeros((TOPK, LANES), I32)
        for i in range(TOPK):
            e1 = jnp.where(ci_hi == i, i1[i:i + 1, :], e1)
            e2 = jnp.where(ci_lo == i, i2[i:i + 1, :], e2)
        p = jnp.exp(sv - sv[0:1, :])
        gates = p / jnp.sum(p, axis=0, keepdims=True)
        et_scr[pl.ds(pl.multiple_of(h * TOPK, TOPK), TOPK), :] = e1 * N_KEYS + e2
        gt_scr[pl.ds(pl.multiple_of(h * TOPK, TOPK), TOPK), :] = gates
        return carry

    lax.fori_loop(0, PEER_HEADS, stage2, 0)
    e_ref[...] = et_scr[...].T
    g_ref[...] = gt_scr[...].T


def route(st):
    T = st.shape[1]
    ng = 2 * PEER_HEADS
    return pl.pallas_call(
        _route_body, grid=(T // LANES,),
        in_specs=[pl.BlockSpec((ng * N_KEYS, LANES), lambda i: (0, i))],
        out_specs=[pl.BlockSpec((LANES, KSEL), lambda i: (i, 0)),
                   pl.BlockSpec((LANES, KSEL), lambda i: (i, 0))],
        out_shape=[jax.ShapeDtypeStruct((T, KSEL), I32),
                   jax.ShapeDtypeStruct((T, KSEL), F32)],
        scratch_shapes=[pltpu.VMEM((ng, TOPK, LANES), F32), pltpu.VMEM((ng, TOPK, LANES), I32),
                        pltpu.VMEM((TOPK, LANES), F32), pltpu.VMEM((TOPK, LANES), I32),
                        pltpu.VMEM((KSEL, LANES), I32), pltpu.VMEM((KSEL, LANES), F32)],
        compiler_params=_cp(("parallel",)), name="peer_route",
    )(st)


NC, NS, L = 2, 16, 16
NW = NC * NS
NJ = D_MODEL // L
R = TOPK
NCH = KSEL // R
G = 16
DW = D_MODEL // 2


def _perm(x, idx):
    return jnp.take_along_axis(x, idx, axis=0, mode="promise_in_bounds")


def pack_table(tab):
    e, d = tab.shape
    tb = lax.bitcast_convert_type(tab.astype(BF), jnp.uint16).astype(jnp.uint32).reshape(e, d // (2 * L), 2, L)
    words = tb[:, :, 0, :] | (tb[:, :, 1, :] << 16)
    return lax.bitcast_convert_type(words.reshape(e, d // 2), I32)


def _halves(w):
    lo = lax.bitcast_convert_type(lax.shift_left(w, 16), F32)
    hi = lax.bitcast_convert_type(jnp.bitwise_and(w, jnp.int32(-65536)), F32)
    return lo, hi


def peer_sc(x, resid, idx, gates, u_tab, v_tab):
    T = x.shape[0]
    tpw = T // NW
    ngroups = tpw // G
    idx3 = idx.reshape(T * NCH, R)
    g3 = gates.reshape(T * NCH, R)
    mesh = plsc.VectorSubcoreMesh(core_axis_name="c", subcore_axis_name="s")

    @functools.partial(
        pl.kernel, mesh=mesh,
        out_type=jax.ShapeDtypeStruct((T, D_MODEL), F32),
        scratch_types=[
            pltpu.VMEM((G, D_MODEL), F32),
            pltpu.VMEM((G, D_MODEL), F32),
            pltpu.VMEM((G * NCH, R), I32),
            pltpu.VMEM((G * NCH, R), F32),
            pltpu.VMEM((R, DW), I32),
            pltpu.VMEM((R, DW), I32),
            pltpu.VMEM((R, DW), I32),
            pltpu.VMEM((R, DW), I32),
            pltpu.SemaphoreType.DMA,
            pltpu.SemaphoreType.DMA,
            pltpu.SemaphoreType.DMA,
            pltpu.SemaphoreType.DMA,
        ],
        compiler_params=pltpu.CompilerParams(needs_layout_passes=False),
        name="peer_experts_sc",
    )
    def k(x_hbm, r_hbm, idx_hbm, g_hbm, u_hbm, v_hbm, out_hbm,
          x_v, out_v, idx_v, g_v, ub0, ub1, vb0, vb1, su0, su1, sv0, sv1):
        wid = lax.axis_index("s") * NC + lax.axis_index("c")
        ubs, vbs, sus, svs = (ub0, ub1), (vb0, vb1), (su0, su1), (sv0, sv1)
        iota = lax.iota(I32, L)

        def gather_copies(c, b):
            return (pltpu.make_async_copy(u_hbm.at[idx_v.at[c]], ubs[b], sus[b]),
                    pltpu.make_async_copy(v_hbm.at[idx_v.at[c]], vbs[b], svs[b]))

        def issue(c, b):
            for cp in gather_copies(c, b):
                cp.start()

        def wait(c, b):
            for cp in gather_copies(c, b):
                cp.wait()

        def compute(c, b):
            ub, vb = ubs[b], vbs[b]
            t = c // NCH

            def ubody(m, accs):
                xa = x_v[t, pl.ds(m * 2 * L, L)]
                xb = x_v[t, pl.ds(m * 2 * L + L, L)]
                out = []
                for kk in range(R):
                    lo, hi = _halves(ub[kk, pl.ds(m * L, L)])
                    out.append(accs[kk] + (xa * lo + xb * hi))
                return tuple(out)

            accs = lax.fori_loop(0, NJ // 2, ubody, tuple(jnp.zeros((L,), F32) for _ in range(R)))
            vecs = list(accs)
            dist = L // 2
            while dist >= 1:
                pidx = jnp.bitwise_xor(iota, dist)
                low = jnp.bitwise_and(iota, dist) == 0
                nxt = []
                for kk in range(dist):
                    a = vecs[kk]
                    bvec = vecs[kk + dist]
                    a = a + _perm(a, pidx)
                    bvec = bvec + _perm(bvec, pidx)
                    nxt.append(jnp.where(low, a, bvec))
                vecs = nxt
                dist //= 2
            hid = vecs[0]
            z = GC * (hid + 0.044715 * hid * hid * hid)
            gel = hid / (1.0 + jnp.exp(-2.0 * z))
            w = g_v[c, :] * gel
            wbs = [_perm(w, jnp.full((L,), kk, I32)) for kk in range(R)]

            @plsc.parallel_loop(0, NJ // 2)
            def _(m):
                pa, pb = [], []
                for kk in range(R):
                    lo, hi = _halves(vb[kk, pl.ds(m * L, L)])
                    pa.append(wbs[kk] * lo)
                    pb.append(wbs[kk] * hi)
                while len(pa) > 1:
                    pa = [pa[i] + pa[i + 1] for i in range(0, len(pa), 2)]
                    pb = [pb[i] + pb[i + 1] for i in range(0, len(pb), 2)]
                out_v[t, pl.ds(m * 2 * L, L)] = out_v[t, pl.ds(m * 2 * L, L)] + pa[0]
                out_v[t, pl.ds(m * 2 * L + L, L)] = out_v[t, pl.ds(m * 2 * L + L, L)] + pb[0]

        def group(g, carry):
            tok0 = wid * tpw + g * G
            pltpu.sync_copy(x_hbm.at[pl.ds(tok0, G)], x_v)
            pltpu.sync_copy(r_hbm.at[pl.ds(tok0, G)], out_v)
            pltpu.sync_copy(idx_hbm.at[pl.ds(tok0 * NCH, G * NCH)], idx_v)
            pltpu.sync_copy(g_hbm.at[pl.ds(tok0 * NCH, G * NCH)], g_v)
            issue(0, 0)

            def cbody(cc, c2):
                c = cc * 2
                issue(c + 1, 1)
                wait(c, 0)
                compute(c, 0)

                @pl.when(c + 2 < G * NCH)
                def _():
                    issue(c + 2, 0)

                wait(c + 1, 1)
                compute(c + 1, 1)
                return c2

            lax.fori_loop(0, G * NCH // 2, cbody, 0)
            pltpu.sync_copy(out_v, out_hbm.at[pl.ds(tok0, G)])
            return carry

        lax.fori_loop(0, ngroups, group, 0)

    return k(x, resid, idx3, g3, u_tab, v_tab)


def _fn_body(x_ref, g_ref, o_ref):
    xf = x_ref[...]
    o_ref[...] = xf * lax.rsqrt(jnp.mean(xf * xf, axis=-1, keepdims=True) + EPS) * g_ref[...]


def final_norm(x, g, tm=1024):
    T, d = x.shape
    return pl.pallas_call(
        _fn_body, grid=(T // tm,),
        in_specs=[pl.BlockSpec((tm, d), lambda i: (i, 0)), pl.BlockSpec((1, d), lambda i: (0, 0))],
        out_specs=pl.BlockSpec((tm, d), lambda i: (i, 0)),
        out_shape=jax.ShapeDtypeStruct((T, d), F32),
        compiler_params=_cp(("parallel",)), name="final_norm",
    )(x, g)


def _prep_layer(w_in, b_forget, conv_dw_w, conv_dw_b, conv_ln_g, conv_ln_b, rg_conv_w, rg_conv_b,
                rg_w_r, rg_b_r, rg_w_i, rg_b_i, rg_lambda, w_out, peer_wq, peer_k1, peer_k2):
    f0 = 3 * D_ATT
    wf = jnp.zeros((D_MODEL, FPAD), BF).at[:, 0:ATT_HEADS].set(w_in[:, f0:f0 + ATT_HEADS].astype(BF))
    bfg = jnp.zeros((1, FPAD), F32).at[0, 0:ATT_HEADS].set(b_forget)
    cw = jnp.zeros((32, D_CONV), F32).at[0:CONV_K].set(conv_dw_w)
    rw = jnp.zeros((8, D_RNN), F32).at[0:RNN_CONV_K].set(rg_conv_w)
    bd = lambda w: jax.scipy.linalg.block_diag(*[w[i] for i in range(RNN_BLOCKS)]).astype(BF)
    row = lambda v: v.reshape(1, -1).astype(F32)
    keys = jnp.stack([peer_k1, peer_k2], axis=1).reshape(2 * PEER_HEADS, N_KEYS, D_HALF).astype(BF)
    return dict(wqk=w_in[:, 0:2 * D_ATT].astype(BF), wvt=w_in[:, 2 * D_ATT:f0].T.astype(BF), wf=wf,
                wrest=w_in[:, f0 + ATT_HEADS:].astype(BF), bfg=bfg,
                cw=cw, cb=row(conv_dw_b), lg=row(conv_ln_g), lb=row(conv_ln_b),
                rw=rw, rb=row(rg_conv_b), wr=bd(rg_w_r), br=row(rg_b_r), wi=bd(rg_w_i), bi=row(rg_b_i),
                lam=row(rg_lambda), woa=w_out[0:D_ATT].astype(BF), wob=w_out[D_ATT:].astype(BF),
                wq=peer_wq.astype(BF), keys=keys)


def kernel(x, norm1_g, w_in, b_forget, conv_dw_w, conv_dw_b, conv_ln_g, conv_ln_b,
           rg_conv_w, rg_conv_b, rg_w_r, rg_b_r, rg_w_i, rg_b_i, rg_lambda, w_out,
           norm2_g, peer_wq, peer_k1, peer_k2, peer_u, peer_v, final_g):
    b, s, d = x.shape
    params = [_prep_layer(w_in[l], b_forget[l], conv_dw_w[l], conv_dw_b[l], conv_ln_g[l], conv_ln_b[l],
                          rg_conv_w[l], rg_conv_b[l], rg_w_r[l], rg_b_r[l], rg_w_i[l], rg_b_i[l], rg_lambda[l],
                          w_out[l], peer_wq[l], peer_k1[l], peer_k2[l]) for l in range(DEPTH)]
    tabs = [(pack_table(peer_u[l]), pack_table(peer_v[l])) for l in range(DEPTH)]
    bs = b // N_SLICES
    outs = []
    for i in range(N_SLICES):
        xt = x[i * bs:(i + 1) * bs].reshape(bs * s, d)
        for l in range(DEPTH):
            p = params[l]
            qkb, vt, rest = in_proj(xt, norm1_g[l].reshape(1, d), p, s)
            y_att = attention(qkb, vt, bs, s)
            y_cr = mixers(rest, p, bs, s)
            x1, h2, st = out_proj(xt, y_att, y_cr, p, norm2_g[l].reshape(1, d))
            experts, gates = route(st)
            xt = peer_sc(h2, x1, experts, gates, tabs[l][0], tabs[l][1])
        outs.append(final_norm(xt, final_g.reshape(1, d)).reshape(bs, s, d))
    return jnp.concatenate(outs, axis=0)
```

```python
import functools
import math

import jax
import jax.numpy as jnp
from jax import lax
from jax.experimental import pallas as pl
from jax.experimental.pallas import tpu as pltpu
from jax.experimental.pallas import tpu_sc as plsc

BF = jnp.bfloat16
F32 = jnp.float32
I32 = jnp.int32

D_MODEL = 1024
DEPTH = 2
ATT_HEADS = 8
ATT_HD = 64
D_ATT = ATT_HEADS * ATT_HD
D_CONV = 256
CONV_K = 31
D_RNN = 256
RNN_BLOCKS = 4
RNN_CONV_K = 4
RG_C = 8.0
EPS = 1e-6
N_REST = 2 * D_CONV + 2 * D_RNN
PEER_HEADS = 8
N_KEYS = 128
D_HALF = 128
TOPK = 16
KSEL = PEER_HEADS * TOPK
GC = 0.7978845608028654
NEG = float("-inf")

N_SLICES = 4
LANES = 128
VMEM_LIMIT = 48 * 1024 * 1024


def _cp(sem):
    return pltpu.CompilerParams(dimension_semantics=sem, vmem_limit_bytes=VMEM_LIMIT)


def _split3(x):
    hi = x.astype(BF)
    r = x - hi.astype(F32)
    mid = r.astype(BF)
    lo = (r - mid.astype(F32)).astype(BF)
    return hi, mid, lo


def _nt(a, b):
    return lax.dot_general(a, b, (((1,), (1,)), ((), ())), preferred_element_type=F32)


def _dot(a, b):
    return jnp.dot(a, b, preferred_element_type=F32)


def _sigmoid(x):
    return 1.0 / (1.0 + jnp.exp(-x))


def _gelu(x):
    return 0.5 * x * (1.0 + jnp.tanh(GC * (x + 0.044715 * x * x * x)))


LOG2E = 1.4426950408889634
NSPLIT = 3
FPAD = 16


def _inproj_body(x_ref, g_ref, wqk_ref, wvt_ref, wf_ref, wrest_ref, bf_ref, tri_ref, place_ref,
                 qkb_ref, vt_ref, rest_ref, carry_ref, *, blocks_per_seq, tm):
    i = pl.program_id(0)
    x = x_ref[...]
    h = x * lax.rsqrt(jnp.mean(x * x, axis=-1, keepdims=True) + EPS) * g_ref[...]
    hb = h.astype(BF)
    qk = _dot(hb, wqk_ref[...])
    col = lax.broadcasted_iota(I32, (1, 2 * D_ATT), 1)
    qk = jnp.where(col < D_ATT, qk * (LOG2E / math.sqrt(ATT_HD)), qk)
    qkb_ref[:, 0:2 * D_ATT] = qk.astype(BF)
    vt_ref[...] = _nt(wvt_ref[...], hb).astype(BF)
    rest_ref[...] = _dot(hb, wrest_ref[...])
    ft = _dot(hb, wf_ref[...]) + bf_ref[...]
    lf = jnp.minimum(ft, 0.0) - jnp.log(1.0 + jnp.exp(-jnp.abs(ft)))
    hi, mid, lo = _split3(lf)
    tri = tri_ref[...]
    cs = _dot(tri, hi) + _dot(tri, mid) + _dot(tri, lo)

    @pl.when(i % blocks_per_seq == 0)
    def _():
        carry_ref[...] = jnp.zeros_like(carry_ref)

    cum = cs + carry_ref[...]
    carry_ref[...] = cum[tm - 1:tm, :]
    pieces = _split3(cum * (-LOG2E))
    kb = _dot(pieces[0], place_ref[0]) + _dot(pieces[1], place_ref[1]) + _dot(pieces[2], place_ref[2])
    qkb_ref[:, 2 * D_ATT:3 * D_ATT] = kb.astype(BF)


def bias_lane(hh, j):
    return (ATT_HD if hh == 0 else 0) + j


def in_proj(x, g, p, seq, tm=512):
    T = x.shape[0]
    tri = (lax.broadcasted_iota(I32, (tm, tm), 0) >= lax.broadcasted_iota(I32, (tm, tm), 1)).astype(BF)
    shp = (NSPLIT, FPAD, D_ATT)
    hd = lax.broadcasted_iota(I32, shp, 1)
    target = (hd // 2) * LANES + jnp.where(hd % 2 == 0, ATT_HD, 0) + lax.broadcasted_iota(I32, shp, 0)
    place = ((lax.broadcasted_iota(I32, shp, 2) == target) & (hd < ATT_HEADS)).astype(BF)
    body = functools.partial(_inproj_body, blocks_per_seq=seq // tm, tm=tm)
    return pl.pallas_call(
        body, grid=(T // tm,),
        in_specs=[pl.BlockSpec((tm, D_MODEL), lambda i: (i, 0)),
                  pl.BlockSpec((1, D_MODEL), lambda i: (0, 0)),
                  pl.BlockSpec((D_MODEL, 2 * D_ATT), lambda i: (0, 0)),
                  pl.BlockSpec((D_ATT, D_MODEL), lambda i: (0, 0)),
                  pl.BlockSpec((D_MODEL, FPAD), lambda i: (0, 0)),
                  pl.BlockSpec((D_MODEL, N_REST), lambda i: (0, 0)),
                  pl.BlockSpec((1, FPAD), lambda i: (0, 0)),
                  pl.BlockSpec((tm, tm), lambda i: (0, 0)),
                  pl.BlockSpec(shp, lambda i: (0, 0, 0))],
        out_specs=[pl.BlockSpec((tm, 3 * D_ATT), lambda i: (i, 0)),
                   pl.BlockSpec((D_ATT, tm), lambda i: (0, i)),
                   pl.BlockSpec((tm, N_REST), lambda i: (i, 0))],
        out_shape=[jax.ShapeDtypeStruct((T, 3 * D_ATT), BF),
                   jax.ShapeDtypeStruct((D_ATT, T), BF),
                   jax.ShapeDtypeStruct((T, N_REST), F32)],
        scratch_shapes=[pltpu.VMEM((1, FPAD), F32)],
        compiler_params=_cp(("arbitrary",)), name="in_proj",
    )(x, g, p["wqk"], p["wvt"], p["wf"], p["wrest"], p["bfg"], tri, place)


def _attn_body(q_ref, k_ref, kb_ref, vt_ref, o_ref, m_ref, acc_ref, *, tq, tk):
    qi = pl.program_id(2)
    ki = pl.program_id(3)

    @pl.when(ki == 0)
    def _():
        m_ref[...] = jnp.full_like(m_ref, NEG)
        acc_ref[...] = jnp.zeros_like(acc_ref)

    lane = lax.broadcasted_iota(I32, (1, LANES), 1)
    first = lane < ATT_HD
    vrow = lax.broadcasted_iota(I32, (LANES, 1), 0) < ATT_HD

    def step(masked):
        q = q_ref[...]
        k = k_ref[...]
        kb = kb_ref[...]
        vt = vt_ref[...]
        if masked:
            keep = (lax.broadcasted_iota(I32, (tk, tq), 0) <= lax.broadcasted_iota(I32, (tk, tq), 1))
        for hh in range(2):
            own = first if hh == 0 else jnp.logical_not(first)
            ones = (lane >= bias_lane(hh, 0)) & (lane < bias_lane(hh, NSPLIT))
            qa = jnp.where(own, q, jnp.where(ones, 1.0, 0.0).astype(BF))
            ka = jnp.where(own, k, kb)
            st = _nt(ka, qa)
            if masked:
                st = jnp.where(keep, st, NEG)
            m_prev = m_ref[hh]
            m_new = jnp.maximum(m_prev, jnp.max(st, axis=0, keepdims=True))
            alpha = jnp.exp2(m_prev - m_new)
            p = jnp.exp2(st - m_new).astype(BF)
            m_ref[hh] = m_new
            vown = vrow if hh == 0 else jnp.logical_not(vrow)
            va = jnp.where(vown, vt, jnp.ones_like(vt))
            acc_ref[hh] = alpha * acc_ref[hh] + _dot(va, p)

    @pl.when(ki < qi)
    def _():
        step(False)

    @pl.when(ki == qi)
    def _():
        step(True)
        a0 = acc_ref[0]
        a1 = acc_ref[1]
        ot = jnp.where(vrow, a0 / a0[ATT_HD:ATT_HD + 1, :], a1 / a1[0:1, :])
        o_ref[...] = ot.T.astype(o_ref.dtype)


def attention(qkb, vt, batch, seq, tq=512):
    T = qkb.shape[0]
    tk = tq
    nq = seq // tq
    npair = ATT_HEADS // 2
    body = functools.partial(_attn_body, tq=tq, tk=tk)
    kblk = lambda b, qi, ki: b * nq + jnp.minimum(ki, qi)
    return pl.pallas_call(
        body, grid=(batch, npair, nq, nq),
        in_specs=[pl.BlockSpec((tq, LANES), lambda b, p, qi, ki: (b * nq + qi, p)),
                  pl.BlockSpec((tk, LANES), lambda b, p, qi, ki: (kblk(b, qi, ki), npair + p)),
                  pl.BlockSpec((tk, LANES), lambda b, p, qi, ki: (kblk(b, qi, ki), 2 * npair + p)),
                  pl.BlockSpec((LANES, tk), lambda b, p, qi, ki: (p, kblk(b, qi, ki)))],
        out_specs=pl.BlockSpec((tq, LANES), lambda b, p, qi, ki: (b * nq + qi, p)),
        out_shape=jax.ShapeDtypeStruct((T, D_ATT), BF),
        scratch_shapes=[pltpu.VMEM((2, 1, tq), F32), pltpu.VMEM((2, LANES, tq), F32)],
        compiler_params=_cp(("parallel", "parallel", "parallel", "arbitrary")), name="fox_attention",
    )(qkb, qkb, qkb, vt)


CONV_HALO = 32
RG_HALO = 8


def _mix_body(rest_ref, cw_ref, cb_ref, lg_ref, lb_ref, rw_ref, rb_ref, wr_ref, br_ref, wi_ref, bi_ref, lam_ref,
              o_ref, ybuf, xbuf, hc, *, ts):
    si = pl.program_id(1)

    @pl.when(si == 0)
    def _():
        ybuf[0:CONV_HALO, :] = jnp.zeros((CONV_HALO, D_CONV), F32)
        xbuf[0:RG_HALO, :] = jnp.zeros((RG_HALO, D_RNN), F32)
        hc[...] = jnp.zeros_like(hc)

    y = rest_ref[:, 0:D_CONV] * _sigmoid(rest_ref[:, D_CONV:2 * D_CONV])
    ybuf[CONV_HALO:CONV_HALO + ts, :] = y
    acc = jnp.zeros((ts, D_CONV), F32)
    for k in range(CONV_K):
        acc = acc + cw_ref[k:k + 1, :] * ybuf[pl.ds(CONV_HALO - (CONV_K - 1) + k, ts), :]
    yc = acc + cb_ref[...]
    mu = jnp.mean(yc, axis=-1, keepdims=True)
    var = jnp.mean(jnp.square(yc - mu), axis=-1, keepdims=True)
    yn = (yc - mu) * lax.rsqrt(var + EPS) * lg_ref[...] + lb_ref[...]
    o_ref[:, 0:D_CONV] = (yn * _sigmoid(yn)).astype(o_ref.dtype)
    ybuf[0:CONV_HALO, :] = ybuf[ts:ts + CONV_HALO, :]

    xbuf[RG_HALO:RG_HALO + ts, :] = rest_ref[:, 2 * D_CONV:2 * D_CONV + D_RNN]
    xc = jnp.zeros((ts, D_RNN), F32)
    for k in range(RNN_CONV_K):
        xc = xc + rw_ref[k:k + 1, :] * xbuf[pl.ds(RG_HALO - (RNN_CONV_K - 1) + k, ts), :]
    xc = xc + rb_ref[...]
    xbuf[0:RG_HALO, :] = xbuf[ts:ts + RG_HALO, :]
    xcb = xc.astype(BF)
    r = _sigmoid(_dot(xcb, wr_ref[...]) + br_ref[...])
    gi = _sigmoid(_dot(xcb, wi_ref[...]) + bi_ref[...])
    nl = -lam_ref[...]
    sp = jnp.maximum(nl, 0.0) + jnp.log(1.0 + jnp.exp(-jnp.abs(nl)))
    log_a = -RG_C * r * sp
    a = jnp.exp(log_a)
    bt = jnp.sqrt(1.0 - jnp.exp(2.0 * log_a)) * (gi * xc)
    row = lax.broadcasted_iota(I32, (ts, 1), 0)
    sh = 1
    while sh < ts:
        live = row >= sh
        a_s = jnp.where(live, pltpu.roll(a, sh, 0), 1.0)
        b_s = jnp.where(live, pltpu.roll(bt, sh, 0), 0.0)
        bt = bt + a * b_s
        a = a * a_s
        sh *= 2
    h = bt + a * hc[...]
    hc[...] = h[ts - 1:ts, :]
    gate_in = rest_ref[:, 2 * D_CONV + D_RNN:2 * D_CONV + 2 * D_RNN]
    o_ref[:, D_CONV:D_CONV + D_RNN] = (h * _gelu(gate_in)).astype(o_ref.dtype)


def mixers(rest, p, batch, seq, ts=512):
    T = rest.shape[0]
    ns = seq // ts
    body = functools.partial(_mix_body, ts=ts)
    vec = lambda: pl.BlockSpec((1, D_CONV), lambda b, s: (0, 0))
    return pl.pallas_call(
        body, grid=(batch, ns),
        in_specs=[pl.BlockSpec((ts, N_REST), lambda b, s: (b * ns + s, 0)),
                  pl.BlockSpec((32, D_CONV), lambda b, s: (0, 0)), vec(), vec(), vec(),
                  pl.BlockSpec((8, D_RNN), lambda b, s: (0, 0)), vec(),
                  pl.BlockSpec((D_RNN, D_RNN), lambda b, s: (0, 0)), vec(),
                  pl.BlockSpec((D_RNN, D_RNN), lambda b, s: (0, 0)), vec(), vec()],
        out_specs=pl.BlockSpec((ts, D_CONV + D_RNN), lambda b, s: (b * ns + s, 0)),
        out_shape=jax.ShapeDtypeStruct((T, D_CONV + D_RNN), BF),
        scratch_shapes=[pltpu.VMEM((ts + CONV_HALO, D_CONV), F32), pltpu.VMEM((ts + RG_HALO, D_RNN), F32),
                        pltpu.VMEM((1, D_RNN), F32)],
        compiler_params=_cp(("arbitrary", "arbitrary")), name="conv_rglru",
    )(rest, p["cw"], p["cb"], p["lg"], p["lb"], p["rw"], p["rb"], p["wr"], p["br"], p["wi"], p["bi"], p["lam"])


def _outproj_body(x_ref, ya_ref, yc_ref, woa_ref, wob_ref, g2_ref, wq_ref, keys_ref, x1_ref, h2_ref, st_ref):
    x1 = x_ref[...] + _dot(ya_ref[...], woa_ref[...]) + _dot(yc_ref[...], wob_ref[...])
    x1_ref[...] = x1
    h2 = x1 * lax.rsqrt(jnp.mean(x1 * x1, axis=-1, keepdims=True) + EPS) * g2_ref[...]
    h2_ref[...] = h2
    q = _dot(h2.astype(BF), wq_ref[...]).astype(BF)
    for g in range(2 * PEER_HEADS):
        st_ref[g * N_KEYS:(g + 1) * N_KEYS, :] = _nt(keys_ref[g], q[:, g * D_HALF:(g + 1) * D_HALF])


def out_proj(x, ya, yc, p, g2, tm=256):
    T = x.shape[0]
    ng = 2 * PEER_HEADS
    return pl.pallas_call(
        _outproj_body, grid=(T // tm,),
        in_specs=[pl.BlockSpec((tm, D_MODEL), lambda i: (i, 0)),
                  pl.BlockSpec((tm, D_ATT), lambda i: (i, 0)),
                  pl.BlockSpec((tm, D_CONV + D_RNN), lambda i: (i, 0)),
                  pl.BlockSpec((D_ATT, D_MODEL), lambda i: (0, 0)),
                  pl.BlockSpec((D_CONV + D_RNN, D_MODEL), lambda i: (0, 0)),
                  pl.BlockSpec((1, D_MODEL), lambda i: (0, 0)),
                  pl.BlockSpec((D_MODEL, ng * D_HALF), lambda i: (0, 0)),
                  pl.BlockSpec((ng, N_KEYS, D_HALF), lambda i: (0, 0, 0))],
        out_specs=[pl.BlockSpec((tm, D_MODEL), lambda i: (i, 0)),
                   pl.BlockSpec((tm, D_MODEL), lambda i: (i, 0)),
                   pl.BlockSpec((ng * N_KEYS, tm), lambda i: (0, i))],
        out_shape=[jax.ShapeDtypeStruct((T, D_MODEL), F32),
                   jax.ShapeDtypeStruct((T, D_MODEL), F32),
                   jax.ShapeDtypeStruct((ng * N_KEYS, T), F32)],
        compiler_params=_cp(("parallel",)), name="out_proj_peer_scores",
    )(x, ya, yc, p["woa"], p["wob"], g2, p["wq"], p["keys"])


BIG_ID = 1 << 20


def _take_rounds(slabs, ids, nrounds):
    vals, picks = [], []
    for _ in range(nrounds):
        m = None
        for s in slabs:
            ms = jnp.max(s, axis=0, keepdims=True)
            m = ms if m is None else jnp.maximum(m, ms)
        pick = None
        for s, idc in zip(slabs, ids):
            ps = jnp.min(jnp.where(s == m, idc, BIG_ID), axis=0, keepdims=True)
            pick = ps if pick is None else jnp.minimum(pick, ps)
        slabs = [jnp.where(idc == pick, NEG, s) for s, idc in zip(slabs, ids)]
        vals.append(m)
        picks.append(pick)
    return vals, picks


def _route_body(st_ref, e_ref, g_ref, v_scr, i_scr, sv_scr, ci_scr, et_scr, gt_scr):
    ng = 2 * PEER_HEADS
    key_id = lax.broadcasted_iota(I32, (N_KEYS, LANES), 0)

    def stage1(g, carry):
        s = st_ref[pl.ds(pl.multiple_of(g * N_KEYS, N_KEYS), N_KEYS), :]
        vals, picks = _take_rounds([s], [key_id], TOPK)
        for r in range(TOPK):
            v_scr[g, r:r + 1, :] = vals[r]
            i_scr[g, r:r + 1, :] = picks[r]
        return carry

    lax.fori_loop(0, ng, stage1, 0)

    j16 = lax.broadcasted_iota(I32, (TOPK, LANES), 0)
    j8 = lax.broadcasted_iota(I32, (8, LANES), 0)

    def stage2(h, carry):
        v1 = v_scr[2 * h]
        v2 = v_scr[2 * h + 1]
        i1 = i_scr[2 * h]
        i2 = i_scr[2 * h + 1]
        slabs = [v1[0:1, :] + v2]
        ids = [j16]
        for i in range(1, TOPK):
            nj = TOPK // (i + 1)
            slabs.append(jnp.where(j8 < nj, v1[i:i + 1, :] + v2[0:8, :], NEG))
            ids.append(j8 + i * TOPK)
        vals, picks = _take_rounds(slabs, ids, TOPK)
        for r in range(TOPK):
            sv_scr[r:r + 1, :] = vals[r]
            ci_scr[r:r + 1, :] = picks[r]
        sv = sv_scr[...]
        ci = ci_scr[...]
        ci_hi = lax.shift_right_logical(ci, 4)
        ci_lo = jnp.bitwise_and(ci, TOPK - 1)
        e1 = jnp.zeros((TOPK, LANES), I32)
        e2 = jnp.zeros((TOPK, LANES), I32)
        for i in range(TOPK):
            e1 = jnp.where(ci_hi == i, i1[i:i + 1, :], e1)
            e2 = jnp.where(ci_lo == i, i2[i:i + 1, :], e2)
        p = jnp.exp(sv - sv[0:1, :])
        gates = p / jnp.sum(p, axis=0, keepdims=True)
        et_scr[pl.ds(pl.multiple_of(h * TOPK, TOPK), TOPK), :] = e1 * N_KEYS + e2
        gt_scr[pl.ds(pl.multiple_of(h * TOPK, TOPK), TOPK), :] = gates
        return carry

    lax.fori_loop(0, PEER_HEADS, stage2, 0)
    e_ref[...] = et_scr[...].T
    g_ref[...] = gt_scr[...].T


def route(st):
    T = st.shape[1]
    ng = 2 * PEER_HEADS
    return pl.pallas_call(
        _route_body, grid=(T // LANES,),
        in_specs=[pl.BlockSpec((ng * N_KEYS, LANES), lambda i: (0, i))],
        out_specs=[pl.BlockSpec((LANES, KSEL), lambda i: (i, 0)),
                   pl.BlockSpec((LANES, KSEL), lambda i: (i, 0))],
        out_shape=[jax.ShapeDtypeStruct((T, KSEL), I32),
                   jax.ShapeDtypeStruct((T, KSEL), F32)],
        scratch_shapes=[pltpu.VMEM((ng, TOPK, LANES), F32), pltpu.VMEM((ng, TOPK, LANES), I32),
                        pltpu.VMEM((TOPK, LANES), F32), pltpu.VMEM((TOPK, LANES), I32),
                        pltpu.VMEM((KSEL, LANES), I32), pltpu.VMEM((KSEL, LANES), F32)],
        compiler_params=_cp(("parallel",)), name="peer_route",
    )(st)


NC, NS, L = 2, 16, 16
NW = NC * NS
NJ = D_MODEL // L
R = TOPK
NCH = KSEL // R
SUB = 2
RD = SUB * R
G = 16
DW = D_MODEL // 2


def _perm(x, idx):
    return jnp.take_along_axis(x, idx, axis=0, mode="promise_in_bounds")


def pack_table(tab):
    e, d = tab.shape
    tb = lax.bitcast_convert_type(tab.astype(BF), jnp.uint16).astype(jnp.uint32).reshape(e, d // (2 * L), 2, L)
    words = tb[:, :, 0, :] | (tb[:, :, 1, :] << 16)
    return lax.bitcast_convert_type(words.reshape(e, d // 2), I32)


def _halves(w):
    lo = lax.bitcast_convert_type(lax.shift_left(w, 16), F32)
    hi = lax.bitcast_convert_type(jnp.bitwise_and(w, jnp.int32(-65536)), F32)
    return lo, hi


def _bf(w):
    return plsc.bitcast(w, BF)


def peer_sc(x, resid, idx, gates, u_tab, v_tab):
    T = x.shape[0]
    tpw = T // NW
    ngroups = tpw // G
    nchunks = G * NCH // SUB
    idx3 = idx.reshape(T * NCH // SUB, RD)
    g3 = gates.reshape(T * NCH, R)
    mesh = plsc.VectorSubcoreMesh(core_axis_name="c", subcore_axis_name="s")

    @functools.partial(
        pl.kernel, mesh=mesh,
        out_type=jax.ShapeDtypeStruct((T, D_MODEL), F32),
        scratch_types=[
            pltpu.VMEM((G, DW), I32),
            pltpu.VMEM((G, D_MODEL), F32),
            pltpu.VMEM((nchunks, RD), I32),
            pltpu.VMEM((G * NCH, R), F32),
            pltpu.VMEM((RD, DW), I32),
            pltpu.VMEM((RD, DW), I32),
            pltpu.VMEM((RD, DW), I32),
            pltpu.VMEM((RD, DW), I32),
            pltpu.SemaphoreType.DMA,
            pltpu.SemaphoreType.DMA,
            pltpu.SemaphoreType.DMA,
            pltpu.SemaphoreType.DMA,
        ],
        compiler_params=pltpu.CompilerParams(needs_layout_passes=False),
        name="peer_experts_sc",
    )
    def k(x_hbm, r_hbm, idx_hbm, g_hbm, u_hbm, v_hbm, out_hbm,
          x_v, out_v, idx_v, g_v, ub0, ub1, vb0, vb1, su0, su1, sv0, sv1):
        wid = lax.axis_index("s") * NC + lax.axis_index("c")
        ubs, vbs, sus, svs = (ub0, ub1), (vb0, vb1), (su0, su1), (sv0, sv1)
        iota = lax.iota(I32, L)

        def gather_copies(c, b):
            return (pltpu.make_async_copy(u_hbm.at[idx_v.at[c]], ubs[b], sus[b]),
                    pltpu.make_async_copy(v_hbm.at[idx_v.at[c]], vbs[b], svs[b]))

        def issue(c, b):
            for cp in gather_copies(c, b):
                cp.start()

        def wait(c, b):
            for cp in gather_copies(c, b):
                cp.wait()

        def compute(c, b, ro):
            ub, vb = ubs[b], vbs[b]
            t = c // NCH

            def ubody(mm, accs):
                x0 = _bf(x_v[t, pl.ds(mm * 2 * L, L)])
                x1 = _bf(x_v[t, pl.ds(mm * 2 * L + L, L)])
                out = []
                for kk in range(R):
                    pr = x0 * _bf(ub[ro + kk, pl.ds(mm * 2 * L, L)]) + x1 * _bf(ub[ro + kk, pl.ds(mm * 2 * L + L, L)])
                    lo, hi = _halves(plsc.bitcast(pr, I32))
                    out.append(accs[kk] + (lo + hi))
                return tuple(out)

            accs = lax.fori_loop(0, NJ // 4, ubody, tuple(jnp.zeros((L,), F32) for _ in range(R)))
            vecs = list(accs)
            dist = L // 2
            while dist >= 1:
                pidx = jnp.bitwise_xor(iota, dist)
                low = jnp.bitwise_and(iota, dist) == 0
                nxt = []
                for kk in range(dist):
                    a = vecs[kk]
                    bvec = vecs[kk + dist]
                    a = a + _perm(a, pidx)
                    bvec = bvec + _perm(bvec, pidx)
                    nxt.append(jnp.where(low, a, bvec))
                vecs = nxt
                dist //= 2
            hid = vecs[0]
            z = GC * (hid + 0.044715 * hid * hid * hid)
            gel = hid / (1.0 + jnp.exp(-2.0 * z))
            w = g_v[c, :] * gel
            wbs = []
            for kk in range(R):
                wb = _perm(w, jnp.full((L,), kk, I32))
                wbs.append(plsc.pack(wb, wb, format=plsc.PackFormat.INTERLEAVED))

            @plsc.parallel_loop(0, NJ // 2)
            def _(m):
                pr = [wbs[kk] * _bf(vb[ro + kk, pl.ds(m * L, L)]) for kk in range(R)]
                for _lvl in range(2):
                    pr = [pr[i] + pr[i + 1] for i in range(0, len(pr), 2)]
                los, his = [], []
                for q in pr:
                    lo, hi = _halves(plsc.bitcast(q, I32))
                    los.append(lo)
                    his.append(hi)
                while len(los) > 1:
                    los = [los[i] + los[i + 1] for i in range(0, len(los), 2)]
                    his = [his[i] + his[i + 1] for i in range(0, len(his), 2)]
                out_v[t, pl.ds(m * 2 * L, L)] = out_v[t, pl.ds(m * 2 * L, L)] + los[0]
                out_v[t, pl.ds(m * 2 * L + L, L)] = out_v[t, pl.ds(m * 2 * L + L, L)] + his[0]

        def group(g, carry):
            tok0 = wid * tpw + g * G
            pltpu.sync_copy(x_hbm.at[pl.ds(tok0, G)], x_v)
            pltpu.sync_copy(r_hbm.at[pl.ds(tok0, G)], out_v)
            pltpu.sync_copy(idx_hbm.at[pl.ds(tok0 * (NCH // SUB), nchunks)], idx_v)
            pltpu.sync_copy(g_hbm.at[pl.ds(tok0 * NCH, G * NCH)], g_v)
            issue(0, 0)

            def cbody(cc, c2):
                c = cc * 2
                issue(c + 1, 1)
                wait(c, 0)
                for sub in range(SUB):
                    compute(c * SUB + sub, 0, sub * R)

                @pl.when(c + 2 < nchunks)
                def _():
                    issue(c + 2, 0)

                wait(c + 1, 1)
                for sub in range(SUB):
                    compute((c + 1) * SUB + sub, 1, sub * R)
                return c2

            lax.fori_loop(0, nchunks // 2, cbody, 0)
            pltpu.sync_copy(out_v, out_hbm.at[pl.ds(tok0, G)])
            return carry

        lax.fori_loop(0, ngroups, group, 0)

    return k(x, resid, idx3, g3, u_tab, v_tab)


def _fn_body(x_ref, g_ref, o_ref):
    xf = x_ref[...]
    o_ref[...] = xf * lax.rsqrt(jnp.mean(xf * xf, axis=-1, keepdims=True) + EPS) * g_ref[...]


def final_norm(x, g, tm=1024):
    T, d = x.shape
    return pl.pallas_call(
        _fn_body, grid=(T // tm,),
        in_specs=[pl.BlockSpec((tm, d), lambda i: (i, 0)), pl.BlockSpec((1, d), lambda i: (0, 0))],
        out_specs=pl.BlockSpec((tm, d), lambda i: (i, 0)),
        out_shape=jax.ShapeDtypeStruct((T, d), F32),
        compiler_params=_cp(("parallel",)), name="final_norm",
    )(x, g)


def _prep_layer(w_in, b_forget, conv_dw_w, conv_dw_b, conv_ln_g, conv_ln_b, rg_conv_w, rg_conv_b,
                rg_w_r, rg_b_r, rg_w_i, rg_b_i, rg_lambda, w_out, peer_wq, peer_k1, peer_k2):
    f0 = 3 * D_ATT
    wf = jnp.zeros((D_MODEL, FPAD), BF).at[:, 0:ATT_HEADS].set(w_in[:, f0:f0 + ATT_HEADS].astype(BF))
    bfg = jnp.zeros((1, FPAD), F32).at[0, 0:ATT_HEADS].set(b_forget)
    cw = jnp.zeros((32, D_CONV), F32).at[0:CONV_K].set(conv_dw_w)
    rw = jnp.zeros((8, D_RNN), F32).at[0:RNN_CONV_K].set(rg_conv_w)
    bd = lambda w: jax.scipy.linalg.block_diag(*[w[i] for i in range(RNN_BLOCKS)]).astype(BF)
    row = lambda v: v.reshape(1, -1).astype(F32)
    keys = jnp.stack([peer_k1, peer_k2], axis=1).reshape(2 * PEER_HEADS, N_KEYS, D_HALF).astype(BF)
    return dict(wqk=w_in[:, 0:2 * D_ATT].astype(BF), wvt=w_in[:, 2 * D_ATT:f0].T.astype(BF), wf=wf,
                wrest=w_in[:, f0 + ATT_HEADS:].astype(BF), bfg=bfg,
                cw=cw, cb=row(conv_dw_b), lg=row(conv_ln_g), lb=row(conv_ln_b),
                rw=rw, rb=row(rg_conv_b), wr=bd(rg_w_r), br=row(rg_b_r), wi=bd(rg_w_i), bi=row(rg_b_i),
                lam=row(rg_lambda), woa=w_out[0:D_ATT].astype(BF), wob=w_out[D_ATT:].astype(BF),
                wq=peer_wq.astype(BF), keys=keys)


def kernel(x, norm1_g, w_in, b_forget, conv_dw_w, conv_dw_b, conv_ln_g, conv_ln_b,
           rg_conv_w, rg_conv_b, rg_w_r, rg_b_r, rg_w_i, rg_b_i, rg_lambda, w_out,
           norm2_g, peer_wq, peer_k1, peer_k2, peer_u, peer_v, final_g):
    b, s, d = x.shape
    params = [_prep_layer(w_in[l], b_forget[l], conv_dw_w[l], conv_dw_b[l], conv_ln_g[l], conv_ln_b[l],
                          rg_conv_w[l], rg_conv_b[l], rg_w_r[l], rg_b_r[l], rg_w_i[l], rg_b_i[l], rg_lambda[l],
                          w_out[l], peer_wq[l], peer_k1[l], peer_k2[l]) for l in range(DEPTH)]
    tabs = [(pack_table(peer_u[l]), pack_table(peer_v[l])) for l in range(DEPTH)]
    bs = b // N_SLICES
    outs = []
    for i in range(N_SLICES):
        xt = x[i * bs:(i + 1) * bs].reshape(bs * s, d)
        for l in range(DEPTH):
            p = params[l]
            qkb, vt, rest = in_proj(xt, norm1_g[l].reshape(1, d), p, s)
            y_att = attention(qkb, vt, bs, s)
            y_cr = mixers(rest, p, bs, s)
            x1, h2, st = out_proj(xt, y_att, y_cr, p, norm2_g[l].reshape(1, d))
            experts, gates = route(st)
            xt = peer_sc(pack_table(h2), x1, experts, gates, tabs[l][0], tabs[l][1])
        outs.append(final_norm(xt, final_g.reshape(1, d)).reshape(bs, s, d))
    return jnp.concatenate(outs, axis=0)
```

```python
import functools
import math

import jax
import jax.numpy as jnp
from jax import lax
from jax.experimental import pallas as pl
from jax.experimental.pallas import tpu as pltpu
from jax.experimental.pallas import tpu_sc as plsc

BF = jnp.bfloat16
F32 = jnp.float32
I32 = jnp.int32

D_MODEL = 1024
DEPTH = 2
ATT_HEADS = 8
ATT_HD = 64
D_ATT = ATT_HEADS * ATT_HD
D_CONV = 256
CONV_K = 31
D_RNN = 256
RNN_BLOCKS = 4
RNN_CONV_K = 4
RG_C = 8.0
EPS = 1e-6
N_REST = 2 * D_CONV + 2 * D_RNN
PEER_HEADS = 8
N_KEYS = 128
D_HALF = 128
TOPK = 16
KSEL = PEER_HEADS * TOPK
GC = 0.7978845608028654
NEG = float("-inf")

N_SLICES = 4
LANES = 128
VMEM_LIMIT = 48 * 1024 * 1024


def _cp(sem):
    return pltpu.CompilerParams(dimension_semantics=sem, vmem_limit_bytes=VMEM_LIMIT)


def _split3(x):
    hi = x.astype(BF)
    r = x - hi.astype(F32)
    mid = r.astype(BF)
    lo = (r - mid.astype(F32)).astype(BF)
    return hi, mid, lo


def _nt(a, b):
    return lax.dot_general(a, b, (((1,), (1,)), ((), ())), preferred_element_type=F32)


def _dot(a, b):
    return jnp.dot(a, b, preferred_element_type=F32)


def _sigmoid(x):
    return 1.0 / (1.0 + jnp.exp(-x))


def _gelu(x):
    return 0.5 * x * (1.0 + jnp.tanh(GC * (x + 0.044715 * x * x * x)))


LOG2E = 1.4426950408889634
NSPLIT = 3
FPAD = 16


def _inproj_body(x_ref, g_ref, wqk_ref, wvt_ref, wf_ref, wrest_ref, bf_ref, tri_ref, place_ref,
                 qkb_ref, vt_ref, rest_ref, carry_ref, *, blocks_per_seq, tm):
    i = pl.program_id(0)
    x = x_ref[...]
    h = x * lax.rsqrt(jnp.mean(x * x, axis=-1, keepdims=True) + EPS) * g_ref[...]
    hb = h.astype(BF)
    qk = _dot(hb, wqk_ref[...])
    col = lax.broadcasted_iota(I32, (1, 2 * D_ATT), 1)
    qk = jnp.where(col < D_ATT, qk * (LOG2E / math.sqrt(ATT_HD)), qk)
    qkb_ref[:, 0:2 * D_ATT] = qk.astype(BF)
    vt_ref[...] = _nt(wvt_ref[...], hb).astype(BF)
    rest_ref[...] = _dot(hb, wrest_ref[...])
    ft = _dot(hb, wf_ref[...]) + bf_ref[...]
    lf = jnp.minimum(ft, 0.0) - jnp.log(1.0 + jnp.exp(-jnp.abs(ft)))
    hi, mid, lo = _split3(lf)
    tri = tri_ref[...]
    cs = _dot(tri, hi) + _dot(tri, mid) + _dot(tri, lo)

    @pl.when(i % blocks_per_seq == 0)
    def _():
        carry_ref[...] = jnp.zeros_like(carry_ref)

    cum = cs + carry_ref[...]
    carry_ref[...] = cum[tm - 1:tm, :]
    pieces = _split3(cum * (-LOG2E))
    kb = _dot(pieces[0], place_ref[0]) + _dot(pieces[1], place_ref[1]) + _dot(pieces[2], place_ref[2])
    qkb_ref[:, 2 * D_ATT:3 * D_ATT] = kb.astype(BF)


def bias_lane(hh, j):
    return (ATT_HD if hh == 0 else 0) + j


def in_proj(x, g, p, seq, tm=512):
    T = x.shape[0]
    tri = (lax.broadcasted_iota(I32, (tm, tm), 0) >= lax.broadcasted_iota(I32, (tm, tm), 1)).astype(BF)
    shp = (NSPLIT, FPAD, D_ATT)
    hd = lax.broadcasted_iota(I32, shp, 1)
    target = (hd // 2) * LANES + jnp.where(hd % 2 == 0, ATT_HD, 0) + lax.broadcasted_iota(I32, shp, 0)
    place = ((lax.broadcasted_iota(I32, shp, 2) == target) & (hd < ATT_HEADS)).astype(BF)
    body = functools.partial(_inproj_body, blocks_per_seq=seq // tm, tm=tm)
    return pl.pallas_call(
        body, grid=(T // tm,),
        in_specs=[pl.BlockSpec((tm, D_MODEL), lambda i: (i, 0)),
                  pl.BlockSpec((1, D_MODEL), lambda i: (0, 0)),
                  pl.BlockSpec((D_MODEL, 2 * D_ATT), lambda i: (0, 0)),
                  pl.BlockSpec((D_ATT, D_MODEL), lambda i: (0, 0)),
                  pl.BlockSpec((D_MODEL, FPAD), lambda i: (0, 0)),
                  pl.BlockSpec((D_MODEL, N_REST), lambda i: (0, 0)),
                  pl.BlockSpec((1, FPAD), lambda i: (0, 0)),
                  pl.BlockSpec((tm, tm), lambda i: (0, 0)),
                  pl.BlockSpec(shp, lambda i: (0, 0, 0))],
        out_specs=[pl.BlockSpec((tm, 3 * D_ATT), lambda i: (i, 0)),
                   pl.BlockSpec((D_ATT, tm), lambda i: (0, i)),
                   pl.BlockSpec((tm, N_REST), lambda i: (i, 0))],
        out_shape=[jax.ShapeDtypeStruct((T, 3 * D_ATT), BF),
                   jax.ShapeDtypeStruct((D_ATT, T), BF),
                   jax.ShapeDtypeStruct((T, N_REST), F32)],
        scratch_shapes=[pltpu.VMEM((1, FPAD), F32)],
        compiler_params=_cp(("arbitrary",)), name="in_proj",
    )(x, g, p["wqk"], p["wvt"], p["wf"], p["wrest"], p["bfg"], tri, place)


def _attn_body(q_ref, k_ref, kb_ref, vt_ref, o_ref, m_ref, acc_ref, *, tq, tk):
    qi = pl.program_id(2)
    ki = pl.program_id(3)

    @pl.when(ki == 0)
    def _():
        m_ref[...] = jnp.full_like(m_ref, NEG)
        acc_ref[...] = jnp.zeros_like(acc_ref)

    lane = lax.broadcasted_iota(I32, (1, LANES), 1)
    first = lane < ATT_HD
    vrow = lax.broadcasted_iota(I32, (LANES, 1), 0) < ATT_HD

    def step(masked):
        q = q_ref[...]
        k = k_ref[...]
        kb = kb_ref[...]
        vt = vt_ref[...]
        if masked:
            keep = (lax.broadcasted_iota(I32, (tk, tq), 0) <= lax.broadcasted_iota(I32, (tk, tq), 1))
        for hh in range(2):
            own = first if hh == 0 else jnp.logical_not(first)
            ones = (lane >= bias_lane(hh, 0)) & (lane < bias_lane(hh, NSPLIT))
            qa = jnp.where(own, q, jnp.where(ones, 1.0, 0.0).astype(BF))
            ka = jnp.where(own, k, kb)
            st = _nt(ka, qa)
            if masked:
                st = jnp.where(keep, st, NEG)
            m_prev = m_ref[hh]
            m_new = jnp.maximum(m_prev, jnp.max(st, axis=0, keepdims=True))
            alpha = jnp.exp2(m_prev - m_new)
            p = jnp.exp2(st - m_new).astype(BF)
            m_ref[hh] = m_new
            vown = vrow if hh == 0 else jnp.logical_not(vrow)
            va = jnp.where(vown, vt, jnp.ones_like(vt))
            acc_ref[hh] = alpha * acc_ref[hh] + _dot(va, p)

    @pl.when(ki < qi)
    def _():
        step(False)

    @pl.when(ki == qi)
    def _():
        step(True)
        a0 = acc_ref[0]
        a1 = acc_ref[1]
        ot = jnp.where(vrow, a0 / a0[ATT_HD:ATT_HD + 1, :], a1 / a1[0:1, :])
        o_ref[...] = ot.T.astype(o_ref.dtype)


def attention(qkb, vt, batch, seq, tq=512):
    T = qkb.shape[0]
    tk = tq
    nq = seq // tq
    npair = ATT_HEADS // 2
    body = functools.partial(_attn_body, tq=tq, tk=tk)
    kblk = lambda b, qi, ki: b * nq + jnp.minimum(ki, qi)
    return pl.pallas_call(
        body, grid=(batch, npair, nq, nq),
        in_specs=[pl.BlockSpec((tq, LANES), lambda b, p, qi, ki: (b * nq + qi, p)),
                  pl.BlockSpec((tk, LANES), lambda b, p, qi, ki: (kblk(b, qi, ki), npair + p)),
                  pl.BlockSpec((tk, LANES), lambda b, p, qi, ki: (kblk(b, qi, ki), 2 * npair + p)),
                  pl.BlockSpec((LANES, tk), lambda b, p, qi, ki: (p, kblk(b, qi, ki)))],
        out_specs=pl.BlockSpec((tq, LANES), lambda b, p, qi, ki: (b * nq + qi, p)),
        out_shape=jax.ShapeDtypeStruct((T, D_ATT), BF),
        scratch_shapes=[pltpu.VMEM((2, 1, tq), F32), pltpu.VMEM((2, LANES, tq), F32)],
        compiler_params=_cp(("parallel", "parallel", "parallel", "arbitrary")), name="fox_attention",
    )(qkb, qkb, qkb, vt)


CONV_HALO = 32
RG_HALO = 8


def _mix_body(rest_ref, cw_ref, cb_ref, lg_ref, lb_ref, rw_ref, rb_ref, wr_ref, br_ref, wi_ref, bi_ref, lam_ref,
              o_ref, ybuf, xbuf, hc, *, ts):
    si = pl.program_id(1)

    @pl.when(si == 0)
    def _():
        ybuf[0:CONV_HALO, :] = jnp.zeros((CONV_HALO, D_CONV), F32)
        xbuf[0:RG_HALO, :] = jnp.zeros((RG_HALO, D_RNN), F32)
        hc[...] = jnp.zeros_like(hc)

    y = rest_ref[:, 0:D_CONV] * _sigmoid(rest_ref[:, D_CONV:2 * D_CONV])
    ybuf[CONV_HALO:CONV_HALO + ts, :] = y
    acc = jnp.zeros((ts, D_CONV), F32)
    for k in range(CONV_K):
        acc = acc + cw_ref[k:k + 1, :] * ybuf[pl.ds(CONV_HALO - (CONV_K - 1) + k, ts), :]
    yc = acc + cb_ref[...]
    mu = jnp.mean(yc, axis=-1, keepdims=True)
    var = jnp.mean(jnp.square(yc - mu), axis=-1, keepdims=True)
    yn = (yc - mu) * lax.rsqrt(var + EPS) * lg_ref[...] + lb_ref[...]
    o_ref[:, 0:D_CONV] = (yn * _sigmoid(yn)).astype(o_ref.dtype)
    ybuf[0:CONV_HALO, :] = ybuf[ts:ts + CONV_HALO, :]

    xbuf[RG_HALO:RG_HALO + ts, :] = rest_ref[:, 2 * D_CONV:2 * D_CONV + D_RNN]
    xc = jnp.zeros((ts, D_RNN), F32)
    for k in range(RNN_CONV_K):
        xc = xc + rw_ref[k:k + 1, :] * xbuf[pl.ds(RG_HALO - (RNN_CONV_K - 1) + k, ts), :]
    xc = xc + rb_ref[...]
    xbuf[0:RG_HALO, :] = xbuf[ts:ts + RG_HALO, :]
    xcb = xc.astype(BF)
    r = _sigmoid(_dot(xcb, wr_ref[...]) + br_ref[...])
    gi = _sigmoid(_dot(xcb, wi_ref[...]) + bi_ref[...])
    nl = -lam_ref[...]
    sp = jnp.maximum(nl, 0.0) + jnp.log(1.0 + jnp.exp(-jnp.abs(nl)))
    log_a = -RG_C * r * sp
    a = jnp.exp(log_a)
    bt = jnp.sqrt(1.0 - jnp.exp(2.0 * log_a)) * (gi * xc)
    row = lax.broadcasted_iota(I32, (ts, 1), 0)
    sh = 1
    while sh < ts:
        live = row >= sh
        a_s = jnp.where(live, pltpu.roll(a, sh, 0), 1.0)
        b_s = jnp.where(live, pltpu.roll(bt, sh, 0), 0.0)
        bt = bt + a * b_s
        a = a * a_s
        sh *= 2
    h = bt + a * hc[...]
    hc[...] = h[ts - 1:ts, :]
    gate_in = rest_ref[:, 2 * D_CONV + D_RNN:2 * D_CONV + 2 * D_RNN]
    o_ref[:, D_CONV:D_CONV + D_RNN] = (h * _gelu(gate_in)).astype(o_ref.dtype)


def mixers(rest, p, batch, seq, ts=512):
    T = rest.shape[0]
    ns = seq // ts
    body = functools.partial(_mix_body, ts=ts)
    vec = lambda: pl.BlockSpec((1, D_CONV), lambda b, s: (0, 0))
    return pl.pallas_call(
        body, grid=(batch, ns),
        in_specs=[pl.BlockSpec((ts, N_REST), lambda b, s: (b * ns + s, 0)),
                  pl.BlockSpec((32, D_CONV), lambda b, s: (0, 0)), vec(), vec(), vec(),
                  pl.BlockSpec((8, D_RNN), lambda b, s: (0, 0)), vec(),
                  pl.BlockSpec((D_RNN, D_RNN), lambda b, s: (0, 0)), vec(),
                  pl.BlockSpec((D_RNN, D_RNN), lambda b, s: (0, 0)), vec(), vec()],
        out_specs=pl.BlockSpec((ts, D_CONV + D_RNN), lambda b, s: (b * ns + s, 0)),
        out_shape=jax.ShapeDtypeStruct((T, D_CONV + D_RNN), BF),
        scratch_shapes=[pltpu.VMEM((ts + CONV_HALO, D_CONV), F32), pltpu.VMEM((ts + RG_HALO, D_RNN), F32),
                        pltpu.VMEM((1, D_RNN), F32)],
        compiler_params=_cp(("arbitrary", "arbitrary")), name="conv_rglru",
    )(rest, p["cw"], p["cb"], p["lg"], p["lb"], p["rw"], p["rb"], p["wr"], p["br"], p["wi"], p["bi"], p["lam"])


def _outproj_body(x_ref, ya_ref, yc_ref, woa_ref, wob_ref, g2_ref, wq_ref, keys_ref, x1_ref, h2_ref, st_ref):
    x1 = x_ref[...] + _dot(ya_ref[...], woa_ref[...]) + _dot(yc_ref[...], wob_ref[...])
    x1_ref[...] = x1
    h2 = x1 * lax.rsqrt(jnp.mean(x1 * x1, axis=-1, keepdims=True) + EPS) * g2_ref[...]
    h2_ref[...] = h2
    q = _dot(h2.astype(BF), wq_ref[...]).astype(BF)
    for g in range(2 * PEER_HEADS):
        st_ref[g * N_KEYS:(g + 1) * N_KEYS, :] = _nt(keys_ref[g], q[:, g * D_HALF:(g + 1) * D_HALF])


def out_proj(x, ya, yc, p, g2, tm=256):
    T = x.shape[0]
    ng = 2 * PEER_HEADS
    return pl.pallas_call(
        _outproj_body, grid=(T // tm,),
        in_specs=[pl.BlockSpec((tm, D_MODEL), lambda i: (i, 0)),
                  pl.BlockSpec((tm, D_ATT), lambda i: (i, 0)),
                  pl.BlockSpec((tm, D_CONV + D_RNN), lambda i: (i, 0)),
                  pl.BlockSpec((D_ATT, D_MODEL), lambda i: (0, 0)),
                  pl.BlockSpec((D_CONV + D_RNN, D_MODEL), lambda i: (0, 0)),
                  pl.BlockSpec((1, D_MODEL), lambda i: (0, 0)),
                  pl.BlockSpec((D_MODEL, ng * D_HALF), lambda i: (0, 0)),
                  pl.BlockSpec((ng, N_KEYS, D_HALF), lambda i: (0, 0, 0))],
        out_specs=[pl.BlockSpec((tm, D_MODEL), lambda i: (i, 0)),
                   pl.BlockSpec((tm, D_MODEL), lambda i: (i, 0)),
                   pl.BlockSpec((ng * N_KEYS, tm), lambda i: (0, i))],
        out_shape=[jax.ShapeDtypeStruct((T, D_MODEL), F32),
                   jax.ShapeDtypeStruct((T, D_MODEL), F32),
                   jax.ShapeDtypeStruct((ng * N_KEYS, T), F32)],
        compiler_params=_cp(("parallel",)), name="out_proj_peer_scores",
    )(x, ya, yc, p["woa"], p["wob"], g2, p["wq"], p["keys"])


BIG_ID = 1 << 20


def _take_rounds(slabs, ids, nrounds):
    vals, picks = [], []
    for _ in range(nrounds):
        m = None
        for s in slabs:
            ms = jnp.max(s, axis=0, keepdims=True)
            m = ms if m is None else jnp.maximum(m, ms)
        pick = None
        for s, idc in zip(slabs, ids):
            ps = jnp.min(jnp.where(s == m, idc, BIG_ID), axis=0, keepdims=True)
            pick = ps if pick is None else jnp.minimum(pick, ps)
        slabs = [jnp.where(idc == pick, NEG, s) for s, idc in zip(slabs, ids)]
        vals.append(m)
        picks.append(pick)
    return vals, picks


def _route_body(st_ref, e_ref, g_ref, v_scr, i_scr, sv_scr, ci_scr, et_scr, gt_scr):
    ng = 2 * PEER_HEADS
    key_id = lax.broadcasted_iota(I32, (N_KEYS, LANES), 0)

    def stage1(g, carry):
        s = st_ref[pl.ds(pl.multiple_of(g * N_KEYS, N_KEYS), N_KEYS), :]
        vals, picks = _take_rounds([s], [key_id], TOPK)
        for r in range(TOPK):
            v_scr[g, r:r + 1, :] = vals[r]
            i_scr[g, r:r + 1, :] = picks[r]
        return carry

    lax.fori_loop(0, ng, stage1, 0)

    j16 = lax.broadcasted_iota(I32, (TOPK, LANES), 0)
    j8 = lax.broadcasted_iota(I32, (8, LANES), 0)

    def stage2(h, carry):
        v1 = v_scr[2 * h]
        v2 = v_scr[2 * h + 1]
        i1 = i_scr[2 * h]
        i2 = i_scr[2 * h + 1]
        slabs = [v1[0:1, :] + v2]
        ids = [j16]
        for i in range(1, TOPK):
            nj = TOPK // (i + 1)
            slabs.append(jnp.where(j8 < nj, v1[i:i + 1, :] + v2[0:8, :], NEG))
            ids.append(j8 + i * TOPK)
        vals, picks = _take_rounds(slabs, ids, TOPK)
        for r in range(TOPK):
            sv_scr[r:r + 1, :] = vals[r]
            ci_scr[r:r + 1, :] = picks[r]
        sv = sv_scr[...]
        ci = ci_scr[...]
        ci_hi = lax.shift_right_logical(ci, 4)
        ci_lo = jnp.bitwise_and(ci, TOPK - 1)
        e1 = jnp.zeros((TOPK, LANES), I32)
        e2 = jnp.zeros((TOPK, LANES), I32)
        for i in range(TOPK):
            e1 = jnp.where(ci_hi == i, i1[i:i + 1, :], e1)
            e2 = jnp.where(ci_lo == i, i2[i:i + 1, :], e2)
        p = jnp.exp(sv - sv[0:1, :])
        gates = p / jnp.sum(p, axis=0, keepdims=True)
        et_scr[pl.ds(pl.multiple_of(h * TOPK, TOPK), TOPK), :] = e1 * N_KEYS + e2
        gt_scr[pl.ds(pl.multiple_of(h * TOPK, TOPK), TOPK), :] = gates
        return carry

    lax.fori_loop(0, PEER_HEADS, stage2, 0)
    e_ref[...] = et_scr[...].T
    g_ref[...] = gt_scr[...].T


def route(st):
    T = st.shape[1]
    ng = 2 * PEER_HEADS
    return pl.pallas_call(
        _route_body, grid=(T // LANES,),
        in_specs=[pl.BlockSpec((ng * N_KEYS, LANES), lambda i: (0, i))],
        out_specs=[pl.BlockSpec((LANES, KSEL), lambda i: (i, 0)),
                   pl.BlockSpec((LANES, KSEL), lambda i: (i, 0))],
        out_shape=[jax.ShapeDtypeStruct((T, KSEL), I32),
                   jax.ShapeDtypeStruct((T, KSEL), F32)],
        scratch_shapes=[pltpu.VMEM((ng, TOPK, LANES), F32), pltpu.VMEM((ng, TOPK, LANES), I32),
                        pltpu.VMEM((TOPK, LANES), F32), pltpu.VMEM((TOPK, LANES), I32),
                        pltpu.VMEM((KSEL, LANES), I32), pltpu.VMEM((KSEL, LANES), F32)],
        compiler_params=_cp(("parallel",)), name="peer_route",
    )(st)


NC, NS, L = 2, 16, 16
NW = NC * NS
NJ = D_MODEL // L
R = TOPK
NCH = KSEL // R
SUB = 2
RD = SUB * R
G = 16
DW = D_MODEL // 2


def _perm(x, idx):
    return jnp.take_along_axis(x, idx, axis=0, mode="promise_in_bounds")


def pack_table(tab):
    e, d = tab.shape
    tb = lax.bitcast_convert_type(tab.astype(BF), jnp.uint16).astype(jnp.uint32).reshape(e, d // (2 * L), 2, L)
    words = tb[:, :, 0, :] | (tb[:, :, 1, :] << 16)
    return lax.bitcast_convert_type(words.reshape(e, d // 2), I32)


def _halves(w):
    lo = lax.bitcast_convert_type(lax.shift_left(w, 16), F32)
    hi = lax.bitcast_convert_type(jnp.bitwise_and(w, jnp.int32(-65536)), F32)
    return lo, hi


def _bf(w):
    return plsc.bitcast(w, BF)


def peer_sc(x, resid, idx, gates, u_tab, v_tab):
    T = x.shape[0]
    tpw = T // NW
    ngroups = tpw // G
    nchunks = G * NCH // SUB
    idx3 = idx.reshape(T * NCH // SUB, RD)
    g3 = gates.reshape(T * NCH, R)
    mesh = plsc.VectorSubcoreMesh(core_axis_name="c", subcore_axis_name="s")

    @functools.partial(
        pl.kernel, mesh=mesh,
        out_type=jax.ShapeDtypeStruct((T, D_MODEL), F32),
        scratch_types=[
            pltpu.VMEM((G, DW), I32),
            pltpu.VMEM((G, D_MODEL), F32),
            pltpu.VMEM((nchunks, RD), I32),
            pltpu.VMEM((G * NCH, R), F32),
            pltpu.VMEM((RD, DW), I32),
            pltpu.VMEM((RD, DW), I32),
            pltpu.VMEM((RD, DW), I32),
            pltpu.VMEM((RD, DW), I32),
            pltpu.SemaphoreType.DMA,
            pltpu.SemaphoreType.DMA,
            pltpu.SemaphoreType.DMA,
            pltpu.SemaphoreType.DMA,
        ],
        compiler_params=pltpu.CompilerParams(needs_layout_passes=False),
        name="peer_experts_sc",
    )
    def k(x_hbm, r_hbm, idx_hbm, g_hbm, u_hbm, v_hbm, out_hbm,
          x_v, out_v, idx_v, g_v, ub0, ub1, vb0, vb1, su0, su1, sv0, sv1):
        wid = lax.axis_index("s") * NC + lax.axis_index("c")
        ubs, vbs, sus, svs = (ub0, ub1), (vb0, vb1), (su0, su1), (sv0, sv1)
        iota = lax.iota(I32, L)

        def gather_copies(c, b):
            return (pltpu.make_async_copy(u_hbm.at[idx_v.at[c]], ubs[b], sus[b]),
                    pltpu.make_async_copy(v_hbm.at[idx_v.at[c]], vbs[b], svs[b]))

        def issue(c, b):
            for cp in gather_copies(c, b):
                cp.start()

        def wait(c, b):
            for cp in gather_copies(c, b):
                cp.wait()

        def compute(c, b, ro):
            ub, vb = ubs[b], vbs[b]
            t = c // NCH

            def ubody(mm, accs):
                x0 = _bf(x_v[t, pl.ds(mm * 2 * L, L)])
                x1 = _bf(x_v[t, pl.ds(mm * 2 * L + L, L)])
                out = []
                for kk in range(R):
                    pr = x0 * _bf(ub[ro + kk, pl.ds(mm * 2 * L, L)]) + x1 * _bf(ub[ro + kk, pl.ds(mm * 2 * L + L, L)])
                    lo, hi = _halves(plsc.bitcast(pr, I32))
                    out.append(accs[kk] + (lo + hi))
                return tuple(out)

            accs = lax.fori_loop(0, NJ // 4, ubody, tuple(jnp.zeros((L,), F32) for _ in range(R)))
            vecs = list(accs)
            dist = L // 2
            while dist >= 1:
                pidx = jnp.bitwise_xor(iota, dist)
                low = jnp.bitwise_and(iota, dist) == 0
                nxt = []
                for kk in range(dist):
                    a = vecs[kk]
                    bvec = vecs[kk + dist]
                    a = a + _perm(a, pidx)
                    bvec = bvec + _perm(bvec, pidx)
                    nxt.append(jnp.where(low, a, bvec))
                vecs = nxt
                dist //= 2
            hid = vecs[0]
            z = GC * (hid + 0.044715 * hid * hid * hid)
            gel = hid / (1.0 + jnp.exp(-2.0 * z))
            w = g_v[c, :] * gel
            wbs = []
            for kk in range(R):
                wb = _perm(w, jnp.full((L,), kk, I32))
                wbs.append(plsc.pack(wb, wb, format=plsc.PackFormat.INTERLEAVED))

            @plsc.parallel_loop(0, NJ // 2)
            def _(m):
                pr = [wbs[kk] * _bf(vb[ro + kk, pl.ds(m * L, L)]) for kk in range(R)]
                for _lvl in range(2):
                    pr = [pr[i] + pr[i + 1] for i in range(0, len(pr), 2)]
                los, his = [], []
                for q in pr:
                    lo, hi = _halves(plsc.bitcast(q, I32))
                    los.append(lo)
                    his.append(hi)
                while len(los) > 1:
                    los = [los[i] + los[i + 1] for i in range(0, len(los), 2)]
                    his = [his[i] + his[i + 1] for i in range(0, len(his), 2)]
                out_v[t, pl.ds(m * 2 * L, L)] = out_v[t, pl.ds(m * 2 * L, L)] + los[0]
                out_v[t, pl.ds(m * 2 * L + L, L)] = out_v[t, pl.ds(m * 2 * L + L, L)] + his[0]

        def group(g, carry):
            tok0 = wid * tpw + g * G
            pltpu.sync_copy(x_hbm.at[pl.ds(tok0, G)], x_v)
            pltpu.sync_copy(r_hbm.at[pl.ds(tok0, G)], out_v)
            pltpu.sync_copy(idx_hbm.at[pl.ds(tok0 * (NCH // SUB), nchunks)], idx_v)
            pltpu.sync_copy(g_hbm.at[pl.ds(tok0 * NCH, G * NCH)], g_v)
            issue(0, 0)

            def cbody(cc, c2):
                c = cc * 2
                issue(c + 1, 1)
                wait(c, 0)
                for sub in range(SUB):
                    compute(c * SUB + sub, 0, sub * R)

                @pl.when(c + 2 < nchunks)
                def _():
                    issue(c + 2, 0)

                wait(c + 1, 1)
                for sub in range(SUB):
                    compute((c + 1) * SUB + sub, 1, sub * R)
                return c2

            lax.fori_loop(0, nchunks // 2, cbody, 0)
            pltpu.sync_copy(out_v, out_hbm.at[pl.ds(tok0, G)])
            return carry

        lax.fori_loop(0, ngroups, group, 0)

    return k(x, resid, idx3, g3, u_tab, v_tab)


def _fn_body(x_ref, g_ref, o_ref):
    xf = x_ref[...]
    o_ref[...] = xf * lax.rsqrt(jnp.mean(xf * xf, axis=-1, keepdims=True) + EPS) * g_ref[...]


def final_norm(x, g, tm=1024):
    T, d = x.shape
    return pl.pallas_call(
        _fn_body, grid=(T // tm,),
        in_specs=[pl.BlockSpec((tm, d), lambda i: (i, 0)), pl.BlockSpec((1, d), lambda i: (0, 0))],
        out_specs=pl.BlockSpec((tm, d), lambda i: (i, 0)),
        out_shape=jax.ShapeDtypeStruct((T, d), F32),
        compiler_params=_cp(("parallel",)), name="final_norm",
    )(x, g)


def _prep_layer(w_in, b_forget, conv_dw_w, conv_dw_b, conv_ln_g, conv_ln_b, rg_conv_w, rg_conv_b,
                rg_w_r, rg_b_r, rg_w_i, rg_b_i, rg_lambda, w_out, peer_wq, peer_k1, peer_k2):
    f0 = 3 * D_ATT
    wf = jnp.zeros((D_MODEL, FPAD), BF).at[:, 0:ATT_HEADS].set(w_in[:, f0:f0 + ATT_HEADS].astype(BF))
    bfg = jnp.zeros((1, FPAD), F32).at[0, 0:ATT_HEADS].set(b_forget)
    cw = jnp.zeros((32, D_CONV), F32).at[0:CONV_K].set(conv_dw_w)
    rw = jnp.zeros((8, D_RNN), F32).at[0:RNN_CONV_K].set(rg_conv_w)
    bd = lambda w: jax.scipy.linalg.block_diag(*[w[i] for i in range(RNN_BLOCKS)]).astype(BF)
    row = lambda v: v.reshape(1, -1).astype(F32)
    keys = jnp.stack([peer_k1, peer_k2], axis=1).reshape(2 * PEER_HEADS, N_KEYS, D_HALF).astype(BF)
    return dict(wqk=w_in[:, 0:2 * D_ATT].astype(BF), wvt=w_in[:, 2 * D_ATT:f0].T.astype(BF), wf=wf,
                wrest=w_in[:, f0 + ATT_HEADS:].astype(BF), bfg=bfg,
                cw=cw, cb=row(conv_dw_b), lg=row(conv_ln_g), lb=row(conv_ln_b),
                rw=rw, rb=row(rg_conv_b), wr=bd(rg_w_r), br=row(rg_b_r), wi=bd(rg_w_i), bi=row(rg_b_i),
                lam=row(rg_lambda), woa=w_out[0:D_ATT].astype(BF), wob=w_out[D_ATT:].astype(BF),
                wq=peer_wq.astype(BF), keys=keys)


def kernel(x, norm1_g, w_in, b_forget, conv_dw_w, conv_dw_b, conv_ln_g, conv_ln_b,
           rg_conv_w, rg_conv_b, rg_w_r, rg_b_r, rg_w_i, rg_b_i, rg_lambda, w_out,
           norm2_g, peer_wq, peer_k1, peer_k2, peer_u, peer_v, final_g):
    b, s, d = x.shape
    params = [_prep_layer(w_in[l], b_forget[l], conv_dw_w[l], conv_dw_b[l], conv_ln_g[l], conv_ln_b[l],
                          rg_conv_w[l], rg_conv_b[l], rg_w_r[l], rg_b_r[l], rg_w_i[l], rg_b_i[l], rg_lambda[l],
                          w_out[l], peer_wq[l], peer_k1[l], peer_k2[l]) for l in range(DEPTH)]
    tabs = [(pack_table(peer_u[l]), pack_table(peer_v[l])) for l in range(DEPTH)]
    bs = b // N_SLICES
    xs = [x[i * bs:(i + 1) * bs].reshape(bs * s, d) for i in range(N_SLICES)]
    prev = None
    for l in range(DEPTH):
        p = params[l]
        for i in range(N_SLICES):
            xt = xs[i]
            if prev is not None:
                xt, prev = lax.optimization_barrier((xt, prev))
            qkb, vt, rest = in_proj(xt, norm1_g[l].reshape(1, d), p, s)
            y_att = attention(qkb, vt, bs, s)
            y_cr = mixers(rest, p, bs, s)
            x1, h2, st = out_proj(xt, y_att, y_cr, p, norm2_g[l].reshape(1, d))
            experts, gates = route(st)
            prev = experts
            xs[i] = peer_sc(pack_table(h2), x1, experts, gates, tabs[l][0], tabs[l][1])
    outs = [final_norm(xt, final_g.reshape(1, d)).reshape(bs, s, d) for xt in xs]
    return jnp.concatenate(outs, axis=0)
```

```python
import functools
import math

import jax
import jax.numpy as jnp
from jax import lax
from jax.experimental import pallas as pl
from jax.experimental.pallas import tpu as pltpu
from jax.experimental.pallas import tpu_sc as plsc

BF = jnp.bfloat16
F32 = jnp.float32
I32 = jnp.int32

D_MODEL = 1024
DEPTH = 2
ATT_HEADS = 8
ATT_HD = 64
D_ATT = ATT_HEADS * ATT_HD
D_CONV = 256
CONV_K = 31
D_RNN = 256
RNN_BLOCKS = 4
RNN_CONV_K = 4
RG_C = 8.0
EPS = 1e-6
N_REST = 2 * D_CONV + 2 * D_RNN
PEER_HEADS = 8
N_KEYS = 128
D_HALF = 128
TOPK = 16
KSEL = PEER_HEADS * TOPK
GC = 0.7978845608028654
NEG = float("-inf")

N_SLICES = 4
LANES = 128
VMEM_LIMIT = 48 * 1024 * 1024


def _cp(sem):
    return pltpu.CompilerParams(dimension_semantics=sem, vmem_limit_bytes=VMEM_LIMIT)


def _split3(x):
    hi = x.astype(BF)
    r = x - hi.astype(F32)
    mid = r.astype(BF)
    lo = (r - mid.astype(F32)).astype(BF)
    return hi, mid, lo


def _nt(a, b):
    return lax.dot_general(a, b, (((1,), (1,)), ((), ())), preferred_element_type=F32)


def _dot(a, b):
    return jnp.dot(a, b, preferred_element_type=F32)


def _sigmoid(x):
    return 1.0 / (1.0 + jnp.exp(-x))


def _gelu(x):
    return 0.5 * x * (1.0 + jnp.tanh(GC * (x + 0.044715 * x * x * x)))


LOG2E = 1.4426950408889634
NSPLIT = 3
FPAD = 16


def _inproj_body(x_ref, g_ref, wqk_ref, wvt_ref, wf_ref, wrest_ref, bf_ref, tri_ref, place_ref,
                 qkb_ref, vt_ref, rest_ref, carry_ref, *, blocks_per_seq, tm):
    i = pl.program_id(0)
    x = x_ref[...]
    h = x * lax.rsqrt(jnp.mean(x * x, axis=-1, keepdims=True) + EPS) * g_ref[...]
    hb = h.astype(BF)
    qk = _dot(hb, wqk_ref[...])
    col = lax.broadcasted_iota(I32, (1, 2 * D_ATT), 1)
    qk = jnp.where(col < D_ATT, qk * (LOG2E / math.sqrt(ATT_HD)), qk)
    qkb_ref[:, 0:2 * D_ATT] = qk.astype(BF)
    vt_ref[...] = _nt(wvt_ref[...], hb).astype(BF)
    rest_ref[...] = _dot(hb, wrest_ref[...])
    ft = _dot(hb, wf_ref[...]) + bf_ref[...]
    lf = jnp.minimum(ft, 0.0) - jnp.log(1.0 + jnp.exp(-jnp.abs(ft)))
    hi, mid, lo = _split3(lf)
    tri = tri_ref[...]
    cs = _dot(tri, hi) + _dot(tri, mid) + _dot(tri, lo)

    @pl.when(i % blocks_per_seq == 0)
    def _():
        carry_ref[...] = jnp.zeros_like(carry_ref)

    cum = cs + carry_ref[...]
    carry_ref[...] = cum[tm - 1:tm, :]
    pieces = _split3(cum * (-LOG2E))
    kb = _dot(pieces[0], place_ref[0]) + _dot(pieces[1], place_ref[1]) + _dot(pieces[2], place_ref[2])
    qkb_ref[:, 2 * D_ATT:3 * D_ATT] = kb.astype(BF)


def bias_lane(hh, j):
    return (ATT_HD if hh == 0 else 0) + j


def in_proj(x, g, p, seq, tm=512):
    T = x.shape[0]
    tri = (lax.broadcasted_iota(I32, (tm, tm), 0) >= lax.broadcasted_iota(I32, (tm, tm), 1)).astype(BF)
    shp = (NSPLIT, FPAD, D_ATT)
    hd = lax.broadcasted_iota(I32, shp, 1)
    target = (hd // 2) * LANES + jnp.where(hd % 2 == 0, ATT_HD, 0) + lax.broadcasted_iota(I32, shp, 0)
    place = ((lax.broadcasted_iota(I32, shp, 2) == target) & (hd < ATT_HEADS)).astype(BF)
    body = functools.partial(_inproj_body, blocks_per_seq=seq // tm, tm=tm)
    return pl.pallas_call(
        body, grid=(T // tm,),
        in_specs=[pl.BlockSpec((tm, D_MODEL), lambda i: (i, 0)),
                  pl.BlockSpec((1, D_MODEL), lambda i: (0, 0)),
                  pl.BlockSpec((D_MODEL, 2 * D_ATT), lambda i: (0, 0)),
                  pl.BlockSpec((D_ATT, D_MODEL), lambda i: (0, 0)),
                  pl.BlockSpec((D_MODEL, FPAD), lambda i: (0, 0)),
                  pl.BlockSpec((D_MODEL, N_REST), lambda i: (0, 0)),
                  pl.BlockSpec((1, FPAD), lambda i: (0, 0)),
                  pl.BlockSpec((tm, tm), lambda i: (0, 0)),
                  pl.BlockSpec(shp, lambda i: (0, 0, 0))],
        out_specs=[pl.BlockSpec((tm, 3 * D_ATT), lambda i: (i, 0)),
                   pl.BlockSpec((D_ATT, tm), lambda i: (0, i)),
                   pl.BlockSpec((tm, N_REST), lambda i: (i, 0))],
        out_shape=[jax.ShapeDtypeStruct((T, 3 * D_ATT), BF),
                   jax.ShapeDtypeStruct((D_ATT, T), BF),
                   jax.ShapeDtypeStruct((T, N_REST), F32)],
        scratch_shapes=[pltpu.VMEM((1, FPAD), F32)],
        compiler_params=_cp(("arbitrary",)), name="in_proj",
    )(x, g, p["wqk"], p["wvt"], p["wf"], p["wrest"], p["bfg"], tri, place)


def _attn_body(q_ref, k_ref, kb_ref, vt_ref, o_ref, m_ref, acc_ref, *, tq, tk):
    qi = pl.program_id(2)
    ki = pl.program_id(3)

    @pl.when(ki == 0)
    def _():
        m_ref[...] = jnp.full_like(m_ref, NEG)
        acc_ref[...] = jnp.zeros_like(acc_ref)

    lane = lax.broadcasted_iota(I32, (1, LANES), 1)
    first = lane < ATT_HD
    vrow = lax.broadcasted_iota(I32, (LANES, 1), 0) < ATT_HD

    def step(masked):
        q = q_ref[...]
        k = k_ref[...]
        kb = kb_ref[...]
        vt = vt_ref[...]
        if masked:
            keep = (lax.broadcasted_iota(I32, (tk, tq), 0) <= lax.broadcasted_iota(I32, (tk, tq), 1))
        for hh in range(2):
            own = first if hh == 0 else jnp.logical_not(first)
            ones = (lane >= bias_lane(hh, 0)) & (lane < bias_lane(hh, NSPLIT))
            qa = jnp.where(own, q, jnp.where(ones, 1.0, 0.0).astype(BF))
            ka = jnp.where(own, k, kb)
            st = _nt(ka, qa)
            if masked:
                st = jnp.where(keep, st, NEG)
            m_prev = m_ref[hh]
            m_new = jnp.maximum(m_prev, jnp.max(st, axis=0, keepdims=True))
            alpha = jnp.exp2(m_prev - m_new)
            p = jnp.exp2(st - m_new).astype(BF)
            m_ref[hh] = m_new
            vown = vrow if hh == 0 else jnp.logical_not(vrow)
            va = jnp.where(vown, vt, jnp.ones_like(vt))
            acc_ref[hh] = alpha * acc_ref[hh] + _dot(va, p)

    @pl.when(ki < qi)
    def _():
        step(False)

    @pl.when(ki == qi)
    def _():
        step(True)
        a0 = acc_ref[0]
        a1 = acc_ref[1]
        ot = jnp.where(vrow, a0 / a0[ATT_HD:ATT_HD + 1, :], a1 / a1[0:1, :])
        o_ref[...] = ot.T.astype(o_ref.dtype)


def attention(qkb, vt, batch, seq, tq=512):
    T = qkb.shape[0]
    tk = tq
    nq = seq // tq
    npair = ATT_HEADS // 2
    body = functools.partial(_attn_body, tq=tq, tk=tk)
    kblk = lambda b, qi, ki: b * nq + jnp.minimum(ki, qi)
    return pl.pallas_call(
        body, grid=(batch, npair, nq, nq),
        in_specs=[pl.BlockSpec((tq, LANES), lambda b, p, qi, ki: (b * nq + qi, p)),
                  pl.BlockSpec((tk, LANES), lambda b, p, qi, ki: (kblk(b, qi, ki), npair + p)),
                  pl.BlockSpec((tk, LANES), lambda b, p, qi, ki: (kblk(b, qi, ki), 2 * npair + p)),
                  pl.BlockSpec((LANES, tk), lambda b, p, qi, ki: (p, kblk(b, qi, ki)))],
        out_specs=pl.BlockSpec((tq, LANES), lambda b, p, qi, ki: (b * nq + qi, p)),
        out_shape=jax.ShapeDtypeStruct((T, D_ATT), BF),
        scratch_shapes=[pltpu.VMEM((2, 1, tq), F32), pltpu.VMEM((2, LANES, tq), F32)],
        compiler_params=_cp(("parallel", "parallel", "parallel", "arbitrary")), name="fox_attention",
    )(qkb, qkb, qkb, vt)


CONV_HALO = 32
RG_HALO = 8


def _mix_body(rest_ref, cw_ref, cb_ref, lg_ref, lb_ref, rw_ref, rb_ref, wr_ref, br_ref, wi_ref, bi_ref, lam_ref,
              o_ref, ybuf, xbuf, hc, *, ts):
    si = pl.program_id(1)

    @pl.when(si == 0)
    def _():
        ybuf[0:CONV_HALO, :] = jnp.zeros((CONV_HALO, D_CONV), F32)
        xbuf[0:RG_HALO, :] = jnp.zeros((RG_HALO, D_RNN), F32)
        hc[...] = jnp.zeros_like(hc)

    y = rest_ref[:, 0:D_CONV] * _sigmoid(rest_ref[:, D_CONV:2 * D_CONV])
    ybuf[CONV_HALO:CONV_HALO + ts, :] = y
    acc = jnp.zeros((ts, D_CONV), F32)
    for k in range(CONV_K):
        acc = acc + cw_ref[k:k + 1, :] * ybuf[pl.ds(CONV_HALO - (CONV_K - 1) + k, ts), :]
    yc = acc + cb_ref[...]
    mu = jnp.mean(yc, axis=-1, keepdims=True)
    var = jnp.mean(jnp.square(yc - mu), axis=-1, keepdims=True)
    yn = (yc - mu) * lax.rsqrt(var + EPS) * lg_ref[...] + lb_ref[...]
    o_ref[:, 0:D_CONV] = (yn * _sigmoid(yn)).astype(o_ref.dtype)
    ybuf[0:CONV_HALO, :] = ybuf[ts:ts + CONV_HALO, :]

    xbuf[RG_HALO:RG_HALO + ts, :] = rest_ref[:, 2 * D_CONV:2 * D_CONV + D_RNN]
    xc = jnp.zeros((ts, D_RNN), F32)
    for k in range(RNN_CONV_K):
        xc = xc + rw_ref[k:k + 1, :] * xbuf[pl.ds(RG_HALO - (RNN_CONV_K - 1) + k, ts), :]
    xc = xc + rb_ref[...]
    xbuf[0:RG_HALO, :] = xbuf[ts:ts + RG_HALO, :]
    xcb = xc.astype(BF)
    r = _sigmoid(_dot(xcb, wr_ref[...]) + br_ref[...])
    gi = _sigmoid(_dot(xcb, wi_ref[...]) + bi_ref[...])
    nl = -lam_ref[...]
    sp = jnp.maximum(nl, 0.0) + jnp.log(1.0 + jnp.exp(-jnp.abs(nl)))
    log_a = -RG_C * r * sp
    a = jnp.exp(log_a)
    bt = jnp.sqrt(1.0 - jnp.exp(2.0 * log_a)) * (gi * xc)
    row = lax.broadcasted_iota(I32, (ts, 1), 0)
    sh = 1
    while sh < ts:
        live = row >= sh
        a_s = jnp.where(live, pltpu.roll(a, sh, 0), 1.0)
        b_s = jnp.where(live, pltpu.roll(bt, sh, 0), 0.0)
        bt = bt + a * b_s
        a = a * a_s
        sh *= 2
    h = bt + a * hc[...]
    hc[...] = h[ts - 1:ts, :]
    gate_in = rest_ref[:, 2 * D_CONV + D_RNN:2 * D_CONV + 2 * D_RNN]
    o_ref[:, D_CONV:D_CONV + D_RNN] = (h * _gelu(gate_in)).astype(o_ref.dtype)


def mixers(rest, p, batch, seq, ts=512):
    T = rest.shape[0]
    ns = seq // ts
    body = functools.partial(_mix_body, ts=ts)
    vec = lambda: pl.BlockSpec((1, D_CONV), lambda b, s: (0, 0))
    return pl.pallas_call(
        body, grid=(batch, ns),
        in_specs=[pl.BlockSpec((ts, N_REST), lambda b, s: (b * ns + s, 0)),
                  pl.BlockSpec((32, D_CONV), lambda b, s: (0, 0)), vec(), vec(), vec(),
                  pl.BlockSpec((8, D_RNN), lambda b, s: (0, 0)), vec(),
                  pl.BlockSpec((D_RNN, D_RNN), lambda b, s: (0, 0)), vec(),
                  pl.BlockSpec((D_RNN, D_RNN), lambda b, s: (0, 0)), vec(), vec()],
        out_specs=pl.BlockSpec((ts, D_CONV + D_RNN), lambda b, s: (b * ns + s, 0)),
        out_shape=jax.ShapeDtypeStruct((T, D_CONV + D_RNN), BF),
        scratch_shapes=[pltpu.VMEM((ts + CONV_HALO, D_CONV), F32), pltpu.VMEM((ts + RG_HALO, D_RNN), F32),
                        pltpu.VMEM((1, D_RNN), F32)],
        compiler_params=_cp(("arbitrary", "arbitrary")), name="conv_rglru",
    )(rest, p["cw"], p["cb"], p["lg"], p["lb"], p["rw"], p["rb"], p["wr"], p["br"], p["wi"], p["bi"], p["lam"])


def _outproj_body(x_ref, ya_ref, yc_ref, woa_ref, wob_ref, g2_ref, wq_ref, keys_ref, x1_ref, h2_ref, st_ref):
    x1 = x_ref[...] + _dot(ya_ref[...], woa_ref[...]) + _dot(yc_ref[...], wob_ref[...])
    x1_ref[...] = x1
    h2 = x1 * lax.rsqrt(jnp.mean(x1 * x1, axis=-1, keepdims=True) + EPS) * g2_ref[...]
    h2_ref[...] = h2
    q = _dot(h2.astype(BF), wq_ref[...]).astype(BF)
    for g in range(2 * PEER_HEADS):
        st_ref[g * N_KEYS:(g + 1) * N_KEYS, :] = _nt(keys_ref[g], q[:, g * D_HALF:(g + 1) * D_HALF])


def out_proj(x, ya, yc, p, g2, tm=256):
    T = x.shape[0]
    ng = 2 * PEER_HEADS
    return pl.pallas_call(
        _outproj_body, grid=(T // tm,),
        in_specs=[pl.BlockSpec((tm, D_MODEL), lambda i: (i, 0)),
                  pl.BlockSpec((tm, D_ATT), lambda i: (i, 0)),
                  pl.BlockSpec((tm, D_CONV + D_RNN), lambda i: (i, 0)),
                  pl.BlockSpec((D_ATT, D_MODEL), lambda i: (0, 0)),
                  pl.BlockSpec((D_CONV + D_RNN, D_MODEL), lambda i: (0, 0)),
                  pl.BlockSpec((1, D_MODEL), lambda i: (0, 0)),
                  pl.BlockSpec((D_MODEL, ng * D_HALF), lambda i: (0, 0)),
                  pl.BlockSpec((ng, N_KEYS, D_HALF), lambda i: (0, 0, 0))],
        out_specs=[pl.BlockSpec((tm, D_MODEL), lambda i: (i, 0)),
                   pl.BlockSpec((tm, D_MODEL), lambda i: (i, 0)),
                   pl.BlockSpec((ng * N_KEYS, tm), lambda i: (0, i))],
        out_shape=[jax.ShapeDtypeStruct((T, D_MODEL), F32),
                   jax.ShapeDtypeStruct((T, D_MODEL), F32),
                   jax.ShapeDtypeStruct((ng * N_KEYS, T), F32)],
        compiler_params=_cp(("parallel",)), name="out_proj_peer_scores",
    )(x, ya, yc, p["woa"], p["wob"], g2, p["wq"], p["keys"])


BIG_ID = 1 << 20
SUBL = 8
NPAR = 4
SEL_CHAIN = 4


def _take_rounds(problems, nrounds):
    state = [list(slabs) for slabs, _ in problems]
    res = [([], []) for _ in problems]
    for _ in range(nrounds):
        for pi, (_, ids) in enumerate(problems):
            slabs = state[pi]
            m8 = slabs[0]
            for sl in slabs[1:]:
                m8 = jnp.maximum(m8, sl)
            m = jnp.max(m8, axis=0, keepdims=True)
            chains = []
            for c0 in range(0, len(slabs), SEL_CHAIN):
                v = jnp.full((SUBL, LANES), BIG_ID, I32)
                for sl, idc in zip(reversed(slabs[c0:c0 + SEL_CHAIN]), reversed(ids[c0:c0 + SEL_CHAIN])):
                    v = jnp.where(sl == m, idc, v)
                chains.append(v)
            while len(chains) > 1:
                chains = [jnp.minimum(chains[i], chains[i + 1]) if i + 1 < len(chains) else chains[i]
                          for i in range(0, len(chains), 2)]
            pick = jnp.min(chains[0], axis=0, keepdims=True)
            state[pi] = [jnp.where(idc == pick, NEG, sl) for sl, idc in zip(slabs, ids)]
            res[pi][0].append(m)
            res[pi][1].append(pick)
    return res


def _route_body(st_ref, e_ref, g_ref, v_scr, i_scr, sv_scr, ci_scr, et_scr, gt_scr):
    ng = 2 * PEER_HEADS
    sub = lax.broadcasted_iota(I32, (SUBL, LANES), 0)
    key_ids = [sub + SUBL * i for i in range(N_KEYS // SUBL)]

    def stage1(gg, carry):
        probs = []
        for q in range(NPAR):
            base = pl.multiple_of((gg * NPAR + q) * N_KEYS, N_KEYS)
            probs.append(([st_ref[pl.ds(base + SUBL * i, SUBL), :] for i in range(N_KEYS // SUBL)], key_ids))
        for q, (vals, picks) in enumerate(_take_rounds(probs, TOPK)):
            for r in range(TOPK):
                v_scr[gg * NPAR + q, r:r + 1, :] = vals[r]
                i_scr[gg * NPAR + q, r:r + 1, :] = picks[r]
        return carry

    lax.fori_loop(0, ng // NPAR, stage1, 0)

    def stage2(hh, carry):
        probs = []
        for q in range(NPAR):
            h = hh * NPAR + q
            v1 = v_scr[2 * h]
            v2 = v_scr[2 * h + 1]
            slabs = [v1[0:1, :] + v2[0:SUBL, :], v1[0:1, :] + v2[SUBL:TOPK, :]]
            ids = [sub, sub + SUBL]
            for i in range(1, TOPK):
                nj = TOPK // (i + 1)
                slabs.append(jnp.where(sub < nj, v1[i:i + 1, :] + v2[0:SUBL, :], NEG))
                ids.append(sub + i * TOPK)
            probs.append((slabs, ids))
        for q, (vals, picks) in enumerate(_take_rounds(probs, TOPK)):
            h = hh * NPAR + q
            i1 = i_scr[2 * h]
            i2 = i_scr[2 * h + 1]
            for r in range(TOPK):
                sv_scr[q, r:r + 1, :] = vals[r]
                ci_scr[q, r:r + 1, :] = picks[r]
            sv = sv_scr[q]
            ci = ci_scr[q]
            ci_hi = lax.shift_right_logical(ci, 4)
            ci_lo = jnp.bitwise_and(ci, TOPK - 1)
            e1 = jnp.zeros((TOPK, LANES), I32)
            e2 = jnp.zeros((TOPK, LANES), I32)
            for i in range(TOPK):
                e1 = jnp.where(ci_hi == i, i1[i:i + 1, :], e1)
                e2 = jnp.where(ci_lo == i, i2[i:i + 1, :], e2)
            p = jnp.exp(sv - sv[0:1, :])
            gates = p / jnp.sum(p, axis=0, keepdims=True)
            et_scr[pl.ds(pl.multiple_of(h * TOPK, TOPK), TOPK), :] = e1 * N_KEYS + e2
            gt_scr[pl.ds(pl.multiple_of(h * TOPK, TOPK), TOPK), :] = gates
        return carry

    lax.fori_loop(0, PEER_HEADS // NPAR, stage2, 0)
    e_ref[...] = et_scr[...].T
    g_ref[...] = gt_scr[...].T


def route(st):
    T = st.shape[1]
    ng = 2 * PEER_HEADS
    return pl.pallas_call(
        _route_body, grid=(T // LANES,),
        in_specs=[pl.BlockSpec((ng * N_KEYS, LANES), lambda i: (0, i))],
        out_specs=[pl.BlockSpec((LANES, KSEL), lambda i: (i, 0)),
                   pl.BlockSpec((LANES, KSEL), lambda i: (i, 0))],
        out_shape=[jax.ShapeDtypeStruct((T, KSEL), I32),
                   jax.ShapeDtypeStruct((T, KSEL), F32)],
        scratch_shapes=[pltpu.VMEM((ng, TOPK, LANES), F32), pltpu.VMEM((ng, TOPK, LANES), I32),
                        pltpu.VMEM((NPAR, TOPK, LANES), F32), pltpu.VMEM((NPAR, TOPK, LANES), I32),
                        pltpu.VMEM((KSEL, LANES), I32), pltpu.VMEM((KSEL, LANES), F32)],
        compiler_params=_cp(("parallel",)), name="peer_route",
    )(st)


NC, NS, L = 2, 16, 16
NW = NC * NS
NJ = D_MODEL // L
R = TOPK
NCH = KSEL // R
SUB = 2
RD = SUB * R
G = 16
DW = D_MODEL // 2


def _perm(x, idx):
    return jnp.take_along_axis(x, idx, axis=0, mode="promise_in_bounds")


def pack_table(tab):
    e, d = tab.shape
    tb = lax.bitcast_convert_type(tab.astype(BF), jnp.uint16).astype(jnp.uint32).reshape(e, d // (2 * L), 2, L)
    words = tb[:, :, 0, :] | (tb[:, :, 1, :] << 16)
    return lax.bitcast_convert_type(words.reshape(e, d // 2), I32)


def _halves(w):
    lo = lax.bitcast_convert_type(lax.shift_left(w, 16), F32)
    hi = lax.bitcast_convert_type(jnp.bitwise_and(w, jnp.int32(-65536)), F32)
    return lo, hi


def _bf(w):
    return plsc.bitcast(w, BF)


def peer_sc(x, resid, idx, gates, u_tab, v_tab):
    T = x.shape[0]
    tpw = T // NW
    ngroups = tpw // G
    nchunks = G * NCH // SUB
    idx3 = idx.reshape(T * NCH // SUB, RD)
    g3 = gates.reshape(T * NCH, R)
    mesh = plsc.VectorSubcoreMesh(core_axis_name="c", subcore_axis_name="s")

    @functools.partial(
        pl.kernel, mesh=mesh,
        out_type=jax.ShapeDtypeStruct((T, D_MODEL), F32),
        scratch_types=[
            pltpu.VMEM((G, DW), I32),
            pltpu.VMEM((G, D_MODEL), F32),
            pltpu.VMEM((nchunks, RD), I32),
            pltpu.VMEM((G * NCH, R), F32),
            pltpu.VMEM((RD, DW), I32),
            pltpu.VMEM((RD, DW), I32),
            pltpu.VMEM((RD, DW), I32),
            pltpu.VMEM((RD, DW), I32),
            pltpu.SemaphoreType.DMA,
            pltpu.SemaphoreType.DMA,
            pltpu.SemaphoreType.DMA,
            pltpu.SemaphoreType.DMA,
        ],
        compiler_params=pltpu.CompilerParams(needs_layout_passes=False),
        name="peer_experts_sc",
    )
    def k(x_hbm, r_hbm, idx_hbm, g_hbm, u_hbm, v_hbm, out_hbm,
          x_v, out_v, idx_v, g_v, ub0, ub1, vb0, vb1, su0, su1, sv0, sv1):
        wid = lax.axis_index("s") * NC + lax.axis_index("c")
        ubs, vbs, sus, svs = (ub0, ub1), (vb0, vb1), (su0, su1), (sv0, sv1)
        iota = lax.iota(I32, L)

        def gather_copies(c, b):
            return (pltpu.make_async_copy(u_hbm.at[idx_v.at[c]], ubs[b], sus[b]),
                    pltpu.make_async_copy(v_hbm.at[idx_v.at[c]], vbs[b], svs[b]))

        def issue(c, b):
            for cp in gather_copies(c, b):
                cp.start()

        def wait(c, b):
            for cp in gather_copies(c, b):
                cp.wait()

        def compute(c, b, ro):
            ub, vb = ubs[b], vbs[b]
            t = c // NCH

            def ubody(mm, accs):
                x0 = _bf(x_v[t, pl.ds(mm * 2 * L, L)])
                x1 = _bf(x_v[t, pl.ds(mm * 2 * L + L, L)])
                out = []
                for kk in range(R):
                    pr = x0 * _bf(ub[ro + kk, pl.ds(mm * 2 * L, L)]) + x1 * _bf(ub[ro + kk, pl.ds(mm * 2 * L + L, L)])
                    lo, hi = _halves(plsc.bitcast(pr, I32))
                    out.append(accs[kk] + (lo + hi))
                return tuple(out)

            accs = lax.fori_loop(0, NJ // 4, ubody, tuple(jnp.zeros((L,), F32) for _ in range(R)))
            vecs = list(accs)
            dist = L // 2
            while dist >= 1:
                pidx = jnp.bitwise_xor(iota, dist)
                low = jnp.bitwise_and(iota, dist) == 0
                nxt = []
                for kk in range(dist):
                    a = vecs[kk]
                    bvec = vecs[kk + dist]
                    a = a + _perm(a, pidx)
                    bvec = bvec + _perm(bvec, pidx)
                    nxt.append(jnp.where(low, a, bvec))
                vecs = nxt
                dist //= 2
            hid = vecs[0]
            z = GC * (hid + 0.044715 * hid * hid * hid)
            gel = hid / (1.0 + jnp.exp(-2.0 * z))
            w = g_v[c, :] * gel
            wbs = []
            for kk in range(R):
                wb = _perm(w, jnp.full((L,), kk, I32))
                wbs.append(plsc.pack(wb, wb, format=plsc.PackFormat.INTERLEAVED))

            @plsc.parallel_loop(0, NJ // 2)
            def _(m):
                pr = [wbs[kk] * _bf(vb[ro + kk, pl.ds(m * L, L)]) for kk in range(R)]
                for _lvl in range(2):
                    pr = [pr[i] + pr[i + 1] for i in range(0, len(pr), 2)]
                los, his = [], []
                for q in pr:
                    lo, hi = _halves(plsc.bitcast(q, I32))
                    los.append(lo)
                    his.append(hi)
                while len(los) > 1:
                    los = [los[i] + los[i + 1] for i in range(0, len(los), 2)]
                    his = [his[i] + his[i + 1] for i in range(0, len(his), 2)]
                out_v[t, pl.ds(m * 2 * L, L)] = out_v[t, pl.ds(m * 2 * L, L)] + los[0]
                out_v[t, pl.ds(m * 2 * L + L, L)] = out_v[t, pl.ds(m * 2 * L + L, L)] + his[0]

        def group(g, carry):
            tok0 = wid * tpw + g * G
            pltpu.sync_copy(x_hbm.at[pl.ds(tok0, G)], x_v)
            pltpu.sync_copy(r_hbm.at[pl.ds(tok0, G)], out_v)
            pltpu.sync_copy(idx_hbm.at[pl.ds(tok0 * (NCH // SUB), nchunks)], idx_v)
            pltpu.sync_copy(g_hbm.at[pl.ds(tok0 * NCH, G * NCH)], g_v)
            issue(0, 0)

            def cbody(cc, c2):
                c = cc * 2
                issue(c + 1, 1)
                wait(c, 0)
                for sub in range(SUB):
                    compute(c * SUB + sub, 0, sub * R)

                @pl.when(c + 2 < nchunks)
                def _():
                    issue(c + 2, 0)

                wait(c + 1, 1)
                for sub in range(SUB):
                    compute((c + 1) * SUB + sub, 1, sub * R)
                return c2

            lax.fori_loop(0, nchunks // 2, cbody, 0)
            pltpu.sync_copy(out_v, out_hbm.at[pl.ds(tok0, G)])
            return carry

        lax.fori_loop(0, ngroups, group, 0)

    return k(x, resid, idx3, g3, u_tab, v_tab)


def _fn_body(x_ref, g_ref, o_ref):
    xf = x_ref[...]
    o_ref[...] = xf * lax.rsqrt(jnp.mean(xf * xf, axis=-1, keepdims=True) + EPS) * g_ref[...]


def final_norm(x, g, tm=1024):
    T, d = x.shape
    return pl.pallas_call(
        _fn_body, grid=(T // tm,),
        in_specs=[pl.BlockSpec((tm, d), lambda i: (i, 0)), pl.BlockSpec((1, d), lambda i: (0, 0))],
        out_specs=pl.BlockSpec((tm, d), lambda i: (i, 0)),
        out_shape=jax.ShapeDtypeStruct((T, d), F32),
        compiler_params=_cp(("parallel",)), name="final_norm",
    )(x, g)


def _prep_layer(w_in, b_forget, conv_dw_w, conv_dw_b, conv_ln_g, conv_ln_b, rg_conv_w, rg_conv_b,
                rg_w_r, rg_b_r, rg_w_i, rg_b_i, rg_lambda, w_out, peer_wq, peer_k1, peer_k2):
    f0 = 3 * D_ATT
    wf = jnp.zeros((D_MODEL, FPAD), BF).at[:, 0:ATT_HEADS].set(w_in[:, f0:f0 + ATT_HEADS].astype(BF))
    bfg = jnp.zeros((1, FPAD), F32).at[0, 0:ATT_HEADS].set(b_forget)
    cw = jnp.zeros((32, D_CONV), F32).at[0:CONV_K].set(conv_dw_w)
    rw = jnp.zeros((8, D_RNN), F32).at[0:RNN_CONV_K].set(rg_conv_w)
    bd = lambda w: jax.scipy.linalg.block_diag(*[w[i] for i in range(RNN_BLOCKS)]).astype(BF)
    row = lambda v: v.reshape(1, -1).astype(F32)
    keys = jnp.stack([peer_k1, peer_k2], axis=1).reshape(2 * PEER_HEADS, N_KEYS, D_HALF).astype(BF)
    return dict(wqk=w_in[:, 0:2 * D_ATT].astype(BF), wvt=w_in[:, 2 * D_ATT:f0].T.astype(BF), wf=wf,
                wrest=w_in[:, f0 + ATT_HEADS:].astype(BF), bfg=bfg,
                cw=cw, cb=row(conv_dw_b), lg=row(conv_ln_g), lb=row(conv_ln_b),
                rw=rw, rb=row(rg_conv_b), wr=bd(rg_w_r), br=row(rg_b_r), wi=bd(rg_w_i), bi=row(rg_b_i),
                lam=row(rg_lambda), woa=w_out[0:D_ATT].astype(BF), wob=w_out[D_ATT:].astype(BF),
                wq=peer_wq.astype(BF), keys=keys)


def kernel(x, norm1_g, w_in, b_forget, conv_dw_w, conv_dw_b, conv_ln_g, conv_ln_b,
           rg_conv_w, rg_conv_b, rg_w_r, rg_b_r, rg_w_i, rg_b_i, rg_lambda, w_out,
           norm2_g, peer_wq, peer_k1, peer_k2, peer_u, peer_v, final_g):
    b, s, d = x.shape
    params = [_prep_layer(w_in[l], b_forget[l], conv_dw_w[l], conv_dw_b[l], conv_ln_g[l], conv_ln_b[l],
                          rg_conv_w[l], rg_conv_b[l], rg_w_r[l], rg_b_r[l], rg_w_i[l], rg_b_i[l], rg_lambda[l],
                          w_out[l], peer_wq[l], peer_k1[l], peer_k2[l]) for l in range(DEPTH)]
    tabs = [(pack_table(peer_u[l]), pack_table(peer_v[l])) for l in range(DEPTH)]
    bs = b // N_SLICES
    xs = [x[i * bs:(i + 1) * bs].reshape(bs * s, d) for i in range(N_SLICES)]
    xs[0], tabs = lax.optimization_barrier((xs[0], tabs))
    prev = None
    for l in range(DEPTH):
        p = params[l]
        for i in range(N_SLICES):
            xt = xs[i]
            if prev is not None:
                xt, prev = lax.optimization_barrier((xt, prev))
            qkb, vt, rest = in_proj(xt, norm1_g[l].reshape(1, d), p, s)
            y_att = attention(qkb, vt, bs, s)
            y_cr = mixers(rest, p, bs, s)
            x1, h2, st = out_proj(xt, y_att, y_cr, p, norm2_g[l].reshape(1, d))
            experts, gates = route(st)
            prev = experts
            xs[i] = peer_sc(pack_table(h2), x1, experts, gates, tabs[l][0], tabs[l][1])
    outs = [final_norm(xt, final_g.reshape(1, d)).reshape(bs, s, d) for xt in xs]
    return jnp.concatenate(outs, axis=0)
```

```python
import functools
import math

import jax
import jax.numpy as jnp
from jax import lax
from jax.experimental import pallas as pl
from jax.experimental.pallas import tpu as pltpu
from jax.experimental.pallas import tpu_sc as plsc

BF = jnp.bfloat16
F32 = jnp.float32
I32 = jnp.int32

D_MODEL = 1024
DEPTH = 2
ATT_HEADS = 8
ATT_HD = 64
D_ATT = ATT_HEADS * ATT_HD
D_CONV = 256
CONV_K = 31
D_RNN = 256
RNN_BLOCKS = 4
RNN_CONV_K = 4
RG_C = 8.0
EPS = 1e-6
N_REST = 2 * D_CONV + 2 * D_RNN
PEER_HEADS = 8
N_KEYS = 128
D_HALF = 128
TOPK = 16
KSEL = PEER_HEADS * TOPK
GC = 0.7978845608028654
NEG = float("-inf")

N_SLICES = 4
LANES = 128
VMEM_LIMIT = 48 * 1024 * 1024


def _cp(sem):
    return pltpu.CompilerParams(dimension_semantics=sem, vmem_limit_bytes=VMEM_LIMIT)


def _split3(x):
    hi = x.astype(BF)
    r = x - hi.astype(F32)
    mid = r.astype(BF)
    lo = (r - mid.astype(F32)).astype(BF)
    return hi, mid, lo


def _nt(a, b):
    return lax.dot_general(a, b, (((1,), (1,)), ((), ())), preferred_element_type=F32)


def _dot(a, b):
    return jnp.dot(a, b, preferred_element_type=F32)


def _sigmoid(x):
    return 1.0 / (1.0 + jnp.exp(-x))


def _gelu(x):
    return 0.5 * x * (1.0 + jnp.tanh(GC * (x + 0.044715 * x * x * x)))


LOG2E = 1.4426950408889634
NSPLIT = 3
FPAD = 16


def _inproj_body(x_ref, g_ref, wqk_ref, wvt_ref, wf_ref, wrest_ref, bf_ref, tri_ref, place_ref,
                 qkb_ref, vt_ref, rest_ref, carry_ref, *, blocks_per_seq, tm):
    i = pl.program_id(0)
    x = x_ref[...]
    h = x * lax.rsqrt(jnp.mean(x * x, axis=-1, keepdims=True) + EPS) * g_ref[...]
    hb = h.astype(BF)
    qk = _dot(hb, wqk_ref[...])
    col = lax.broadcasted_iota(I32, (1, 2 * D_ATT), 1)
    qk = jnp.where(col < D_ATT, qk * (LOG2E / math.sqrt(ATT_HD)), qk)
    qkb_ref[:, 0:2 * D_ATT] = qk.astype(BF)
    vt_ref[...] = _nt(wvt_ref[...], hb).astype(BF)
    rest_ref[...] = _dot(hb, wrest_ref[...])
    ft = _dot(hb, wf_ref[...]) + bf_ref[...]
    lf = jnp.minimum(ft, 0.0) - jnp.log(1.0 + jnp.exp(-jnp.abs(ft)))
    hi, mid, lo = _split3(lf)
    tri = tri_ref[...]
    cs = _dot(tri, hi) + _dot(tri, mid) + _dot(tri, lo)

    @pl.when(i % blocks_per_seq == 0)
    def _():
        carry_ref[...] = jnp.zeros_like(carry_ref)

    cum = cs + carry_ref[...]
    carry_ref[...] = cum[tm - 1:tm, :]
    pieces = _split3(cum * (-LOG2E))
    kb = _dot(pieces[0], place_ref[0]) + _dot(pieces[1], place_ref[1]) + _dot(pieces[2], place_ref[2])
    qkb_ref[:, 2 * D_ATT:3 * D_ATT] = kb.astype(BF)


def bias_lane(hh, j):
    return (ATT_HD if hh == 0 else 0) + j


def in_proj(x, g, p, seq, tm=512):
    T = x.shape[0]
    tri = (lax.broadcasted_iota(I32, (tm, tm), 0) >= lax.broadcasted_iota(I32, (tm, tm), 1)).astype(BF)
    shp = (NSPLIT, FPAD, D_ATT)
    hd = lax.broadcasted_iota(I32, shp, 1)
    target = (hd // 2) * LANES + jnp.where(hd % 2 == 0, ATT_HD, 0) + lax.broadcasted_iota(I32, shp, 0)
    place = ((lax.broadcasted_iota(I32, shp, 2) == target) & (hd < ATT_HEADS)).astype(BF)
    body = functools.partial(_inproj_body, blocks_per_seq=seq // tm, tm=tm)
    return pl.pallas_call(
        body, grid=(T // tm,),
        in_specs=[pl.BlockSpec((tm, D_MODEL), lambda i: (i, 0)),
                  pl.BlockSpec((1, D_MODEL), lambda i: (0, 0)),
                  pl.BlockSpec((D_MODEL, 2 * D_ATT), lambda i: (0, 0)),
                  pl.BlockSpec((D_ATT, D_MODEL), lambda i: (0, 0)),
                  pl.BlockSpec((D_MODEL, FPAD), lambda i: (0, 0)),
                  pl.BlockSpec((D_MODEL, N_REST), lambda i: (0, 0)),
                  pl.BlockSpec((1, FPAD), lambda i: (0, 0)),
                  pl.BlockSpec((tm, tm), lambda i: (0, 0)),
                  pl.BlockSpec(shp, lambda i: (0, 0, 0))],
        out_specs=[pl.BlockSpec((tm, 3 * D_ATT), lambda i: (i, 0)),
                   pl.BlockSpec((D_ATT, tm), lambda i: (0, i)),
                   pl.BlockSpec((tm, N_REST), lambda i: (i, 0))],
        out_shape=[jax.ShapeDtypeStruct((T, 3 * D_ATT), BF),
                   jax.ShapeDtypeStruct((D_ATT, T), BF),
                   jax.ShapeDtypeStruct((T, N_REST), F32)],
        scratch_shapes=[pltpu.VMEM((1, FPAD), F32)],
        compiler_params=_cp(("arbitrary",)), name="in_proj",
    )(x, g, p["wqk"], p["wvt"], p["wf"], p["wrest"], p["bfg"], tri, place)


def _attn_body(q_ref, k_ref, kb_ref, vt_ref, o_ref, m_ref, acc_ref, *, tq, tk):
    qi = pl.program_id(2)
    ki = pl.program_id(3)

    @pl.when(ki == 0)
    def _():
        m_ref[...] = jnp.full_like(m_ref, NEG)
        acc_ref[...] = jnp.zeros_like(acc_ref)

    lane = lax.broadcasted_iota(I32, (1, LANES), 1)
    first = lane < ATT_HD
    vrow = lax.broadcasted_iota(I32, (LANES, 1), 0) < ATT_HD

    def step(masked):
        q = q_ref[...]
        k = k_ref[...]
        kb = kb_ref[...]
        vt = vt_ref[...]
        if masked:
            keep = (lax.broadcasted_iota(I32, (tk, tq), 0) <= lax.broadcasted_iota(I32, (tk, tq), 1))
        for hh in range(2):
            own = first if hh == 0 else jnp.logical_not(first)
            ones = (lane >= bias_lane(hh, 0)) & (lane < bias_lane(hh, NSPLIT))
            qa = jnp.where(own, q, jnp.where(ones, 1.0, 0.0).astype(BF))
            ka = jnp.where(own, k, kb)
            st = _nt(ka, qa)
            if masked:
                st = jnp.where(keep, st, NEG)
            m_prev = m_ref[hh]
            m_new = jnp.maximum(m_prev, jnp.max(st, axis=0, keepdims=True))
            alpha = jnp.exp2(m_prev - m_new)
            p = jnp.exp2(st - m_new).astype(BF)
            m_ref[hh] = m_new
            vown = vrow if hh == 0 else jnp.logical_not(vrow)
            va = jnp.where(vown, vt, jnp.ones_like(vt))
            acc_ref[hh] = alpha * acc_ref[hh] + _dot(va, p)

    @pl.when(ki < qi)
    def _():
        step(False)

    @pl.when(ki == qi)
    def _():
        step(True)
        a0 = acc_ref[0]
        a1 = acc_ref[1]
        ot = jnp.where(vrow, a0 / a0[ATT_HD:ATT_HD + 1, :], a1 / a1[0:1, :])
        o_ref[...] = ot.T.astype(o_ref.dtype)


def attention(qkb, vt, batch, seq, tq=512):
    T = qkb.shape[0]
    tk = tq
    nq = seq // tq
    npair = ATT_HEADS // 2
    body = functools.partial(_attn_body, tq=tq, tk=tk)
    kblk = lambda b, qi, ki: b * nq + jnp.minimum(ki, qi)
    return pl.pallas_call(
        body, grid=(batch, npair, nq, nq),
        in_specs=[pl.BlockSpec((tq, LANES), lambda b, p, qi, ki: (b * nq + qi, p)),
                  pl.BlockSpec((tk, LANES), lambda b, p, qi, ki: (kblk(b, qi, ki), npair + p)),
                  pl.BlockSpec((tk, LANES), lambda b, p, qi, ki: (kblk(b, qi, ki), 2 * npair + p)),
                  pl.BlockSpec((LANES, tk), lambda b, p, qi, ki: (p, kblk(b, qi, ki)))],
        out_specs=pl.BlockSpec((tq, LANES), lambda b, p, qi, ki: (b * nq + qi, p)),
        out_shape=jax.ShapeDtypeStruct((T, D_ATT), BF),
        scratch_shapes=[pltpu.VMEM((2, 1, tq), F32), pltpu.VMEM((2, LANES, tq), F32)],
        compiler_params=_cp(("parallel", "parallel", "parallel", "arbitrary")), name="fox_attention",
    )(qkb, qkb, qkb, vt)


CONV_HALO = 32
RG_HALO = 8


def _mix_body(rest_ref, cw_ref, cb_ref, lg_ref, lb_ref, rw_ref, rb_ref, wr_ref, br_ref, wi_ref, bi_ref, lam_ref,
              o_ref, ybuf, xbuf, hc, *, ts):
    si = pl.program_id(1)

    @pl.when(si == 0)
    def _():
        ybuf[0:CONV_HALO, :] = jnp.zeros((CONV_HALO, D_CONV), F32)
        xbuf[0:RG_HALO, :] = jnp.zeros((RG_HALO, D_RNN), F32)
        hc[...] = jnp.zeros_like(hc)

    y = rest_ref[:, 0:D_CONV] * _sigmoid(rest_ref[:, D_CONV:2 * D_CONV])
    ybuf[CONV_HALO:CONV_HALO + ts, :] = y
    acc = jnp.zeros((ts, D_CONV), F32)
    for k in range(CONV_K):
        acc = acc + cw_ref[k:k + 1, :] * ybuf[pl.ds(CONV_HALO - (CONV_K - 1) + k, ts), :]
    yc = acc + cb_ref[...]
    mu = jnp.mean(yc, axis=-1, keepdims=True)
    var = jnp.mean(jnp.square(yc - mu), axis=-1, keepdims=True)
    yn = (yc - mu) * lax.rsqrt(var + EPS) * lg_ref[...] + lb_ref[...]
    o_ref[:, 0:D_CONV] = (yn * _sigmoid(yn)).astype(o_ref.dtype)
    ybuf[0:CONV_HALO, :] = ybuf[ts:ts + CONV_HALO, :]

    xbuf[RG_HALO:RG_HALO + ts, :] = rest_ref[:, 2 * D_CONV:2 * D_CONV + D_RNN]
    xc = jnp.zeros((ts, D_RNN), F32)
    for k in range(RNN_CONV_K):
        xc = xc + rw_ref[k:k + 1, :] * xbuf[pl.ds(RG_HALO - (RNN_CONV_K - 1) + k, ts), :]
    xc = xc + rb_ref[...]
    xbuf[0:RG_HALO, :] = xbuf[ts:ts + RG_HALO, :]
    xcb = xc.astype(BF)
    r = _sigmoid(_dot(xcb, wr_ref[...]) + br_ref[...])
    gi = _sigmoid(_dot(xcb, wi_ref[...]) + bi_ref[...])
    nl = -lam_ref[...]
    sp = jnp.maximum(nl, 0.0) + jnp.log(1.0 + jnp.exp(-jnp.abs(nl)))
    log_a = -RG_C * r * sp
    a = jnp.exp(log_a)
    bt = jnp.sqrt(1.0 - jnp.exp(2.0 * log_a)) * (gi * xc)
    row = lax.broadcasted_iota(I32, (ts, 1), 0)
    sh = 1
    while sh < ts:
        live = row >= sh
        a_s = jnp.where(live, pltpu.roll(a, sh, 0), 1.0)
        b_s = jnp.where(live, pltpu.roll(bt, sh, 0), 0.0)
        bt = bt + a * b_s
        a = a * a_s
        sh *= 2
    h = bt + a * hc[...]
    hc[...] = h[ts - 1:ts, :]
    gate_in = rest_ref[:, 2 * D_CONV + D_RNN:2 * D_CONV + 2 * D_RNN]
    o_ref[:, D_CONV:D_CONV + D_RNN] = (h * _gelu(gate_in)).astype(o_ref.dtype)


def mixers(rest, p, batch, seq, ts=512):
    T = rest.shape[0]
    ns = seq // ts
    body = functools.partial(_mix_body, ts=ts)
    vec = lambda: pl.BlockSpec((1, D_CONV), lambda b, s: (0, 0))
    return pl.pallas_call(
        body, grid=(batch, ns),
        in_specs=[pl.BlockSpec((ts, N_REST), lambda b, s: (b * ns + s, 0)),
                  pl.BlockSpec((32, D_CONV), lambda b, s: (0, 0)), vec(), vec(), vec(),
                  pl.BlockSpec((8, D_RNN), lambda b, s: (0, 0)), vec(),
                  pl.BlockSpec((D_RNN, D_RNN), lambda b, s: (0, 0)), vec(),
                  pl.BlockSpec((D_RNN, D_RNN), lambda b, s: (0, 0)), vec(), vec()],
        out_specs=pl.BlockSpec((ts, D_CONV + D_RNN), lambda b, s: (b * ns + s, 0)),
        out_shape=jax.ShapeDtypeStruct((T, D_CONV + D_RNN), BF),
        scratch_shapes=[pltpu.VMEM((ts + CONV_HALO, D_CONV), F32), pltpu.VMEM((ts + RG_HALO, D_RNN), F32),
                        pltpu.VMEM((1, D_RNN), F32)],
        compiler_params=_cp(("arbitrary", "arbitrary")), name="conv_rglru",
    )(rest, p["cw"], p["cb"], p["lg"], p["lb"], p["rw"], p["rb"], p["wr"], p["br"], p["wi"], p["bi"], p["lam"])


def pack_pairs(a):
    half = a.shape[1] // 2
    r = lax.bitcast_convert_type(a, I32)
    r = r + jnp.int32(0x7FFF) + jnp.bitwise_and(lax.shift_right_logical(r, 16), 1)
    lo = lax.shift_right_logical(r[:, :half], 16)
    hi = jnp.bitwise_and(r[:, half:], jnp.int32(-65536))
    return jnp.bitwise_or(hi, lo)


def _outproj_body(x_ref, ya_ref, yc_ref, woa_ref, wob_ref, g2_ref, wq_ref, keys_ref, x1_ref, h2p_ref, st_ref):
    x1 = x_ref[...] + _dot(ya_ref[...], woa_ref[...]) + _dot(yc_ref[...], wob_ref[...])
    x1_ref[...] = x1
    h2 = x1 * lax.rsqrt(jnp.mean(x1 * x1, axis=-1, keepdims=True) + EPS) * g2_ref[...]
    h2p_ref[...] = pack_pairs(h2)
    q = _dot(h2.astype(BF), wq_ref[...]).astype(BF)
    for g in range(2 * PEER_HEADS):
        st_ref[g * N_KEYS:(g + 1) * N_KEYS, :] = _nt(keys_ref[g], q[:, g * D_HALF:(g + 1) * D_HALF])


def out_proj(x, ya, yc, p, g2, tm=256):
    T = x.shape[0]
    ng = 2 * PEER_HEADS
    return pl.pallas_call(
        _outproj_body, grid=(T // tm,),
        in_specs=[pl.BlockSpec((tm, D_MODEL), lambda i: (i, 0)),
                  pl.BlockSpec((tm, D_ATT), lambda i: (i, 0)),
                  pl.BlockSpec((tm, D_CONV + D_RNN), lambda i: (i, 0)),
                  pl.BlockSpec((D_ATT, D_MODEL), lambda i: (0, 0)),
                  pl.BlockSpec((D_CONV + D_RNN, D_MODEL), lambda i: (0, 0)),
                  pl.BlockSpec((1, D_MODEL), lambda i: (0, 0)),
                  pl.BlockSpec((D_MODEL, ng * D_HALF), lambda i: (0, 0)),
                  pl.BlockSpec((ng, N_KEYS, D_HALF), lambda i: (0, 0, 0))],
        out_specs=[pl.BlockSpec((tm, D_MODEL), lambda i: (i, 0)),
                   pl.BlockSpec((tm, D_MODEL // 2), lambda i: (i, 0)),
                   pl.BlockSpec((ng * N_KEYS, tm), lambda i: (0, i))],
        out_shape=[jax.ShapeDtypeStruct((T, D_MODEL), F32),
                   jax.ShapeDtypeStruct((T, D_MODEL // 2), I32),
                   jax.ShapeDtypeStruct((ng * N_KEYS, T), F32)],
        compiler_params=_cp(("parallel",)), name="out_proj_peer_scores",
    )(x, ya, yc, p["woa"], p["wob"], g2, p["wq"], p["keys"])


BIG_ID = 1 << 20
SUBL = 8
NPAR = 4
SEL_CHAIN = 4


def _take_rounds(problems, nrounds):
    state = [list(slabs) for slabs, _ in problems]
    res = [([], []) for _ in problems]
    for _ in range(nrounds):
        for pi, (_, ids) in enumerate(problems):
            slabs = state[pi]
            m8 = slabs[0]
            for sl in slabs[1:]:
                m8 = jnp.maximum(m8, sl)
            m = jnp.max(m8, axis=0, keepdims=True)
            chains = []
            for c0 in range(0, len(slabs), SEL_CHAIN):
                v = jnp.full((SUBL, LANES), BIG_ID, I32)
                for sl, idc in zip(reversed(slabs[c0:c0 + SEL_CHAIN]), reversed(ids[c0:c0 + SEL_CHAIN])):
                    v = jnp.where(sl == m, idc, v)
                chains.append(v)
            while len(chains) > 1:
                chains = [jnp.minimum(chains[i], chains[i + 1]) if i + 1 < len(chains) else chains[i]
                          for i in range(0, len(chains), 2)]
            pick = jnp.min(chains[0], axis=0, keepdims=True)
            state[pi] = [jnp.where(idc == pick, NEG, sl) for sl, idc in zip(slabs, ids)]
            res[pi][0].append(m)
            res[pi][1].append(pick)
    return res


def _route_body(st_ref, e_ref, g_ref, v_scr, i_scr, sv_scr, ci_scr, et_scr, gt_scr):
    ng = 2 * PEER_HEADS
    sub = lax.broadcasted_iota(I32, (SUBL, LANES), 0)
    key_ids = [sub + SUBL * i for i in range(N_KEYS // SUBL)]

    def stage1(gg, carry):
        probs = []
        for q in range(NPAR):
            base = pl.multiple_of((gg * NPAR + q) * N_KEYS, N_KEYS)
            probs.append(([st_ref[pl.ds(base + SUBL * i, SUBL), :] for i in range(N_KEYS // SUBL)], key_ids))
        for q, (vals, picks) in enumerate(_take_rounds(probs, TOPK)):
            for r in range(TOPK):
                v_scr[gg * NPAR + q, r:r + 1, :] = vals[r]
                i_scr[gg * NPAR + q, r:r + 1, :] = picks[r]
        return carry

    lax.fori_loop(0, ng // NPAR, stage1, 0)

    def stage2(hh, carry):
        probs = []
        for q in range(NPAR):
            h = hh * NPAR + q
            v1 = v_scr[2 * h]
            v2 = v_scr[2 * h + 1]
            slabs = [v1[0:1, :] + v2[0:SUBL, :], v1[0:1, :] + v2[SUBL:TOPK, :]]
            ids = [sub, sub + SUBL]
            for i in range(1, TOPK):
                nj = TOPK // (i + 1)
                slabs.append(jnp.where(sub < nj, v1[i:i + 1, :] + v2[0:SUBL, :], NEG))
                ids.append(sub + i * TOPK)
            probs.append((slabs, ids))
        for q, (vals, picks) in enumerate(_take_rounds(probs, TOPK)):
            h = hh * NPAR + q
            i1 = i_scr[2 * h]
            i2 = i_scr[2 * h + 1]
            for r in range(TOPK):
                sv_scr[q, r:r + 1, :] = vals[r]
                ci_scr[q, r:r + 1, :] = picks[r]
            sv = sv_scr[q]
            ci = ci_scr[q]
            ci_hi = lax.shift_right_logical(ci, 4)
            ci_lo = jnp.bitwise_and(ci, TOPK - 1)
            e1 = jnp.zeros((TOPK, LANES), I32)
            e2 = jnp.zeros((TOPK, LANES), I32)
            for i in range(TOPK):
                e1 = jnp.where(ci_hi == i, i1[i:i + 1, :], e1)
                e2 = jnp.where(ci_lo == i, i2[i:i + 1, :], e2)
            p = jnp.exp(sv - sv[0:1, :])
            gates = p / jnp.sum(p, axis=0, keepdims=True)
            et_scr[pl.ds(pl.multiple_of(h * TOPK, TOPK), TOPK), :] = e1 * N_KEYS + e2
            gt_scr[pl.ds(pl.multiple_of(h * TOPK, TOPK), TOPK), :] = gates
        return carry

    lax.fori_loop(0, PEER_HEADS // NPAR, stage2, 0)
    e_ref[...] = et_scr[...].T
    g_ref[...] = gt_scr[...].T


def route(st):
    T = st.shape[1]
    ng = 2 * PEER_HEADS
    return pl.pallas_call(
        _route_body, grid=(T // LANES,),
        in_specs=[pl.BlockSpec((ng * N_KEYS, LANES), lambda i: (0, i))],
        out_specs=[pl.BlockSpec((LANES, KSEL), lambda i: (i, 0)),
                   pl.BlockSpec((LANES, KSEL), lambda i: (i, 0))],
        out_shape=[jax.ShapeDtypeStruct((T, KSEL), I32),
                   jax.ShapeDtypeStruct((T, KSEL), F32)],
        scratch_shapes=[pltpu.VMEM((ng, TOPK, LANES), F32), pltpu.VMEM((ng, TOPK, LANES), I32),
                        pltpu.VMEM((NPAR, TOPK, LANES), F32), pltpu.VMEM((NPAR, TOPK, LANES), I32),
                        pltpu.VMEM((KSEL, LANES), I32), pltpu.VMEM((KSEL, LANES), F32)],
        compiler_params=_cp(("parallel",)), name="peer_route",
    )(st)


NC, NS, L = 2, 16, 16
NW = NC * NS
NJ = D_MODEL // L
R = TOPK
NCH = KSEL // R
SUB = 2
RD = SUB * R
G = 16
DW = D_MODEL // 2


def _perm(x, idx):
    return jnp.take_along_axis(x, idx, axis=0, mode="promise_in_bounds")


def _halves(w):
    lo = lax.bitcast_convert_type(lax.shift_left(w, 16), F32)
    hi = lax.bitcast_convert_type(jnp.bitwise_and(w, jnp.int32(-65536)), F32)
    return lo, hi


def _bf(w):
    return plsc.bitcast(w, BF)


def peer_sc(x, resid, idx, gates, u_tab, v_tab):
    T = x.shape[0]
    tpw = T // NW
    ngroups = tpw // G
    nchunks = G * NCH // SUB
    idx3 = idx.reshape(T * NCH // SUB, RD)
    g3 = gates.reshape(T * NCH, R)
    mesh = plsc.VectorSubcoreMesh(core_axis_name="c", subcore_axis_name="s")

    @functools.partial(
        pl.kernel, mesh=mesh,
        out_type=jax.ShapeDtypeStruct((T, D_MODEL), F32),
        scratch_types=[
            pltpu.VMEM((G, DW), I32),
            pltpu.VMEM((G, D_MODEL), F32),
            pltpu.VMEM((nchunks, RD), I32),
            pltpu.VMEM((G * NCH, R), F32),
            pltpu.VMEM((RD, DW), I32),
            pltpu.VMEM((RD, DW), I32),
            pltpu.VMEM((RD, DW), I32),
            pltpu.VMEM((RD, DW), I32),
            pltpu.SemaphoreType.DMA,
            pltpu.SemaphoreType.DMA,
            pltpu.SemaphoreType.DMA,
            pltpu.SemaphoreType.DMA,
        ],
        compiler_params=pltpu.CompilerParams(needs_layout_passes=False),
        name="peer_experts_sc",
    )
    def k(x_hbm, r_hbm, idx_hbm, g_hbm, u_hbm, v_hbm, out_hbm,
          x_v, out_v, idx_v, g_v, ub0, ub1, vb0, vb1, su0, su1, sv0, sv1):
        wid = lax.axis_index("s") * NC + lax.axis_index("c")
        ubs, vbs, sus, svs = (ub0, ub1), (vb0, vb1), (su0, su1), (sv0, sv1)
        iota = lax.iota(I32, L)

        def gather_copies(c, b):
            return (pltpu.make_async_copy(u_hbm.at[idx_v.at[c]], ubs[b], sus[b]),
                    pltpu.make_async_copy(v_hbm.at[idx_v.at[c]], vbs[b], svs[b]))

        def issue(c, b):
            for cp in gather_copies(c, b):
                cp.start()

        def wait(c, b):
            for cp in gather_copies(c, b):
                cp.wait()

        def compute(c, b, ro):
            ub, vb = ubs[b], vbs[b]
            t = c // NCH

            def ubody(mm, accs):
                x0 = _bf(x_v[t, pl.ds(mm * 2 * L, L)])
                x1 = _bf(x_v[t, pl.ds(mm * 2 * L + L, L)])
                out = []
                for kk in range(R):
                    pr = x0 * _bf(ub[ro + kk, pl.ds(mm * 2 * L, L)]) + x1 * _bf(ub[ro + kk, pl.ds(mm * 2 * L + L, L)])
                    lo, hi = _halves(plsc.bitcast(pr, I32))
                    out.append(accs[kk] + (lo + hi))
                return tuple(out)

            accs = lax.fori_loop(0, NJ // 4, ubody, tuple(jnp.zeros((L,), F32) for _ in range(R)))
            vecs = list(accs)
            dist = L // 2
            while dist >= 1:
                pidx = jnp.bitwise_xor(iota, dist)
                low = jnp.bitwise_and(iota, dist) == 0
                nxt = []
                for kk in range(dist):
                    a = vecs[kk]
                    bvec = vecs[kk + dist]
                    a = a + _perm(a, pidx)
                    bvec = bvec + _perm(bvec, pidx)
                    nxt.append(jnp.where(low, a, bvec))
                vecs = nxt
                dist //= 2
            hid = vecs[0]
            z = GC * (hid + 0.044715 * hid * hid * hid)
            gel = hid / (1.0 + jnp.exp(-2.0 * z))
            w = g_v[c, :] * gel
            wbs = []
            for kk in range(R):
                wb = _perm(w, jnp.full((L,), kk, I32))
                wbs.append(plsc.pack(wb, wb, format=plsc.PackFormat.INTERLEAVED))

            @plsc.parallel_loop(0, NJ // 2)
            def _(m):
                pr = [wbs[kk] * _bf(vb[ro + kk, pl.ds(m * L, L)]) for kk in range(R)]
                for _lvl in range(2):
                    pr = [pr[i] + pr[i + 1] for i in range(0, len(pr), 2)]
                los, his = [], []
                for q in pr:
                    lo, hi = _halves(plsc.bitcast(q, I32))
                    los.append(lo)
                    his.append(hi)
                while len(los) > 1:
                    los = [los[i] + los[i + 1] for i in range(0, len(los), 2)]
                    his = [his[i] + his[i + 1] for i in range(0, len(his), 2)]
                out_v[t, pl.ds(m * L, L)] = out_v[t, pl.ds(m * L, L)] + los[0]
                out_v[t, pl.ds(DW + m * L, L)] = out_v[t, pl.ds(DW + m * L, L)] + his[0]

        def group(g, carry):
            tok0 = wid * tpw + g * G
            pltpu.sync_copy(x_hbm.at[pl.ds(tok0, G)], x_v)
            pltpu.sync_copy(r_hbm.at[pl.ds(tok0, G)], out_v)
            pltpu.sync_copy(idx_hbm.at[pl.ds(tok0 * (NCH // SUB), nchunks)], idx_v)
            pltpu.sync_copy(g_hbm.at[pl.ds(tok0 * NCH, G * NCH)], g_v)
            issue(0, 0)

            def cbody(cc, c2):
                c = cc * 2
                issue(c + 1, 1)
                wait(c, 0)
                for sub in range(SUB):
                    compute(c * SUB + sub, 0, sub * R)

                @pl.when(c + 2 < nchunks)
                def _():
                    issue(c + 2, 0)

                wait(c + 1, 1)
                for sub in range(SUB):
                    compute((c + 1) * SUB + sub, 1, sub * R)
                return c2

            lax.fori_loop(0, nchunks // 2, cbody, 0)
            pltpu.sync_copy(out_v, out_hbm.at[pl.ds(tok0, G)])
            return carry

        lax.fori_loop(0, ngroups, group, 0)

    return k(x, resid, idx3, g3, u_tab, v_tab)


def _fn_body(x_ref, g_ref, o_ref):
    xf = x_ref[...]
    o_ref[...] = xf * lax.rsqrt(jnp.mean(xf * xf, axis=-1, keepdims=True) + EPS) * g_ref[...]


def final_norm(x, g, tm=1024):
    T, d = x.shape
    return pl.pallas_call(
        _fn_body, grid=(T // tm,),
        in_specs=[pl.BlockSpec((tm, d), lambda i: (i, 0)), pl.BlockSpec((1, d), lambda i: (0, 0))],
        out_specs=pl.BlockSpec((tm, d), lambda i: (i, 0)),
        out_shape=jax.ShapeDtypeStruct((T, d), F32),
        compiler_params=_cp(("parallel",)), name="final_norm",
    )(x, g)


def _prep_layer(w_in, b_forget, conv_dw_w, conv_dw_b, conv_ln_g, conv_ln_b, rg_conv_w, rg_conv_b,
                rg_w_r, rg_b_r, rg_w_i, rg_b_i, rg_lambda, w_out, peer_wq, peer_k1, peer_k2):
    f0 = 3 * D_ATT
    wf = jnp.zeros((D_MODEL, FPAD), BF).at[:, 0:ATT_HEADS].set(w_in[:, f0:f0 + ATT_HEADS].astype(BF))
    bfg = jnp.zeros((1, FPAD), F32).at[0, 0:ATT_HEADS].set(b_forget)
    cw = jnp.zeros((32, D_CONV), F32).at[0:CONV_K].set(conv_dw_w)
    rw = jnp.zeros((8, D_RNN), F32).at[0:RNN_CONV_K].set(rg_conv_w)
    bd = lambda w: jax.scipy.linalg.block_diag(*[w[i] for i in range(RNN_BLOCKS)]).astype(BF)
    row = lambda v: v.reshape(1, -1).astype(F32)
    keys = jnp.stack([peer_k1, peer_k2], axis=1).reshape(2 * PEER_HEADS, N_KEYS, D_HALF).astype(BF)
    return dict(wqk=w_in[:, 0:2 * D_ATT].astype(BF), wvt=w_in[:, 2 * D_ATT:f0].T.astype(BF), wf=wf,
                wrest=w_in[:, f0 + ATT_HEADS:].astype(BF), bfg=bfg,
                cw=cw, cb=row(conv_dw_b), lg=row(conv_ln_g), lb=row(conv_ln_b),
                rw=rw, rb=row(rg_conv_b), wr=bd(rg_w_r), br=row(rg_b_r), wi=bd(rg_w_i), bi=row(rg_b_i),
                lam=row(rg_lambda), woa=w_out[0:D_ATT].astype(BF), wob=w_out[D_ATT:].astype(BF),
                wq=peer_wq.astype(BF), keys=keys)


def kernel(x, norm1_g, w_in, b_forget, conv_dw_w, conv_dw_b, conv_ln_g, conv_ln_b,
           rg_conv_w, rg_conv_b, rg_w_r, rg_b_r, rg_w_i, rg_b_i, rg_lambda, w_out,
           norm2_g, peer_wq, peer_k1, peer_k2, peer_u, peer_v, final_g):
    b, s, d = x.shape
    params = [_prep_layer(w_in[l], b_forget[l], conv_dw_w[l], conv_dw_b[l], conv_ln_g[l], conv_ln_b[l],
                          rg_conv_w[l], rg_conv_b[l], rg_w_r[l], rg_b_r[l], rg_w_i[l], rg_b_i[l], rg_lambda[l],
                          w_out[l], peer_wq[l], peer_k1[l], peer_k2[l]) for l in range(DEPTH)]
    tabs = [(pack_pairs(peer_u[l]), pack_pairs(peer_v[l])) for l in range(DEPTH)]
    bs = b // N_SLICES
    xs = [x[i * bs:(i + 1) * bs].reshape(bs * s, d) for i in range(N_SLICES)]
    xs[0], tabs = lax.optimization_barrier((xs[0], tabs))
    prev = None
    for l in range(DEPTH):
        p = params[l]
        for i in range(N_SLICES):
            xt = xs[i]
            if prev is not None:
                xt, prev = lax.optimization_barrier((xt, prev))
            qkb, vt, rest = in_proj(xt, norm1_g[l].reshape(1, d), p, s)
            y_att = attention(qkb, vt, bs, s)
            y_cr = mixers(rest, p, bs, s)
            x1, h2p, st = out_proj(xt, y_att, y_cr, p, norm2_g[l].reshape(1, d))
            experts, gates = route(st)
            prev = experts
            xs[i] = peer_sc(h2p, x1, experts, gates, tabs[l][0], tabs[l][1])
    outs = [final_norm(xt, final_g.reshape(1, d)).reshape(bs, s, d) for xt in xs]
    return jnp.concatenate(outs, axis=0)
```

```python
import functools
import math

import jax
import jax.numpy as jnp
from jax import lax
from jax.experimental import pallas as pl
from jax.experimental.pallas import tpu as pltpu
from jax.experimental.pallas import tpu_sc as plsc

BF = jnp.bfloat16
F32 = jnp.float32
I32 = jnp.int32

D_MODEL = 1024
DEPTH = 2
ATT_HEADS = 8
ATT_HD = 64
D_ATT = ATT_HEADS * ATT_HD
D_CONV = 256
CONV_K = 31
D_RNN = 256
RNN_BLOCKS = 4
RNN_CONV_K = 4
RG_C = 8.0
EPS = 1e-6
N_REST = 2 * D_CONV + 2 * D_RNN
PEER_HEADS = 8
N_KEYS = 128
D_HALF = 128
TOPK = 16
KSEL = PEER_HEADS * TOPK
GC = 0.7978845608028654
NEG = float("-inf")

N_SLICES = 4
LANES = 128
VMEM_LIMIT = 48 * 1024 * 1024


def _cp(sem):
    return pltpu.CompilerParams(dimension_semantics=sem, vmem_limit_bytes=VMEM_LIMIT)


def _split3(x):
    hi = x.astype(BF)
    r = x - hi.astype(F32)
    mid = r.astype(BF)
    lo = (r - mid.astype(F32)).astype(BF)
    return hi, mid, lo


def _nt(a, b):
    return lax.dot_general(a, b, (((1,), (1,)), ((), ())), preferred_element_type=F32)


def _dot(a, b):
    return jnp.dot(a, b, preferred_element_type=F32)


def _sigmoid(x):
    return 1.0 / (1.0 + jnp.exp(-x))


def _gelu(x):
    return 0.5 * x * (1.0 + jnp.tanh(GC * (x + 0.044715 * x * x * x)))


LOG2E = 1.4426950408889634
NSPLIT = 3
FPAD = 16


def _inproj_body(x_ref, g_ref, wqk_ref, wvt_ref, wf_ref, wrest_ref, bf_ref, tri_ref, place_ref,
                 qkb_ref, vt_ref, rest_ref, carry_ref, *, blocks_per_seq, tm):
    i = pl.program_id(0)
    x = x_ref[...]
    h = x * lax.rsqrt(jnp.mean(x * x, axis=-1, keepdims=True) + EPS) * g_ref[...]
    hb = h.astype(BF)
    qk = _dot(hb, wqk_ref[...])
    col = lax.broadcasted_iota(I32, (1, 2 * D_ATT), 1)
    qk = jnp.where(col < D_ATT, qk * (LOG2E / math.sqrt(ATT_HD)), qk)
    qkb_ref[:, 0:2 * D_ATT] = qk.astype(BF)
    vt_ref[...] = _nt(wvt_ref[...], hb).astype(BF)
    rest_ref[...] = _dot(hb, wrest_ref[...])
    ft = _dot(hb, wf_ref[...]) + bf_ref[...]
    lf = jnp.minimum(ft, 0.0) - jnp.log(1.0 + jnp.exp(-jnp.abs(ft)))
    hi, mid, lo = _split3(lf)
    tri = tri_ref[...]
    cs = _dot(tri, hi) + _dot(tri, mid) + _dot(tri, lo)

    @pl.when(i % blocks_per_seq == 0)
    def _():
        carry_ref[...] = jnp.zeros_like(carry_ref)

    cum = cs + carry_ref[...]
    carry_ref[...] = cum[tm - 1:tm, :]
    pieces = _split3(cum * (-LOG2E))
    kb = _dot(pieces[0], place_ref[0]) + _dot(pieces[1], place_ref[1]) + _dot(pieces[2], place_ref[2])
    qkb_ref[:, 2 * D_ATT:3 * D_ATT] = kb.astype(BF)


def bias_lane(hh, j):
    return (ATT_HD if hh == 0 else 0) + j


def in_proj(x, g, p, seq, tm=512):
    T = x.shape[0]
    tri = (lax.broadcasted_iota(I32, (tm, tm), 0) >= lax.broadcasted_iota(I32, (tm, tm), 1)).astype(BF)
    shp = (NSPLIT, FPAD, D_ATT)
    hd = lax.broadcasted_iota(I32, shp, 1)
    target = (hd // 2) * LANES + jnp.where(hd % 2 == 0, ATT_HD, 0) + lax.broadcasted_iota(I32, shp, 0)
    place = ((lax.broadcasted_iota(I32, shp, 2) == target) & (hd < ATT_HEADS)).astype(BF)
    body = functools.partial(_inproj_body, blocks_per_seq=seq // tm, tm=tm)
    return pl.pallas_call(
        body, grid=(T // tm,),
        in_specs=[pl.BlockSpec((tm, D_MODEL), lambda i: (i, 0)),
                  pl.BlockSpec((1, D_MODEL), lambda i: (0, 0)),
                  pl.BlockSpec((D_MODEL, 2 * D_ATT), lambda i: (0, 0)),
                  pl.BlockSpec((D_ATT, D_MODEL), lambda i: (0, 0)),
                  pl.BlockSpec((D_MODEL, FPAD), lambda i: (0, 0)),
                  pl.BlockSpec((D_MODEL, N_REST), lambda i: (0, 0)),
                  pl.BlockSpec((1, FPAD), lambda i: (0, 0)),
                  pl.BlockSpec((tm, tm), lambda i: (0, 0)),
                  pl.BlockSpec(shp, lambda i: (0, 0, 0))],
        out_specs=[pl.BlockSpec((tm, 3 * D_ATT), lambda i: (i, 0)),
                   pl.BlockSpec((D_ATT, tm), lambda i: (0, i)),
                   pl.BlockSpec((tm, N_REST), lambda i: (i, 0))],
        out_shape=[jax.ShapeDtypeStruct((T, 3 * D_ATT), BF),
                   jax.ShapeDtypeStruct((D_ATT, T), BF),
                   jax.ShapeDtypeStruct((T, N_REST), F32)],
        scratch_shapes=[pltpu.VMEM((1, FPAD), F32)],
        compiler_params=_cp(("arbitrary",)), name="in_proj",
    )(x, g, p["wqk"], p["wvt"], p["wf"], p["wrest"], p["bfg"], tri, place)


def _attn_body(q_ref, k_ref, kb_ref, vt_ref, o_ref, m_ref, acc_ref, *, tq, tk):
    qi = pl.program_id(2)
    ki = pl.program_id(3)

    @pl.when(ki == 0)
    def _():
        m_ref[...] = jnp.full_like(m_ref, NEG)
        acc_ref[...] = jnp.zeros_like(acc_ref)

    lane = lax.broadcasted_iota(I32, (1, LANES), 1)
    first = lane < ATT_HD
    vrow = lax.broadcasted_iota(I32, (LANES, 1), 0) < ATT_HD

    def step(masked):
        q = q_ref[...]
        k = k_ref[...]
        kb = kb_ref[...]
        vt = vt_ref[...]
        if masked:
            keep = (lax.broadcasted_iota(I32, (tk, tq), 0) <= lax.broadcasted_iota(I32, (tk, tq), 1))
        for hh in range(2):
            own = first if hh == 0 else jnp.logical_not(first)
            ones = (lane >= bias_lane(hh, 0)) & (lane < bias_lane(hh, NSPLIT))
            qa = jnp.where(own, q, jnp.where(ones, 1.0, 0.0).astype(BF))
            ka = jnp.where(own, k, kb)
            st = _nt(ka, qa)
            if masked:
                st = jnp.where(keep, st, NEG)
            m_prev = m_ref[hh]
            m_new = jnp.maximum(m_prev, jnp.max(st, axis=0, keepdims=True))
            alpha = jnp.exp2(m_prev - m_new)
            p = jnp.exp2(st - m_new).astype(BF)
            m_ref[hh] = m_new
            vown = vrow if hh == 0 else jnp.logical_not(vrow)
            va = jnp.where(vown, vt, jnp.ones_like(vt))
            acc_ref[hh] = alpha * acc_ref[hh] + _dot(va, p)

    @pl.when(ki < qi)
    def _():
        step(False)

    @pl.when(ki == qi)
    def _():
        step(True)
        a0 = acc_ref[0]
        a1 = acc_ref[1]
        ot = jnp.where(vrow, a0 / a0[ATT_HD:ATT_HD + 1, :], a1 / a1[0:1, :])
        o_ref[...] = ot.T.astype(o_ref.dtype)


def attention(qkb, vt, batch, seq, tq=512):
    T = qkb.shape[0]
    tk = tq
    nq = seq // tq
    npair = ATT_HEADS // 2
    body = functools.partial(_attn_body, tq=tq, tk=tk)
    kblk = lambda b, qi, ki: b * nq + jnp.minimum(ki, qi)
    return pl.pallas_call(
        body, grid=(batch, npair, nq, nq),
        in_specs=[pl.BlockSpec((tq, LANES), lambda b, p, qi, ki: (b * nq + qi, p)),
                  pl.BlockSpec((tk, LANES), lambda b, p, qi, ki: (kblk(b, qi, ki), npair + p)),
                  pl.BlockSpec((tk, LANES), lambda b, p, qi, ki: (kblk(b, qi, ki), 2 * npair + p)),
                  pl.BlockSpec((LANES, tk), lambda b, p, qi, ki: (p, kblk(b, qi, ki)))],
        out_specs=pl.BlockSpec((tq, LANES), lambda b, p, qi, ki: (b * nq + qi, p)),
        out_shape=jax.ShapeDtypeStruct((T, D_ATT), BF),
        scratch_shapes=[pltpu.VMEM((2, 1, tq), F32), pltpu.VMEM((2, LANES, tq), F32)],
        compiler_params=_cp(("parallel", "parallel", "parallel", "arbitrary")), name="fox_attention",
    )(qkb, qkb, qkb, vt)


CONV_HALO = 32
RG_HALO = 8


def _mix_body(rest_ref, cw_ref, cb_ref, lg_ref, lb_ref, rw_ref, rb_ref, wr_ref, br_ref, wi_ref, bi_ref, lam_ref,
              o_ref, ybuf, xbuf, hc, *, ts):
    si = pl.program_id(1)

    @pl.when(si == 0)
    def _():
        ybuf[0:CONV_HALO, :] = jnp.zeros((CONV_HALO, D_CONV), F32)
        xbuf[0:RG_HALO, :] = jnp.zeros((RG_HALO, D_RNN), F32)
        hc[...] = jnp.zeros_like(hc)

    y = rest_ref[:, 0:D_CONV] * _sigmoid(rest_ref[:, D_CONV:2 * D_CONV])
    ybuf[CONV_HALO:CONV_HALO + ts, :] = y
    acc = jnp.zeros((ts, D_CONV), F32)
    for k in range(CONV_K):
        acc = acc + cw_ref[k:k + 1, :] * ybuf[pl.ds(CONV_HALO - (CONV_K - 1) + k, ts), :]
    yc = acc + cb_ref[...]
    mu = jnp.mean(yc, axis=-1, keepdims=True)
    var = jnp.mean(jnp.square(yc - mu), axis=-1, keepdims=True)
    yn = (yc - mu) * lax.rsqrt(var + EPS) * lg_ref[...] + lb_ref[...]
    o_ref[:, 0:D_CONV] = (yn * _sigmoid(yn)).astype(o_ref.dtype)
    ybuf[0:CONV_HALO, :] = ybuf[ts:ts + CONV_HALO, :]

    xbuf[RG_HALO:RG_HALO + ts, :] = rest_ref[:, 2 * D_CONV:2 * D_CONV + D_RNN]
    xc = jnp.zeros((ts, D_RNN), F32)
    for k in range(RNN_CONV_K):
        xc = xc + rw_ref[k:k + 1, :] * xbuf[pl.ds(RG_HALO - (RNN_CONV_K - 1) + k, ts), :]
    xc = xc + rb_ref[...]
    xbuf[0:RG_HALO, :] = xbuf[ts:ts + RG_HALO, :]
    xcb = xc.astype(BF)
    r = _sigmoid(_dot(xcb, wr_ref[...]) + br_ref[...])
    gi = _sigmoid(_dot(xcb, wi_ref[...]) + bi_ref[...])
    nl = -lam_ref[...]
    sp = jnp.maximum(nl, 0.0) + jnp.log(1.0 + jnp.exp(-jnp.abs(nl)))
    log_a = -RG_C * r * sp
    a = jnp.exp(log_a)
    bt = jnp.sqrt(1.0 - jnp.exp(2.0 * log_a)) * (gi * xc)
    row = lax.broadcasted_iota(I32, (ts, 1), 0)
    sh = 1
    while sh < ts:
        live = row >= sh
        a_s = jnp.where(live, pltpu.roll(a, sh, 0), 1.0)
        b_s = jnp.where(live, pltpu.roll(bt, sh, 0), 0.0)
        bt = bt + a * b_s
        a = a * a_s
        sh *= 2
    h = bt + a * hc[...]
    hc[...] = h[ts - 1:ts, :]
    gate_in = rest_ref[:, 2 * D_CONV + D_RNN:2 * D_CONV + 2 * D_RNN]
    o_ref[:, D_CONV:D_CONV + D_RNN] = (h * _gelu(gate_in)).astype(o_ref.dtype)


def mixers(rest, p, batch, seq, ts=512):
    T = rest.shape[0]
    ns = seq // ts
    body = functools.partial(_mix_body, ts=ts)
    vec = lambda: pl.BlockSpec((1, D_CONV), lambda b, s: (0, 0))
    return pl.pallas_call(
        body, grid=(batch, ns),
        in_specs=[pl.BlockSpec((ts, N_REST), lambda b, s: (b * ns + s, 0)),
                  pl.BlockSpec((32, D_CONV), lambda b, s: (0, 0)), vec(), vec(), vec(),
                  pl.BlockSpec((8, D_RNN), lambda b, s: (0, 0)), vec(),
                  pl.BlockSpec((D_RNN, D_RNN), lambda b, s: (0, 0)), vec(),
                  pl.BlockSpec((D_RNN, D_RNN), lambda b, s: (0, 0)), vec(), vec()],
        out_specs=pl.BlockSpec((ts, D_CONV + D_RNN), lambda b, s: (b * ns + s, 0)),
        out_shape=jax.ShapeDtypeStruct((T, D_CONV + D_RNN), BF),
        scratch_shapes=[pltpu.VMEM((ts + CONV_HALO, D_CONV), F32), pltpu.VMEM((ts + RG_HALO, D_RNN), F32),
                        pltpu.VMEM((1, D_RNN), F32)],
        compiler_params=_cp(("arbitrary", "arbitrary")), name="conv_rglru",
    )(rest, p["cw"], p["cb"], p["lg"], p["lb"], p["rw"], p["rb"], p["wr"], p["br"], p["wi"], p["bi"], p["lam"])


def pack_pairs(a):
    half = a.shape[1] // 2
    r = lax.bitcast_convert_type(a, I32)
    r = r + jnp.int32(0x7FFF) + jnp.bitwise_and(lax.shift_right_logical(r, 16), 1)
    lo = lax.shift_right_logical(r[:, :half], 16)
    hi = jnp.bitwise_and(r[:, half:], jnp.int32(-65536))
    return jnp.bitwise_or(hi, lo)


def _outproj_body(x_ref, ya_ref, yc_ref, woa_ref, wob_ref, g2_ref, wq_ref, keys_ref, x1_ref, h2p_ref, st_ref):
    x1 = x_ref[...] + _dot(ya_ref[...], woa_ref[...]) + _dot(yc_ref[...], wob_ref[...])
    x1_ref[...] = x1
    h2 = x1 * lax.rsqrt(jnp.mean(x1 * x1, axis=-1, keepdims=True) + EPS) * g2_ref[...]
    h2p_ref[...] = pack_pairs(h2)
    q = _dot(h2.astype(BF), wq_ref[...]).astype(BF)
    for g in range(2 * PEER_HEADS):
        st_ref[g * N_KEYS:(g + 1) * N_KEYS, :] = _nt(keys_ref[g], q[:, g * D_HALF:(g + 1) * D_HALF])


def out_proj(x, ya, yc, p, g2, tm=256):
    T = x.shape[0]
    ng = 2 * PEER_HEADS
    return pl.pallas_call(
        _outproj_body, grid=(T // tm,),
        in_specs=[pl.BlockSpec((tm, D_MODEL), lambda i: (i, 0)),
                  pl.BlockSpec((tm, D_ATT), lambda i: (i, 0)),
                  pl.BlockSpec((tm, D_CONV + D_RNN), lambda i: (i, 0)),
                  pl.BlockSpec((D_ATT, D_MODEL), lambda i: (0, 0)),
                  pl.BlockSpec((D_CONV + D_RNN, D_MODEL), lambda i: (0, 0)),
                  pl.BlockSpec((1, D_MODEL), lambda i: (0, 0)),
                  pl.BlockSpec((D_MODEL, ng * D_HALF), lambda i: (0, 0)),
                  pl.BlockSpec((ng, N_KEYS, D_HALF), lambda i: (0, 0, 0))],
        out_specs=[pl.BlockSpec((tm, D_MODEL), lambda i: (i, 0)),
                   pl.BlockSpec((tm, D_MODEL // 2), lambda i: (i, 0)),
                   pl.BlockSpec((ng * N_KEYS, tm), lambda i: (0, i))],
        out_shape=[jax.ShapeDtypeStruct((T, D_MODEL), F32),
                   jax.ShapeDtypeStruct((T, D_MODEL // 2), I32),
                   jax.ShapeDtypeStruct((ng * N_KEYS, T), F32)],
        compiler_params=_cp(("parallel",)), name="out_proj_peer_scores",
    )(x, ya, yc, p["woa"], p["wob"], g2, p["wq"], p["keys"])


BIG_ID = 1 << 20
SUBL = 8
NPAR = 4
SEL_CHAIN = 4


def _take_rounds(problems, nrounds):
    state = [list(slabs) for slabs, _ in problems]
    res = [([], []) for _ in problems]
    for _ in range(nrounds):
        for pi, (_, ids) in enumerate(problems):
            slabs = state[pi]
            m8 = slabs[0]
            for sl in slabs[1:]:
                m8 = jnp.maximum(m8, sl)
            m = jnp.max(m8, axis=0, keepdims=True)
            chains = []
            for c0 in range(0, len(slabs), SEL_CHAIN):
                v = jnp.full((SUBL, LANES), BIG_ID, I32)
                for sl, idc in zip(reversed(slabs[c0:c0 + SEL_CHAIN]), reversed(ids[c0:c0 + SEL_CHAIN])):
                    v = jnp.where(sl == m, idc, v)
                chains.append(v)
            while len(chains) > 1:
                chains = [jnp.minimum(chains[i], chains[i + 1]) if i + 1 < len(chains) else chains[i]
                          for i in range(0, len(chains), 2)]
            pick = jnp.min(chains[0], axis=0, keepdims=True)
            state[pi] = [jnp.where(idc == pick, NEG, sl) for sl, idc in zip(slabs, ids)]
            res[pi][0].append(m)
            res[pi][1].append(pick)
    return res


def _route_body(st_ref, e_ref, g_ref, v_scr, i_scr, sv_scr, ci_scr, et_scr, gt_scr):
    ng = 2 * PEER_HEADS
    sub = lax.broadcasted_iota(I32, (SUBL, LANES), 0)
    key_ids = [sub + SUBL * i for i in range(N_KEYS // SUBL)]

    def stage1(gg, carry):
        probs = []
        for q in range(NPAR):
            base = pl.multiple_of((gg * NPAR + q) * N_KEYS, N_KEYS)
            probs.append(([st_ref[pl.ds(base + SUBL * i, SUBL), :] for i in range(N_KEYS // SUBL)], key_ids))
        for q, (vals, picks) in enumerate(_take_rounds(probs, TOPK)):
            for r in range(TOPK):
                v_scr[gg * NPAR + q, r:r + 1, :] = vals[r]
                i_scr[gg * NPAR + q, r:r + 1, :] = picks[r]
        return carry

    lax.fori_loop(0, ng // NPAR, stage1, 0)

    def stage2(hh, carry):
        probs = []
        for q in range(NPAR):
            h = hh * NPAR + q
            v1 = v_scr[2 * h]
            v2 = v_scr[2 * h + 1]
            slabs = [v1[0:1, :] + v2[0:SUBL, :], v1[0:1, :] + v2[SUBL:TOPK, :]]
            ids = [sub, sub + SUBL]
            for i in range(1, TOPK):
                nj = TOPK // (i + 1)
                slabs.append(jnp.where(sub < nj, v1[i:i + 1, :] + v2[0:SUBL, :], NEG))
                ids.append(sub + i * TOPK)
            probs.append((slabs, ids))
        for q, (vals, picks) in enumerate(_take_rounds(probs, TOPK)):
            h = hh * NPAR + q
            i1 = i_scr[2 * h]
            i2 = i_scr[2 * h + 1]
            for r in range(TOPK):
                sv_scr[q, r:r + 1, :] = vals[r]
                ci_scr[q, r:r + 1, :] = picks[r]
            sv = sv_scr[q]
            ci = ci_scr[q]
            ci_hi = lax.shift_right_logical(ci, 4)
            ci_lo = jnp.bitwise_and(ci, TOPK - 1)
            e1 = jnp.zeros((TOPK, LANES), I32)
            e2 = jnp.zeros((TOPK, LANES), I32)
            for i in range(TOPK):
                e1 = jnp.where(ci_hi == i, i1[i:i + 1, :], e1)
                e2 = jnp.where(ci_lo == i, i2[i:i + 1, :], e2)
            p = jnp.exp(sv - sv[0:1, :])
            gates = p / jnp.sum(p, axis=0, keepdims=True)
            et_scr[pl.ds(pl.multiple_of(h * TOPK, TOPK), TOPK), :] = e1 * N_KEYS + e2
            gt_scr[pl.ds(pl.multiple_of(h * TOPK, TOPK), TOPK), :] = gates
        return carry

    lax.fori_loop(0, PEER_HEADS // NPAR, stage2, 0)
    e_ref[...] = et_scr[...].T
    g_ref[...] = gt_scr[...].T


def route(st):
    T = st.shape[1]
    ng = 2 * PEER_HEADS
    return pl.pallas_call(
        _route_body, grid=(T // LANES,),
        in_specs=[pl.BlockSpec((ng * N_KEYS, LANES), lambda i: (0, i))],
        out_specs=[pl.BlockSpec((LANES, KSEL), lambda i: (i, 0)),
                   pl.BlockSpec((LANES, KSEL), lambda i: (i, 0))],
        out_shape=[jax.ShapeDtypeStruct((T, KSEL), I32),
                   jax.ShapeDtypeStruct((T, KSEL), F32)],
        scratch_shapes=[pltpu.VMEM((ng, TOPK, LANES), F32), pltpu.VMEM((ng, TOPK, LANES), I32),
                        pltpu.VMEM((NPAR, TOPK, LANES), F32), pltpu.VMEM((NPAR, TOPK, LANES), I32),
                        pltpu.VMEM((KSEL, LANES), I32), pltpu.VMEM((KSEL, LANES), F32)],
        compiler_params=_cp(("parallel",)), name="peer_route",
    )(st)


NC, NS, L = 2, 16, 16
NW = NC * NS
NJ = D_MODEL // L
R = TOPK
NCH = KSEL // R
SUB = 2
RD = SUB * R
G = 16
DW = D_MODEL // 2


def _perm(x, idx):
    return jnp.take_along_axis(x, idx, axis=0, mode="promise_in_bounds")


def _halves(w):
    lo = lax.bitcast_convert_type(lax.shift_left(w, 16), F32)
    hi = lax.bitcast_convert_type(jnp.bitwise_and(w, jnp.int32(-65536)), F32)
    return lo, hi


def _bf(w):
    return plsc.bitcast(w, BF)


def peer_sc(x, resid, idx, gates, uv_tab):
    T = x.shape[0]
    tpw = T // NW
    ngroups = tpw // G
    nchunks = G * NCH // SUB
    idx3 = idx.reshape(T * NCH // SUB, RD)
    g3 = gates.reshape(T * NCH, R)
    mesh = plsc.VectorSubcoreMesh(core_axis_name="c", subcore_axis_name="s")

    @functools.partial(
        pl.kernel, mesh=mesh,
        out_type=jax.ShapeDtypeStruct((T, D_MODEL), F32),
        scratch_types=[
            pltpu.VMEM((G, DW), I32),
            pltpu.VMEM((G, D_MODEL), F32),
            pltpu.VMEM((nchunks, RD), I32),
            pltpu.VMEM((G * NCH, R), F32),
            pltpu.VMEM((RD, 2 * DW), I32),
            pltpu.VMEM((RD, 2 * DW), I32),
            pltpu.SemaphoreType.DMA,
            pltpu.SemaphoreType.DMA,
        ],
        compiler_params=pltpu.CompilerParams(needs_layout_passes=False),
        name="peer_experts_sc",
    )
    def k(x_hbm, r_hbm, idx_hbm, g_hbm, uv_hbm, out_hbm, x_v, out_v, idx_v, g_v, buf0, buf1, sem0, sem1):
        wid = lax.axis_index("s") * NC + lax.axis_index("c")
        bufs, sems = (buf0, buf1), (sem0, sem1)
        iota = lax.iota(I32, L)

        def gather_copy(c, b):
            return pltpu.make_async_copy(uv_hbm.at[idx_v.at[c]], bufs[b], sems[b])

        def issue(c, b):
            gather_copy(c, b).start()

        def wait(c, b):
            gather_copy(c, b).wait()

        def compute(c, b, ro):
            ub = vb = bufs[b]
            t = c // NCH

            def ubody(mm, accs):
                x0 = _bf(x_v[t, pl.ds(mm * 2 * L, L)])
                x1 = _bf(x_v[t, pl.ds(mm * 2 * L + L, L)])
                out = []
                for kk in range(R):
                    pr = x0 * _bf(ub[ro + kk, pl.ds(mm * 2 * L, L)]) + x1 * _bf(ub[ro + kk, pl.ds(mm * 2 * L + L, L)])
                    lo, hi = _halves(plsc.bitcast(pr, I32))
                    out.append(accs[kk] + (lo + hi))
                return tuple(out)

            accs = lax.fori_loop(0, NJ // 4, ubody, tuple(jnp.zeros((L,), F32) for _ in range(R)))
            vecs = list(accs)
            dist = L // 2
            while dist >= 1:
                pidx = jnp.bitwise_xor(iota, dist)
                low = jnp.bitwise_and(iota, dist) == 0
                nxt = []
                for kk in range(dist):
                    a = vecs[kk]
                    bvec = vecs[kk + dist]
                    a = a + _perm(a, pidx)
                    bvec = bvec + _perm(bvec, pidx)
                    nxt.append(jnp.where(low, a, bvec))
                vecs = nxt
                dist //= 2
            hid = vecs[0]
            z = GC * (hid + 0.044715 * hid * hid * hid)
            gel = hid / (1.0 + jnp.exp(-2.0 * z))
            w = g_v[c, :] * gel
            wbs = []
            for kk in range(R):
                wb = _perm(w, jnp.full((L,), kk, I32))
                wbs.append(plsc.pack(wb, wb, format=plsc.PackFormat.INTERLEAVED))

            @plsc.parallel_loop(0, NJ // 2)
            def _(m):
                pr = [wbs[kk] * _bf(vb[ro + kk, pl.ds(DW + m * L, L)]) for kk in range(R)]
                for _lvl in range(2):
                    pr = [pr[i] + pr[i + 1] for i in range(0, len(pr), 2)]
                los, his = [], []
                for q in pr:
                    lo, hi = _halves(plsc.bitcast(q, I32))
                    los.append(lo)
                    his.append(hi)
                while len(los) > 1:
                    los = [los[i] + los[i + 1] for i in range(0, len(los), 2)]
                    his = [his[i] + his[i + 1] for i in range(0, len(his), 2)]
                out_v[t, pl.ds(m * L, L)] = out_v[t, pl.ds(m * L, L)] + los[0]
                out_v[t, pl.ds(DW + m * L, L)] = out_v[t, pl.ds(DW + m * L, L)] + his[0]

        def group(g, carry):
            tok0 = wid * tpw + g * G
            pltpu.sync_copy(x_hbm.at[pl.ds(tok0, G)], x_v)
            pltpu.sync_copy(r_hbm.at[pl.ds(tok0, G)], out_v)
            pltpu.sync_copy(idx_hbm.at[pl.ds(tok0 * (NCH // SUB), nchunks)], idx_v)
            pltpu.sync_copy(g_hbm.at[pl.ds(tok0 * NCH, G * NCH)], g_v)
            issue(0, 0)

            def cbody(cc, c2):
                c = cc * 2
                issue(c + 1, 1)
                wait(c, 0)
                for sub in range(SUB):
                    compute(c * SUB + sub, 0, sub * R)

                @pl.when(c + 2 < nchunks)
                def _():
                    issue(c + 2, 0)

                wait(c + 1, 1)
                for sub in range(SUB):
                    compute((c + 1) * SUB + sub, 1, sub * R)
                return c2

            lax.fori_loop(0, nchunks // 2, cbody, 0)
            pltpu.sync_copy(out_v, out_hbm.at[pl.ds(tok0, G)])
            return carry

        lax.fori_loop(0, ngroups, group, 0)

    return k(x, resid, idx3, g3, uv_tab)


def _fn_body(x_ref, g_ref, o_ref):
    xf = x_ref[...]
    o_ref[...] = xf * lax.rsqrt(jnp.mean(xf * xf, axis=-1, keepdims=True) + EPS) * g_ref[...]


def final_norm(x, g, tm=1024):
    T, d = x.shape
    return pl.pallas_call(
        _fn_body, grid=(T // tm,),
        in_specs=[pl.BlockSpec((tm, d), lambda i: (i, 0)), pl.BlockSpec((1, d), lambda i: (0, 0))],
        out_specs=pl.BlockSpec((tm, d), lambda i: (i, 0)),
        out_shape=jax.ShapeDtypeStruct((T, d), F32),
        compiler_params=_cp(("parallel",)), name="final_norm",
    )(x, g)


def _prep_layer(w_in, b_forget, conv_dw_w, conv_dw_b, conv_ln_g, conv_ln_b, rg_conv_w, rg_conv_b,
                rg_w_r, rg_b_r, rg_w_i, rg_b_i, rg_lambda, w_out, peer_wq, peer_k1, peer_k2):
    f0 = 3 * D_ATT
    wf = jnp.zeros((D_MODEL, FPAD), BF).at[:, 0:ATT_HEADS].set(w_in[:, f0:f0 + ATT_HEADS].astype(BF))
    bfg = jnp.zeros((1, FPAD), F32).at[0, 0:ATT_HEADS].set(b_forget)
    cw = jnp.zeros((32, D_CONV), F32).at[0:CONV_K].set(conv_dw_w)
    rw = jnp.zeros((8, D_RNN), F32).at[0:RNN_CONV_K].set(rg_conv_w)
    bd = lambda w: jax.scipy.linalg.block_diag(*[w[i] for i in range(RNN_BLOCKS)]).astype(BF)
    row = lambda v: v.reshape(1, -1).astype(F32)
    keys = jnp.stack([peer_k1, peer_k2], axis=1).reshape(2 * PEER_HEADS, N_KEYS, D_HALF).astype(BF)
    return dict(wqk=w_in[:, 0:2 * D_ATT].astype(BF), wvt=w_in[:, 2 * D_ATT:f0].T.astype(BF), wf=wf,
                wrest=w_in[:, f0 + ATT_HEADS:].astype(BF), bfg=bfg,
                cw=cw, cb=row(conv_dw_b), lg=row(conv_ln_g), lb=row(conv_ln_b),
                rw=rw, rb=row(rg_conv_b), wr=bd(rg_w_r), br=row(rg_b_r), wi=bd(rg_w_i), bi=row(rg_b_i),
                lam=row(rg_lambda), woa=w_out[0:D_ATT].astype(BF), wob=w_out[D_ATT:].astype(BF),
                wq=peer_wq.astype(BF), keys=keys)


def kernel(x, norm1_g, w_in, b_forget, conv_dw_w, conv_dw_b, conv_ln_g, conv_ln_b,
           rg_conv_w, rg_conv_b, rg_w_r, rg_b_r, rg_w_i, rg_b_i, rg_lambda, w_out,
           norm2_g, peer_wq, peer_k1, peer_k2, peer_u, peer_v, final_g):
    b, s, d = x.shape
    params = [_prep_layer(w_in[l], b_forget[l], conv_dw_w[l], conv_dw_b[l], conv_ln_g[l], conv_ln_b[l],
                          rg_conv_w[l], rg_conv_b[l], rg_w_r[l], rg_b_r[l], rg_w_i[l], rg_b_i[l], rg_lambda[l],
                          w_out[l], peer_wq[l], peer_k1[l], peer_k2[l]) for l in range(DEPTH)]
    tabs = [jnp.concatenate([pack_pairs(peer_u[l]), pack_pairs(peer_v[l])], axis=1) for l in range(DEPTH)]
    bs = b // N_SLICES
    xs = [x[i * bs:(i + 1) * bs].reshape(bs * s, d) for i in range(N_SLICES)]
    for l in range(DEPTH):
        xs[l], tabs[l] = lax.optimization_barrier((xs[l], tabs[l]))
    prev = None
    for l in range(DEPTH):
        p = params[l]
        for i in range(N_SLICES):
            xt = xs[i]
            if prev is not None:
                xt, prev = lax.optimization_barrier((xt, prev))
            qkb, vt, rest = in_proj(xt, norm1_g[l].reshape(1, d), p, s)
            y_att = attention(qkb, vt, bs, s)
            y_cr = mixers(rest, p, bs, s)
            x1, h2p, st = out_proj(xt, y_att, y_cr, p, norm2_g[l].reshape(1, d))
            experts, gates = route(st)
            prev = experts
            xs[i] = peer_sc(h2p, x1, experts, gates, tabs[l])
    outs = [final_norm(xt, final_g.reshape(1, d)).reshape(bs, s, d) for xt in xs]
    return jnp.concatenate(outs, axis=0)
```

```python
import functools
import math

import jax
import jax.numpy as jnp
from jax import lax
from jax.experimental import pallas as pl
from jax.experimental.pallas import tpu as pltpu
from jax.experimental.pallas import tpu_sc as plsc

BF = jnp.bfloat16
F32 = jnp.float32
I32 = jnp.int32

D_MODEL = 1024
DEPTH = 2
ATT_HEADS = 8
ATT_HD = 64
D_ATT = ATT_HEADS * ATT_HD
D_CONV = 256
CONV_K = 31
D_RNN = 256
RNN_BLOCKS = 4
RNN_CONV_K = 4
RG_C = 8.0
EPS = 1e-6
N_REST = 2 * D_CONV + 2 * D_RNN
PEER_HEADS = 8
N_KEYS = 128
D_HALF = 128
TOPK = 16
KSEL = PEER_HEADS * TOPK
GC = 0.7978845608028654
NEG = float("-inf")

N_SLICES = 4
LANES = 128
VMEM_LIMIT = 48 * 1024 * 1024


def _cp(sem):
    return pltpu.CompilerParams(dimension_semantics=sem, vmem_limit_bytes=VMEM_LIMIT)


def _split3(x):
    hi = x.astype(BF)
    r = x - hi.astype(F32)
    mid = r.astype(BF)
    lo = (r - mid.astype(F32)).astype(BF)
    return hi, mid, lo


def _nt(a, b):
    return lax.dot_general(a, b, (((1,), (1,)), ((), ())), preferred_element_type=F32)


def _dot(a, b):
    return jnp.dot(a, b, preferred_element_type=F32)


def _sigmoid(x):
    return 1.0 / (1.0 + jnp.exp(-x))


def _gelu(x):
    return 0.5 * x * (1.0 + jnp.tanh(GC * (x + 0.044715 * x * x * x)))


LOG2E = 1.4426950408889634
NSPLIT = 3
FPAD = 16


def _inproj_body(x_ref, g_ref, wqk_ref, wvt_ref, wf_ref, wrest_ref, bf_ref, tri_ref, place_ref,
                 qkb_ref, vt_ref, rest_ref, carry_ref, *, blocks_per_seq, tm):
    i = pl.program_id(0)
    x = x_ref[...]
    h = x * lax.rsqrt(jnp.mean(x * x, axis=-1, keepdims=True) + EPS) * g_ref[...]
    hb = h.astype(BF)
    qk = _dot(hb, wqk_ref[...])
    col = lax.broadcasted_iota(I32, (1, 2 * D_ATT), 1)
    qk = jnp.where(col < D_ATT, qk * (LOG2E / math.sqrt(ATT_HD)), qk)
    qkb_ref[:, 0:2 * D_ATT] = qk.astype(BF)
    vt_ref[...] = _nt(wvt_ref[...], hb).astype(BF)
    rest_ref[...] = _dot(hb, wrest_ref[...])
    ft = _dot(hb, wf_ref[...]) + bf_ref[...]
    lf = jnp.minimum(ft, 0.0) - jnp.log(1.0 + jnp.exp(-jnp.abs(ft)))
    hi, mid, lo = _split3(lf)
    tri = tri_ref[...]
    cs = _dot(tri, hi) + _dot(tri, mid) + _dot(tri, lo)

    @pl.when(i % blocks_per_seq == 0)
    def _():
        carry_ref[...] = jnp.zeros_like(carry_ref)

    cum = cs + carry_ref[...]
    carry_ref[...] = cum[tm - 1:tm, :]
    pieces = _split3(cum * (-LOG2E))
    kb = _dot(pieces[0], place_ref[0]) + _dot(pieces[1], place_ref[1]) + _dot(pieces[2], place_ref[2])
    qkb_ref[:, 2 * D_ATT:3 * D_ATT] = kb.astype(BF)


def bias_lane(hh, j):
    return (ATT_HD if hh == 0 else 0) + j


def in_proj(x, g, p, seq, T, row0=0, tm=512):
    blk0 = row0 // tm
    tri =(lax.broadcasted_iota(I32, (tm, tm), 0) >= lax.broadcasted_iota(I32, (tm, tm), 1)).astype(BF)
    shp = (NSPLIT, FPAD, D_ATT)
    hd = lax.broadcasted_iota(I32, shp, 1)
    target = (hd // 2) * LANES + jnp.where(hd % 2 == 0, ATT_HD, 0) + lax.broadcasted_iota(I32, shp, 0)
    place = ((lax.broadcasted_iota(I32, shp, 2) == target) & (hd < ATT_HEADS)).astype(BF)
    body = functools.partial(_inproj_body, blocks_per_seq=seq // tm, tm=tm)
    return pl.pallas_call(
        body, grid=(T // tm,),
        in_specs=[pl.BlockSpec((tm, D_MODEL), lambda i: (i + blk0, 0)),
                  pl.BlockSpec((1, D_MODEL), lambda i: (0, 0)),
                  pl.BlockSpec((D_MODEL, 2 * D_ATT), lambda i: (0, 0)),
                  pl.BlockSpec((D_ATT, D_MODEL), lambda i: (0, 0)),
                  pl.BlockSpec((D_MODEL, FPAD), lambda i: (0, 0)),
                  pl.BlockSpec((D_MODEL, N_REST), lambda i: (0, 0)),
                  pl.BlockSpec((1, FPAD), lambda i: (0, 0)),
                  pl.BlockSpec((tm, tm), lambda i: (0, 0)),
                  pl.BlockSpec(shp, lambda i: (0, 0, 0))],
        out_specs=[pl.BlockSpec((tm, 3 * D_ATT), lambda i: (i, 0)),
                   pl.BlockSpec((D_ATT, tm), lambda i: (0, i)),
                   pl.BlockSpec((tm, N_REST), lambda i: (i, 0))],
        out_shape=[jax.ShapeDtypeStruct((T, 3 * D_ATT), BF),
                   jax.ShapeDtypeStruct((D_ATT, T), BF),
                   jax.ShapeDtypeStruct((T, N_REST), F32)],
        scratch_shapes=[pltpu.VMEM((1, FPAD), F32)],
        compiler_params=_cp(("arbitrary",)), name="in_proj",
    )(x, g, p["wqk"], p["wvt"], p["wf"], p["wrest"], p["bfg"], tri, place)


def _attn_body(q_ref, k_ref, kb_ref, vt_ref, o_ref, m_ref, acc_ref, *, tq, tk):
    qi = pl.program_id(2)
    ki = pl.program_id(3)

    @pl.when(ki == 0)
    def _():
        m_ref[...] = jnp.full_like(m_ref, NEG)
        acc_ref[...] = jnp.zeros_like(acc_ref)

    lane = lax.broadcasted_iota(I32, (1, LANES), 1)
    first = lane < ATT_HD
    vrow = lax.broadcasted_iota(I32, (LANES, 1), 0) < ATT_HD

    def step(masked):
        q = q_ref[...]
        k = k_ref[...]
        kb = kb_ref[...]
        vt = vt_ref[...]
        if masked:
            keep = (lax.broadcasted_iota(I32, (tk, tq), 0) <= lax.broadcasted_iota(I32, (tk, tq), 1))
        for hh in range(2):
            own = first if hh == 0 else jnp.logical_not(first)
            ones = (lane >= bias_lane(hh, 0)) & (lane < bias_lane(hh, NSPLIT))
            qa = jnp.where(own, q, jnp.where(ones, 1.0, 0.0).astype(BF))
            ka = jnp.where(own, k, kb)
            st = _nt(ka, qa)
            if masked:
                st = jnp.where(keep, st, NEG)
            m_prev = m_ref[hh]
            m_new = jnp.maximum(m_prev, jnp.max(st, axis=0, keepdims=True))
            alpha = jnp.exp2(m_prev - m_new)
            p = jnp.exp2(st - m_new).astype(BF)
            m_ref[hh] = m_new
            vown = vrow if hh == 0 else jnp.logical_not(vrow)
            va = jnp.where(vown, vt, jnp.ones_like(vt))
            acc_ref[hh] = alpha * acc_ref[hh] + _dot(va, p)

    @pl.when(ki < qi)
    def _():
        step(False)

    @pl.when(ki == qi)
    def _():
        step(True)
        a0 = acc_ref[0]
        a1 = acc_ref[1]
        ot = jnp.where(vrow, a0 / a0[ATT_HD:ATT_HD + 1, :], a1 / a1[0:1, :])
        o_ref[...] = ot.T.astype(o_ref.dtype)


def attention(qkb, vt, batch, seq, tq=512):
    T = qkb.shape[0]
    tk = tq
    nq = seq // tq
    npair = ATT_HEADS // 2
    body = functools.partial(_attn_body, tq=tq, tk=tk)
    kblk = lambda b, qi, ki: b * nq + jnp.minimum(ki, qi)
    return pl.pallas_call(
        body, grid=(batch, npair, nq, nq),
        in_specs=[pl.BlockSpec((tq, LANES), lambda b, p, qi, ki: (b * nq + qi, p)),
                  pl.BlockSpec((tk, LANES), lambda b, p, qi, ki: (kblk(b, qi, ki), npair + p)),
                  pl.BlockSpec((tk, LANES), lambda b, p, qi, ki: (kblk(b, qi, ki), 2 * npair + p)),
                  pl.BlockSpec((LANES, tk), lambda b, p, qi, ki: (p, kblk(b, qi, ki)))],
        out_specs=pl.BlockSpec((tq, LANES), lambda b, p, qi, ki: (b * nq + qi, p)),
        out_shape=jax.ShapeDtypeStruct((T, D_ATT), BF),
        scratch_shapes=[pltpu.VMEM((2, 1, tq), F32), pltpu.VMEM((2, LANES, tq), F32)],
        compiler_params=_cp(("parallel", "parallel", "parallel", "arbitrary")), name="fox_attention",
    )(qkb, qkb, qkb, vt)


CONV_HALO = 32
RG_HALO = 8


def _mix_body(rest_ref, cw_ref, cb_ref, lg_ref, lb_ref, rw_ref, rb_ref, wr_ref, br_ref, wi_ref, bi_ref, lam_ref,
              o_ref, ybuf, xbuf, hc, *, ts):
    si = pl.program_id(1)

    @pl.when(si == 0)
    def _():
        ybuf[0:CONV_HALO, :] = jnp.zeros((CONV_HALO, D_CONV), F32)
        xbuf[0:RG_HALO, :] = jnp.zeros((RG_HALO, D_RNN), F32)
        hc[...] = jnp.zeros_like(hc)

    y = rest_ref[:, 0:D_CONV] * _sigmoid(rest_ref[:, D_CONV:2 * D_CONV])
    ybuf[CONV_HALO:CONV_HALO + ts, :] = y
    acc = jnp.zeros((ts, D_CONV), F32)
    for k in range(CONV_K):
        acc = acc + cw_ref[k:k + 1, :] * ybuf[pl.ds(CONV_HALO - (CONV_K - 1) + k, ts), :]
    yc = acc + cb_ref[...]
    mu = jnp.mean(yc, axis=-1, keepdims=True)
    var = jnp.mean(jnp.square(yc - mu), axis=-1, keepdims=True)
    yn = (yc - mu) * lax.rsqrt(var + EPS) * lg_ref[...] + lb_ref[...]
    o_ref[:, 0:D_CONV] = (yn * _sigmoid(yn)).astype(o_ref.dtype)
    ybuf[0:CONV_HALO, :] = ybuf[ts:ts + CONV_HALO, :]

    xbuf[RG_HALO:RG_HALO + ts, :] = rest_ref[:, 2 * D_CONV:2 * D_CONV + D_RNN]
    xc = jnp.zeros((ts, D_RNN), F32)
    for k in range(RNN_CONV_K):
        xc = xc + rw_ref[k:k + 1, :] * xbuf[pl.ds(RG_HALO - (RNN_CONV_K - 1) + k, ts), :]
    xc = xc + rb_ref[...]
    xbuf[0:RG_HALO, :] = xbuf[ts:ts + RG_HALO, :]
    xcb = xc.astype(BF)
    r = _sigmoid(_dot(xcb, wr_ref[...]) + br_ref[...])
    gi = _sigmoid(_dot(xcb, wi_ref[...]) + bi_ref[...])
    nl = -lam_ref[...]
    sp = jnp.maximum(nl, 0.0) + jnp.log(1.0 + jnp.exp(-jnp.abs(nl)))
    log_a = -RG_C * r * sp
    a = jnp.exp(log_a)
    bt = jnp.sqrt(1.0 - jnp.exp(2.0 * log_a)) * (gi * xc)
    row = lax.broadcasted_iota(I32, (ts, 1), 0)
    sh = 1
    while sh < ts:
        live = row >= sh
        a_s = jnp.where(live, pltpu.roll(a, sh, 0), 1.0)
        b_s = jnp.where(live, pltpu.roll(bt, sh, 0), 0.0)
        bt = bt + a * b_s
        a = a * a_s
        sh *= 2
    h = bt + a * hc[...]
    hc[...] = h[ts - 1:ts, :]
    gate_in = rest_ref[:, 2 * D_CONV + D_RNN:2 * D_CONV + 2 * D_RNN]
    o_ref[:, D_CONV:D_CONV + D_RNN] = (h * _gelu(gate_in)).astype(o_ref.dtype)


def mixers(rest, p, batch, seq, ts=512):
    T = rest.shape[0]
    ns = seq // ts
    body = functools.partial(_mix_body, ts=ts)
    vec = lambda: pl.BlockSpec((1, D_CONV), lambda b, s: (0, 0))
    return pl.pallas_call(
        body, grid=(batch, ns),
        in_specs=[pl.BlockSpec((ts, N_REST), lambda b, s: (b * ns + s, 0)),
                  pl.BlockSpec((32, D_CONV), lambda b, s: (0, 0)), vec(), vec(), vec(),
                  pl.BlockSpec((8, D_RNN), lambda b, s: (0, 0)), vec(),
                  pl.BlockSpec((D_RNN, D_RNN), lambda b, s: (0, 0)), vec(),
                  pl.BlockSpec((D_RNN, D_RNN), lambda b, s: (0, 0)), vec(), vec()],
        out_specs=pl.BlockSpec((ts, D_CONV + D_RNN), lambda b, s: (b * ns + s, 0)),
        out_shape=jax.ShapeDtypeStruct((T, D_CONV + D_RNN), BF),
        scratch_shapes=[pltpu.VMEM((ts + CONV_HALO, D_CONV), F32), pltpu.VMEM((ts + RG_HALO, D_RNN), F32),
                        pltpu.VMEM((1, D_RNN), F32)],
        compiler_params=_cp(("arbitrary", "arbitrary")), name="conv_rglru",
    )(rest, p["cw"], p["cb"], p["lg"], p["lb"], p["rw"], p["rb"], p["wr"], p["br"], p["wi"], p["bi"], p["lam"])


def pack_pairs(a):
    half = a.shape[1] // 2
    r = lax.bitcast_convert_type(a, I32)
    r = r + jnp.int32(0x7FFF) + jnp.bitwise_and(lax.shift_right_logical(r, 16), 1)
    lo = lax.shift_right_logical(r[:, :half], 16)
    hi = jnp.bitwise_and(r[:, half:], jnp.int32(-65536))
    return jnp.bitwise_or(hi, lo)


def _outproj_body(x_ref, ya_ref, yc_ref, woa_ref, wob_ref, g2_ref, wq_ref, keys_ref, x1_ref, h2p_ref, st_ref):
    x1 = x_ref[...] + _dot(ya_ref[...], woa_ref[...]) + _dot(yc_ref[...], wob_ref[...])
    x1_ref[...] = x1
    h2 = x1 * lax.rsqrt(jnp.mean(x1 * x1, axis=-1, keepdims=True) + EPS) * g2_ref[...]
    h2p_ref[...] = pack_pairs(h2)
    q = _dot(h2.astype(BF), wq_ref[...]).astype(BF)
    for g in range(2 * PEER_HEADS):
        st_ref[g * N_KEYS:(g + 1) * N_KEYS, :] = _nt(keys_ref[g], q[:, g * D_HALF:(g + 1) * D_HALF])


def out_proj(x, ya, yc, p, g2, row0=0, tm=256):
    T = ya.shape[0]
    ng = 2 * PEER_HEADS
    blk0 = row0 // tm
    return pl.pallas_call(
        _outproj_body, grid=(T // tm,),
        in_specs=[pl.BlockSpec((tm, D_MODEL), lambda i: (i + blk0, 0)),
                  pl.BlockSpec((tm, D_ATT), lambda i: (i, 0)),
                  pl.BlockSpec((tm, D_CONV + D_RNN), lambda i: (i, 0)),
                  pl.BlockSpec((D_ATT, D_MODEL), lambda i: (0, 0)),
                  pl.BlockSpec((D_CONV + D_RNN, D_MODEL), lambda i: (0, 0)),
                  pl.BlockSpec((1, D_MODEL), lambda i: (0, 0)),
                  pl.BlockSpec((D_MODEL, ng * D_HALF), lambda i: (0, 0)),
                  pl.BlockSpec((ng, N_KEYS, D_HALF), lambda i: (0, 0, 0))],
        out_specs=[pl.BlockSpec((tm, D_MODEL), lambda i: (i, 0)),
                   pl.BlockSpec((tm, D_MODEL // 2), lambda i: (i, 0)),
                   pl.BlockSpec((ng * N_KEYS, tm), lambda i: (0, i))],
        out_shape=[jax.ShapeDtypeStruct((T, D_MODEL), F32),
                   jax.ShapeDtypeStruct((T, D_MODEL // 2), I32),
                   jax.ShapeDtypeStruct((ng * N_KEYS, T), F32)],
        compiler_params=_cp(("parallel",)), name="out_proj_peer_scores",
    )(x, ya, yc, p["woa"], p["wob"], g2, p["wq"], p["keys"])


BIG_ID = 1 << 20
SUBL = 8
NPAR = 4
SEL_CHAIN = 4


def _take_rounds(problems, nrounds):
    state = [list(slabs) for slabs, _ in problems]
    res = [([], []) for _ in problems]
    for _ in range(nrounds):
        for pi, (_, ids) in enumerate(problems):
            slabs = state[pi]
            m8 = slabs[0]
            for sl in slabs[1:]:
                m8 = jnp.maximum(m8, sl)
            m = jnp.max(m8, axis=0, keepdims=True)
            chains = []
            for c0 in range(0, len(slabs), SEL_CHAIN):
                v = jnp.full((SUBL, LANES), BIG_ID, I32)
                for sl, idc in zip(reversed(slabs[c0:c0 + SEL_CHAIN]), reversed(ids[c0:c0 + SEL_CHAIN])):
                    v = jnp.where(sl == m, idc, v)
                chains.append(v)
            while len(chains) > 1:
                chains = [jnp.minimum(chains[i], chains[i + 1]) if i + 1 < len(chains) else chains[i]
                          for i in range(0, len(chains), 2)]
            pick = jnp.min(chains[0], axis=0, keepdims=True)
            state[pi] = [jnp.where(idc == pick, NEG, sl) for sl, idc in zip(slabs, ids)]
            res[pi][0].append(m)
            res[pi][1].append(pick)
    return res


def _route_body(st_ref, e_ref, g_ref, v_scr, i_scr, sv_scr, ci_scr, et_scr, gt_scr):
    ng = 2 * PEER_HEADS
    sub = lax.broadcasted_iota(I32, (SUBL, LANES), 0)
    key_ids = [sub + SUBL * i for i in range(N_KEYS // SUBL)]

    def stage1(gg, carry):
        probs = []
        for q in range(NPAR):
            base = pl.multiple_of((gg * NPAR + q) * N_KEYS, N_KEYS)
            probs.append(([st_ref[pl.ds(base + SUBL * i, SUBL), :] for i in range(N_KEYS // SUBL)], key_ids))
        for q, (vals, picks) in enumerate(_take_rounds(probs, TOPK)):
            for r in range(TOPK):
                v_scr[gg * NPAR + q, r:r + 1, :] = vals[r]
                i_scr[gg * NPAR + q, r:r + 1, :] = picks[r]
        return carry

    lax.fori_loop(0, ng // NPAR, stage1, 0)

    def stage2(hh, carry):
        probs = []
        for q in range(NPAR):
            h = hh * NPAR + q
            v1 = v_scr[2 * h]
            v2 = v_scr[2 * h + 1]
            slabs = [v1[0:1, :] + v2[0:SUBL, :], v1[0:1, :] + v2[SUBL:TOPK, :]]
            ids = [sub, sub + SUBL]
            for i in range(1, TOPK):
                nj = TOPK // (i + 1)
                slabs.append(jnp.where(sub < nj, v1[i:i + 1, :] + v2[0:SUBL, :], NEG))
                ids.append(sub + i * TOPK)
            probs.append((slabs, ids))
        for q, (vals, picks) in enumerate(_take_rounds(probs, TOPK)):
            h = hh * NPAR + q
            i1 = i_scr[2 * h]
            i2 = i_scr[2 * h + 1]
            for r in range(TOPK):
                sv_scr[q, r:r + 1, :] = vals[r]
                ci_scr[q, r:r + 1, :] = picks[r]
            sv = sv_scr[q]
            ci = ci_scr[q]
            ci_hi = lax.shift_right_logical(ci, 4)
            ci_lo = jnp.bitwise_and(ci, TOPK - 1)
            e1 = jnp.zeros((TOPK, LANES), I32)
            e2 = jnp.zeros((TOPK, LANES), I32)
            for i in range(TOPK):
                e1 = jnp.where(ci_hi == i, i1[i:i + 1, :], e1)
                e2 = jnp.where(ci_lo == i, i2[i:i + 1, :], e2)
            p = jnp.exp(sv - sv[0:1, :])
            gates = p / jnp.sum(p, axis=0, keepdims=True)
            et_scr[pl.ds(pl.multiple_of(h * TOPK, TOPK), TOPK), :] = e1 * N_KEYS + e2
            gt_scr[pl.ds(pl.multiple_of(h * TOPK, TOPK), TOPK), :] = gates
        return carry

    lax.fori_loop(0, PEER_HEADS // NPAR, stage2, 0)
    e_ref[...] = et_scr[...].T
    g_ref[...] = gt_scr[...].T


def route(st):
    T = st.shape[1]
    ng = 2 * PEER_HEADS
    return pl.pallas_call(
        _route_body, grid=(T // LANES,),
        in_specs=[pl.BlockSpec((ng * N_KEYS, LANES), lambda i: (0, i))],
        out_specs=[pl.BlockSpec((LANES, KSEL), lambda i: (i, 0)),
                   pl.BlockSpec((LANES, KSEL), lambda i: (i, 0))],
        out_shape=[jax.ShapeDtypeStruct((T, KSEL), I32),
                   jax.ShapeDtypeStruct((T, KSEL), F32)],
        scratch_shapes=[pltpu.VMEM((ng, TOPK, LANES), F32), pltpu.VMEM((ng, TOPK, LANES), I32),
                        pltpu.VMEM((NPAR, TOPK, LANES), F32), pltpu.VMEM((NPAR, TOPK, LANES), I32),
                        pltpu.VMEM((KSEL, LANES), I32), pltpu.VMEM((KSEL, LANES), F32)],
        compiler_params=_cp(("parallel",)), name="peer_route",
    )(st)


NC, NS, L = 2, 16, 16
NW = NC * NS
NJ = D_MODEL // L
R = TOPK
NCH = KSEL // R
SUB = 2
RD = SUB * R
G = 16
DW = D_MODEL // 2


def _perm(x, idx):
    return jnp.take_along_axis(x, idx, axis=0, mode="promise_in_bounds")


def _halves(w):
    lo = lax.bitcast_convert_type(lax.shift_left(w, 16), F32)
    hi = lax.bitcast_convert_type(jnp.bitwise_and(w, jnp.int32(-65536)), F32)
    return lo, hi


def _bf(w):
    return plsc.bitcast(w, BF)


def peer_sc(x, resid, idx, gates, uv_tab):
    T = x.shape[0]
    tpw = T // NW
    ngroups = tpw // G
    nchunks = G * NCH // SUB
    idx3 = idx.reshape(T * NCH // SUB, RD)
    g3 = gates.reshape(T * NCH, R)
    mesh = plsc.VectorSubcoreMesh(core_axis_name="c", subcore_axis_name="s")

    @functools.partial(
        pl.kernel, mesh=mesh,
        out_type=jax.ShapeDtypeStruct((T, D_MODEL), F32),
        scratch_types=[
            pltpu.VMEM((G, DW), I32),
            pltpu.VMEM((G, D_MODEL), F32),
            pltpu.VMEM((nchunks, RD), I32),
            pltpu.VMEM((G * NCH, R), F32),
            pltpu.VMEM((RD, 2 * DW), I32),
            pltpu.VMEM((RD, 2 * DW), I32),
            pltpu.SemaphoreType.DMA,
            pltpu.SemaphoreType.DMA,
        ],
        compiler_params=pltpu.CompilerParams(needs_layout_passes=False),
        name="peer_experts_sc",
    )
    def k(x_hbm, r_hbm, idx_hbm, g_hbm, uv_hbm, out_hbm, x_v, out_v, idx_v, g_v, buf0, buf1, sem0, sem1):
        wid = lax.axis_index("s") * NC + lax.axis_index("c")
        bufs, sems = (buf0, buf1), (sem0, sem1)
        iota = lax.iota(I32, L)

        def gather_copy(c, b):
            return pltpu.make_async_copy(uv_hbm.at[idx_v.at[c]], bufs[b], sems[b])

        def issue(c, b):
            gather_copy(c, b).start()

        def wait(c, b):
            gather_copy(c, b).wait()

        def compute(c, b, ro):
            ub = vb = bufs[b]
            t = c // NCH

            def ubody(mm, accs):
                x0 = _bf(x_v[t, pl.ds(mm * 2 * L, L)])
                x1 = _bf(x_v[t, pl.ds(mm * 2 * L + L, L)])
                out = []
                for kk in range(R):
                    pr = x0 * _bf(ub[ro + kk, pl.ds(mm * 2 * L, L)]) + x1 * _bf(ub[ro + kk, pl.ds(mm * 2 * L + L, L)])
                    lo, hi = _halves(plsc.bitcast(pr, I32))
                    out.append(accs[kk] + (lo + hi))
                return tuple(out)

            accs = lax.fori_loop(0, NJ // 4, ubody, tuple(jnp.zeros((L,), F32) for _ in range(R)))
            vecs = list(accs)
            dist = L // 2
            while dist >= 1:
                pidx = jnp.bitwise_xor(iota, dist)
                low = jnp.bitwise_and(iota, dist) == 0
                nxt = []
                for kk in range(dist):
                    a = vecs[kk]
                    bvec = vecs[kk + dist]
                    a = a + _perm(a, pidx)
                    bvec = bvec + _perm(bvec, pidx)
                    nxt.append(jnp.where(low, a, bvec))
                vecs = nxt
                dist //= 2
            hid = vecs[0]
            z = GC * (hid + 0.044715 * hid * hid * hid)
            gel = hid / (1.0 + jnp.exp(-2.0 * z))
            w = g_v[c, :] * gel
            wbs = []
            for kk in range(R):
                wb = _perm(w, jnp.full((L,), kk, I32))
                wbs.append(plsc.pack(wb, wb, format=plsc.PackFormat.INTERLEAVED))

            @plsc.parallel_loop(0, NJ // 2)
            def _(m):
                pr = [wbs[kk] * _bf(vb[ro + kk, pl.ds(DW + m * L, L)]) for kk in range(R)]
                for _lvl in range(2):
                    pr = [pr[i] + pr[i + 1] for i in range(0, len(pr), 2)]
                los, his = [], []
                for q in pr:
                    lo, hi = _halves(plsc.bitcast(q, I32))
                    los.append(lo)
                    his.append(hi)
                while len(los) > 1:
                    los = [los[i] + los[i + 1] for i in range(0, len(los), 2)]
                    his = [his[i] + his[i + 1] for i in range(0, len(his), 2)]
                out_v[t, pl.ds(m * L, L)] = out_v[t, pl.ds(m * L, L)] + los[0]
                out_v[t, pl.ds(DW + m * L, L)] = out_v[t, pl.ds(DW + m * L, L)] + his[0]

        def group(g, carry):
            tok0 = wid * tpw + g * G
            pltpu.sync_copy(x_hbm.at[pl.ds(tok0, G)], x_v)
            pltpu.sync_copy(r_hbm.at[pl.ds(tok0, G)], out_v)
            pltpu.sync_copy(idx_hbm.at[pl.ds(tok0 * (NCH // SUB), nchunks)], idx_v)
            pltpu.sync_copy(g_hbm.at[pl.ds(tok0 * NCH, G * NCH)], g_v)
            issue(0, 0)

            def cbody(cc, c2):
                c = cc * 2
                issue(c + 1, 1)
                wait(c, 0)
                for sub in range(SUB):
                    compute(c * SUB + sub, 0, sub * R)

                @pl.when(c + 2 < nchunks)
                def _():
                    issue(c + 2, 0)

                wait(c + 1, 1)
                for sub in range(SUB):
                    compute((c + 1) * SUB + sub, 1, sub * R)
                return c2

            lax.fori_loop(0, nchunks // 2, cbody, 0)
            pltpu.sync_copy(out_v, out_hbm.at[pl.ds(tok0, G)])
            return carry

        lax.fori_loop(0, ngroups, group, 0)

    return k(x, resid, idx3, g3, uv_tab)


def _fn_body(x_ref, g_ref, o_ref):
    xf = x_ref[...]
    o_ref[...] = xf * lax.rsqrt(jnp.mean(xf * xf, axis=-1, keepdims=True) + EPS) * g_ref[...]


def _fn_body_into(x_ref, g_ref, prev_ref, o_ref):
    del prev_ref
    _fn_body(x_ref, g_ref, o_ref)


def final_norm(x, g, out, row0, t_total, tm=1024):
    T, d = x.shape
    blk0 = row0 // tm
    common = dict(grid=(T // tm,), out_specs=pl.BlockSpec((tm, d), lambda i: (i + blk0, 0)),
                  out_shape=jax.ShapeDtypeStruct((t_total, d), F32),
                  compiler_params=_cp(("parallel",)), name="final_norm")
    specs = [pl.BlockSpec((tm, d), lambda i: (i, 0)), pl.BlockSpec((1, d), lambda i: (0, 0))]
    if out is None:
        return pl.pallas_call(_fn_body, in_specs=specs, **common)(x, g)
    return pl.pallas_call(_fn_body_into, in_specs=specs + [pl.BlockSpec(memory_space=pl.ANY)],
                          input_output_aliases={2: 0}, **common)(x, g, out)


def _pack_body(u_ref, v_ref, o_ref):
    o_ref[:, 0:D_MODEL // 2] = pack_pairs(u_ref[...])
    o_ref[:, D_MODEL // 2:D_MODEL] = pack_pairs(v_ref[...])


def pack_tables(u, v, tm=512):
    e, d = u.shape
    return pl.pallas_call(
        _pack_body, grid=(e // tm,),
        in_specs=[pl.BlockSpec((tm, d), lambda i: (i, 0)), pl.BlockSpec((tm, d), lambda i: (i, 0))],
        out_specs=pl.BlockSpec((tm, d), lambda i: (i, 0)),
        out_shape=jax.ShapeDtypeStruct((e, d), I32),
        compiler_params=_cp(("parallel",)), name="pack_tables",
    )(u, v)


def _prep_layer(w_in, b_forget, conv_dw_w, conv_dw_b, conv_ln_g, conv_ln_b, rg_conv_w, rg_conv_b,
                rg_w_r, rg_b_r, rg_w_i, rg_b_i, rg_lambda, w_out, peer_wq, peer_k1, peer_k2):
    f0 = 3 * D_ATT
    wf = jnp.zeros((D_MODEL, FPAD), BF).at[:, 0:ATT_HEADS].set(w_in[:, f0:f0 + ATT_HEADS].astype(BF))
    bfg = jnp.zeros((1, FPAD), F32).at[0, 0:ATT_HEADS].set(b_forget)
    cw = jnp.zeros((32, D_CONV), F32).at[0:CONV_K].set(conv_dw_w)
    rw = jnp.zeros((8, D_RNN), F32).at[0:RNN_CONV_K].set(rg_conv_w)
    bd = lambda w: jax.scipy.linalg.block_diag(*[w[i] for i in range(RNN_BLOCKS)]).astype(BF)
    row = lambda v: v.reshape(1, -1).astype(F32)
    keys = jnp.stack([peer_k1, peer_k2], axis=1).reshape(2 * PEER_HEADS, N_KEYS, D_HALF).astype(BF)
    return dict(wqk=w_in[:, 0:2 * D_ATT].astype(BF), wvt=w_in[:, 2 * D_ATT:f0].T.astype(BF), wf=wf,
                wrest=w_in[:, f0 + ATT_HEADS:].astype(BF), bfg=bfg,
                cw=cw, cb=row(conv_dw_b), lg=row(conv_ln_g), lb=row(conv_ln_b),
                rw=rw, rb=row(rg_conv_b), wr=bd(rg_w_r), br=row(rg_b_r), wi=bd(rg_w_i), bi=row(rg_b_i),
                lam=row(rg_lambda), woa=w_out[0:D_ATT].astype(BF), wob=w_out[D_ATT:].astype(BF),
                wq=peer_wq.astype(BF), keys=keys)


def kernel(x, norm1_g, w_in, b_forget, conv_dw_w, conv_dw_b, conv_ln_g, conv_ln_b,
           rg_conv_w, rg_conv_b, rg_w_r, rg_b_r, rg_w_i, rg_b_i, rg_lambda, w_out,
           norm2_g, peer_wq, peer_k1, peer_k2, peer_u, peer_v, final_g):
    b, s, d = x.shape
    params = [_prep_layer(w_in[l], b_forget[l], conv_dw_w[l], conv_dw_b[l], conv_ln_g[l], conv_ln_b[l],
                          rg_conv_w[l], rg_conv_b[l], rg_w_r[l], rg_b_r[l], rg_w_i[l], rg_b_i[l], rg_lambda[l],
                          w_out[l], peer_wq[l], peer_k1[l], peer_k2[l]) for l in range(DEPTH)]
    tabs = [pack_tables(peer_u[l], peer_v[l]) for l in range(DEPTH)]
    bs = b // N_SLICES
    T = bs * s
    xf = x.reshape(b * s, d)
    xs = [None] * N_SLICES
    prev = None
    for l in range(DEPTH):
        p = params[l]
        for i in range(N_SLICES):
            src, row0 = (xf, i * T) if l == 0 else (xs[i], 0)
            tie_tab = l == 0 and i < DEPTH
            deps = ([prev] if prev is not None else []) + ([tabs[i]] if tie_tab else [])
            if deps:
                tied = lax.optimization_barrier((src, *deps))
                src = tied[0]
                if prev is not None:
                    prev = tied[1]
                if tie_tab:
                    tabs[i] = tied[-1]
            qkb, vt, rest = in_proj(src, norm1_g[l].reshape(1, d), p, s, T, row0)
            y_att = attention(qkb, vt, bs, s)
            y_cr = mixers(rest, p, bs, s)
            x1, h2p, st = out_proj(src, y_att, y_cr, p, norm2_g[l].reshape(1, d), row0)
            experts, gates = route(st)
            prev = experts
            xs[i] = peer_sc(h2p, x1, experts, gates, tabs[l])
    out = None
    for i in range(N_SLICES):
        out = final_norm(xs[i], final_g.reshape(1, d), out, i * T, b * s)
    return out.reshape(b, s, d)
```

```python
import functools
import math

import jax
import jax.numpy as jnp
from jax import lax
from jax.experimental import pallas as pl
from jax.experimental.pallas import tpu as pltpu
from jax.experimental.pallas import tpu_sc as plsc

BF = jnp.bfloat16
F32 = jnp.float32
I32 = jnp.int32

D_MODEL = 1024
DEPTH = 2
ATT_HEADS = 8
ATT_HD = 64
D_ATT = ATT_HEADS * ATT_HD
D_CONV = 256
CONV_K = 31
D_RNN = 256
RNN_BLOCKS = 4
RNN_CONV_K = 4
RG_C = 8.0
EPS = 1e-6
N_REST = 2 * D_CONV + 2 * D_RNN
PEER_HEADS = 8
N_KEYS = 128
D_HALF = 128
TOPK = 16
KSEL = PEER_HEADS * TOPK
GC = 0.7978845608028654
NEG = float("-inf")

N_SLICES = 4
LANES = 128
VMEM_LIMIT = 48 * 1024 * 1024


def _cp(sem):
    return pltpu.CompilerParams(dimension_semantics=sem, vmem_limit_bytes=VMEM_LIMIT)


def _split3(x):
    hi = x.astype(BF)
    r = x - hi.astype(F32)
    mid = r.astype(BF)
    lo = (r - mid.astype(F32)).astype(BF)
    return hi, mid, lo


def _nt(a, b):
    return lax.dot_general(a, b, (((1,), (1,)), ((), ())), preferred_element_type=F32)


def _dot(a, b):
    return jnp.dot(a, b, preferred_element_type=F32)


def _sigmoid(x):
    return 1.0 / (1.0 + jnp.exp(-x))


def _gelu(x):
    return 0.5 * x * (1.0 + jnp.tanh(GC * (x + 0.044715 * x * x * x)))


LOG2E = 1.4426950408889634
NSPLIT = 3
FPAD = 16


def _inproj_body(x_ref, g_ref, wqk_ref, wvt_ref, wf_ref, wrest_ref, bf_ref, tri_ref, place_ref,
                 qkb_ref, vt_ref, rest_ref, carry_ref, *, blocks_per_seq, tm):
    i = pl.program_id(0)
    x = x_ref[...]
    h = x * lax.rsqrt(jnp.mean(x * x, axis=-1, keepdims=True) + EPS) * g_ref[...]
    hb = h.astype(BF)
    qk = _dot(hb, wqk_ref[...])
    col = lax.broadcasted_iota(I32, (1, 2 * D_ATT), 1)
    qk = jnp.where(col < D_ATT, qk * (LOG2E / math.sqrt(ATT_HD)), qk)
    qkb_ref[:, 0:2 * D_ATT] = qk.astype(BF)
    vt_ref[...] = _nt(wvt_ref[...], hb).astype(BF)
    rest_ref[...] = _dot(hb, wrest_ref[...])
    ft = _dot(hb, wf_ref[...]) + bf_ref[...]
    lf = jnp.minimum(ft, 0.0) - jnp.log(1.0 + jnp.exp(-jnp.abs(ft)))
    hi, mid, lo = _split3(lf)
    tri = tri_ref[...]
    cs = _dot(tri, hi) + _dot(tri, mid) + _dot(tri, lo)

    @pl.when(i % blocks_per_seq == 0)
    def _():
        carry_ref[...] = jnp.zeros_like(carry_ref)

    cum = cs + carry_ref[...]
    carry_ref[...] = cum[tm - 1:tm, :]
    pieces = _split3(cum * (-LOG2E))
    kb = _dot(pieces[0], place_ref[0]) + _dot(pieces[1], place_ref[1]) + _dot(pieces[2], place_ref[2])
    qkb_ref[:, 2 * D_ATT:3 * D_ATT] = kb.astype(BF)


def bias_lane(hh, j):
    return (ATT_HD if hh == 0 else 0) + j


def in_proj(x, g, p, seq, T, row0=0, tm=512):
    blk0 = row0 // tm
    tri =(lax.broadcasted_iota(I32, (tm, tm), 0) >= lax.broadcasted_iota(I32, (tm, tm), 1)).astype(BF)
    shp = (NSPLIT, FPAD, D_ATT)
    hd = lax.broadcasted_iota(I32, shp, 1)
    target = (hd // 2) * LANES + jnp.where(hd % 2 == 0, ATT_HD, 0) + lax.broadcasted_iota(I32, shp, 0)
    place = ((lax.broadcasted_iota(I32, shp, 2) == target) & (hd < ATT_HEADS)).astype(BF)
    body = functools.partial(_inproj_body, blocks_per_seq=seq // tm, tm=tm)
    return pl.pallas_call(
        body, grid=(T // tm,),
        in_specs=[pl.BlockSpec((tm, D_MODEL), lambda i: (i + blk0, 0)),
                  pl.BlockSpec((1, D_MODEL), lambda i: (0, 0)),
                  pl.BlockSpec((D_MODEL, 2 * D_ATT), lambda i: (0, 0)),
                  pl.BlockSpec((D_ATT, D_MODEL), lambda i: (0, 0)),
                  pl.BlockSpec((D_MODEL, FPAD), lambda i: (0, 0)),
                  pl.BlockSpec((D_MODEL, N_REST), lambda i: (0, 0)),
                  pl.BlockSpec((1, FPAD), lambda i: (0, 0)),
                  pl.BlockSpec((tm, tm), lambda i: (0, 0)),
                  pl.BlockSpec(shp, lambda i: (0, 0, 0))],
        out_specs=[pl.BlockSpec((tm, 3 * D_ATT), lambda i: (i, 0)),
                   pl.BlockSpec((D_ATT, tm), lambda i: (0, i)),
                   pl.BlockSpec((tm, N_REST), lambda i: (i, 0))],
        out_shape=[jax.ShapeDtypeStruct((T, 3 * D_ATT), BF),
                   jax.ShapeDtypeStruct((D_ATT, T), BF),
                   jax.ShapeDtypeStruct((T, N_REST), F32)],
        scratch_shapes=[pltpu.VMEM((1, FPAD), F32)],
        compiler_params=_cp(("arbitrary",)), name="in_proj",
    )(x, g, p["wqk"], p["wvt"], p["wf"], p["wrest"], p["bfg"], tri, place)


def _attn_body(q_ref, k_ref, kb_ref, vt_ref, o_ref, m_ref, acc_ref, *, tq, tk):
    qi = pl.program_id(2)
    ki = pl.program_id(3)

    @pl.when(ki == 0)
    def _():
        m_ref[...] = jnp.full_like(m_ref, NEG)
        acc_ref[...] = jnp.zeros_like(acc_ref)

    lane = lax.broadcasted_iota(I32, (1, LANES), 1)
    first = lane < ATT_HD
    vrow = lax.broadcasted_iota(I32, (LANES, 1), 0) < ATT_HD

    def step(masked):
        q = q_ref[...]
        k = k_ref[...]
        kb = kb_ref[...]
        vt = vt_ref[...]
        if masked:
            keep = (lax.broadcasted_iota(I32, (tk, tq), 0) <= lax.broadcasted_iota(I32, (tk, tq), 1))
        for hh in range(2):
            own = first if hh == 0 else jnp.logical_not(first)
            ones = (lane >= bias_lane(hh, 0)) & (lane < bias_lane(hh, NSPLIT))
            qa = jnp.where(own, q, jnp.where(ones, 1.0, 0.0).astype(BF))
            ka = jnp.where(own, k, kb)
            st = _nt(ka, qa)
            if masked:
                st = jnp.where(keep, st, NEG)
            m_prev = m_ref[hh]
            m_new = jnp.maximum(m_prev, jnp.max(st, axis=0, keepdims=True))
            alpha = jnp.exp2(m_prev - m_new)
            p = jnp.exp2(st - m_new).astype(BF)
            m_ref[hh] = m_new
            vown = vrow if hh == 0 else jnp.logical_not(vrow)
            va = jnp.where(vown, vt, jnp.ones_like(vt))
            acc_ref[hh] = alpha * acc_ref[hh] + _dot(va, p)

    @pl.when(ki < qi)
    def _():
        step(False)

    @pl.when(ki == qi)
    def _():
        step(True)
        a0 = acc_ref[0]
        a1 = acc_ref[1]
        ot = jnp.where(vrow, a0 / a0[ATT_HD:ATT_HD + 1, :], a1 / a1[0:1, :])
        o_ref[...] = ot.T.astype(o_ref.dtype)


def attention(qkb, vt, batch, seq, tq=512):
    T = qkb.shape[0]
    tk = tq
    nq = seq // tq
    npair = ATT_HEADS // 2
    body = functools.partial(_attn_body, tq=tq, tk=tk)
    kblk = lambda b, qi, ki: b * nq + jnp.minimum(ki, qi)
    return pl.pallas_call(
        body, grid=(batch, npair, nq, nq),
        in_specs=[pl.BlockSpec((tq, LANES), lambda b, p, qi, ki: (b * nq + qi, p)),
                  pl.BlockSpec((tk, LANES), lambda b, p, qi, ki: (kblk(b, qi, ki), npair + p)),
                  pl.BlockSpec((tk, LANES), lambda b, p, qi, ki: (kblk(b, qi, ki), 2 * npair + p)),
                  pl.BlockSpec((LANES, tk), lambda b, p, qi, ki: (p, kblk(b, qi, ki)))],
        out_specs=pl.BlockSpec((tq, LANES), lambda b, p, qi, ki: (b * nq + qi, p)),
        out_shape=jax.ShapeDtypeStruct((T, D_ATT), BF),
        scratch_shapes=[pltpu.VMEM((2, 1, tq), F32), pltpu.VMEM((2, LANES, tq), F32)],
        compiler_params=_cp(("parallel", "parallel", "parallel", "arbitrary")), name="fox_attention",
    )(qkb, qkb, qkb, vt)


CONV_HALO = 32
RG_HALO = 8


def _mix_body(rest_ref, cw_ref, cb_ref, lg_ref, lb_ref, rw_ref, rb_ref, wr_ref, br_ref, wi_ref, bi_ref, lam_ref,
              o_ref, ybuf, xbuf, hc, *, ts):
    si = pl.program_id(1)

    @pl.when(si == 0)
    def _():
        ybuf[0:CONV_HALO, :] = jnp.zeros((CONV_HALO, D_CONV), F32)
        xbuf[0:RG_HALO, :] = jnp.zeros((RG_HALO, D_RNN), F32)
        hc[...] = jnp.zeros_like(hc)

    y = rest_ref[:, 0:D_CONV] * _sigmoid(rest_ref[:, D_CONV:2 * D_CONV])
    ybuf[CONV_HALO:CONV_HALO + ts, :] = y
    acc = jnp.zeros((ts, D_CONV), F32)
    for k in range(CONV_K):
        acc = acc + cw_ref[k:k + 1, :] * ybuf[pl.ds(CONV_HALO - (CONV_K - 1) + k, ts), :]
    yc = acc + cb_ref[...]
    mu = jnp.mean(yc, axis=-1, keepdims=True)
    var = jnp.mean(jnp.square(yc - mu), axis=-1, keepdims=True)
    yn = (yc - mu) * lax.rsqrt(var + EPS) * lg_ref[...] + lb_ref[...]
    o_ref[:, 0:D_CONV] = (yn * _sigmoid(yn)).astype(o_ref.dtype)
    ybuf[0:CONV_HALO, :] = ybuf[ts:ts + CONV_HALO, :]

    xbuf[RG_HALO:RG_HALO + ts, :] = rest_ref[:, 2 * D_CONV:2 * D_CONV + D_RNN]
    xc = jnp.zeros((ts, D_RNN), F32)
    for k in range(RNN_CONV_K):
        xc = xc + rw_ref[k:k + 1, :] * xbuf[pl.ds(RG_HALO - (RNN_CONV_K - 1) + k, ts), :]
    xc = xc + rb_ref[...]
    xbuf[0:RG_HALO, :] = xbuf[ts:ts + RG_HALO, :]
    xcb = xc.astype(BF)
    r = _sigmoid(_dot(xcb, wr_ref[...]) + br_ref[...])
    gi = _sigmoid(_dot(xcb, wi_ref[...]) + bi_ref[...])
    nl = -lam_ref[...]
    sp = jnp.maximum(nl, 0.0) + jnp.log(1.0 + jnp.exp(-jnp.abs(nl)))
    log_a = -RG_C * r * sp
    a = jnp.exp(log_a)
    bt = jnp.sqrt(1.0 - jnp.exp(2.0 * log_a)) * (gi * xc)
    row = lax.broadcasted_iota(I32, (ts, 1), 0)
    sh = 1
    while sh < ts:
        live = row >= sh
        a_s = jnp.where(live, pltpu.roll(a, sh, 0), 1.0)
        b_s = jnp.where(live, pltpu.roll(bt, sh, 0), 0.0)
        bt = bt + a * b_s
        a = a * a_s
        sh *= 2
    h = bt + a * hc[...]
    hc[...] = h[ts - 1:ts, :]
    gate_in = rest_ref[:, 2 * D_CONV + D_RNN:2 * D_CONV + 2 * D_RNN]
    o_ref[:, D_CONV:D_CONV + D_RNN] = (h * _gelu(gate_in)).astype(o_ref.dtype)


def mixers(rest, p, batch, seq, ts=512):
    T = rest.shape[0]
    ns = seq // ts
    body = functools.partial(_mix_body, ts=ts)
    vec = lambda: pl.BlockSpec((1, D_CONV), lambda b, s: (0, 0))
    return pl.pallas_call(
        body, grid=(batch, ns),
        in_specs=[pl.BlockSpec((ts, N_REST), lambda b, s: (b * ns + s, 0)),
                  pl.BlockSpec((32, D_CONV), lambda b, s: (0, 0)), vec(), vec(), vec(),
                  pl.BlockSpec((8, D_RNN), lambda b, s: (0, 0)), vec(),
                  pl.BlockSpec((D_RNN, D_RNN), lambda b, s: (0, 0)), vec(),
                  pl.BlockSpec((D_RNN, D_RNN), lambda b, s: (0, 0)), vec(), vec()],
        out_specs=pl.BlockSpec((ts, D_CONV + D_RNN), lambda b, s: (b * ns + s, 0)),
        out_shape=jax.ShapeDtypeStruct((T, D_CONV + D_RNN), BF),
        scratch_shapes=[pltpu.VMEM((ts + CONV_HALO, D_CONV), F32), pltpu.VMEM((ts + RG_HALO, D_RNN), F32),
                        pltpu.VMEM((1, D_RNN), F32)],
        compiler_params=_cp(("arbitrary", "arbitrary")), name="conv_rglru",
    )(rest, p["cw"], p["cb"], p["lg"], p["lb"], p["rw"], p["rb"], p["wr"], p["br"], p["wi"], p["bi"], p["lam"])


def pack_pairs(a):
    half = a.shape[1] // 2
    r = lax.bitcast_convert_type(a, I32)
    r = r + jnp.int32(0x7FFF) + jnp.bitwise_and(lax.shift_right_logical(r, 16), 1)
    lo = lax.shift_right_logical(r[:, :half], 16)
    hi = jnp.bitwise_and(r[:, half:], jnp.int32(-65536))
    return jnp.bitwise_or(hi, lo)


def _outproj_body(x_ref, ya_ref, yc_ref, woa_ref, wob_ref, g2_ref, wq_ref, keys_ref, x1_ref, h2p_ref, st_ref):
    x1 = x_ref[...] + _dot(ya_ref[...], woa_ref[...]) + _dot(yc_ref[...], wob_ref[...])
    x1_ref[...] = x1
    h2 = x1 * lax.rsqrt(jnp.mean(x1 * x1, axis=-1, keepdims=True) + EPS) * g2_ref[...]
    h2p_ref[...] = pack_pairs(h2)
    q = _dot(h2.astype(BF), wq_ref[...]).astype(BF)
    for g in range(2 * PEER_HEADS):
        st_ref[g * N_KEYS:(g + 1) * N_KEYS, :] = _nt(keys_ref[g], q[:, g * D_HALF:(g + 1) * D_HALF])


def out_proj(x, ya, yc, p, g2, row0=0, tm=256):
    T = ya.shape[0]
    ng = 2 * PEER_HEADS
    blk0 = row0 // tm
    return pl.pallas_call(
        _outproj_body, grid=(T // tm,),
        in_specs=[pl.BlockSpec((tm, D_MODEL), lambda i: (i + blk0, 0)),
                  pl.BlockSpec((tm, D_ATT), lambda i: (i, 0)),
                  pl.BlockSpec((tm, D_CONV + D_RNN), lambda i: (i, 0)),
                  pl.BlockSpec((D_ATT, D_MODEL), lambda i: (0, 0)),
                  pl.BlockSpec((D_CONV + D_RNN, D_MODEL), lambda i: (0, 0)),
                  pl.BlockSpec((1, D_MODEL), lambda i: (0, 0)),
                  pl.BlockSpec((D_MODEL, ng * D_HALF), lambda i: (0, 0)),
                  pl.BlockSpec((ng, N_KEYS, D_HALF), lambda i: (0, 0, 0))],
        out_specs=[pl.BlockSpec((tm, D_MODEL), lambda i: (i, 0)),
                   pl.BlockSpec((tm, D_MODEL // 2), lambda i: (i, 0)),
                   pl.BlockSpec((ng * N_KEYS, tm), lambda i: (0, i))],
        out_shape=[jax.ShapeDtypeStruct((T, D_MODEL), F32),
                   jax.ShapeDtypeStruct((T, D_MODEL // 2), I32),
                   jax.ShapeDtypeStruct((ng * N_KEYS, T), F32)],
        compiler_params=_cp(("parallel",)), name="out_proj_peer_scores",
    )(x, ya, yc, p["woa"], p["wob"], g2, p["wq"], p["keys"])


BIG_ID = 1 << 20
SUBL = 8
NPAR = 4
SEL_CHAIN = 4


def _take_rounds(problems, nrounds):
    state = [list(slabs) for slabs, _ in problems]
    res = [([], []) for _ in problems]
    for _ in range(nrounds):
        for pi, (_, ids) in enumerate(problems):
            slabs = state[pi]
            m8 = slabs[0]
            for sl in slabs[1:]:
                m8 = jnp.maximum(m8, sl)
            m = jnp.max(m8, axis=0, keepdims=True)
            chains = []
            for c0 in range(0, len(slabs), SEL_CHAIN):
                v = jnp.full((SUBL, LANES), BIG_ID, I32)
                for sl, idc in zip(reversed(slabs[c0:c0 + SEL_CHAIN]), reversed(ids[c0:c0 + SEL_CHAIN])):
                    v = jnp.where(sl == m, idc, v)
                chains.append(v)
            while len(chains) > 1:
                chains = [jnp.minimum(chains[i], chains[i + 1]) if i + 1 < len(chains) else chains[i]
                          for i in range(0, len(chains), 2)]
            pick = jnp.min(chains[0], axis=0, keepdims=True)
            state[pi] = [jnp.where(idc == pick, NEG, sl) for sl, idc in zip(slabs, ids)]
            res[pi][0].append(m)
            res[pi][1].append(pick)
    return res


def _route_body(st_ref, e_ref, g_ref, v_scr, i_scr, sv_scr, ci_scr, et_scr, gt_scr):
    ng = 2 * PEER_HEADS
    sub = lax.broadcasted_iota(I32, (SUBL, LANES), 0)
    key_ids = [sub + SUBL * i for i in range(N_KEYS // SUBL)]

    def stage1(gg, carry):
        probs = []
        for q in range(NPAR):
            base = pl.multiple_of((gg * NPAR + q) * N_KEYS, N_KEYS)
            probs.append(([st_ref[pl.ds(base + SUBL * i, SUBL), :] for i in range(N_KEYS // SUBL)], key_ids))
        for q, (vals, picks) in enumerate(_take_rounds(probs, TOPK)):
            for r in range(TOPK):
                v_scr[gg * NPAR + q, r:r + 1, :] = vals[r]
                i_scr[gg * NPAR + q, r:r + 1, :] = picks[r]
        return carry

    lax.fori_loop(0, ng // NPAR, stage1, 0)

    def stage2(hh, carry):
        probs = []
        for q in range(NPAR):
            h = hh * NPAR + q
            v1 = v_scr[2 * h]
            v2 = v_scr[2 * h + 1]
            slabs = [v1[0:1, :] + v2[0:SUBL, :], v1[0:1, :] + v2[SUBL:TOPK, :]]
            ids = [sub, sub + SUBL]
            for i in range(1, TOPK):
                nj = TOPK // (i + 1)
                slabs.append(jnp.where(sub < nj, v1[i:i + 1, :] + v2[0:SUBL, :], NEG))
                ids.append(sub + i * TOPK)
            probs.append((slabs, ids))
        for q, (vals, picks) in enumerate(_take_rounds(probs, TOPK)):
            h = hh * NPAR + q
            i1 = i_scr[2 * h]
            i2 = i_scr[2 * h + 1]
            for r in range(TOPK):
                sv_scr[q, r:r + 1, :] = vals[r]
                ci_scr[q, r:r + 1, :] = picks[r]
            sv = sv_scr[q]
            ci = ci_scr[q]
            ci_hi = lax.shift_right_logical(ci, 4)
            ci_lo = jnp.bitwise_and(ci, TOPK - 1)
            e1 = jnp.zeros((TOPK, LANES), I32)
            e2 = jnp.zeros((TOPK, LANES), I32)
            for i in range(TOPK):
                e1 = jnp.where(ci_hi == i, i1[i:i + 1, :], e1)
                e2 = jnp.where(ci_lo == i, i2[i:i + 1, :], e2)
            p = jnp.exp(sv - sv[0:1, :])
            gates = p / jnp.sum(p, axis=0, keepdims=True)
            et_scr[pl.ds(pl.multiple_of(h * TOPK, TOPK), TOPK), :] = e1 * N_KEYS + e2
            gt_scr[pl.ds(pl.multiple_of(h * TOPK, TOPK), TOPK), :] = gates
        return carry

    lax.fori_loop(0, PEER_HEADS // NPAR, stage2, 0)
    e_ref[...] = et_scr[...].T
    g_ref[...] = gt_scr[...].T


def route(st):
    T = st.shape[1]
    ng = 2 * PEER_HEADS
    return pl.pallas_call(
        _route_body, grid=(T // LANES,),
        in_specs=[pl.BlockSpec((ng * N_KEYS, LANES), lambda i: (0, i))],
        out_specs=[pl.BlockSpec((LANES, KSEL), lambda i: (i, 0)),
                   pl.BlockSpec((LANES, KSEL), lambda i: (i, 0))],
        out_shape=[jax.ShapeDtypeStruct((T, KSEL), I32),
                   jax.ShapeDtypeStruct((T, KSEL), F32)],
        scratch_shapes=[pltpu.VMEM((ng, TOPK, LANES), F32), pltpu.VMEM((ng, TOPK, LANES), I32),
                        pltpu.VMEM((NPAR, TOPK, LANES), F32), pltpu.VMEM((NPAR, TOPK, LANES), I32),
                        pltpu.VMEM((KSEL, LANES), I32), pltpu.VMEM((KSEL, LANES), F32)],
        compiler_params=_cp(("parallel",)), name="peer_route",
    )(st)


NC, NS, L = 2, 16, 16
NW = NC * NS
NJ = D_MODEL // L
R = TOPK
NCH = KSEL // R
SUB = 1
NB = 4
RD = SUB * R
G = 16
DW = D_MODEL // 2


def _perm(x, idx):
    return jnp.take_along_axis(x, idx, axis=0, mode="promise_in_bounds")


def _halves(w):
    lo = lax.bitcast_convert_type(lax.shift_left(w, 16), F32)
    hi = lax.bitcast_convert_type(jnp.bitwise_and(w, jnp.int32(-65536)), F32)
    return lo, hi


def _bf(w):
    return plsc.bitcast(w, BF)


def peer_sc(x, resid, idx, gates, uv_tab):
    T = x.shape[0]
    tpw = T // NW
    ngroups = tpw // G
    nchunks = G * NCH // SUB
    idx3 = idx.reshape(T * NCH // SUB, RD)
    g3 = gates.reshape(T * NCH, R)
    mesh = plsc.VectorSubcoreMesh(core_axis_name="c", subcore_axis_name="s")

    @functools.partial(
        pl.kernel, mesh=mesh,
        out_type=jax.ShapeDtypeStruct((T, D_MODEL), F32),
        scratch_types=[
            pltpu.VMEM((G, DW), I32),
            pltpu.VMEM((G, D_MODEL), F32),
            pltpu.VMEM((nchunks, RD), I32),
            pltpu.VMEM((G * NCH, R), F32),
        ] + [pltpu.VMEM((RD, 2 * DW), I32) for _ in range(NB)]
          + [pltpu.SemaphoreType.DMA for _ in range(NB)],
        compiler_params=pltpu.CompilerParams(needs_layout_passes=False),
        name="peer_experts_sc",
    )
    def k(x_hbm, r_hbm, idx_hbm, g_hbm, uv_hbm, out_hbm, x_v, out_v, idx_v, g_v, *ring):
        wid = lax.axis_index("s") * NC + lax.axis_index("c")
        bufs, sems = ring[:NB], ring[NB:]
        iota = lax.iota(I32, L)

        def gather_copy(c, b):
            return pltpu.make_async_copy(uv_hbm.at[idx_v.at[c]], bufs[b], sems[b])

        def issue(c, b):
            gather_copy(c, b).start()

        def wait(c, b):
            gather_copy(c, b).wait()

        def compute(c, b, ro):
            ub = vb = bufs[b]
            t = c // NCH

            def ubody(mm, accs):
                x0 = _bf(x_v[t, pl.ds(mm * 2 * L, L)])
                x1 = _bf(x_v[t, pl.ds(mm * 2 * L + L, L)])
                out = []
                for kk in range(R):
                    pr = x0 * _bf(ub[ro + kk, pl.ds(mm * 2 * L, L)]) + x1 * _bf(ub[ro + kk, pl.ds(mm * 2 * L + L, L)])
                    lo, hi = _halves(plsc.bitcast(pr, I32))
                    out.append(accs[kk] + (lo + hi))
                return tuple(out)

            accs = lax.fori_loop(0, NJ // 4, ubody, tuple(jnp.zeros((L,), F32) for _ in range(R)))
            vecs = list(accs)
            dist = L // 2
            while dist >= 1:
                pidx = jnp.bitwise_xor(iota, dist)
                low = jnp.bitwise_and(iota, dist) == 0
                nxt = []
                for kk in range(dist):
                    a = vecs[kk]
                    bvec = vecs[kk + dist]
                    a = a + _perm(a, pidx)
                    bvec = bvec + _perm(bvec, pidx)
                    nxt.append(jnp.where(low, a, bvec))
                vecs = nxt
                dist //= 2
            hid = vecs[0]
            z = GC * (hid + 0.044715 * hid * hid * hid)
            gel = hid / (1.0 + jnp.exp(-2.0 * z))
            w = g_v[c, :] * gel
            wbs = []
            for kk in range(R):
                wb = _perm(w, jnp.full((L,), kk, I32))
                wbs.append(plsc.pack(wb, wb, format=plsc.PackFormat.INTERLEAVED))

            @plsc.parallel_loop(0, NJ // 2)
            def _(m):
                pr = [wbs[kk] * _bf(vb[ro + kk, pl.ds(DW + m * L, L)]) for kk in range(R)]
                for _lvl in range(2):
                    pr = [pr[i] + pr[i + 1] for i in range(0, len(pr), 2)]
                los, his = [], []
                for q in pr:
                    lo, hi = _halves(plsc.bitcast(q, I32))
                    los.append(lo)
                    his.append(hi)
                while len(los) > 1:
                    los = [los[i] + los[i + 1] for i in range(0, len(los), 2)]
                    his = [his[i] + his[i + 1] for i in range(0, len(his), 2)]
                out_v[t, pl.ds(m * L, L)] = out_v[t, pl.ds(m * L, L)] + los[0]
                out_v[t, pl.ds(DW + m * L, L)] = out_v[t, pl.ds(DW + m * L, L)] + his[0]

        def group(g, carry):
            tok0 = wid * tpw + g * G
            pltpu.sync_copy(idx_hbm.at[pl.ds(tok0 * (NCH // SUB), nchunks)], idx_v)
            for b in range(NB - 1):
                issue(b, b)
            pltpu.sync_copy(x_hbm.at[pl.ds(tok0, G)], x_v)
            pltpu.sync_copy(r_hbm.at[pl.ds(tok0, G)], out_v)
            pltpu.sync_copy(g_hbm.at[pl.ds(tok0 * NCH, G * NCH)], g_v)

            def cbody(cc, c2):
                for b in range(NB):
                    c = cc * NB + b

                    @pl.when(c + NB - 1 < nchunks)
                    def _():
                        issue(c + NB - 1, (b + NB - 1) % NB)

                    wait(c, b)
                    for sub in range(SUB):
                        compute(c * SUB + sub, b, sub * R)
                return c2

            lax.fori_loop(0, nchunks // NB, cbody, 0)
            pltpu.sync_copy(out_v, out_hbm.at[pl.ds(tok0, G)])
            return carry

        lax.fori_loop(0, ngroups, group, 0)

    return k(x, resid, idx3, g3, uv_tab)


def _fn_body(x_ref, g_ref, o_ref):
    xf = x_ref[...]
    o_ref[...] = xf * lax.rsqrt(jnp.mean(xf * xf, axis=-1, keepdims=True) + EPS) * g_ref[...]


def _fn_body_into(x_ref, g_ref, prev_ref, o_ref):
    del prev_ref
    _fn_body(x_ref, g_ref, o_ref)


def final_norm(x, g, out, row0, t_total, tm=1024):
    T, d = x.shape
    blk0 = row0 // tm
    common = dict(grid=(T // tm,), out_specs=pl.BlockSpec((tm, d), lambda i: (i + blk0, 0)),
                  out_shape=jax.ShapeDtypeStruct((t_total, d), F32),
                  compiler_params=_cp(("parallel",)), name="final_norm")
    specs = [pl.BlockSpec((tm, d), lambda i: (i, 0)), pl.BlockSpec((1, d), lambda i: (0, 0))]
    if out is None:
        return pl.pallas_call(_fn_body, in_specs=specs, **common)(x, g)
    return pl.pallas_call(_fn_body_into, in_specs=specs + [pl.BlockSpec(memory_space=pl.ANY)],
                          input_output_aliases={2: 0}, **common)(x, g, out)


def _pack_body(u_ref, v_ref, o_ref):
    o_ref[:, 0:D_MODEL // 2] = pack_pairs(u_ref[...])
    o_ref[:, D_MODEL // 2:D_MODEL] = pack_pairs(v_ref[...])


def pack_tables(u, v, tm=512):
    e, d = u.shape
    return pl.pallas_call(
        _pack_body, grid=(e // tm,),
        in_specs=[pl.BlockSpec((tm, d), lambda i: (i, 0)), pl.BlockSpec((tm, d), lambda i: (i, 0))],
        out_specs=pl.BlockSpec((tm, d), lambda i: (i, 0)),
        out_shape=jax.ShapeDtypeStruct((e, d), I32),
        compiler_params=_cp(("parallel",)), name="pack_tables",
    )(u, v)


def _prep_layer(w_in, b_forget, conv_dw_w, conv_dw_b, conv_ln_g, conv_ln_b, rg_conv_w, rg_conv_b,
                rg_w_r, rg_b_r, rg_w_i, rg_b_i, rg_lambda, w_out, peer_wq, peer_k1, peer_k2):
    f0 = 3 * D_ATT
    wf = jnp.zeros((D_MODEL, FPAD), BF).at[:, 0:ATT_HEADS].set(w_in[:, f0:f0 + ATT_HEADS].astype(BF))
    bfg = jnp.zeros((1, FPAD), F32).at[0, 0:ATT_HEADS].set(b_forget)
    cw = jnp.zeros((32, D_CONV), F32).at[0:CONV_K].set(conv_dw_w)
    rw = jnp.zeros((8, D_RNN), F32).at[0:RNN_CONV_K].set(rg_conv_w)
    bd = lambda w: jax.scipy.linalg.block_diag(*[w[i] for i in range(RNN_BLOCKS)]).astype(BF)
    row = lambda v: v.reshape(1, -1).astype(F32)
    keys = jnp.stack([peer_k1, peer_k2], axis=1).reshape(2 * PEER_HEADS, N_KEYS, D_HALF).astype(BF)
    return dict(wqk=w_in[:, 0:2 * D_ATT].astype(BF), wvt=w_in[:, 2 * D_ATT:f0].T.astype(BF), wf=wf,
                wrest=w_in[:, f0 + ATT_HEADS:].astype(BF), bfg=bfg,
                cw=cw, cb=row(conv_dw_b), lg=row(conv_ln_g), lb=row(conv_ln_b),
                rw=rw, rb=row(rg_conv_b), wr=bd(rg_w_r), br=row(rg_b_r), wi=bd(rg_w_i), bi=row(rg_b_i),
                lam=row(rg_lambda), woa=w_out[0:D_ATT].astype(BF), wob=w_out[D_ATT:].astype(BF),
                wq=peer_wq.astype(BF), keys=keys)


def kernel(x, norm1_g, w_in, b_forget, conv_dw_w, conv_dw_b, conv_ln_g, conv_ln_b,
           rg_conv_w, rg_conv_b, rg_w_r, rg_b_r, rg_w_i, rg_b_i, rg_lambda, w_out,
           norm2_g, peer_wq, peer_k1, peer_k2, peer_u, peer_v, final_g):
    b, s, d = x.shape
    params = [_prep_layer(w_in[l], b_forget[l], conv_dw_w[l], conv_dw_b[l], conv_ln_g[l], conv_ln_b[l],
                          rg_conv_w[l], rg_conv_b[l], rg_w_r[l], rg_b_r[l], rg_w_i[l], rg_b_i[l], rg_lambda[l],
                          w_out[l], peer_wq[l], peer_k1[l], peer_k2[l]) for l in range(DEPTH)]
    tabs = [pack_tables(peer_u[l], peer_v[l]) for l in range(DEPTH)]
    bs = b // N_SLICES
    T = bs * s
    xf = x.reshape(b * s, d)
    xs = [None] * N_SLICES
    prev = None
    for l in range(DEPTH):
        p = params[l]
        for i in range(N_SLICES):
            src, row0 = (xf, i * T) if l == 0 else (xs[i], 0)
            tie_tab = l == 0 and i < DEPTH
            deps = ([prev] if prev is not None else []) + ([tabs[i]] if tie_tab else [])
            if deps:
                tied = lax.optimization_barrier((src, *deps))
                src = tied[0]
                if prev is not None:
                    prev = tied[1]
                if tie_tab:
                    tabs[i] = tied[-1]
            qkb, vt, rest = in_proj(src, norm1_g[l].reshape(1, d), p, s, T, row0)
            y_att = attention(qkb, vt, bs, s)
            y_cr = mixers(rest, p, bs, s)
            x1, h2p, st = out_proj(src, y_att, y_cr, p, norm2_g[l].reshape(1, d), row0)
            experts, gates = route(st)
            prev = experts
            xs[i] = peer_sc(h2p, x1, experts, gates, tabs[l])
    out = None
    for i in range(N_SLICES):
        out = final_norm(xs[i], final_g.reshape(1, d), out, i * T, b * s)
    return out.reshape(b, s, d)
```

```python
import functools
import math

import jax
import jax.numpy as jnp
from jax import lax
from jax.experimental import pallas as pl
from jax.experimental.pallas import tpu as pltpu
from jax.experimental.pallas import tpu_sc as plsc

BF = jnp.bfloat16
F32 = jnp.float32
I32 = jnp.int32

D_MODEL = 1024
DEPTH = 2
ATT_HEADS = 8
ATT_HD = 64
D_ATT = ATT_HEADS * ATT_HD
D_CONV = 256
CONV_K = 31
D_RNN = 256
RNN_BLOCKS = 4
RNN_CONV_K = 4
RG_C = 8.0
EPS = 1e-6
N_REST = 2 * D_CONV + 2 * D_RNN
PEER_HEADS = 8
N_KEYS = 128
D_HALF = 128
TOPK = 16
KSEL = PEER_HEADS * TOPK
GC = 0.7978845608028654
NEG = float("-inf")

N_SLICES = 4
LANES = 128
VMEM_LIMIT = 48 * 1024 * 1024


def _cp(sem):
    return pltpu.CompilerParams(dimension_semantics=sem, vmem_limit_bytes=VMEM_LIMIT)


def _split3(x):
    hi = x.astype(BF)
    r = x - hi.astype(F32)
    mid = r.astype(BF)
    lo = (r - mid.astype(F32)).astype(BF)
    return hi, mid, lo


def _nt(a, b):
    return lax.dot_general(a, b, (((1,), (1,)), ((), ())), preferred_element_type=F32)


def _dot(a, b):
    return jnp.dot(a, b, preferred_element_type=F32)


def _sigmoid(x):
    return 1.0 / (1.0 + jnp.exp(-x))


def _gelu(x):
    return 0.5 * x * (1.0 + jnp.tanh(GC * (x + 0.044715 * x * x * x)))


LOG2E = 1.4426950408889634
NSPLIT = 3
FPAD = 16


def _inproj_body(x_ref, g_ref, wqk_ref, wvt_ref, wf_ref, wrest_ref, bf_ref, tri_ref, place_ref,
                 qkb_ref, vt_ref, rest_ref, carry_ref, *, blocks_per_seq, tm):
    i = pl.program_id(0)
    x = x_ref[...]
    h = x * lax.rsqrt(jnp.mean(x * x, axis=-1, keepdims=True) + EPS) * g_ref[...]
    hb = h.astype(BF)
    qk = _dot(hb, wqk_ref[...])
    col = lax.broadcasted_iota(I32, (1, 2 * D_ATT), 1)
    qk = jnp.where(col < D_ATT, qk * (LOG2E / math.sqrt(ATT_HD)), qk)
    qkb_ref[:, 0:2 * D_ATT] = qk.astype(BF)
    vt_ref[...] = _nt(wvt_ref[...], hb).astype(BF)
    rest_ref[...] = _dot(hb, wrest_ref[...])
    ft = _dot(hb, wf_ref[...]) + bf_ref[...]
    lf = jnp.minimum(ft, 0.0) - jnp.log(1.0 + jnp.exp(-jnp.abs(ft)))
    hi, mid, lo = _split3(lf)
    tri = tri_ref[...]
    cs = _dot(tri, hi) + _dot(tri, mid) + _dot(tri, lo)

    @pl.when(i % blocks_per_seq == 0)
    def _():
        carry_ref[...] = jnp.zeros_like(carry_ref)

    cum = cs + carry_ref[...]
    carry_ref[...] = cum[tm - 1:tm, :]
    pieces = _split3(cum * (-LOG2E))
    kb = _dot(pieces[0], place_ref[0]) + _dot(pieces[1], place_ref[1]) + _dot(pieces[2], place_ref[2])
    qkb_ref[:, 2 * D_ATT:3 * D_ATT] = kb.astype(BF)


def bias_lane(hh, j):
    return (ATT_HD if hh == 0 else 0) + j


def in_proj(x, g, p, seq, T, row0=0, tm=512):
    blk0 = row0 // tm
    tri =(lax.broadcasted_iota(I32, (tm, tm), 0) >= lax.broadcasted_iota(I32, (tm, tm), 1)).astype(BF)
    shp = (NSPLIT, FPAD, D_ATT)
    hd = lax.broadcasted_iota(I32, shp, 1)
    target = (hd // 2) * LANES + jnp.where(hd % 2 == 0, ATT_HD, 0) + lax.broadcasted_iota(I32, shp, 0)
    place = ((lax.broadcasted_iota(I32, shp, 2) == target) & (hd < ATT_HEADS)).astype(BF)
    body = functools.partial(_inproj_body, blocks_per_seq=seq // tm, tm=tm)
    return pl.pallas_call(
        body, grid=(T // tm,),
        in_specs=[pl.BlockSpec((tm, D_MODEL), lambda i: (i + blk0, 0)),
                  pl.BlockSpec((1, D_MODEL), lambda i: (0, 0)),
                  pl.BlockSpec((D_MODEL, 2 * D_ATT), lambda i: (0, 0)),
                  pl.BlockSpec((D_ATT, D_MODEL), lambda i: (0, 0)),
                  pl.BlockSpec((D_MODEL, FPAD), lambda i: (0, 0)),
                  pl.BlockSpec((D_MODEL, N_REST), lambda i: (0, 0)),
                  pl.BlockSpec((1, FPAD), lambda i: (0, 0)),
                  pl.BlockSpec((tm, tm), lambda i: (0, 0)),
                  pl.BlockSpec(shp, lambda i: (0, 0, 0))],
        out_specs=[pl.BlockSpec((tm, 3 * D_ATT), lambda i: (i, 0)),
                   pl.BlockSpec((D_ATT, tm), lambda i: (0, i)),
                   pl.BlockSpec((tm, N_REST), lambda i: (i, 0))],
        out_shape=[jax.ShapeDtypeStruct((T, 3 * D_ATT), BF),
                   jax.ShapeDtypeStruct((D_ATT, T), BF),
                   jax.ShapeDtypeStruct((T, N_REST), F32)],
        scratch_shapes=[pltpu.VMEM((1, FPAD), F32)],
        compiler_params=_cp(("arbitrary",)), name="in_proj",
    )(x, g, p["wqk"], p["wvt"], p["wf"], p["wrest"], p["bfg"], tri, place)


def _attn_body(q_ref, k_ref, kb_ref, vt_ref, o_ref, m_ref, acc_ref, *, tq, tk):
    qi = pl.program_id(2)
    ki = pl.program_id(3)

    @pl.when(ki == 0)
    def _():
        m_ref[...] = jnp.full_like(m_ref, NEG)
        acc_ref[...] = jnp.zeros_like(acc_ref)

    lane = lax.broadcasted_iota(I32, (1, LANES), 1)
    first = lane < ATT_HD
    vrow = lax.broadcasted_iota(I32, (LANES, 1), 0) < ATT_HD

    def step(masked):
        q = q_ref[...]
        k = k_ref[...]
        kb = kb_ref[...]
        vt = vt_ref[...]
        if masked:
            keep = (lax.broadcasted_iota(I32, (tk, tq), 0) <= lax.broadcasted_iota(I32, (tk, tq), 1))
        for hh in range(2):
            own = first if hh == 0 else jnp.logical_not(first)
            ones = (lane >= bias_lane(hh, 0)) & (lane < bias_lane(hh, NSPLIT))
            qa = jnp.where(own, q, jnp.where(ones, 1.0, 0.0).astype(BF))
            ka = jnp.where(own, k, kb)
            st = _nt(ka, qa)
            if masked:
                st = jnp.where(keep, st, NEG)
            m_prev = m_ref[hh]
            m_new = jnp.maximum(m_prev, jnp.max(st, axis=0, keepdims=True))
            alpha = jnp.exp2(m_prev - m_new)
            p = jnp.exp2(st - m_new).astype(BF)
            m_ref[hh] = m_new
            vown = vrow if hh == 0 else jnp.logical_not(vrow)
            va = jnp.where(vown, vt, jnp.ones_like(vt))
            acc_ref[hh] = alpha * acc_ref[hh] + _dot(va, p)

    @pl.when(ki < qi)
    def _():
        step(False)

    @pl.when(ki == qi)
    def _():
        step(True)
        a0 = acc_ref[0]
        a1 = acc_ref[1]
        ot = jnp.where(vrow, a0 / a0[ATT_HD:ATT_HD + 1, :], a1 / a1[0:1, :])
        o_ref[...] = ot.T.astype(o_ref.dtype)


def attention(qkb, vt, batch, seq, tq=512):
    T = qkb.shape[0]
    tk = tq
    nq = seq // tq
    npair = ATT_HEADS // 2
    body = functools.partial(_attn_body, tq=tq, tk=tk)
    kblk = lambda b, qi, ki: b * nq + jnp.minimum(ki, qi)
    return pl.pallas_call(
        body, grid=(batch, npair, nq, nq),
        in_specs=[pl.BlockSpec((tq, LANES), lambda b, p, qi, ki: (b * nq + qi, p)),
                  pl.BlockSpec((tk, LANES), lambda b, p, qi, ki: (kblk(b, qi, ki), npair + p)),
                  pl.BlockSpec((tk, LANES), lambda b, p, qi, ki: (kblk(b, qi, ki), 2 * npair + p)),
                  pl.BlockSpec((LANES, tk), lambda b, p, qi, ki: (p, kblk(b, qi, ki)))],
        out_specs=pl.BlockSpec((tq, LANES), lambda b, p, qi, ki: (b * nq + qi, p)),
        out_shape=jax.ShapeDtypeStruct((T, D_ATT), BF),
        scratch_shapes=[pltpu.VMEM((2, 1, tq), F32), pltpu.VMEM((2, LANES, tq), F32)],
        compiler_params=_cp(("parallel", "parallel", "parallel", "arbitrary")), name="fox_attention",
    )(qkb, qkb, qkb, vt)


CONV_HALO = 32
RG_HALO = 8


def _mix_body(rest_ref, cw_ref, cb_ref, lg_ref, lb_ref, rw_ref, rb_ref, wr_ref, br_ref, wi_ref, bi_ref, lam_ref,
              o_ref, ybuf, xbuf, hc, *, ts):
    si = pl.program_id(1)

    @pl.when(si == 0)
    def _():
        ybuf[0:CONV_HALO, :] = jnp.zeros((CONV_HALO, D_CONV), F32)
        xbuf[0:RG_HALO, :] = jnp.zeros((RG_HALO, D_RNN), F32)
        hc[...] = jnp.zeros_like(hc)

    y = rest_ref[:, 0:D_CONV] * _sigmoid(rest_ref[:, D_CONV:2 * D_CONV])
    ybuf[CONV_HALO:CONV_HALO + ts, :] = y
    acc = jnp.zeros((ts, D_CONV), F32)
    for k in range(CONV_K):
        acc = acc + cw_ref[k:k + 1, :] * ybuf[pl.ds(CONV_HALO - (CONV_K - 1) + k, ts), :]
    yc = acc + cb_ref[...]
    mu = jnp.mean(yc, axis=-1, keepdims=True)
    var = jnp.mean(jnp.square(yc - mu), axis=-1, keepdims=True)
    yn = (yc - mu) * lax.rsqrt(var + EPS) * lg_ref[...] + lb_ref[...]
    o_ref[:, 0:D_CONV] = (yn * _sigmoid(yn)).astype(o_ref.dtype)
    ybuf[0:CONV_HALO, :] = ybuf[ts:ts + CONV_HALO, :]

    xbuf[RG_HALO:RG_HALO + ts, :] = rest_ref[:, 2 * D_CONV:2 * D_CONV + D_RNN]
    xc = jnp.zeros((ts, D_RNN), F32)
    for k in range(RNN_CONV_K):
        xc = xc + rw_ref[k:k + 1, :] * xbuf[pl.ds(RG_HALO - (RNN_CONV_K - 1) + k, ts), :]
    xc = xc + rb_ref[...]
    xbuf[0:RG_HALO, :] = xbuf[ts:ts + RG_HALO, :]
    xcb = xc.astype(BF)
    r = _sigmoid(_dot(xcb, wr_ref[...]) + br_ref[...])
    gi = _sigmoid(_dot(xcb, wi_ref[...]) + bi_ref[...])
    nl = -lam_ref[...]
    sp = jnp.maximum(nl, 0.0) + jnp.log(1.0 + jnp.exp(-jnp.abs(nl)))
    log_a = -RG_C * r * sp
    a = jnp.exp(log_a)
    bt = jnp.sqrt(1.0 - jnp.exp(2.0 * log_a)) * (gi * xc)
    row = lax.broadcasted_iota(I32, (ts, 1), 0)
    sh = 1
    while sh < ts:
        live = row >= sh
        a_s = jnp.where(live, pltpu.roll(a, sh, 0), 1.0)
        b_s = jnp.where(live, pltpu.roll(bt, sh, 0), 0.0)
        bt = bt + a * b_s
        a = a * a_s
        sh *= 2
    h = bt + a * hc[...]
    hc[...] = h[ts - 1:ts, :]
    gate_in = rest_ref[:, 2 * D_CONV + D_RNN:2 * D_CONV + 2 * D_RNN]
    o_ref[:, D_CONV:D_CONV + D_RNN] = (h * _gelu(gate_in)).astype(o_ref.dtype)


def mixers(rest, p, batch, seq, ts=512):
    T = rest.shape[0]
    ns = seq // ts
    body = functools.partial(_mix_body, ts=ts)
    vec = lambda: pl.BlockSpec((1, D_CONV), lambda b, s: (0, 0))
    return pl.pallas_call(
        body, grid=(batch, ns),
        in_specs=[pl.BlockSpec((ts, N_REST), lambda b, s: (b * ns + s, 0)),
                  pl.BlockSpec((32, D_CONV), lambda b, s: (0, 0)), vec(), vec(), vec(),
                  pl.BlockSpec((8, D_RNN), lambda b, s: (0, 0)), vec(),
                  pl.BlockSpec((D_RNN, D_RNN), lambda b, s: (0, 0)), vec(),
                  pl.BlockSpec((D_RNN, D_RNN), lambda b, s: (0, 0)), vec(), vec()],
        out_specs=pl.BlockSpec((ts, D_CONV + D_RNN), lambda b, s: (b * ns + s, 0)),
        out_shape=jax.ShapeDtypeStruct((T, D_CONV + D_RNN), BF),
        scratch_shapes=[pltpu.VMEM((ts + CONV_HALO, D_CONV), F32), pltpu.VMEM((ts + RG_HALO, D_RNN), F32),
                        pltpu.VMEM((1, D_RNN), F32)],
        compiler_params=_cp(("arbitrary", "arbitrary")), name="conv_rglru",
    )(rest, p["cw"], p["cb"], p["lg"], p["lb"], p["rw"], p["rb"], p["wr"], p["br"], p["wi"], p["bi"], p["lam"])


def pack_pairs(a):
    half = a.shape[1] // 2
    r = lax.bitcast_convert_type(a, I32)
    r = r + jnp.int32(0x7FFF) + jnp.bitwise_and(lax.shift_right_logical(r, 16), 1)
    lo = lax.shift_right_logical(r[:, :half], 16)
    hi = jnp.bitwise_and(r[:, half:], jnp.int32(-65536))
    return jnp.bitwise_or(hi, lo)


def _outproj_body(x_ref, ya_ref, yc_ref, woa_ref, wob_ref, g2_ref, wq_ref, keys_ref, x1_ref, h2p_ref, st_ref):
    x1 = x_ref[...] + _dot(ya_ref[...], woa_ref[...]) + _dot(yc_ref[...], wob_ref[...])
    x1_ref[...] = x1
    h2 = x1 * lax.rsqrt(jnp.mean(x1 * x1, axis=-1, keepdims=True) + EPS) * g2_ref[...]
    h2p_ref[...] = pack_pairs(h2)
    q = _dot(h2.astype(BF), wq_ref[...]).astype(BF)
    for g in range(2 * PEER_HEADS):
        st_ref[g * N_KEYS:(g + 1) * N_KEYS, :] = _nt(keys_ref[g], q[:, g * D_HALF:(g + 1) * D_HALF])


def out_proj(x, ya, yc, p, g2, row0=0, tm=256):
    T = ya.shape[0]
    ng = 2 * PEER_HEADS
    blk0 = row0 // tm
    return pl.pallas_call(
        _outproj_body, grid=(T // tm,),
        in_specs=[pl.BlockSpec((tm, D_MODEL), lambda i: (i + blk0, 0)),
                  pl.BlockSpec((tm, D_ATT), lambda i: (i, 0)),
                  pl.BlockSpec((tm, D_CONV + D_RNN), lambda i: (i, 0)),
                  pl.BlockSpec((D_ATT, D_MODEL), lambda i: (0, 0)),
                  pl.BlockSpec((D_CONV + D_RNN, D_MODEL), lambda i: (0, 0)),
                  pl.BlockSpec((1, D_MODEL), lambda i: (0, 0)),
                  pl.BlockSpec((D_MODEL, ng * D_HALF), lambda i: (0, 0)),
                  pl.BlockSpec((ng, N_KEYS, D_HALF), lambda i: (0, 0, 0))],
        out_specs=[pl.BlockSpec((tm, D_MODEL), lambda i: (i, 0)),
                   pl.BlockSpec((tm, D_MODEL // 2), lambda i: (i, 0)),
                   pl.BlockSpec((ng * N_KEYS, tm), lambda i: (0, i))],
        out_shape=[jax.ShapeDtypeStruct((T, D_MODEL), F32),
                   jax.ShapeDtypeStruct((T, D_MODEL // 2), I32),
                   jax.ShapeDtypeStruct((ng * N_KEYS, T), F32)],
        compiler_params=_cp(("parallel",)), name="out_proj_peer_scores",
    )(x, ya, yc, p["woa"], p["wob"], g2, p["wq"], p["keys"])


BIG_ID = 1 << 20
SUBL = 8
NPAR = 4
SEL_CHAIN = 4


def _take_rounds(problems, nrounds):
    state = [list(slabs) for slabs, _ in problems]
    res = [([], []) for _ in problems]
    for _ in range(nrounds):
        for pi, (_, ids) in enumerate(problems):
            slabs = state[pi]
            m8 = slabs[0]
            for sl in slabs[1:]:
                m8 = jnp.maximum(m8, sl)
            m = jnp.max(m8, axis=0, keepdims=True)
            chains = []
            for c0 in range(0, len(slabs), SEL_CHAIN):
                v = jnp.full((SUBL, LANES), BIG_ID, I32)
                for sl, idc in zip(reversed(slabs[c0:c0 + SEL_CHAIN]), reversed(ids[c0:c0 + SEL_CHAIN])):
                    v = jnp.where(sl == m, idc, v)
                chains.append(v)
            while len(chains) > 1:
                chains = [jnp.minimum(chains[i], chains[i + 1]) if i + 1 < len(chains) else chains[i]
                          for i in range(0, len(chains), 2)]
            pick = jnp.min(chains[0], axis=0, keepdims=True)
            state[pi] = [jnp.where(idc == pick, NEG, sl) for sl, idc in zip(slabs, ids)]
            res[pi][0].append(m)
            res[pi][1].append(pick)
    return res


def _route_body(st_ref, e_ref, g_ref, v_scr, i_scr, sv_scr, ci_scr, et_scr, gt_scr):
    ng = 2 * PEER_HEADS
    sub = lax.broadcasted_iota(I32, (SUBL, LANES), 0)
    key_ids = [sub + SUBL * i for i in range(N_KEYS // SUBL)]

    def stage1(gg, carry):
        probs = []
        for q in range(NPAR):
            base = pl.multiple_of((gg * NPAR + q) * N_KEYS, N_KEYS)
            probs.append(([st_ref[pl.ds(base + SUBL * i, SUBL), :] for i in range(N_KEYS // SUBL)], key_ids))
        for q, (vals, picks) in enumerate(_take_rounds(probs, TOPK)):
            for r in range(TOPK):
                v_scr[gg * NPAR + q, r:r + 1, :] = vals[r]
                i_scr[gg * NPAR + q, r:r + 1, :] = picks[r]
        return carry

    lax.fori_loop(0, ng // NPAR, stage1, 0)

    def stage2(hh, carry):
        probs = []
        for q in range(NPAR):
            h = hh * NPAR + q
            v1 = v_scr[2 * h]
            v2 = v_scr[2 * h + 1]
            slabs = [v1[0:1, :] + v2[0:SUBL, :], v1[0:1, :] + v2[SUBL:TOPK, :]]
            ids = [sub, sub + SUBL]
            for i in range(1, TOPK):
                nj = TOPK // (i + 1)
                slabs.append(jnp.where(sub < nj, v1[i:i + 1, :] + v2[0:SUBL, :], NEG))
                ids.append(sub + i * TOPK)
            probs.append((slabs, ids))
        for q, (vals, picks) in enumerate(_take_rounds(probs, TOPK)):
            h = hh * NPAR + q
            i1 = i_scr[2 * h]
            i2 = i_scr[2 * h + 1]
            for r in range(TOPK):
                sv_scr[q, r:r + 1, :] = vals[r]
                ci_scr[q, r:r + 1, :] = picks[r]
            sv = sv_scr[q]
            ci = ci_scr[q]
            ci_hi = lax.shift_right_logical(ci, 4)
            ci_lo = jnp.bitwise_and(ci, TOPK - 1)
            e1 = jnp.zeros((TOPK, LANES), I32)
            e2 = jnp.zeros((TOPK, LANES), I32)
            for i in range(TOPK):
                e1 = jnp.where(ci_hi == i, i1[i:i + 1, :], e1)
                e2 = jnp.where(ci_lo == i, i2[i:i + 1, :], e2)
            p = jnp.exp(sv - sv[0:1, :])
            gates = p / jnp.sum(p, axis=0, keepdims=True)
            et_scr[pl.ds(pl.multiple_of(h * TOPK, TOPK), TOPK), :] = e1 * N_KEYS + e2
            gt_scr[pl.ds(pl.multiple_of(h * TOPK, TOPK), TOPK), :] = gates
        return carry

    lax.fori_loop(0, PEER_HEADS // NPAR, stage2, 0)
    e_ref[...] = et_scr[...].T
    g_ref[...] = gt_scr[...].T


def route(st):
    T = st.shape[1]
    ng = 2 * PEER_HEADS
    return pl.pallas_call(
        _route_body, grid=(T // LANES,),
        in_specs=[pl.BlockSpec((ng * N_KEYS, LANES), lambda i: (0, i))],
        out_specs=[pl.BlockSpec((LANES, KSEL), lambda i: (i, 0)),
                   pl.BlockSpec((LANES, KSEL), lambda i: (i, 0))],
        out_shape=[jax.ShapeDtypeStruct((T, KSEL), I32),
                   jax.ShapeDtypeStruct((T, KSEL), F32)],
        scratch_shapes=[pltpu.VMEM((ng, TOPK, LANES), F32), pltpu.VMEM((ng, TOPK, LANES), I32),
                        pltpu.VMEM((NPAR, TOPK, LANES), F32), pltpu.VMEM((NPAR, TOPK, LANES), I32),
                        pltpu.VMEM((KSEL, LANES), I32), pltpu.VMEM((KSEL, LANES), F32)],
        compiler_params=_cp(("parallel",)), name="peer_route",
    )(st)


NC, NS, L = 2, 16, 16
NW = NC * NS
NJ = D_MODEL // L
R = TOPK
NCH = KSEL // R
NB = 4
USTEP = 2
G = 8
NSLOT = 2
DW = D_MODEL // 2


def _perm(x, idx):
    return jnp.take_along_axis(x, idx, axis=0, mode="promise_in_bounds")


def _halves(w):
    lo = lax.bitcast_convert_type(lax.shift_left(w, 16), F32)
    hi = lax.bitcast_convert_type(jnp.bitwise_and(w, jnp.int32(-65536)), F32)
    return lo, hi


def _bf(w):
    return plsc.bitcast(w, BF)


def peer_sc(x, resid, idx, gates, uv_tab):
    T = x.shape[0]
    tpw = T // NW
    ngroups = tpw // G
    nchunks = G * NCH
    idx3 = idx.reshape(T * NCH, R)
    g3 = gates.reshape(T * NCH, R)
    mesh = plsc.VectorSubcoreMesh(core_axis_name="c", subcore_axis_name="s")

    @functools.partial(
        pl.kernel, mesh=mesh,
        out_type=jax.ShapeDtypeStruct((T, D_MODEL), F32),
        scratch_types=[
            pltpu.VMEM((NSLOT, G, DW), I32),
            pltpu.VMEM((NSLOT, G, D_MODEL), F32),
            pltpu.VMEM((NSLOT, nchunks, R), I32),
            pltpu.VMEM((NSLOT, nchunks, R), F32),
            pltpu.SemaphoreType.DMA((NSLOT,)),
            pltpu.SemaphoreType.DMA((NSLOT,)),
            pltpu.SemaphoreType.DMA((NSLOT,)),
        ] + [pltpu.VMEM((R, 2 * DW), I32) for _ in range(NB)]
          + [pltpu.SemaphoreType.DMA for _ in range(NB)],
        compiler_params=pltpu.CompilerParams(needs_layout_passes=False),
        name="peer_experts_sc",
    )
    def k(x_hbm, r_hbm, idx_hbm, g_hbm, uv_hbm, out_hbm, x_v, out_v, idx_v, g_v, st_sem, ix_sem, wb_sem, *ring):
        wid = lax.axis_index("s") * NC + lax.axis_index("c")
        bufs, sems = ring[:NB], ring[NB:]
        iota = lax.iota(I32, L)

        def tok0_of(g):
            return wid * tpw + g * G

        def stage_copies(g, slot):
            t0 = tok0_of(g)
            return (pltpu.make_async_copy(x_hbm.at[pl.ds(t0, G)], x_v.at[slot], st_sem.at[slot]),
                    pltpu.make_async_copy(r_hbm.at[pl.ds(t0, G)], out_v.at[slot], st_sem.at[slot]),
                    pltpu.make_async_copy(g_hbm.at[pl.ds(t0 * NCH, nchunks)], g_v.at[slot], st_sem.at[slot]))

        def idx_copy(g, slot):
            return pltpu.make_async_copy(idx_hbm.at[pl.ds(tok0_of(g) * NCH, nchunks)], idx_v.at[slot], ix_sem.at[slot])

        def wb_copy(g, slot):
            return pltpu.make_async_copy(out_v.at[slot], out_hbm.at[pl.ds(tok0_of(g), G)], wb_sem.at[slot])

        def gather_copy(slot, c, b):
            return pltpu.make_async_copy(uv_hbm.at[idx_v.at[slot, c]], bufs[b], sems[b])

        def compute(slot, c, b):
            ub = vb = bufs[b]
            t = c // NCH

            def ubody(mm, accs):
                xs_ = [_bf(x_v[slot, t, pl.ds((mm * USTEP + q) * L, L)]) for q in range(USTEP)]
                out = []
                for kk in range(R):
                    pr = [xs_[q] * _bf(ub[kk, pl.ds((mm * USTEP + q) * L, L)]) for q in range(USTEP)]
                    while len(pr) > 1:
                        pr = [pr[i] + pr[i + 1] for i in range(0, len(pr), 2)]
                    lo, hi = _halves(plsc.bitcast(pr[0], I32))
                    out.append(accs[kk] + (lo + hi))
                return tuple(out)

            accs = lax.fori_loop(0, NJ // (2 * USTEP), ubody, tuple(jnp.zeros((L,), F32) for _ in range(R)))
            vecs = list(accs)
            dist = L // 2
            while dist >= 1:
                pidx = jnp.bitwise_xor(iota, dist)
                low = jnp.bitwise_and(iota, dist) == 0
                nxt = []
                for kk in range(dist):
                    a = vecs[kk]
                    bvec = vecs[kk + dist]
                    a = a + _perm(a, pidx)
                    bvec = bvec + _perm(bvec, pidx)
                    nxt.append(jnp.where(low, a, bvec))
                vecs = nxt
                dist //= 2
            hid = vecs[0]
            z = GC * (hid + 0.044715 * hid * hid * hid)
            gel = hid / (1.0 + jnp.exp(-2.0 * z))
            w = g_v[slot, c, :] * gel
            wbs = []
            for kk in range(R):
                wb = _perm(w, jnp.full((L,), kk, I32))
                wbs.append(plsc.pack(wb, wb, format=plsc.PackFormat.INTERLEAVED))

            @plsc.parallel_loop(0, NJ // 2)
            def _(m):
                pr = [wbs[kk] * _bf(vb[kk, pl.ds(DW + m * L, L)]) for kk in range(R)]
                for _lvl in range(2):
                    pr = [pr[i] + pr[i + 1] for i in range(0, len(pr), 2)]
                los, his = [], []
                for q in pr:
                    lo, hi = _halves(plsc.bitcast(q, I32))
                    los.append(lo)
                    his.append(hi)
                while len(los) > 1:
                    los = [los[i] + los[i + 1] for i in range(0, len(los), 2)]
                    his = [his[i] + his[i + 1] for i in range(0, len(his), 2)]
                out_v[slot, t, pl.ds(m * L, L)] = out_v[slot, t, pl.ds(m * L, L)] + los[0]
                out_v[slot, t, pl.ds(DW + m * L, L)] = out_v[slot, t, pl.ds(DW + m * L, L)] + his[0]

        idx_copy(0, 0).start()
        for cp in stage_copies(0, 0):
            cp.start()
        idx_copy(0, 0).wait()
        for b in range(NB - 1):
            gather_copy(0, b, b).start()

        def group(g, carry):
            slot = g % NSLOT
            nslot = 1 - slot
            has_next = g + 1 < ngroups

            @pl.when(g >= 1)
            def _():
                wb_copy(g - 1, nslot).wait()

            @pl.when(has_next)
            def _():
                idx_copy(g + 1, nslot).start()
                for cp in stage_copies(g + 1, nslot):
                    cp.start()

            for cp in stage_copies(g, slot):
                cp.wait()

            def cbody(cc, c2):
                for b in range(NB):
                    c = cc * NB + b
                    cn = c + NB - 1
                    nb = (b + NB - 1) % NB

                    @pl.when(cn < nchunks)
                    def _():
                        gather_copy(slot, cn, nb).start()

                    @pl.when(jnp.logical_and(cn >= nchunks, has_next))
                    def _():
                        @pl.when(cn == nchunks)
                        def _():
                            idx_copy(g + 1, nslot).wait()

                        gather_copy(nslot, cn - nchunks, nb).start()

                    gather_copy(slot, c, b).wait()
                    compute(slot, c, b)
                return c2

            lax.fori_loop(0, nchunks // NB, cbody, 0)
            wb_copy(g, slot).start()
            return carry

        lax.fori_loop(0, ngroups, group, 0)
        wb_copy(ngroups - 1, (ngroups - 1) % NSLOT).wait()

    return k(x, resid, idx3, g3, uv_tab)


def _fn_body(x_ref, g_ref, o_ref):
    xf = x_ref[...]
    o_ref[...] = xf * lax.rsqrt(jnp.mean(xf * xf, axis=-1, keepdims=True) + EPS) * g_ref[...]


def _fn_body_into(x_ref, g_ref, prev_ref, o_ref):
    del prev_ref
    _fn_body(x_ref, g_ref, o_ref)


def final_norm(x, g, out, row0, t_total, tm=1024):
    T, d = x.shape
    blk0 = row0 // tm
    common = dict(grid=(T // tm,), out_specs=pl.BlockSpec((tm, d), lambda i: (i + blk0, 0)),
                  out_shape=jax.ShapeDtypeStruct((t_total, d), F32),
                  compiler_params=_cp(("parallel",)), name="final_norm")
    specs = [pl.BlockSpec((tm, d), lambda i: (i, 0)), pl.BlockSpec((1, d), lambda i: (0, 0))]
    if out is None:
        return pl.pallas_call(_fn_body, in_specs=specs, **common)(x, g)
    return pl.pallas_call(_fn_body_into, in_specs=specs + [pl.BlockSpec(memory_space=pl.ANY)],
                          input_output_aliases={2: 0}, **common)(x, g, out)


def _pack_body(u_ref, v_ref, o_ref):
    o_ref[:, 0:D_MODEL // 2] = pack_pairs(u_ref[...])
    o_ref[:, D_MODEL // 2:D_MODEL] = pack_pairs(v_ref[...])


def pack_tables(u, v, tm=512):
    e, d = u.shape
    return pl.pallas_call(
        _pack_body, grid=(e // tm,),
        in_specs=[pl.BlockSpec((tm, d), lambda i: (i, 0)), pl.BlockSpec((tm, d), lambda i: (i, 0))],
        out_specs=pl.BlockSpec((tm, d), lambda i: (i, 0)),
        out_shape=jax.ShapeDtypeStruct((e, d), I32),
        compiler_params=_cp(("parallel",)), name="pack_tables",
    )(u, v)


def _prep_layer(w_in, b_forget, conv_dw_w, conv_dw_b, conv_ln_g, conv_ln_b, rg_conv_w, rg_conv_b,
                rg_w_r, rg_b_r, rg_w_i, rg_b_i, rg_lambda, w_out, peer_wq, peer_k1, peer_k2):
    f0 = 3 * D_ATT
    wf = jnp.zeros((D_MODEL, FPAD), BF).at[:, 0:ATT_HEADS].set(w_in[:, f0:f0 + ATT_HEADS].astype(BF))
    bfg = jnp.zeros((1, FPAD), F32).at[0, 0:ATT_HEADS].set(b_forget)
    cw = jnp.zeros((32, D_CONV), F32).at[0:CONV_K].set(conv_dw_w)
    rw = jnp.zeros((8, D_RNN), F32).at[0:RNN_CONV_K].set(rg_conv_w)
    bd = lambda w: jax.scipy.linalg.block_diag(*[w[i] for i in range(RNN_BLOCKS)]).astype(BF)
    row = lambda v: v.reshape(1, -1).astype(F32)
    keys = jnp.stack([peer_k1, peer_k2], axis=1).reshape(2 * PEER_HEADS, N_KEYS, D_HALF).astype(BF)
    return dict(wqk=w_in[:, 0:2 * D_ATT].astype(BF), wvt=w_in[:, 2 * D_ATT:f0].T.astype(BF), wf=wf,
                wrest=w_in[:, f0 + ATT_HEADS:].astype(BF), bfg=bfg,
                cw=cw, cb=row(conv_dw_b), lg=row(conv_ln_g), lb=row(conv_ln_b),
                rw=rw, rb=row(rg_conv_b), wr=bd(rg_w_r), br=row(rg_b_r), wi=bd(rg_w_i), bi=row(rg_b_i),
                lam=row(rg_lambda), woa=w_out[0:D_ATT].astype(BF), wob=w_out[D_ATT:].astype(BF),
                wq=peer_wq.astype(BF), keys=keys)


def kernel(x, norm1_g, w_in, b_forget, conv_dw_w, conv_dw_b, conv_ln_g, conv_ln_b,
           rg_conv_w, rg_conv_b, rg_w_r, rg_b_r, rg_w_i, rg_b_i, rg_lambda, w_out,
           norm2_g, peer_wq, peer_k1, peer_k2, peer_u, peer_v, final_g):
    b, s, d = x.shape
    params = [_prep_layer(w_in[l], b_forget[l], conv_dw_w[l], conv_dw_b[l], conv_ln_g[l], conv_ln_b[l],
                          rg_conv_w[l], rg_conv_b[l], rg_w_r[l], rg_b_r[l], rg_w_i[l], rg_b_i[l], rg_lambda[l],
                          w_out[l], peer_wq[l], peer_k1[l], peer_k2[l]) for l in range(DEPTH)]
    tabs = [pack_tables(peer_u[l], peer_v[l]) for l in range(DEPTH)]
    bs = b // N_SLICES
    T = bs * s
    xf = x.reshape(b * s, d)
    xs = [None] * N_SLICES
    prev = None
    for l in range(DEPTH):
        p = params[l]
        for i in range(N_SLICES):
            src, row0 = (xf, i * T) if l == 0 else (xs[i], 0)
            tie_tab = l == 0 and i < DEPTH
            deps = ([prev] if prev is not None else []) + ([tabs[i]] if tie_tab else [])
            if deps:
                tied = lax.optimization_barrier((src, *deps))
                src = tied[0]
                if prev is not None:
                    prev = tied[1]
                if tie_tab:
                    tabs[i] = tied[-1]
            qkb, vt, rest = in_proj(src, norm1_g[l].reshape(1, d), p, s, T, row0)
            y_att = attention(qkb, vt, bs, s)
            y_cr = mixers(rest, p, bs, s)
            x1, h2p, st = out_proj(src, y_att, y_cr, p, norm2_g[l].reshape(1, d), row0)
            experts, gates = route(st)
            prev = experts
            xs[i] = peer_sc(h2p, x1, experts, gates, tabs[l])
    out = None
    for i in range(N_SLICES):
        out = final_norm(xs[i], final_g.reshape(1, d), out, i * T, b * s)
    return out.reshape(b, s, d)
```

```python
import functools
import math

import jax
import jax.numpy as jnp
from jax import lax
from jax.experimental import pallas as pl
from jax.experimental.pallas import tpu as pltpu
from jax.experimental.pallas import tpu_sc as plsc

BF = jnp.bfloat16
F32 = jnp.float32
I32 = jnp.int32

D_MODEL = 1024
DEPTH = 2
ATT_HEADS = 8
ATT_HD = 64
D_ATT = ATT_HEADS * ATT_HD
D_CONV = 256
CONV_K = 31
D_RNN = 256
RNN_BLOCKS = 4
RNN_CONV_K = 4
RG_C = 8.0
EPS = 1e-6
N_REST = 2 * D_CONV + 2 * D_RNN
PEER_HEADS = 8
N_KEYS = 128
D_HALF = 128
TOPK = 16
KSEL = PEER_HEADS * TOPK
GC = 0.7978845608028654
NEG = float("-inf")

FIRST_PIECES = 4
N_SLICES = 4
LANES = 128
VMEM_LIMIT = 48 * 1024 * 1024


def _cp(sem):
    return pltpu.CompilerParams(dimension_semantics=sem, vmem_limit_bytes=VMEM_LIMIT)


def _split3(x):
    hi = x.astype(BF)
    r = x - hi.astype(F32)
    mid = r.astype(BF)
    lo = (r - mid.astype(F32)).astype(BF)
    return hi, mid, lo


def _nt(a, b):
    return lax.dot_general(a, b, (((1,), (1,)), ((), ())), preferred_element_type=F32)


def _dot(a, b):
    return jnp.dot(a, b, preferred_element_type=F32)


def _sigmoid(x):
    return 1.0 / (1.0 + jnp.exp(-x))


def _gelu(x):
    return 0.5 * x * (1.0 + jnp.tanh(GC * (x + 0.044715 * x * x * x)))


LOG2E = 1.4426950408889634
NSPLIT = 3
FPAD = 16


def _inproj_body(x_ref, g_ref, wqk_ref, wvt_ref, wf_ref, wrest_ref, bf_ref, tri_ref, place_ref,
                 qkb_ref, vt_ref, rest_ref, carry_ref, *, blocks_per_seq, tm):
    i = pl.program_id(0)
    x = x_ref[...]
    h = x * lax.rsqrt(jnp.mean(x * x, axis=-1, keepdims=True) + EPS) * g_ref[...]
    hb = h.astype(BF)
    qk = _dot(hb, wqk_ref[...])
    col = lax.broadcasted_iota(I32, (1, 2 * D_ATT), 1)
    qk = jnp.where(col < D_ATT, qk * (LOG2E / math.sqrt(ATT_HD)), qk)
    qkb_ref[:, 0:2 * D_ATT] = qk.astype(BF)
    vt_ref[...] = _nt(wvt_ref[...], hb).astype(BF)
    rest_ref[...] = _dot(hb, wrest_ref[...])
    ft = _dot(hb, wf_ref[...]) + bf_ref[...]
    lf = jnp.minimum(ft, 0.0) - jnp.log(1.0 + jnp.exp(-jnp.abs(ft)))
    hi, mid, lo = _split3(lf)
    tri = tri_ref[...]
    cs = _dot(tri, hi) + _dot(tri, mid) + _dot(tri, lo)

    @pl.when(i % blocks_per_seq == 0)
    def _():
        carry_ref[...] = jnp.zeros_like(carry_ref)

    cum = cs + carry_ref[...]
    carry_ref[...] = cum[tm - 1:tm, :]
    pieces = _split3(cum * (-LOG2E))
    kb = _dot(pieces[0], place_ref[0]) + _dot(pieces[1], place_ref[1]) + _dot(pieces[2], place_ref[2])
    qkb_ref[:, 2 * D_ATT:3 * D_ATT] = kb.astype(BF)


def bias_lane(hh, j):
    return (ATT_HD if hh == 0 else 0) + j


def in_proj(x, g, p, seq, T, row0=0, tm=512):
    blk0 = row0 // tm
    tri =(lax.broadcasted_iota(I32, (tm, tm), 0) >= lax.broadcasted_iota(I32, (tm, tm), 1)).astype(BF)
    shp = (NSPLIT, FPAD, D_ATT)
    hd = lax.broadcasted_iota(I32, shp, 1)
    target = (hd // 2) * LANES + jnp.where(hd % 2 == 0, ATT_HD, 0) + lax.broadcasted_iota(I32, shp, 0)
    place = ((lax.broadcasted_iota(I32, shp, 2) == target) & (hd < ATT_HEADS)).astype(BF)
    body = functools.partial(_inproj_body, blocks_per_seq=seq // tm, tm=tm)
    return pl.pallas_call(
        body, grid=(T // tm,),
        in_specs=[pl.BlockSpec((tm, D_MODEL), lambda i: (i + blk0, 0)),
                  pl.BlockSpec((1, D_MODEL), lambda i: (0, 0)),
                  pl.BlockSpec((D_MODEL, 2 * D_ATT), lambda i: (0, 0)),
                  pl.BlockSpec((D_ATT, D_MODEL), lambda i: (0, 0)),
                  pl.BlockSpec((D_MODEL, FPAD), lambda i: (0, 0)),
                  pl.BlockSpec((D_MODEL, N_REST), lambda i: (0, 0)),
                  pl.BlockSpec((1, FPAD), lambda i: (0, 0)),
                  pl.BlockSpec((tm, tm), lambda i: (0, 0)),
                  pl.BlockSpec(shp, lambda i: (0, 0, 0))],
        out_specs=[pl.BlockSpec((tm, 3 * D_ATT), lambda i: (i, 0)),
                   pl.BlockSpec((D_ATT, tm), lambda i: (0, i)),
                   pl.BlockSpec((tm, N_REST), lambda i: (i, 0))],
        out_shape=[jax.ShapeDtypeStruct((T, 3 * D_ATT), BF),
                   jax.ShapeDtypeStruct((D_ATT, T), BF),
                   jax.ShapeDtypeStruct((T, N_REST), F32)],
        scratch_shapes=[pltpu.VMEM((1, FPAD), F32)],
        compiler_params=_cp(("arbitrary",)), name="in_proj",
    )(x, g, p["wqk"], p["wvt"], p["wf"], p["wrest"], p["bfg"], tri, place)


def _attn_body(q_ref, k_ref, kb_ref, vt_ref, o_ref, m_ref, acc_ref, *, tq, tk):
    qi = pl.program_id(2)
    ki = pl.program_id(3)

    @pl.when(ki == 0)
    def _():
        m_ref[...] = jnp.full_like(m_ref, NEG)
        acc_ref[...] = jnp.zeros_like(acc_ref)

    lane = lax.broadcasted_iota(I32, (1, LANES), 1)
    first = lane < ATT_HD
    vrow = lax.broadcasted_iota(I32, (LANES, 1), 0) < ATT_HD

    def step(masked):
        q = q_ref[...]
        k = k_ref[...]
        kb = kb_ref[...]
        vt = vt_ref[...]
        if masked:
            keep = (lax.broadcasted_iota(I32, (tk, tq), 0) <= lax.broadcasted_iota(I32, (tk, tq), 1))
        for hh in range(2):
            own = first if hh == 0 else jnp.logical_not(first)
            ones = (lane >= bias_lane(hh, 0)) & (lane < bias_lane(hh, NSPLIT))
            qa = jnp.where(own, q, jnp.where(ones, 1.0, 0.0).astype(BF))
            ka = jnp.where(own, k, kb)
            st = _nt(ka, qa)
            if masked:
                st = jnp.where(keep, st, NEG)
            m_prev = m_ref[hh]
            m_new = jnp.maximum(m_prev, jnp.max(st, axis=0, keepdims=True))
            alpha = jnp.exp2(m_prev - m_new)
            p = jnp.exp2(st - m_new).astype(BF)
            m_ref[hh] = m_new
            vown = vrow if hh == 0 else jnp.logical_not(vrow)
            va = jnp.where(vown, vt, jnp.ones_like(vt))
            acc_ref[hh] = alpha * acc_ref[hh] + _dot(va, p)

    @pl.when(ki < qi)
    def _():
        step(False)

    @pl.when(ki == qi)
    def _():
        step(True)
        a0 = acc_ref[0]
        a1 = acc_ref[1]
        ot = jnp.where(vrow, a0 / a0[ATT_HD:ATT_HD + 1, :], a1 / a1[0:1, :])
        o_ref[...] = ot.T.astype(o_ref.dtype)


def attention(qkb, vt, batch, seq, tq=512):
    T = qkb.shape[0]
    tk = tq
    nq = seq // tq
    npair = ATT_HEADS // 2
    body = functools.partial(_attn_body, tq=tq, tk=tk)
    kblk = lambda b, qi, ki: b * nq + jnp.minimum(ki, qi)
    return pl.pallas_call(
        body, grid=(batch, npair, nq, nq),
        in_specs=[pl.BlockSpec((tq, LANES), lambda b, p, qi, ki: (b * nq + qi, p)),
                  pl.BlockSpec((tk, LANES), lambda b, p, qi, ki: (kblk(b, qi, ki), npair + p)),
                  pl.BlockSpec((tk, LANES), lambda b, p, qi, ki: (kblk(b, qi, ki), 2 * npair + p)),
                  pl.BlockSpec((LANES, tk), lambda b, p, qi, ki: (p, kblk(b, qi, ki)))],
        out_specs=pl.BlockSpec((tq, LANES), lambda b, p, qi, ki: (b * nq + qi, p)),
        out_shape=jax.ShapeDtypeStruct((T, D_ATT), BF),
        scratch_shapes=[pltpu.VMEM((2, 1, tq), F32), pltpu.VMEM((2, LANES, tq), F32)],
        compiler_params=_cp(("parallel", "parallel", "parallel", "arbitrary")), name="fox_attention",
    )(qkb, qkb, qkb, vt)


CONV_HALO = 32
RG_HALO = 8


def _mix_body(rest_ref, cw_ref, cb_ref, lg_ref, lb_ref, rw_ref, rb_ref, wr_ref, br_ref, wi_ref, bi_ref, lam_ref,
              o_ref, ybuf, xbuf, hc, *, ts):
    si = pl.program_id(1)

    @pl.when(si == 0)
    def _():
        ybuf[0:CONV_HALO, :] = jnp.zeros((CONV_HALO, D_CONV), F32)
        xbuf[0:RG_HALO, :] = jnp.zeros((RG_HALO, D_RNN), F32)
        hc[...] = jnp.zeros_like(hc)

    y = rest_ref[:, 0:D_CONV] * _sigmoid(rest_ref[:, D_CONV:2 * D_CONV])
    ybuf[CONV_HALO:CONV_HALO + ts, :] = y
    acc = jnp.zeros((ts, D_CONV), F32)
    for k in range(CONV_K):
        acc = acc + cw_ref[k:k + 1, :] * ybuf[pl.ds(CONV_HALO - (CONV_K - 1) + k, ts), :]
    yc = acc + cb_ref[...]
    mu = jnp.mean(yc, axis=-1, keepdims=True)
    var = jnp.mean(jnp.square(yc - mu), axis=-1, keepdims=True)
    yn = (yc - mu) * lax.rsqrt(var + EPS) * lg_ref[...] + lb_ref[...]
    o_ref[:, 0:D_CONV] = (yn * _sigmoid(yn)).astype(o_ref.dtype)
    ybuf[0:CONV_HALO, :] = ybuf[ts:ts + CONV_HALO, :]

    xbuf[RG_HALO:RG_HALO + ts, :] = rest_ref[:, 2 * D_CONV:2 * D_CONV + D_RNN]
    xc = jnp.zeros((ts, D_RNN), F32)
    for k in range(RNN_CONV_K):
        xc = xc + rw_ref[k:k + 1, :] * xbuf[pl.ds(RG_HALO - (RNN_CONV_K - 1) + k, ts), :]
    xc = xc + rb_ref[...]
    xbuf[0:RG_HALO, :] = xbuf[ts:ts + RG_HALO, :]
    xcb = xc.astype(BF)
    r = _sigmoid(_dot(xcb, wr_ref[...]) + br_ref[...])
    gi = _sigmoid(_dot(xcb, wi_ref[...]) + bi_ref[...])
    nl = -lam_ref[...]
    sp = jnp.maximum(nl, 0.0) + jnp.log(1.0 + jnp.exp(-jnp.abs(nl)))
    log_a = -RG_C * r * sp
    a = jnp.exp(log_a)
    bt = jnp.sqrt(1.0 - jnp.exp(2.0 * log_a)) * (gi * xc)
    row = lax.broadcasted_iota(I32, (ts, 1), 0)
    sh = 1
    while sh < ts:
        live = row >= sh
        a_s = jnp.where(live, pltpu.roll(a, sh, 0), 1.0)
        b_s = jnp.where(live, pltpu.roll(bt, sh, 0), 0.0)
        bt = bt + a * b_s
        a = a * a_s
        sh *= 2
    h = bt + a * hc[...]
    hc[...] = h[ts - 1:ts, :]
    gate_in = rest_ref[:, 2 * D_CONV + D_RNN:2 * D_CONV + 2 * D_RNN]
    o_ref[:, D_CONV:D_CONV + D_RNN] = (h * _gelu(gate_in)).astype(o_ref.dtype)


def mixers(rest, p, batch, seq, ts=512):
    T = rest.shape[0]
    ns = seq // ts
    body = functools.partial(_mix_body, ts=ts)
    vec = lambda: pl.BlockSpec((1, D_CONV), lambda b, s: (0, 0))
    return pl.pallas_call(
        body, grid=(batch, ns),
        in_specs=[pl.BlockSpec((ts, N_REST), lambda b, s: (b * ns + s, 0)),
                  pl.BlockSpec((32, D_CONV), lambda b, s: (0, 0)), vec(), vec(), vec(),
                  pl.BlockSpec((8, D_RNN), lambda b, s: (0, 0)), vec(),
                  pl.BlockSpec((D_RNN, D_RNN), lambda b, s: (0, 0)), vec(),
                  pl.BlockSpec((D_RNN, D_RNN), lambda b, s: (0, 0)), vec(), vec()],
        out_specs=pl.BlockSpec((ts, D_CONV + D_RNN), lambda b, s: (b * ns + s, 0)),
        out_shape=jax.ShapeDtypeStruct((T, D_CONV + D_RNN), BF),
        scratch_shapes=[pltpu.VMEM((ts + CONV_HALO, D_CONV), F32), pltpu.VMEM((ts + RG_HALO, D_RNN), F32),
                        pltpu.VMEM((1, D_RNN), F32)],
        compiler_params=_cp(("arbitrary", "arbitrary")), name="conv_rglru",
    )(rest, p["cw"], p["cb"], p["lg"], p["lb"], p["rw"], p["rb"], p["wr"], p["br"], p["wi"], p["bi"], p["lam"])


def pack_pairs(a):
    half = a.shape[1] // 2
    r = lax.bitcast_convert_type(a, I32)
    r = r + jnp.int32(0x7FFF) + jnp.bitwise_and(lax.shift_right_logical(r, 16), 1)
    lo = lax.shift_right_logical(r[:, :half], 16)
    hi = jnp.bitwise_and(r[:, half:], jnp.int32(-65536))
    return jnp.bitwise_or(hi, lo)


def _outproj_body(x_ref, ya_ref, yc_ref, woa_ref, wob_ref, g2_ref, wq_ref, keys_ref, x1_ref, h2p_ref, st_ref):
    x1 = x_ref[...] + _dot(ya_ref[...], woa_ref[...]) + _dot(yc_ref[...], wob_ref[...])
    x1_ref[...] = x1
    h2 = x1 * lax.rsqrt(jnp.mean(x1 * x1, axis=-1, keepdims=True) + EPS) * g2_ref[...]
    h2p_ref[...] = pack_pairs(h2)
    q = _dot(h2.astype(BF), wq_ref[...]).astype(BF)
    for g in range(2 * PEER_HEADS):
        st_ref[g * N_KEYS:(g + 1) * N_KEYS, :] = _nt(keys_ref[g], q[:, g * D_HALF:(g + 1) * D_HALF])


def out_proj(x, ya, yc, p, g2, row0=0, sub0=0, T=None, tm=256):
    T = ya.shape[0] if T is None else T
    ng = 2 * PEER_HEADS
    blk0 = (row0 + sub0) // tm
    sblk = sub0 // tm
    return pl.pallas_call(
        _outproj_body, grid=(T // tm,),
        in_specs=[pl.BlockSpec((tm, D_MODEL), lambda i: (i + blk0, 0)),
                  pl.BlockSpec((tm, D_ATT), lambda i: (i + sblk, 0)),
                  pl.BlockSpec((tm, D_CONV + D_RNN), lambda i: (i + sblk, 0)),
                  pl.BlockSpec((D_ATT, D_MODEL), lambda i: (0, 0)),
                  pl.BlockSpec((D_CONV + D_RNN, D_MODEL), lambda i: (0, 0)),
                  pl.BlockSpec((1, D_MODEL), lambda i: (0, 0)),
                  pl.BlockSpec((D_MODEL, ng * D_HALF), lambda i: (0, 0)),
                  pl.BlockSpec((ng, N_KEYS, D_HALF), lambda i: (0, 0, 0))],
        out_specs=[pl.BlockSpec((tm, D_MODEL), lambda i: (i, 0)),
                   pl.BlockSpec((tm, D_MODEL // 2), lambda i: (i, 0)),
                   pl.BlockSpec((ng * N_KEYS, tm), lambda i: (0, i))],
        out_shape=[jax.ShapeDtypeStruct((T, D_MODEL), F32),
                   jax.ShapeDtypeStruct((T, D_MODEL // 2), I32),
                   jax.ShapeDtypeStruct((ng * N_KEYS, T), F32)],
        compiler_params=_cp(("parallel",)), name="out_proj_peer_scores",
    )(x, ya, yc, p["woa"], p["wob"], g2, p["wq"], p["keys"])


BIG_ID = 1 << 20
SUBL = 8
NPAR = 4
SEL_CHAIN = 4


def _take_rounds(problems, nrounds):
    state = [list(slabs) for slabs, _ in problems]
    res = [([], []) for _ in problems]
    for _ in range(nrounds):
        for pi, (_, ids) in enumerate(problems):
            slabs = state[pi]
            m8 = slabs[0]
            for sl in slabs[1:]:
                m8 = jnp.maximum(m8, sl)
            m = jnp.max(m8, axis=0, keepdims=True)
            chains = []
            for c0 in range(0, len(slabs), SEL_CHAIN):
                v = jnp.full((SUBL, LANES), BIG_ID, I32)
                for sl, idc in zip(reversed(slabs[c0:c0 + SEL_CHAIN]), reversed(ids[c0:c0 + SEL_CHAIN])):
                    v = jnp.where(sl == m, idc, v)
                chains.append(v)
            while len(chains) > 1:
                chains = [jnp.minimum(chains[i], chains[i + 1]) if i + 1 < len(chains) else chains[i]
                          for i in range(0, len(chains), 2)]
            pick = jnp.min(chains[0], axis=0, keepdims=True)
            state[pi] = [jnp.where(idc == pick, NEG, sl) for sl, idc in zip(slabs, ids)]
            res[pi][0].append(m)
            res[pi][1].append(pick)
    return res


def _route_body(st_ref, e_ref, g_ref, v_scr, i_scr, sv_scr, ci_scr, et_scr, gt_scr):
    ng = 2 * PEER_HEADS
    sub = lax.broadcasted_iota(I32, (SUBL, LANES), 0)
    key_ids = [sub + SUBL * i for i in range(N_KEYS // SUBL)]

    def stage1(gg, carry):
        probs = []
        for q in range(NPAR):
            base = pl.multiple_of((gg * NPAR + q) * N_KEYS, N_KEYS)
            probs.append(([st_ref[pl.ds(base + SUBL * i, SUBL), :] for i in range(N_KEYS // SUBL)], key_ids))
        for q, (vals, picks) in enumerate(_take_rounds(probs, TOPK)):
            for r in range(TOPK):
                v_scr[gg * NPAR + q, r:r + 1, :] = vals[r]
                i_scr[gg * NPAR + q, r:r + 1, :] = picks[r]
        return carry

    lax.fori_loop(0, ng // NPAR, stage1, 0)

    def stage2(hh, carry):
        probs = []
        for q in range(NPAR):
            h = hh * NPAR + q
            v1 = v_scr[2 * h]
            v2 = v_scr[2 * h + 1]
            slabs = [v1[0:1, :] + v2[0:SUBL, :], v1[0:1, :] + v2[SUBL:TOPK, :]]
            ids = [sub, sub + SUBL]
            for i in range(1, TOPK):
                nj = TOPK // (i + 1)
                slabs.append(jnp.where(sub < nj, v1[i:i + 1, :] + v2[0:SUBL, :], NEG))
                ids.append(sub + i * TOPK)
            probs.append((slabs, ids))
        for q, (vals, picks) in enumerate(_take_rounds(probs, TOPK)):
            h = hh * NPAR + q
            i1 = i_scr[2 * h]
            i2 = i_scr[2 * h + 1]
            for r in range(TOPK):
                sv_scr[q, r:r + 1, :] = vals[r]
                ci_scr[q, r:r + 1, :] = picks[r]
            sv = sv_scr[q]
            ci = ci_scr[q]
            ci_hi = lax.shift_right_logical(ci, 4)
            ci_lo = jnp.bitwise_and(ci, TOPK - 1)
            e1 = jnp.zeros((TOPK, LANES), I32)
            e2 = jnp.zeros((TOPK, LANES), I32)
            for i in range(TOPK):
                e1 = jnp.where(ci_hi == i, i1[i:i + 1, :], e1)
                e2 = jnp.where(ci_lo == i, i2[i:i + 1, :], e2)
            p = jnp.exp(sv - sv[0:1, :])
            gates = p / jnp.sum(p, axis=0, keepdims=True)
            et_scr[pl.ds(pl.multiple_of(h * TOPK, TOPK), TOPK), :] = e1 * N_KEYS + e2
            gt_scr[pl.ds(pl.multiple_of(h * TOPK, TOPK), TOPK), :] = gates
        return carry

    lax.fori_loop(0, PEER_HEADS // NPAR, stage2, 0)
    e_ref[...] = et_scr[...].T
    g_ref[...] = gt_scr[...].T


def route(st):
    T = st.shape[1]
    ng = 2 * PEER_HEADS
    return pl.pallas_call(
        _route_body, grid=(T // LANES,),
        in_specs=[pl.BlockSpec((ng * N_KEYS, LANES), lambda i: (0, i))],
        out_specs=[pl.BlockSpec((LANES, KSEL), lambda i: (i, 0)),
                   pl.BlockSpec((LANES, KSEL), lambda i: (i, 0))],
        out_shape=[jax.ShapeDtypeStruct((T, KSEL), I32),
                   jax.ShapeDtypeStruct((T, KSEL), F32)],
        scratch_shapes=[pltpu.VMEM((ng, TOPK, LANES), F32), pltpu.VMEM((ng, TOPK, LANES), I32),
                        pltpu.VMEM((NPAR, TOPK, LANES), F32), pltpu.VMEM((NPAR, TOPK, LANES), I32),
                        pltpu.VMEM((KSEL, LANES), I32), pltpu.VMEM((KSEL, LANES), F32)],
        compiler_params=_cp(("parallel",)), name="peer_route",
    )(st)


NC, NS, L = 2, 16, 16
NW = NC * NS
NJ = D_MODEL // L
R = TOPK
NCH = KSEL // R
NB = 4
USTEP = 2
G = 8
NSLOT = 2
DW = D_MODEL // 2


def _perm(x, idx):
    return jnp.take_along_axis(x, idx, axis=0, mode="promise_in_bounds")


def _halves(w):
    lo = lax.bitcast_convert_type(lax.shift_left(w, 16), F32)
    hi = lax.bitcast_convert_type(jnp.bitwise_and(w, jnp.int32(-65536)), F32)
    return lo, hi


def _bf(w):
    return plsc.bitcast(w, BF)


def peer_sc(x, resid, idx, gates, uv_tab):
    T = x.shape[0]
    tpw = T // NW
    ngroups = tpw // G
    nchunks = G * NCH
    idx3 = idx.reshape(T * NCH, R)
    g3 = gates.reshape(T * NCH, R)
    mesh = plsc.VectorSubcoreMesh(core_axis_name="c", subcore_axis_name="s")

    @functools.partial(
        pl.kernel, mesh=mesh,
        out_type=jax.ShapeDtypeStruct((T, D_MODEL), F32),
        scratch_types=[
            pltpu.VMEM((NSLOT, G, DW), I32),
            pltpu.VMEM((NSLOT, G, D_MODEL), F32),
            pltpu.VMEM((NSLOT, nchunks, R), I32),
            pltpu.VMEM((NSLOT, nchunks, R), F32),
            pltpu.SemaphoreType.DMA((NSLOT,)),
            pltpu.SemaphoreType.DMA((NSLOT,)),
            pltpu.SemaphoreType.DMA((NSLOT,)),
        ] + [pltpu.VMEM((R, 2 * DW), I32) for _ in range(NB)]
          + [pltpu.SemaphoreType.DMA for _ in range(NB)],
        compiler_params=pltpu.CompilerParams(needs_layout_passes=False),
        name="peer_experts_sc",
    )
    def k(x_hbm, r_hbm, idx_hbm, g_hbm, uv_hbm, out_hbm, x_v, out_v, idx_v, g_v, st_sem, ix_sem, wb_sem, *ring):
        wid = lax.axis_index("s") * NC + lax.axis_index("c")
        bufs, sems = ring[:NB], ring[NB:]
        iota = lax.iota(I32, L)

        def tok0_of(g):
            return wid * tpw + g * G

        def stage_copies(g, slot):
            t0 = tok0_of(g)
            return (pltpu.make_async_copy(x_hbm.at[pl.ds(t0, G)], x_v.at[slot], st_sem.at[slot]),
                    pltpu.make_async_copy(r_hbm.at[pl.ds(t0, G)], out_v.at[slot], st_sem.at[slot]),
                    pltpu.make_async_copy(g_hbm.at[pl.ds(t0 * NCH, nchunks)], g_v.at[slot], st_sem.at[slot]))

        def idx_copy(g, slot):
            return pltpu.make_async_copy(idx_hbm.at[pl.ds(tok0_of(g) * NCH, nchunks)], idx_v.at[slot], ix_sem.at[slot])

        def wb_copy(g, slot):
            return pltpu.make_async_copy(out_v.at[slot], out_hbm.at[pl.ds(tok0_of(g), G)], wb_sem.at[slot])

        def gather_copy(slot, c, b):
            return pltpu.make_async_copy(uv_hbm.at[idx_v.at[slot, c]], bufs[b], sems[b])

        def compute(slot, c, b):
            ub = vb = bufs[b]
            t = c // NCH

            def ubody(mm, accs):
                xs_ = [_bf(x_v[slot, t, pl.ds((mm * USTEP + q) * L, L)]) for q in range(USTEP)]
                out = []
                for kk in range(R):
                    pr = [xs_[q] * _bf(ub[kk, pl.ds((mm * USTEP + q) * L, L)]) for q in range(USTEP)]
                    while len(pr) > 1:
                        pr = [pr[i] + pr[i + 1] for i in range(0, len(pr), 2)]
                    lo, hi = _halves(plsc.bitcast(pr[0], I32))
                    out.append(accs[kk] + (lo + hi))
                return tuple(out)

            accs = lax.fori_loop(0, NJ // (2 * USTEP), ubody, tuple(jnp.zeros((L,), F32) for _ in range(R)))
            vecs = list(accs)
            dist = L // 2
            while dist >= 1:
                pidx = jnp.bitwise_xor(iota, dist)
                low = jnp.bitwise_and(iota, dist) == 0
                nxt = []
                for kk in range(dist):
                    a = vecs[kk]
                    bvec = vecs[kk + dist]
                    a = a + _perm(a, pidx)
                    bvec = bvec + _perm(bvec, pidx)
                    nxt.append(jnp.where(low, a, bvec))
                vecs = nxt
                dist //= 2
            hid = vecs[0]
            z = GC * (hid + 0.044715 * hid * hid * hid)
            gel = hid / (1.0 + jnp.exp(-2.0 * z))
            w = g_v[slot, c, :] * gel
            wbs = []
            for kk in range(R):
                wb = _perm(w, jnp.full((L,), kk, I32))
                wbs.append(plsc.pack(wb, wb, format=plsc.PackFormat.INTERLEAVED))

            @plsc.parallel_loop(0, NJ // 2)
            def _(m):
                pr = [wbs[kk] * _bf(vb[kk, pl.ds(DW + m * L, L)]) for kk in range(R)]
                for _lvl in range(2):
                    pr = [pr[i] + pr[i + 1] for i in range(0, len(pr), 2)]
                los, his = [], []
                for q in pr:
                    lo, hi = _halves(plsc.bitcast(q, I32))
                    los.append(lo)
                    his.append(hi)
                while len(los) > 1:
                    los = [los[i] + los[i + 1] for i in range(0, len(los), 2)]
                    his = [his[i] + his[i + 1] for i in range(0, len(his), 2)]
                out_v[slot, t, pl.ds(m * L, L)] = out_v[slot, t, pl.ds(m * L, L)] + los[0]
                out_v[slot, t, pl.ds(DW + m * L, L)] = out_v[slot, t, pl.ds(DW + m * L, L)] + his[0]

        idx_copy(0, 0).start()
        for cp in stage_copies(0, 0):
            cp.start()
        idx_copy(0, 0).wait()
        for b in range(NB - 1):
            gather_copy(0, b, b).start()

        def group(g, carry):
            slot = g % NSLOT
            nslot = 1 - slot
            has_next = g + 1 < ngroups

            @pl.when(g >= 1)
            def _():
                wb_copy(g - 1, nslot).wait()

            @pl.when(has_next)
            def _():
                idx_copy(g + 1, nslot).start()
                for cp in stage_copies(g + 1, nslot):
                    cp.start()

            for cp in stage_copies(g, slot):
                cp.wait()

            def cbody(cc, c2):
                for b in range(NB):
                    c = cc * NB + b
                    cn = c + NB - 1
                    nb = (b + NB - 1) % NB

                    @pl.when(cn < nchunks)
                    def _():
                        gather_copy(slot, cn, nb).start()

                    @pl.when(jnp.logical_and(cn >= nchunks, has_next))
                    def _():
                        @pl.when(cn == nchunks)
                        def _():
                            idx_copy(g + 1, nslot).wait()

                        gather_copy(nslot, cn - nchunks, nb).start()

                    gather_copy(slot, c, b).wait()
                    compute(slot, c, b)
                return c2

            lax.fori_loop(0, nchunks // NB, cbody, 0)
            wb_copy(g, slot).start()
            return carry

        lax.fori_loop(0, ngroups, group, 0)
        wb_copy(ngroups - 1, (ngroups - 1) % NSLOT).wait()

    return k(x, resid, idx3, g3, uv_tab)


def _fn_body(x_ref, g_ref, o_ref):
    xf = x_ref[...]
    o_ref[...] = xf * lax.rsqrt(jnp.mean(xf * xf, axis=-1, keepdims=True) + EPS) * g_ref[...]


def _fn_body_into(x_ref, g_ref, prev_ref, o_ref):
    del prev_ref
    _fn_body(x_ref, g_ref, o_ref)


def final_norm(x, g, out, row0, t_total, tm=1024):
    T, d = x.shape
    blk0 = row0 // tm
    common = dict(grid=(T // tm,), out_specs=pl.BlockSpec((tm, d), lambda i: (i + blk0, 0)),
                  out_shape=jax.ShapeDtypeStruct((t_total, d), F32),
                  compiler_params=_cp(("parallel",)), name="final_norm")
    specs = [pl.BlockSpec((tm, d), lambda i: (i, 0)), pl.BlockSpec((1, d), lambda i: (0, 0))]
    if out is None:
        return pl.pallas_call(_fn_body, in_specs=specs, **common)(x, g)
    return pl.pallas_call(_fn_body_into, in_specs=specs + [pl.BlockSpec(memory_space=pl.ANY)],
                          input_output_aliases={2: 0}, **common)(x, g, out)


def _pack_body(u_ref, v_ref, o_ref):
    o_ref[:, 0:D_MODEL // 2] = pack_pairs(u_ref[...])
    o_ref[:, D_MODEL // 2:D_MODEL] = pack_pairs(v_ref[...])


def pack_tables(u, v, tm=512):
    e, d = u.shape
    return pl.pallas_call(
        _pack_body, grid=(e // tm,),
        in_specs=[pl.BlockSpec((tm, d), lambda i: (i, 0)), pl.BlockSpec((tm, d), lambda i: (i, 0))],
        out_specs=pl.BlockSpec((tm, d), lambda i: (i, 0)),
        out_shape=jax.ShapeDtypeStruct((e, d), I32),
        compiler_params=_cp(("parallel",)), name="pack_tables",
    )(u, v)


def _prep_layer(w_in, b_forget, conv_dw_w, conv_dw_b, conv_ln_g, conv_ln_b, rg_conv_w, rg_conv_b,
                rg_w_r, rg_b_r, rg_w_i, rg_b_i, rg_lambda, w_out, peer_wq, peer_k1, peer_k2):
    f0 = 3 * D_ATT
    wf = jnp.zeros((D_MODEL, FPAD), BF).at[:, 0:ATT_HEADS].set(w_in[:, f0:f0 + ATT_HEADS].astype(BF))
    bfg = jnp.zeros((1, FPAD), F32).at[0, 0:ATT_HEADS].set(b_forget)
    cw = jnp.zeros((32, D_CONV), F32).at[0:CONV_K].set(conv_dw_w)
    rw = jnp.zeros((8, D_RNN), F32).at[0:RNN_CONV_K].set(rg_conv_w)
    bd = lambda w: jax.scipy.linalg.block_diag(*[w[i] for i in range(RNN_BLOCKS)]).astype(BF)
    row = lambda v: v.reshape(1, -1).astype(F32)
    keys = jnp.stack([peer_k1, peer_k2], axis=1).reshape(2 * PEER_HEADS, N_KEYS, D_HALF).astype(BF)
    return dict(wqk=w_in[:, 0:2 * D_ATT].astype(BF), wvt=w_in[:, 2 * D_ATT:f0].T.astype(BF), wf=wf,
                wrest=w_in[:, f0 + ATT_HEADS:].astype(BF), bfg=bfg,
                cw=cw, cb=row(conv_dw_b), lg=row(conv_ln_g), lb=row(conv_ln_b),
                rw=rw, rb=row(rg_conv_b), wr=bd(rg_w_r), br=row(rg_b_r), wi=bd(rg_w_i), bi=row(rg_b_i),
                lam=row(rg_lambda), woa=w_out[0:D_ATT].astype(BF), wob=w_out[D_ATT:].astype(BF),
                wq=peer_wq.astype(BF), keys=keys)


def kernel(x, norm1_g, w_in, b_forget, conv_dw_w, conv_dw_b, conv_ln_g, conv_ln_b,
           rg_conv_w, rg_conv_b, rg_w_r, rg_b_r, rg_w_i, rg_b_i, rg_lambda, w_out,
           norm2_g, peer_wq, peer_k1, peer_k2, peer_u, peer_v, final_g):
    b, s, d = x.shape
    params = [_prep_layer(w_in[l], b_forget[l], conv_dw_w[l], conv_dw_b[l], conv_ln_g[l], conv_ln_b[l],
                          rg_conv_w[l], rg_conv_b[l], rg_w_r[l], rg_b_r[l], rg_w_i[l], rg_b_i[l], rg_lambda[l],
                          w_out[l], peer_wq[l], peer_k1[l], peer_k2[l]) for l in range(DEPTH)]
    tabs = [pack_tables(peer_u[l], peer_v[l]) for l in range(DEPTH)]
    bs = b // N_SLICES
    T = bs * s
    xf = x.reshape(b * s, d)
    xs = [None] * N_SLICES
    prev = None
    for l in range(DEPTH):
        p = params[l]
        for i in range(N_SLICES):
            src, row0 = (xf, i * T) if l == 0 else (xs[i], 0)
            tie_tab = l == 0 and i < DEPTH
            deps = ([prev] if prev is not None else []) + ([tabs[i]] if tie_tab else [])
            if deps:
                tied = lax.optimization_barrier((src, *deps))
                src = tied[0]
                if prev is not None:
                    prev = tied[1]
                if tie_tab:
                    tabs[i] = tied[-1]
            qkb, vt, rest = in_proj(src, norm1_g[l].reshape(1, d), p, s, T, row0)
            y_att = attention(qkb, vt, bs, s)
            y_cr = mixers(rest, p, bs, s)
            npiece = FIRST_PIECES if (l == 0 and i == 0) else 1
            tp = T // npiece
            pieces = []
            for j in range(npiece):
                if j > 0:
                    src, experts = lax.optimization_barrier((src, experts))
                x1, h2p, st = out_proj(src, y_att, y_cr, p, norm2_g[l].reshape(1, d), row0, j * tp, tp)
                experts, gates = route(st)
                pieces.append(peer_sc(h2p, x1, experts, gates, tabs[l]))
            prev = experts
            xs[i] = pieces[0] if npiece == 1 else jnp.concatenate(pieces, axis=0)
    out = None
    for i in range(N_SLICES):
        out = final_norm(xs[i], final_g.reshape(1, d), out, i * T, b * s)
    return out.reshape(b, s, d)
```

```python
import functools
import math

import jax
import jax.numpy as jnp
from jax import lax
from jax.experimental import pallas as pl
from jax.experimental.pallas import tpu as pltpu
from jax.experimental.pallas import tpu_sc as plsc

BF = jnp.bfloat16
F32 = jnp.float32
I32 = jnp.int32

D_MODEL = 1024
DEPTH = 2
ATT_HEADS = 8
ATT_HD = 64
D_ATT = ATT_HEADS * ATT_HD
D_CONV = 256
CONV_K = 31
D_RNN = 256
RNN_BLOCKS = 4
RNN_CONV_K = 4
RG_C = 8.0
EPS = 1e-6
N_REST = 2 * D_CONV + 2 * D_RNN
PEER_HEADS = 8
N_KEYS = 128
D_HALF = 128
TOPK = 16
KSEL = PEER_HEADS * TOPK
GC = 0.7978845608028654
NEG = float("-inf")

N_SLICES = 4
LANES = 128
VMEM_LIMIT = 48 * 1024 * 1024


def _cp(sem):
    return pltpu.CompilerParams(dimension_semantics=sem, vmem_limit_bytes=VMEM_LIMIT)


def _split3(x):
    hi = x.astype(BF)
    r = x - hi.astype(F32)
    mid = r.astype(BF)
    lo = (r - mid.astype(F32)).astype(BF)
    return hi, mid, lo


def _nt(a, b):
    return lax.dot_general(a, b, (((1,), (1,)), ((), ())), preferred_element_type=F32)


def _dot(a, b):
    return jnp.dot(a, b, preferred_element_type=F32)


def _sigmoid(x):
    return 1.0 / (1.0 + jnp.exp(-x))


def _gelu(x):
    return 0.5 * x * (1.0 + jnp.tanh(GC * (x + 0.044715 * x * x * x)))


LOG2E = 1.4426950408889634
NSPLIT = 3
FPAD = 16


def _inproj_body(x_ref, g_ref, wqk_ref, wvt_ref, wf_ref, wrest_ref, bf_ref, tri_ref, place_ref,
                 qkb_ref, vt_ref, rest_ref, carry_ref, *, blocks_per_seq, tm):
    i = pl.program_id(0)
    x = x_ref[...]
    h = x * lax.rsqrt(jnp.mean(x * x, axis=-1, keepdims=True) + EPS) * g_ref[...]
    hb = h.astype(BF)
    qk = _dot(hb, wqk_ref[...])
    col = lax.broadcasted_iota(I32, (1, 2 * D_ATT), 1)
    qk = jnp.where(col < D_ATT, qk * (LOG2E / math.sqrt(ATT_HD)), qk)
    qkb_ref[:, 0:2 * D_ATT] = qk.astype(BF)
    vt_ref[...] = _nt(wvt_ref[...], hb).astype(BF)
    rest_ref[...] = _dot(hb, wrest_ref[...])
    ft = _dot(hb, wf_ref[...]) + bf_ref[...]
    lf = jnp.minimum(ft, 0.0) - jnp.log(1.0 + jnp.exp(-jnp.abs(ft)))
    hi, mid, lo = _split3(lf)
    tri = tri_ref[...]
    cs = _dot(tri, hi) + _dot(tri, mid) + _dot(tri, lo)

    @pl.when(i % blocks_per_seq == 0)
    def _():
        carry_ref[...] = jnp.zeros_like(carry_ref)

    cum = cs + carry_ref[...]
    carry_ref[...] = cum[tm - 1:tm, :]
    pieces = _split3(cum * (-LOG2E))
    kb = _dot(pieces[0], place_ref[0]) + _dot(pieces[1], place_ref[1]) + _dot(pieces[2], place_ref[2])
    qkb_ref[:, 2 * D_ATT:3 * D_ATT] = kb.astype(BF)


def bias_lane(hh, j):
    return (ATT_HD if hh == 0 else 0) + j


def in_proj(x, g, p, seq, T, row0=0, tm=512):
    blk0 = row0 // tm
    tri =(lax.broadcasted_iota(I32, (tm, tm), 0) >= lax.broadcasted_iota(I32, (tm, tm), 1)).astype(BF)
    shp = (NSPLIT, FPAD, D_ATT)
    hd = lax.broadcasted_iota(I32, shp, 1)
    target = (hd // 2) * LANES + jnp.where(hd % 2 == 0, ATT_HD, 0) + lax.broadcasted_iota(I32, shp, 0)
    place = ((lax.broadcasted_iota(I32, shp, 2) == target) & (hd < ATT_HEADS)).astype(BF)
    body = functools.partial(_inproj_body, blocks_per_seq=seq // tm, tm=tm)
    return pl.pallas_call(
        body, grid=(T // tm,),
        in_specs=[pl.BlockSpec((tm, D_MODEL), lambda i: (i + blk0, 0)),
                  pl.BlockSpec((1, D_MODEL), lambda i: (0, 0)),
                  pl.BlockSpec((D_MODEL, 2 * D_ATT), lambda i: (0, 0)),
                  pl.BlockSpec((D_ATT, D_MODEL), lambda i: (0, 0)),
                  pl.BlockSpec((D_MODEL, FPAD), lambda i: (0, 0)),
                  pl.BlockSpec((D_MODEL, N_REST), lambda i: (0, 0)),
                  pl.BlockSpec((1, FPAD), lambda i: (0, 0)),
                  pl.BlockSpec((tm, tm), lambda i: (0, 0)),
                  pl.BlockSpec(shp, lambda i: (0, 0, 0))],
        out_specs=[pl.BlockSpec((tm, 3 * D_ATT), lambda i: (i, 0)),
                   pl.BlockSpec((D_ATT, tm), lambda i: (0, i)),
                   pl.BlockSpec((tm, N_REST), lambda i: (i, 0))],
        out_shape=[jax.ShapeDtypeStruct((T, 3 * D_ATT), BF),
                   jax.ShapeDtypeStruct((D_ATT, T), BF),
                   jax.ShapeDtypeStruct((T, N_REST), F32)],
        scratch_shapes=[pltpu.VMEM((1, FPAD), F32)],
        compiler_params=_cp(("arbitrary",)), name="in_proj",
    )(x, g, p["wqk"], p["wvt"], p["wf"], p["wrest"], p["bfg"], tri, place)


def _attn_body(q_ref, k_ref, kb_ref, vt_ref, o_ref, m_ref, acc_ref, *, tq, tk):
    qi = pl.program_id(2)
    ki = pl.program_id(3)

    @pl.when(ki == 0)
    def _():
        m_ref[...] = jnp.full_like(m_ref, NEG)
        acc_ref[...] = jnp.zeros_like(acc_ref)

    lane = lax.broadcasted_iota(I32, (1, LANES), 1)
    first = lane < ATT_HD
    vrow = lax.broadcasted_iota(I32, (LANES, 1), 0) < ATT_HD

    def step(masked):
        q = q_ref[...]
        k = k_ref[...]
        kb = kb_ref[...]
        vt = vt_ref[...]
        if masked:
            keep = (lax.broadcasted_iota(I32, (tk, tq), 0) <= lax.broadcasted_iota(I32, (tk, tq), 1))
        for hh in range(2):
            own = first if hh == 0 else jnp.logical_not(first)
            ones = (lane >= bias_lane(hh, 0)) & (lane < bias_lane(hh, NSPLIT))
            qa = jnp.where(own, q, jnp.where(ones, 1.0, 0.0).astype(BF))
            ka = jnp.where(own, k, kb)
            st = _nt(ka, qa)
            if masked:
                st = jnp.where(keep, st, NEG)
            m_prev = m_ref[hh]
            m_new = jnp.maximum(m_prev, jnp.max(st, axis=0, keepdims=True))
            alpha = jnp.exp2(m_prev - m_new)
            p = jnp.exp2(st - m_new).astype(BF)
            m_ref[hh] = m_new
            vown = vrow if hh == 0 else jnp.logical_not(vrow)
            va = jnp.where(vown, vt, jnp.ones_like(vt))
            acc_ref[hh] = alpha * acc_ref[hh] + _dot(va, p)

    @pl.when(ki < qi)
    def _():
        step(False)

    @pl.when(ki == qi)
    def _():
        step(True)
        a0 = acc_ref[0]
        a1 = acc_ref[1]
        ot = jnp.where(vrow, a0 / a0[ATT_HD:ATT_HD + 1, :], a1 / a1[0:1, :])
        o_ref[...] = ot.T.astype(o_ref.dtype)


def attention(qkb, vt, batch, seq, tq=512):
    T = qkb.shape[0]
    tk = tq
    nq = seq // tq
    npair = ATT_HEADS // 2
    body = functools.partial(_attn_body, tq=tq, tk=tk)
    kblk = lambda b, qi, ki: b * nq + jnp.minimum(ki, qi)
    return pl.pallas_call(
        body, grid=(batch, npair, nq, nq),
        in_specs=[pl.BlockSpec((tq, LANES), lambda b, p, qi, ki: (b * nq + qi, p)),
                  pl.BlockSpec((tk, LANES), lambda b, p, qi, ki: (kblk(b, qi, ki), npair + p)),
                  pl.BlockSpec((tk, LANES), lambda b, p, qi, ki: (kblk(b, qi, ki), 2 * npair + p)),
                  pl.BlockSpec((LANES, tk), lambda b, p, qi, ki: (p, kblk(b, qi, ki)))],
        out_specs=pl.BlockSpec((tq, LANES), lambda b, p, qi, ki: (b * nq + qi, p)),
        out_shape=jax.ShapeDtypeStruct((T, D_ATT), BF),
        scratch_shapes=[pltpu.VMEM((2, 1, tq), F32), pltpu.VMEM((2, LANES, tq), F32)],
        compiler_params=_cp(("parallel", "parallel", "parallel", "arbitrary")), name="fox_attention",
    )(qkb, qkb, qkb, vt)


CONV_HALO = 32
RG_HALO = 8


def _mix_body(rest_ref, cw_ref, cb_ref, lg_ref, lb_ref, rw_ref, rb_ref, wr_ref, br_ref, wi_ref, bi_ref, lam_ref,
              o_ref, ybuf, xbuf, hc, *, ts):
    si = pl.program_id(1)

    @pl.when(si == 0)
    def _():
        ybuf[0:CONV_HALO, :] = jnp.zeros((CONV_HALO, D_CONV), F32)
        xbuf[0:RG_HALO, :] = jnp.zeros((RG_HALO, D_RNN), F32)
        hc[...] = jnp.zeros_like(hc)

    y = rest_ref[:, 0:D_CONV] * _sigmoid(rest_ref[:, D_CONV:2 * D_CONV])
    ybuf[CONV_HALO:CONV_HALO + ts, :] = y
    acc = jnp.zeros((ts, D_CONV), F32)
    for k in range(CONV_K):
        acc = acc + cw_ref[k:k + 1, :] * ybuf[pl.ds(CONV_HALO - (CONV_K - 1) + k, ts), :]
    yc = acc + cb_ref[...]
    mu = jnp.mean(yc, axis=-1, keepdims=True)
    var = jnp.mean(jnp.square(yc - mu), axis=-1, keepdims=True)
    yn = (yc - mu) * lax.rsqrt(var + EPS) * lg_ref[...] + lb_ref[...]
    o_ref[:, 0:D_CONV] = (yn * _sigmoid(yn)).astype(o_ref.dtype)
    ybuf[0:CONV_HALO, :] = ybuf[ts:ts + CONV_HALO, :]

    xbuf[RG_HALO:RG_HALO + ts, :] = rest_ref[:, 2 * D_CONV:2 * D_CONV + D_RNN]
    xc = jnp.zeros((ts, D_RNN), F32)
    for k in range(RNN_CONV_K):
        xc = xc + rw_ref[k:k + 1, :] * xbuf[pl.ds(RG_HALO - (RNN_CONV_K - 1) + k, ts), :]
    xc = xc + rb_ref[...]
    xbuf[0:RG_HALO, :] = xbuf[ts:ts + RG_HALO, :]
    xcb = xc.astype(BF)
    r = _sigmoid(_dot(xcb, wr_ref[...]) + br_ref[...])
    gi = _sigmoid(_dot(xcb, wi_ref[...]) + bi_ref[...])
    nl = -lam_ref[...]
    sp = jnp.maximum(nl, 0.0) + jnp.log(1.0 + jnp.exp(-jnp.abs(nl)))
    log_a = -RG_C * r * sp
    a = jnp.exp(log_a)
    bt = jnp.sqrt(1.0 - jnp.exp(2.0 * log_a)) * (gi * xc)
    row = lax.broadcasted_iota(I32, (ts, 1), 0)
    sh = 1
    while sh < ts:
        live = row >= sh
        a_s = jnp.where(live, pltpu.roll(a, sh, 0), 1.0)
        b_s = jnp.where(live, pltpu.roll(bt, sh, 0), 0.0)
        bt = bt + a * b_s
        a = a * a_s
        sh *= 2
    h = bt + a * hc[...]
    hc[...] = h[ts - 1:ts, :]
    gate_in = rest_ref[:, 2 * D_CONV + D_RNN:2 * D_CONV + 2 * D_RNN]
    o_ref[:, D_CONV:D_CONV + D_RNN] = (h * _gelu(gate_in)).astype(o_ref.dtype)


def mixers(rest, p, batch, seq, ts=512):
    T = rest.shape[0]
    ns = seq // ts
    body = functools.partial(_mix_body, ts=ts)
    vec = lambda: pl.BlockSpec((1, D_CONV), lambda b, s: (0, 0))
    return pl.pallas_call(
        body, grid=(batch, ns),
        in_specs=[pl.BlockSpec((ts, N_REST), lambda b, s: (b * ns + s, 0)),
                  pl.BlockSpec((32, D_CONV), lambda b, s: (0, 0)), vec(), vec(), vec(),
                  pl.BlockSpec((8, D_RNN), lambda b, s: (0, 0)), vec(),
                  pl.BlockSpec((D_RNN, D_RNN), lambda b, s: (0, 0)), vec(),
                  pl.BlockSpec((D_RNN, D_RNN), lambda b, s: (0, 0)), vec(), vec()],
        out_specs=pl.BlockSpec((ts, D_CONV + D_RNN), lambda b, s: (b * ns + s, 0)),
        out_shape=jax.ShapeDtypeStruct((T, D_CONV + D_RNN), BF),
        scratch_shapes=[pltpu.VMEM((ts + CONV_HALO, D_CONV), F32), pltpu.VMEM((ts + RG_HALO, D_RNN), F32),
                        pltpu.VMEM((1, D_RNN), F32)],
        compiler_params=_cp(("arbitrary", "arbitrary")), name="conv_rglru",
    )(rest, p["cw"], p["cb"], p["lg"], p["lb"], p["rw"], p["rb"], p["wr"], p["br"], p["wi"], p["bi"], p["lam"])


def pack_pairs(a):
    half = a.shape[1] // 2
    r = lax.bitcast_convert_type(a, I32)
    r = r + jnp.int32(0x7FFF) + jnp.bitwise_and(lax.shift_right_logical(r, 16), 1)
    lo = lax.shift_right_logical(r[:, :half], 16)
    hi = jnp.bitwise_and(r[:, half:], jnp.int32(-65536))
    return jnp.bitwise_or(hi, lo)


def _outproj_body(x_ref, ya_ref, yc_ref, woa_ref, wob_ref, g2_ref, wq_ref, keys_ref, x1_ref, h2p_ref, st_ref):
    x1 = x_ref[...] + _dot(ya_ref[...], woa_ref[...]) + _dot(yc_ref[...], wob_ref[...])
    x1_ref[...] = x1
    h2 = x1 * lax.rsqrt(jnp.mean(x1 * x1, axis=-1, keepdims=True) + EPS) * g2_ref[...]
    h2p_ref[...] = pack_pairs(h2)
    q = _dot(h2.astype(BF), wq_ref[...]).astype(BF)
    for g in range(2 * PEER_HEADS):
        st_ref[g * N_KEYS:(g + 1) * N_KEYS, :] = _nt(keys_ref[g], q[:, g * D_HALF:(g + 1) * D_HALF])


def out_proj(x, ya, yc, p, g2, row0=0, tm=256):
    T = ya.shape[0]
    ng = 2 * PEER_HEADS
    blk0 = row0 // tm
    return pl.pallas_call(
        _outproj_body, grid=(T // tm,),
        in_specs=[pl.BlockSpec((tm, D_MODEL), lambda i: (i + blk0, 0)),
                  pl.BlockSpec((tm, D_ATT), lambda i: (i, 0)),
                  pl.BlockSpec((tm, D_CONV + D_RNN), lambda i: (i, 0)),
                  pl.BlockSpec((D_ATT, D_MODEL), lambda i: (0, 0)),
                  pl.BlockSpec((D_CONV + D_RNN, D_MODEL), lambda i: (0, 0)),
                  pl.BlockSpec((1, D_MODEL), lambda i: (0, 0)),
                  pl.BlockSpec((D_MODEL, ng * D_HALF), lambda i: (0, 0)),
                  pl.BlockSpec((ng, N_KEYS, D_HALF), lambda i: (0, 0, 0))],
        out_specs=[pl.BlockSpec((tm, D_MODEL), lambda i: (i, 0)),
                   pl.BlockSpec((tm, D_MODEL // 2), lambda i: (i, 0)),
                   pl.BlockSpec((ng * N_KEYS, tm), lambda i: (0, i))],
        out_shape=[jax.ShapeDtypeStruct((T, D_MODEL), F32),
                   jax.ShapeDtypeStruct((T, D_MODEL // 2), I32),
                   jax.ShapeDtypeStruct((ng * N_KEYS, T), F32)],
        compiler_params=_cp(("parallel",)), name="out_proj_peer_scores",
    )(x, ya, yc, p["woa"], p["wob"], g2, p["wq"], p["keys"])


BIG_ID = 1 << 20
SUBL = 8
NPAR = 4
SEL_CHAIN = 4


def _take_rounds(problems, nrounds):
    state = [list(slabs) for slabs, _ in problems]
    res = [([], []) for _ in problems]
    for _ in range(nrounds):
        for pi, (_, ids) in enumerate(problems):
            slabs = state[pi]
            m8 = slabs[0]
            for sl in slabs[1:]:
                m8 = jnp.maximum(m8, sl)
            m = jnp.max(m8, axis=0, keepdims=True)
            chains = []
            for c0 in range(0, len(slabs), SEL_CHAIN):
                v = jnp.full((SUBL, LANES), BIG_ID, I32)
                for sl, idc in zip(reversed(slabs[c0:c0 + SEL_CHAIN]), reversed(ids[c0:c0 + SEL_CHAIN])):
                    v = jnp.where(sl == m, idc, v)
                chains.append(v)
            while len(chains) > 1:
                chains = [jnp.minimum(chains[i], chains[i + 1]) if i + 1 < len(chains) else chains[i]
                          for i in range(0, len(chains), 2)]
            pick = jnp.min(chains[0], axis=0, keepdims=True)
            state[pi] = [jnp.where(idc == pick, NEG, sl) for sl, idc in zip(slabs, ids)]
            res[pi][0].append(m)
            res[pi][1].append(pick)
    return res


def _route_body(st_ref, e_ref, g_ref, v_scr, i_scr, sv_scr, ci_scr, et_scr, gt_scr):
    ng = 2 * PEER_HEADS
    sub = lax.broadcasted_iota(I32, (SUBL, LANES), 0)
    key_ids = [sub + SUBL * i for i in range(N_KEYS // SUBL)]

    def stage1(gg, carry):
        probs = []
        for q in range(NPAR):
            base = pl.multiple_of((gg * NPAR + q) * N_KEYS, N_KEYS)
            probs.append(([st_ref[pl.ds(base + SUBL * i, SUBL), :] for i in range(N_KEYS // SUBL)], key_ids))
        for q, (vals, picks) in enumerate(_take_rounds(probs, TOPK)):
            for r in range(TOPK):
                v_scr[gg * NPAR + q, r:r + 1, :] = vals[r]
                i_scr[gg * NPAR + q, r:r + 1, :] = picks[r]
        return carry

    lax.fori_loop(0, ng // NPAR, stage1, 0)

    def stage2(hh, carry):
        probs = []
        for q in range(NPAR):
            h = hh * NPAR + q
            v1 = v_scr[2 * h]
            v2 = v_scr[2 * h + 1]
            slabs = [v1[0:1, :] + v2[0:SUBL, :], v1[0:1, :] + v2[SUBL:TOPK, :]]
            ids = [sub, sub + SUBL]
            for i in range(1, TOPK):
                nj = TOPK // (i + 1)
                slabs.append(jnp.where(sub < nj, v1[i:i + 1, :] + v2[0:SUBL, :], NEG))
                ids.append(sub + i * TOPK)
            probs.append((slabs, ids))
        for q, (vals, picks) in enumerate(_take_rounds(probs, TOPK)):
            h = hh * NPAR + q
            i1 = i_scr[2 * h]
            i2 = i_scr[2 * h + 1]
            for r in range(TOPK):
                sv_scr[q, r:r + 1, :] = vals[r]
                ci_scr[q, r:r + 1, :] = picks[r]
            sv = sv_scr[q]
            ci = ci_scr[q]
            ci_hi = lax.shift_right_logical(ci, 4)
            ci_lo = jnp.bitwise_and(ci, TOPK - 1)
            e1 = jnp.zeros((TOPK, LANES), I32)
            e2 = jnp.zeros((TOPK, LANES), I32)
            for i in range(TOPK):
                e1 = jnp.where(ci_hi == i, i1[i:i + 1, :], e1)
                e2 = jnp.where(ci_lo == i, i2[i:i + 1, :], e2)
            p = jnp.exp(sv - sv[0:1, :])
            gates = p / jnp.sum(p, axis=0, keepdims=True)
            et_scr[pl.ds(pl.multiple_of(h * TOPK, TOPK), TOPK), :] = e1 * N_KEYS + e2
            gt_scr[pl.ds(pl.multiple_of(h * TOPK, TOPK), TOPK), :] = gates
        return carry

    lax.fori_loop(0, PEER_HEADS // NPAR, stage2, 0)
    e_ref[...] = et_scr[...].T
    g_ref[...] = gt_scr[...].T


def route(st):
    T = st.shape[1]
    ng = 2 * PEER_HEADS
    return pl.pallas_call(
        _route_body, grid=(T // LANES,),
        in_specs=[pl.BlockSpec((ng * N_KEYS, LANES), lambda i: (0, i))],
        out_specs=[pl.BlockSpec((LANES, KSEL), lambda i: (i, 0)),
                   pl.BlockSpec((LANES, KSEL), lambda i: (i, 0))],
        out_shape=[jax.ShapeDtypeStruct((T, KSEL), I32),
                   jax.ShapeDtypeStruct((T, KSEL), F32)],
        scratch_shapes=[pltpu.VMEM((ng, TOPK, LANES), F32), pltpu.VMEM((ng, TOPK, LANES), I32),
                        pltpu.VMEM((NPAR, TOPK, LANES), F32), pltpu.VMEM((NPAR, TOPK, LANES), I32),
                        pltpu.VMEM((KSEL, LANES), I32), pltpu.VMEM((KSEL, LANES), F32)],
        compiler_params=_cp(("parallel",)), name="peer_route",
    )(st)


NC, NS, L = 2, 16, 16
NW = NC * NS
NJ = D_MODEL // L
R = TOPK
NCH = KSEL // R
NB = 4
USTEP = 2
G = 8
NSLOT = 2
DW = D_MODEL // 2


def _perm(x, idx):
    return jnp.take_along_axis(x, idx, axis=0, mode="promise_in_bounds")


def _halves(w):
    lo = lax.bitcast_convert_type(lax.shift_left(w, 16), F32)
    hi = lax.bitcast_convert_type(jnp.bitwise_and(w, jnp.int32(-65536)), F32)
    return lo, hi


def _bf(w):
    return plsc.bitcast(w, BF)


def peer_sc(x, resid, idx, gates, uv_tab):
    T = x.shape[0]
    tpw = T // NW
    ngroups = tpw // G
    nchunks = G * NCH
    idx3 = idx.reshape(T * NCH, R)
    g3 = gates.reshape(T * NCH, R)
    mesh = plsc.VectorSubcoreMesh(core_axis_name="c", subcore_axis_name="s")

    @functools.partial(
        pl.kernel, mesh=mesh,
        out_type=jax.ShapeDtypeStruct((T, D_MODEL), F32),
        scratch_types=[
            pltpu.VMEM((NSLOT, G, DW), I32),
            pltpu.VMEM((NSLOT, G, D_MODEL), F32),
            pltpu.VMEM((NSLOT, nchunks, R), I32),
            pltpu.VMEM((NSLOT, nchunks, R), F32),
            pltpu.SemaphoreType.DMA((NSLOT,)),
            pltpu.SemaphoreType.DMA((NSLOT,)),
            pltpu.SemaphoreType.DMA((NSLOT,)),
            pltpu.VMEM((NB, R, 2 * DW), I32),
            pltpu.SemaphoreType.DMA((NB,)),
        ],
        compiler_params=pltpu.CompilerParams(needs_layout_passes=False),
        name="peer_experts_sc",
    )
    def k(x_hbm, r_hbm, idx_hbm, g_hbm, uv_hbm, out_hbm, x_v, out_v, idx_v, g_v, st_sem, ix_sem, wb_sem, ring, ring_sem):
        wid = lax.axis_index("s") * NC + lax.axis_index("c")
        iota = lax.iota(I32, L)

        def tok0_of(g):
            return wid * tpw + g * G

        def stage_copies(g, slot):
            t0 = tok0_of(g)
            return (pltpu.make_async_copy(x_hbm.at[pl.ds(t0, G)], x_v.at[slot], st_sem.at[slot]),
                    pltpu.make_async_copy(r_hbm.at[pl.ds(t0, G)], out_v.at[slot], st_sem.at[slot]),
                    pltpu.make_async_copy(g_hbm.at[pl.ds(t0 * NCH, nchunks)], g_v.at[slot], st_sem.at[slot]))

        def idx_copy(g, slot):
            return pltpu.make_async_copy(idx_hbm.at[pl.ds(tok0_of(g) * NCH, nchunks)], idx_v.at[slot], ix_sem.at[slot])

        def wb_copy(g, slot):
            return pltpu.make_async_copy(out_v.at[slot], out_hbm.at[pl.ds(tok0_of(g), G)], wb_sem.at[slot])

        def gather_copy(slot, c, b):
            return pltpu.make_async_copy(uv_hbm.at[idx_v.at[slot, c]], ring.at[b], ring_sem.at[b])

        def compute(slot, c, b):
            t = c // NCH

            def ubody(mm, accs):
                xs_ = [_bf(x_v[slot, t, pl.ds((mm * USTEP + q) * L, L)]) for q in range(USTEP)]
                out = []
                for kk in range(R):
                    pr = [xs_[q] * _bf(ring[b, kk, pl.ds((mm * USTEP + q) * L, L)]) for q in range(USTEP)]
                    while len(pr) > 1:
                        pr = [pr[i] + pr[i + 1] for i in range(0, len(pr), 2)]
                    lo, hi = _halves(plsc.bitcast(pr[0], I32))
                    out.append(accs[kk] + (lo + hi))
                return tuple(out)

            accs = lax.fori_loop(0, NJ // (2 * USTEP), ubody, tuple(jnp.zeros((L,), F32) for _ in range(R)))
            vecs = list(accs)
            dist = L // 2
            while dist >= 1:
                pidx = jnp.bitwise_xor(iota, dist)
                low = jnp.bitwise_and(iota, dist) == 0
                nxt = []
                for kk in range(dist):
                    a = vecs[kk]
                    bvec = vecs[kk + dist]
                    a = a + _perm(a, pidx)
                    bvec = bvec + _perm(bvec, pidx)
                    nxt.append(jnp.where(low, a, bvec))
                vecs = nxt
                dist //= 2
            hid = vecs[0]
            z = GC * (hid + 0.044715 * hid * hid * hid)
            gel = hid / (1.0 + jnp.exp(-2.0 * z))
            w = g_v[slot, c, :] * gel
            wbs = []
            for kk in range(R):
                wb = _perm(w, jnp.full((L,), kk, I32))
                wbs.append(plsc.pack(wb, wb, format=plsc.PackFormat.INTERLEAVED))

            @plsc.parallel_loop(0, NJ // 2)
            def _(m):
                pr = [wbs[kk] * _bf(ring[b, kk, pl.ds(DW + m * L, L)]) for kk in range(R)]
                for _lvl in range(2):
                    pr = [pr[i] + pr[i + 1] for i in range(0, len(pr), 2)]
                los, his = [], []
                for q in pr:
                    lo, hi = _halves(plsc.bitcast(q, I32))
                    los.append(lo)
                    his.append(hi)
                while len(los) > 1:
                    los = [los[i] + los[i + 1] for i in range(0, len(los), 2)]
                    his = [his[i] + his[i + 1] for i in range(0, len(his), 2)]
                out_v[slot, t, pl.ds(m * L, L)] = out_v[slot, t, pl.ds(m * L, L)] + los[0]
                out_v[slot, t, pl.ds(DW + m * L, L)] = out_v[slot, t, pl.ds(DW + m * L, L)] + his[0]

        idx_copy(0, 0).start()
        for cp in stage_copies(0, 0):
            cp.start()
        idx_copy(0, 0).wait()
        for b in range(NB - 1):
            gather_copy(0, b, b).start()

        def group(g, carry):
            slot = g % NSLOT
            nslot = 1 - slot
            has_next = g + 1 < ngroups

            @pl.when(g >= 1)
            def _():
                wb_copy(g - 1, nslot).wait()

            @pl.when(has_next)
            def _():
                idx_copy(g + 1, nslot).start()
                for cp in stage_copies(g + 1, nslot):
                    cp.start()

            for cp in stage_copies(g, slot):
                cp.wait()

            def cbody(c, c2):
                b = c % NB
                cn = c + NB - 1
                nb = cn % NB

                @pl.when(cn < nchunks)
                def _():
                    gather_copy(slot, cn, nb).start()

                @pl.when(jnp.logical_and(cn >= nchunks, has_next))
                def _():
                    @pl.when(cn == nchunks)
                    def _():
                        idx_copy(g + 1, nslot).wait()

                    gather_copy(nslot, cn - nchunks, nb).start()

                gather_copy(slot, c, b).wait()
                compute(slot, c, b)
                return c2

            lax.fori_loop(0, nchunks, cbody, 0)
            wb_copy(g, slot).start()
            return carry

        lax.fori_loop(0, ngroups, group, 0)
        wb_copy(ngroups - 1, (ngroups - 1) % NSLOT).wait()

    return k(x, resid, idx3, g3, uv_tab)


def _fn_body(x_ref, g_ref, o_ref):
    xf = x_ref[...]
    o_ref[...] = xf * lax.rsqrt(jnp.mean(xf * xf, axis=-1, keepdims=True) + EPS) * g_ref[...]


def _fn_body_into(x_ref, g_ref, prev_ref, o_ref):
    del prev_ref
    _fn_body(x_ref, g_ref, o_ref)


def final_norm(x, g, out, row0, t_total, tm=1024):
    T, d = x.shape
    blk0 = row0 // tm
    common = dict(grid=(T // tm,), out_specs=pl.BlockSpec((tm, d), lambda i: (i + blk0, 0)),
                  out_shape=jax.ShapeDtypeStruct((t_total, d), F32),
                  compiler_params=_cp(("parallel",)), name="final_norm")
    specs = [pl.BlockSpec((tm, d), lambda i: (i, 0)), pl.BlockSpec((1, d), lambda i: (0, 0))]
    if out is None:
        return pl.pallas_call(_fn_body, in_specs=specs, **common)(x, g)
    return pl.pallas_call(_fn_body_into, in_specs=specs + [pl.BlockSpec(memory_space=pl.ANY)],
                          input_output_aliases={2: 0}, **common)(x, g, out)


def _pack_body(u_ref, v_ref, o_ref):
    o_ref[:, 0:D_MODEL // 2] = pack_pairs(u_ref[...])
    o_ref[:, D_MODEL // 2:D_MODEL] = pack_pairs(v_ref[...])


def pack_tables(u, v, tm=512):
    e, d = u.shape
    return pl.pallas_call(
        _pack_body, grid=(e // tm,),
        in_specs=[pl.BlockSpec((tm, d), lambda i: (i, 0)), pl.BlockSpec((tm, d), lambda i: (i, 0))],
        out_specs=pl.BlockSpec((tm, d), lambda i: (i, 0)),
        out_shape=jax.ShapeDtypeStruct((e, d), I32),
        compiler_params=_cp(("parallel",)), name="pack_tables",
    )(u, v)


def _prep_layer(w_in, b_forget, conv_dw_w, conv_dw_b, conv_ln_g, conv_ln_b, rg_conv_w, rg_conv_b,
                rg_w_r, rg_b_r, rg_w_i, rg_b_i, rg_lambda, w_out, peer_wq, peer_k1, peer_k2):
    f0 = 3 * D_ATT
    wf = jnp.zeros((D_MODEL, FPAD), BF).at[:, 0:ATT_HEADS].set(w_in[:, f0:f0 + ATT_HEADS].astype(BF))
    bfg = jnp.zeros((1, FPAD), F32).at[0, 0:ATT_HEADS].set(b_forget)
    cw = jnp.zeros((32, D_CONV), F32).at[0:CONV_K].set(conv_dw_w)
    rw = jnp.zeros((8, D_RNN), F32).at[0:RNN_CONV_K].set(rg_conv_w)
    bd = lambda w: jax.scipy.linalg.block_diag(*[w[i] for i in range(RNN_BLOCKS)]).astype(BF)
    row = lambda v: v.reshape(1, -1).astype(F32)
    keys = jnp.stack([peer_k1, peer_k2], axis=1).reshape(2 * PEER_HEADS, N_KEYS, D_HALF).astype(BF)
    return dict(wqk=w_in[:, 0:2 * D_ATT].astype(BF), wvt=w_in[:, 2 * D_ATT:f0].T.astype(BF), wf=wf,
                wrest=w_in[:, f0 + ATT_HEADS:].astype(BF), bfg=bfg,
                cw=cw, cb=row(conv_dw_b), lg=row(conv_ln_g), lb=row(conv_ln_b),
                rw=rw, rb=row(rg_conv_b), wr=bd(rg_w_r), br=row(rg_b_r), wi=bd(rg_w_i), bi=row(rg_b_i),
                lam=row(rg_lambda), woa=w_out[0:D_ATT].astype(BF), wob=w_out[D_ATT:].astype(BF),
                wq=peer_wq.astype(BF), keys=keys)


def kernel(x, norm1_g, w_in, b_forget, conv_dw_w, conv_dw_b, conv_ln_g, conv_ln_b,
           rg_conv_w, rg_conv_b, rg_w_r, rg_b_r, rg_w_i, rg_b_i, rg_lambda, w_out,
           norm2_g, peer_wq, peer_k1, peer_k2, peer_u, peer_v, final_g):
    b, s, d = x.shape
    params = [_prep_layer(w_in[l], b_forget[l], conv_dw_w[l], conv_dw_b[l], conv_ln_g[l], conv_ln_b[l],
                          rg_conv_w[l], rg_conv_b[l], rg_w_r[l], rg_b_r[l], rg_w_i[l], rg_b_i[l], rg_lambda[l],
                          w_out[l], peer_wq[l], peer_k1[l], peer_k2[l]) for l in range(DEPTH)]
    tabs = [pack_tables(peer_u[l], peer_v[l]) for l in range(DEPTH)]
    bs = b // N_SLICES
    T = bs * s
    xf = x.reshape(b * s, d)
    xs = [None] * N_SLICES
    prev = None
    for l in range(DEPTH):
        p = params[l]
        for i in range(N_SLICES):
            src, row0 = (xf, i * T) if l == 0 else (xs[i], 0)
            tie_tab = l == 0 and i < DEPTH
            deps = ([prev] if prev is not None else []) + ([tabs[i]] if tie_tab else [])
            if deps:
                tied = lax.optimization_barrier((src, *deps))
                src = tied[0]
                if prev is not None:
                    prev = tied[1]
                if tie_tab:
                    tabs[i] = tied[-1]
            qkb, vt, rest = in_proj(src, norm1_g[l].reshape(1, d), p, s, T, row0)
            y_att = attention(qkb, vt, bs, s)
            y_cr = mixers(rest, p, bs, s)
            x1, h2p, st = out_proj(src, y_att, y_cr, p, norm2_g[l].reshape(1, d), row0)
            experts, gates = route(st)
            prev = experts
            xs[i] = peer_sc(h2p, x1, experts, gates, tabs[l])
    out = None
    for i in range(N_SLICES):
        out = final_norm(xs[i], final_g.reshape(1, d), out, i * T, b * s)
    return out.reshape(b, s, d)
```

```python
import functools
import math

import jax
import jax.numpy as jnp
from jax import lax
from jax.experimental import pallas as pl
from jax.experimental.pallas import tpu as pltpu
from jax.experimental.pallas import tpu_sc as plsc

BF = jnp.bfloat16
F32 = jnp.float32
I32 = jnp.int32

D_MODEL = 1024
DEPTH = 2
ATT_HEADS = 8
ATT_HD = 64
D_ATT = ATT_HEADS * ATT_HD
D_CONV = 256
CONV_K = 31
D_RNN = 256
RNN_BLOCKS = 4
RNN_CONV_K = 4
RG_C = 8.0
EPS = 1e-6
N_REST = 2 * D_CONV + 2 * D_RNN
PEER_HEADS = 8
N_KEYS = 128
D_HALF = 128
TOPK = 16
KSEL = PEER_HEADS * TOPK
GC = 0.7978845608028654
NEG = float("-inf")

FIRST_PIECES = 4
LATE_PREP_SLICE = 2
N_SLICES = 4
LANES = 128
VMEM_LIMIT = 48 * 1024 * 1024


def _cp(sem):
    return pltpu.CompilerParams(dimension_semantics=sem, vmem_limit_bytes=VMEM_LIMIT)


def _split3(x):
    hi = x.astype(BF)
    r = x - hi.astype(F32)
    mid = r.astype(BF)
    lo = (r - mid.astype(F32)).astype(BF)
    return hi, mid, lo


def _nt(a, b):
    return lax.dot_general(a, b, (((1,), (1,)), ((), ())), preferred_element_type=F32)


def _dot(a, b):
    return jnp.dot(a, b, preferred_element_type=F32)


def _sigmoid(x):
    return 1.0 / (1.0 + jnp.exp(-x))


def _gelu(x):
    return 0.5 * x * (1.0 + jnp.tanh(GC * (x + 0.044715 * x * x * x)))


LOG2E = 1.4426950408889634
NSPLIT = 3
FPAD = 16


def _inproj_body(x_ref, g_ref, wqk_ref, wvt_ref, wf_ref, wrest_ref, bf_ref, tri_ref, place_ref,
                 qkb_ref, vt_ref, rest_ref, carry_ref, *, blocks_per_seq, tm):
    i = pl.program_id(0)
    x = x_ref[...]
    h = x * lax.rsqrt(jnp.mean(x * x, axis=-1, keepdims=True) + EPS) * g_ref[...]
    hb = h.astype(BF)
    qk = _dot(hb, wqk_ref[...])
    col = lax.broadcasted_iota(I32, (1, 2 * D_ATT), 1)
    qk = jnp.where(col < D_ATT, qk * (LOG2E / math.sqrt(ATT_HD)), qk)
    qkb_ref[:, 0:2 * D_ATT] = qk.astype(BF)
    vt_ref[...] = _nt(wvt_ref[...], hb).astype(BF)
    rest_ref[...] = _dot(hb, wrest_ref[...])
    ft = _dot(hb, wf_ref[...]) + bf_ref[...]
    lf = jnp.minimum(ft, 0.0) - jnp.log(1.0 + jnp.exp(-jnp.abs(ft)))
    hi, mid, lo = _split3(lf)
    tri = tri_ref[...]
    cs = _dot(tri, hi) + _dot(tri, mid) + _dot(tri, lo)

    @pl.when(i % blocks_per_seq == 0)
    def _():
        carry_ref[...] = jnp.zeros_like(carry_ref)

    cum = cs + carry_ref[...]
    carry_ref[...] = cum[tm - 1:tm, :]
    pieces = _split3(cum * (-LOG2E))
    kb = _dot(pieces[0], place_ref[0]) + _dot(pieces[1], place_ref[1]) + _dot(pieces[2], place_ref[2])
    qkb_ref[:, 2 * D_ATT:3 * D_ATT] = kb.astype(BF)


def bias_lane(hh, j):
    return (ATT_HD if hh == 0 else 0) + j


def in_proj(x, g, p, seq, T, row0=0, tm=512):
    blk0 = row0 // tm
    tri =(lax.broadcasted_iota(I32, (tm, tm), 0) >= lax.broadcasted_iota(I32, (tm, tm), 1)).astype(BF)
    shp = (NSPLIT, FPAD, D_ATT)
    hd = lax.broadcasted_iota(I32, shp, 1)
    target = (hd // 2) * LANES + jnp.where(hd % 2 == 0, ATT_HD, 0) + lax.broadcasted_iota(I32, shp, 0)
    place = ((lax.broadcasted_iota(I32, shp, 2) == target) & (hd < ATT_HEADS)).astype(BF)
    body = functools.partial(_inproj_body, blocks_per_seq=seq // tm, tm=tm)
    return pl.pallas_call(
        body, grid=(T // tm,),
        in_specs=[pl.BlockSpec((tm, D_MODEL), lambda i: (i + blk0, 0)),
                  pl.BlockSpec((1, D_MODEL), lambda i: (0, 0)),
                  pl.BlockSpec((D_MODEL, 2 * D_ATT), lambda i: (0, 0)),
                  pl.BlockSpec((D_ATT, D_MODEL), lambda i: (0, 0)),
                  pl.BlockSpec((D_MODEL, FPAD), lambda i: (0, 0)),
                  pl.BlockSpec((D_MODEL, N_REST), lambda i: (0, 0)),
                  pl.BlockSpec((1, FPAD), lambda i: (0, 0)),
                  pl.BlockSpec((tm, tm), lambda i: (0, 0)),
                  pl.BlockSpec(shp, lambda i: (0, 0, 0))],
        out_specs=[pl.BlockSpec((tm, 3 * D_ATT), lambda i: (i, 0)),
                   pl.BlockSpec((D_ATT, tm), lambda i: (0, i)),
                   pl.BlockSpec((tm, N_REST), lambda i: (i, 0))],
        out_shape=[jax.ShapeDtypeStruct((T, 3 * D_ATT), BF),
                   jax.ShapeDtypeStruct((D_ATT, T), BF),
                   jax.ShapeDtypeStruct((T, N_REST), F32)],
        scratch_shapes=[pltpu.VMEM((1, FPAD), F32)],
        compiler_params=_cp(("arbitrary",)), name="in_proj",
    )(x, g, p["wqk"], p["wvt"], p["wf"], p["wrest"], p["bfg"], tri, place)


def _attn_body(q_ref, k_ref, kb_ref, vt_ref, o_ref, m_ref, acc_ref, *, tq, tk):
    qi = pl.program_id(2)
    ki = pl.program_id(3)

    @pl.when(ki == 0)
    def _():
        m_ref[...] = jnp.full_like(m_ref, NEG)
        acc_ref[...] = jnp.zeros_like(acc_ref)

    lane = lax.broadcasted_iota(I32, (1, LANES), 1)
    first = lane < ATT_HD
    vrow = lax.broadcasted_iota(I32, (LANES, 1), 0) < ATT_HD

    def step(masked):
        q = q_ref[...]
        k = k_ref[...]
        kb = kb_ref[...]
        vt = vt_ref[...]
        if masked:
            keep = (lax.broadcasted_iota(I32, (tk, tq), 0) <= lax.broadcasted_iota(I32, (tk, tq), 1))
        for hh in range(2):
            own = first if hh == 0 else jnp.logical_not(first)
            ones = (lane >= bias_lane(hh, 0)) & (lane < bias_lane(hh, NSPLIT))
            qa = jnp.where(own, q, jnp.where(ones, 1.0, 0.0).astype(BF))
            ka = jnp.where(own, k, kb)
            st = _nt(ka, qa)
            if masked:
                st = jnp.where(keep, st, NEG)
            m_prev = m_ref[hh]
            m_new = jnp.maximum(m_prev, jnp.max(st, axis=0, keepdims=True))
            alpha = jnp.exp2(m_prev - m_new)
            p = jnp.exp2(st - m_new).astype(BF)
            m_ref[hh] = m_new
            vown = vrow if hh == 0 else jnp.logical_not(vrow)
            va = jnp.where(vown, vt, jnp.ones_like(vt))
            acc_ref[hh] = alpha * acc_ref[hh] + _dot(va, p)

    @pl.when(ki < qi)
    def _():
        step(False)

    @pl.when(ki == qi)
    def _():
        step(True)
        a0 = acc_ref[0]
        a1 = acc_ref[1]
        ot = jnp.where(vrow, a0 / a0[ATT_HD:ATT_HD + 1, :], a1 / a1[0:1, :])
        o_ref[...] = ot.T.astype(o_ref.dtype)


def attention(qkb, vt, batch, seq, tq=512):
    T = qkb.shape[0]
    tk = tq
    nq = seq // tq
    npair = ATT_HEADS // 2
    body = functools.partial(_attn_body, tq=tq, tk=tk)
    kblk = lambda b, qi, ki: b * nq + jnp.minimum(ki, qi)
    return pl.pallas_call(
        body, grid=(batch, npair, nq, nq),
        in_specs=[pl.BlockSpec((tq, LANES), lambda b, p, qi, ki: (b * nq + qi, p)),
                  pl.BlockSpec((tk, LANES), lambda b, p, qi, ki: (kblk(b, qi, ki), npair + p)),
                  pl.BlockSpec((tk, LANES), lambda b, p, qi, ki: (kblk(b, qi, ki), 2 * npair + p)),
                  pl.BlockSpec((LANES, tk), lambda b, p, qi, ki: (p, kblk(b, qi, ki)))],
        out_specs=pl.BlockSpec((tq, LANES), lambda b, p, qi, ki: (b * nq + qi, p)),
        out_shape=jax.ShapeDtypeStruct((T, D_ATT), BF),
        scratch_shapes=[pltpu.VMEM((2, 1, tq), F32), pltpu.VMEM((2, LANES, tq), F32)],
        compiler_params=_cp(("parallel", "parallel", "parallel", "arbitrary")), name="fox_attention",
    )(qkb, qkb, qkb, vt)


CONV_HALO = 32
RG_HALO = 8


def _mix_body(rest_ref, cw_ref, cb_ref, lg_ref, lb_ref, rw_ref, rb_ref, wr_ref, br_ref, wi_ref, bi_ref, lam_ref,
              o_ref, ybuf, xbuf, hc, *, ts):
    si = pl.program_id(1)

    @pl.when(si == 0)
    def _():
        ybuf[0:CONV_HALO, :] = jnp.zeros((CONV_HALO, D_CONV), F32)
        xbuf[0:RG_HALO, :] = jnp.zeros((RG_HALO, D_RNN), F32)
        hc[...] = jnp.zeros_like(hc)

    y = rest_ref[:, 0:D_CONV] * _sigmoid(rest_ref[:, D_CONV:2 * D_CONV])
    ybuf[CONV_HALO:CONV_HALO + ts, :] = y
    acc = jnp.zeros((ts, D_CONV), F32)
    for k in range(CONV_K):
        acc = acc + cw_ref[k:k + 1, :] * ybuf[pl.ds(CONV_HALO - (CONV_K - 1) + k, ts), :]
    yc = acc + cb_ref[...]
    mu = jnp.mean(yc, axis=-1, keepdims=True)
    var = jnp.mean(jnp.square(yc - mu), axis=-1, keepdims=True)
    yn = (yc - mu) * lax.rsqrt(var + EPS) * lg_ref[...] + lb_ref[...]
    o_ref[:, 0:D_CONV] = (yn * _sigmoid(yn)).astype(o_ref.dtype)
    ybuf[0:CONV_HALO, :] = ybuf[ts:ts + CONV_HALO, :]

    xbuf[RG_HALO:RG_HALO + ts, :] = rest_ref[:, 2 * D_CONV:2 * D_CONV + D_RNN]
    xc = jnp.zeros((ts, D_RNN), F32)
    for k in range(RNN_CONV_K):
        xc = xc + rw_ref[k:k + 1, :] * xbuf[pl.ds(RG_HALO - (RNN_CONV_K - 1) + k, ts), :]
    xc = xc + rb_ref[...]
    xbuf[0:RG_HALO, :] = xbuf[ts:ts + RG_HALO, :]
    xcb = xc.astype(BF)
    r = _sigmoid(_dot(xcb, wr_ref[...]) + br_ref[...])
    gi = _sigmoid(_dot(xcb, wi_ref[...]) + bi_ref[...])
    nl = -lam_ref[...]
    sp = jnp.maximum(nl, 0.0) + jnp.log(1.0 + jnp.exp(-jnp.abs(nl)))
    log_a = -RG_C * r * sp
    a = jnp.exp(log_a)
    bt = jnp.sqrt(1.0 - jnp.exp(2.0 * log_a)) * (gi * xc)
    row = lax.broadcasted_iota(I32, (ts, 1), 0)
    sh = 1
    while sh < ts:
        live = row >= sh
        a_s = jnp.where(live, pltpu.roll(a, sh, 0), 1.0)
        b_s = jnp.where(live, pltpu.roll(bt, sh, 0), 0.0)
        bt = bt + a * b_s
        a = a * a_s
        sh *= 2
    h = bt + a * hc[...]
    hc[...] = h[ts - 1:ts, :]
    gate_in = rest_ref[:, 2 * D_CONV + D_RNN:2 * D_CONV + 2 * D_RNN]
    o_ref[:, D_CONV:D_CONV + D_RNN] = (h * _gelu(gate_in)).astype(o_ref.dtype)


def mixers(rest, p, batch, seq, ts=512):
    T = rest.shape[0]
    ns = seq // ts
    body = functools.partial(_mix_body, ts=ts)
    vec = lambda: pl.BlockSpec((1, D_CONV), lambda b, s: (0, 0))
    return pl.pallas_call(
        body, grid=(batch, ns),
        in_specs=[pl.BlockSpec((ts, N_REST), lambda b, s: (b * ns + s, 0)),
                  pl.BlockSpec((32, D_CONV), lambda b, s: (0, 0)), vec(), vec(), vec(),
                  pl.BlockSpec((8, D_RNN), lambda b, s: (0, 0)), vec(),
                  pl.BlockSpec((D_RNN, D_RNN), lambda b, s: (0, 0)), vec(),
                  pl.BlockSpec((D_RNN, D_RNN), lambda b, s: (0, 0)), vec(), vec()],
        out_specs=pl.BlockSpec((ts, D_CONV + D_RNN), lambda b, s: (b * ns + s, 0)),
        out_shape=jax.ShapeDtypeStruct((T, D_CONV + D_RNN), BF),
        scratch_shapes=[pltpu.VMEM((ts + CONV_HALO, D_CONV), F32), pltpu.VMEM((ts + RG_HALO, D_RNN), F32),
                        pltpu.VMEM((1, D_RNN), F32)],
        compiler_params=_cp(("arbitrary", "arbitrary")), name="conv_rglru",
    )(rest, p["cw"], p["cb"], p["lg"], p["lb"], p["rw"], p["rb"], p["wr"], p["br"], p["wi"], p["bi"], p["lam"])


def pack_pairs(a):
    half = a.shape[1] // 2
    r = lax.bitcast_convert_type(a, I32)
    r = r + jnp.int32(0x7FFF) + jnp.bitwise_and(lax.shift_right_logical(r, 16), 1)
    lo = lax.shift_right_logical(r[:, :half], 16)
    hi = jnp.bitwise_and(r[:, half:], jnp.int32(-65536))
    return jnp.bitwise_or(hi, lo)


def _outproj_body(x_ref, ya_ref, yc_ref, woa_ref, wob_ref, g2_ref, wq_ref, keys_ref, x1_ref, h2p_ref, st_ref):
    x1 = x_ref[...] + _dot(ya_ref[...], woa_ref[...]) + _dot(yc_ref[...], wob_ref[...])
    x1_ref[...] = x1
    h2 = x1 * lax.rsqrt(jnp.mean(x1 * x1, axis=-1, keepdims=True) + EPS) * g2_ref[...]
    h2p_ref[...] = pack_pairs(h2)
    q = _dot(h2.astype(BF), wq_ref[...]).astype(BF)
    for g in range(2 * PEER_HEADS):
        st_ref[g * N_KEYS:(g + 1) * N_KEYS, :] = _nt(keys_ref[g], q[:, g * D_HALF:(g + 1) * D_HALF])


def out_proj(x, ya, yc, p, g2, row0=0, sub0=0, T=None, tm=256):
    T = ya.shape[0] if T is None else T
    ng = 2 * PEER_HEADS
    blk0 = (row0 + sub0) // tm
    sblk = sub0 // tm
    return pl.pallas_call(
        _outproj_body, grid=(T // tm,),
        in_specs=[pl.BlockSpec((tm, D_MODEL), lambda i: (i + blk0, 0)),
                  pl.BlockSpec((tm, D_ATT), lambda i: (i + sblk, 0)),
                  pl.BlockSpec((tm, D_CONV + D_RNN), lambda i: (i + sblk, 0)),
                  pl.BlockSpec((D_ATT, D_MODEL), lambda i: (0, 0)),
                  pl.BlockSpec((D_CONV + D_RNN, D_MODEL), lambda i: (0, 0)),
                  pl.BlockSpec((1, D_MODEL), lambda i: (0, 0)),
                  pl.BlockSpec((D_MODEL, ng * D_HALF), lambda i: (0, 0)),
                  pl.BlockSpec((ng, N_KEYS, D_HALF), lambda i: (0, 0, 0))],
        out_specs=[pl.BlockSpec((tm, D_MODEL), lambda i: (i, 0)),
                   pl.BlockSpec((tm, D_MODEL // 2), lambda i: (i, 0)),
                   pl.BlockSpec((ng * N_KEYS, tm), lambda i: (0, i))],
        out_shape=[jax.ShapeDtypeStruct((T, D_MODEL), F32),
                   jax.ShapeDtypeStruct((T, D_MODEL // 2), I32),
                   jax.ShapeDtypeStruct((ng * N_KEYS, T), F32)],
        compiler_params=_cp(("parallel",)), name="out_proj_peer_scores",
    )(x, ya, yc, p["woa"], p["wob"], g2, p["wq"], p["keys"])


BIG_ID = 1 << 20
SUBL = 8
NPAR = 4
NPAR1 = 8
SEL_CHAIN = 4


def _take_rounds(problems, nrounds):
    state = [list(slabs) for slabs, _ in problems]
    res = [([], []) for _ in problems]
    for _ in range(nrounds):
        for pi, (_, ids) in enumerate(problems):
            slabs = state[pi]
            m8 = slabs[0]
            for sl in slabs[1:]:
                m8 = jnp.maximum(m8, sl)
            m = jnp.max(m8, axis=0, keepdims=True)
            chains = []
            for c0 in range(0, len(slabs), SEL_CHAIN):
                v = jnp.full((SUBL, LANES), BIG_ID, I32)
                for sl, idc in zip(reversed(slabs[c0:c0 + SEL_CHAIN]), reversed(ids[c0:c0 + SEL_CHAIN])):
                    v = jnp.where(sl == m, idc, v)
                chains.append(v)
            while len(chains) > 1:
                chains = [jnp.minimum(chains[i], chains[i + 1]) if i + 1 < len(chains) else chains[i]
                          for i in range(0, len(chains), 2)]
            pick = jnp.min(chains[0], axis=0, keepdims=True)
            state[pi] = [jnp.where(idc == pick, NEG, sl) for sl, idc in zip(slabs, ids)]
            res[pi][0].append(m)
            res[pi][1].append(pick)
    return res


def _route_body(st_ref, e_ref, g_ref, v_scr, i_scr, sv_scr, ci_scr, et_scr, gt_scr):
    ng = 2 * PEER_HEADS
    sub = lax.broadcasted_iota(I32, (SUBL, LANES), 0)
    key_ids = [sub + SUBL * i for i in range(N_KEYS // SUBL)]

    def stage1(gg, carry):
        probs = []
        for q in range(NPAR1):
            base = pl.multiple_of((gg * NPAR1 + q) * N_KEYS, N_KEYS)
            probs.append(([st_ref[pl.ds(base + SUBL * i, SUBL), :] for i in range(N_KEYS // SUBL)], key_ids))
        for q, (vals, picks) in enumerate(_take_rounds(probs, TOPK)):
            for r in range(TOPK):
                v_scr[gg * NPAR1 + q, r:r + 1, :] = vals[r]
                i_scr[gg * NPAR1 + q, r:r + 1, :] = picks[r]
        return carry

    lax.fori_loop(0, ng // NPAR1, stage1, 0)

    def stage2(hh, carry):
        probs = []
        for q in range(NPAR):
            h = hh * NPAR + q
            v1 = v_scr[2 * h]
            v2 = v_scr[2 * h + 1]
            slabs = [v1[0:1, :] + v2[0:SUBL, :], v1[0:1, :] + v2[SUBL:TOPK, :]]
            ids = [sub, sub + SUBL]
            for i in range(1, TOPK):
                nj = TOPK // (i + 1)
                slabs.append(jnp.where(sub < nj, v1[i:i + 1, :] + v2[0:SUBL, :], NEG))
                ids.append(sub + i * TOPK)
            probs.append((slabs, ids))
        for q, (vals, picks) in enumerate(_take_rounds(probs, TOPK)):
            h = hh * NPAR + q
            i1 = i_scr[2 * h]
            i2 = i_scr[2 * h + 1]
            for r in range(TOPK):
                sv_scr[q, r:r + 1, :] = vals[r]
                ci_scr[q, r:r + 1, :] = picks[r]
            sv = sv_scr[q]
            ci = ci_scr[q]
            ci_hi = lax.shift_right_logical(ci, 4)
            ci_lo = jnp.bitwise_and(ci, TOPK - 1)
            e1 = jnp.zeros((TOPK, LANES), I32)
            e2 = jnp.zeros((TOPK, LANES), I32)
            for i in range(TOPK):
                e1 = jnp.where(ci_hi == i, i1[i:i + 1, :], e1)
                e2 = jnp.where(ci_lo == i, i2[i:i + 1, :], e2)
            p = jnp.exp(sv - sv[0:1, :])
            gates = p / jnp.sum(p, axis=0, keepdims=True)
            et_scr[pl.ds(pl.multiple_of(h * TOPK, TOPK), TOPK), :] = e1 * N_KEYS + e2
            gt_scr[pl.ds(pl.multiple_of(h * TOPK, TOPK), TOPK), :] = gates
        return carry

    lax.fori_loop(0, PEER_HEADS // NPAR, stage2, 0)
    e_ref[...] = et_scr[...].T
    g_ref[...] = gt_scr[...].T


def route(st):
    T = st.shape[1]
    ng = 2 * PEER_HEADS
    return pl.pallas_call(
        _route_body, grid=(T // LANES,),
        in_specs=[pl.BlockSpec((ng * N_KEYS, LANES), lambda i: (0, i))],
        out_specs=[pl.BlockSpec((LANES, KSEL), lambda i: (i, 0)),
                   pl.BlockSpec((LANES, KSEL), lambda i: (i, 0))],
        out_shape=[jax.ShapeDtypeStruct((T, KSEL), I32),
                   jax.ShapeDtypeStruct((T, KSEL), F32)],
        scratch_shapes=[pltpu.VMEM((ng, TOPK, LANES), F32), pltpu.VMEM((ng, TOPK, LANES), I32),
                        pltpu.VMEM((NPAR, TOPK, LANES), F32), pltpu.VMEM((NPAR, TOPK, LANES), I32),
                        pltpu.VMEM((KSEL, LANES), I32), pltpu.VMEM((KSEL, LANES), F32)],
        compiler_params=_cp(("parallel",)), name="peer_route",
    )(st)


NC, NS, L = 2, 16, 16
NW = NC * NS
NJ = D_MODEL // L
R = TOPK
NCH = KSEL // R
NB = 4
USTEP = 2
G = 8
NSLOT = 2
DW = D_MODEL // 2


def _perm(x, idx):
    return jnp.take_along_axis(x, idx, axis=0, mode="promise_in_bounds")


def _halves(w):
    lo = lax.bitcast_convert_type(lax.shift_left(w, 16), F32)
    hi = lax.bitcast_convert_type(jnp.bitwise_and(w, jnp.int32(-65536)), F32)
    return lo, hi


def _bf(w):
    return plsc.bitcast(w, BF)


def peer_sc(x, resid, idx, gates, uv_tab):
    T = x.shape[0]
    tpw = T // NW
    ngroups = tpw // G
    nchunks = G * NCH
    idx3 = idx.reshape(T * NCH, R)
    g3 = gates.reshape(T * NCH, R)
    mesh = plsc.VectorSubcoreMesh(core_axis_name="c", subcore_axis_name="s")

    @functools.partial(
        pl.kernel, mesh=mesh,
        out_type=jax.ShapeDtypeStruct((T, D_MODEL), F32),
        scratch_types=[
            pltpu.VMEM((NSLOT, G, DW), I32),
            pltpu.VMEM((NSLOT, G, D_MODEL), F32),
            pltpu.VMEM((NSLOT, nchunks, R), I32),
            pltpu.VMEM((NSLOT, nchunks, R), F32),
            pltpu.SemaphoreType.DMA((NSLOT,)),
            pltpu.SemaphoreType.DMA((NSLOT,)),
            pltpu.SemaphoreType.DMA((NSLOT,)),
        ] + [pltpu.VMEM((R, 2 * DW), I32) for _ in range(NB)]
          + [pltpu.SemaphoreType.DMA for _ in range(NB)],
        compiler_params=pltpu.CompilerParams(needs_layout_passes=False),
        name="peer_experts_sc",
    )
    def k(x_hbm, r_hbm, idx_hbm, g_hbm, uv_hbm, out_hbm, x_v, out_v, idx_v, g_v, st_sem, ix_sem, wb_sem, *ring):
        wid = lax.axis_index("s") * NC + lax.axis_index("c")
        bufs, sems = ring[:NB], ring[NB:]
        iota = lax.iota(I32, L)

        def tok0_of(g):
            return wid * tpw + g * G

        def stage_copies(g, slot):
            t0 = tok0_of(g)
            return (pltpu.make_async_copy(x_hbm.at[pl.ds(t0, G)], x_v.at[slot], st_sem.at[slot]),
                    pltpu.make_async_copy(r_hbm.at[pl.ds(t0, G)], out_v.at[slot], st_sem.at[slot]),
                    pltpu.make_async_copy(g_hbm.at[pl.ds(t0 * NCH, nchunks)], g_v.at[slot], st_sem.at[slot]))

        def idx_copy(g, slot):
            return pltpu.make_async_copy(idx_hbm.at[pl.ds(tok0_of(g) * NCH, nchunks)], idx_v.at[slot], ix_sem.at[slot])

        def wb_copy(g, slot):
            return pltpu.make_async_copy(out_v.at[slot], out_hbm.at[pl.ds(tok0_of(g), G)], wb_sem.at[slot])

        def gather_copy(slot, c, b):
            return pltpu.make_async_copy(uv_hbm.at[idx_v.at[slot, c]], bufs[b], sems[b])

        def compute(slot, c, b):
            ub = vb = bufs[b]
            t = c // NCH

            def ubody(mm, accs):
                xs_ = [_bf(x_v[slot, t, pl.ds((mm * USTEP + q) * L, L)]) for q in range(USTEP)]
                out = []
                for kk in range(R):
                    pr = [xs_[q] * _bf(ub[kk, pl.ds((mm * USTEP + q) * L, L)]) for q in range(USTEP)]
                    while len(pr) > 1:
                        pr = [pr[i] + pr[i + 1] for i in range(0, len(pr), 2)]
                    lo, hi = _halves(plsc.bitcast(pr[0], I32))
                    out.append(accs[kk] + (lo + hi))
                return tuple(out)

            accs = lax.fori_loop(0, NJ // (2 * USTEP), ubody, tuple(jnp.zeros((L,), F32) for _ in range(R)))
            vecs = list(accs)
            dist = L // 2
            while dist >= 1:
                pidx = jnp.bitwise_xor(iota, dist)
                low = jnp.bitwise_and(iota, dist) == 0
                nxt = []
                for kk in range(dist):
                    a = vecs[kk]
                    bvec = vecs[kk + dist]
                    a = a + _perm(a, pidx)
                    bvec = bvec + _perm(bvec, pidx)
                    nxt.append(jnp.where(low, a, bvec))
                vecs = nxt
                dist //= 2
            hid = vecs[0]
            z = GC * (hid + 0.044715 * hid * hid * hid)
            gel = hid / (1.0 + jnp.exp(-2.0 * z))
            w = g_v[slot, c, :] * gel
            wbs = []
            for kk in range(R):
                wb = _perm(w, jnp.full((L,), kk, I32))
                wbs.append(plsc.pack(wb, wb, format=plsc.PackFormat.INTERLEAVED))

            @plsc.parallel_loop(0, NJ // 2)
            def _(m):
                pr = [wbs[kk] * _bf(vb[kk, pl.ds(DW + m * L, L)]) for kk in range(R)]
                for _lvl in range(2):
                    pr = [pr[i] + pr[i + 1] for i in range(0, len(pr), 2)]
                los, his = [], []
                for q in pr:
                    lo, hi = _halves(plsc.bitcast(q, I32))
                    los.append(lo)
                    his.append(hi)
                while len(los) > 1:
                    los = [los[i] + los[i + 1] for i in range(0, len(los), 2)]
                    his = [his[i] + his[i + 1] for i in range(0, len(his), 2)]
                out_v[slot, t, pl.ds(m * L, L)] = out_v[slot, t, pl.ds(m * L, L)] + los[0]
                out_v[slot, t, pl.ds(DW + m * L, L)] = out_v[slot, t, pl.ds(DW + m * L, L)] + his[0]

        idx_copy(0, 0).start()
        for cp in stage_copies(0, 0):
            cp.start()
        idx_copy(0, 0).wait()
        for b in range(NB - 1):
            gather_copy(0, b, b).start()

        def group(g, carry):
            slot = g % NSLOT
            nslot = 1 - slot
            has_next = g + 1 < ngroups

            @pl.when(g >= 1)
            def _():
                wb_copy(g - 1, nslot).wait()

            @pl.when(has_next)
            def _():
                idx_copy(g + 1, nslot).start()
                for cp in stage_copies(g + 1, nslot):
                    cp.start()

            for cp in stage_copies(g, slot):
                cp.wait()

            def cbody(cc, c2):
                for b in range(NB):
                    c = cc * NB + b
                    cn = c + NB - 1
                    nb = (b + NB - 1) % NB

                    @pl.when(cn < nchunks)
                    def _():
                        gather_copy(slot, cn, nb).start()

                    @pl.when(jnp.logical_and(cn >= nchunks, has_next))
                    def _():
                        @pl.when(cn == nchunks)
                        def _():
                            idx_copy(g + 1, nslot).wait()

                        gather_copy(nslot, cn - nchunks, nb).start()

                    gather_copy(slot, c, b).wait()
                    compute(slot, c, b)
                return c2

            lax.fori_loop(0, nchunks // NB, cbody, 0)
            wb_copy(g, slot).start()
            return carry

        lax.fori_loop(0, ngroups, group, 0)
        wb_copy(ngroups - 1, (ngroups - 1) % NSLOT).wait()

    return k(x, resid, idx3, g3, uv_tab)


def _fn_body(x_ref, g_ref, o_ref):
    xf = x_ref[...]
    o_ref[...] = xf * lax.rsqrt(jnp.mean(xf * xf, axis=-1, keepdims=True) + EPS) * g_ref[...]


def _fn_body_into(x_ref, g_ref, prev_ref, o_ref):
    del prev_ref
    _fn_body(x_ref, g_ref, o_ref)


def final_norm(x, g, out, row0, t_total, tm=1024):
    T, d = x.shape
    blk0 = row0 // tm
    common = dict(grid=(T // tm,), out_specs=pl.BlockSpec((tm, d), lambda i: (i + blk0, 0)),
                  out_shape=jax.ShapeDtypeStruct((t_total, d), F32),
                  compiler_params=_cp(("parallel",)), name="final_norm")
    specs = [pl.BlockSpec((tm, d), lambda i: (i, 0)), pl.BlockSpec((1, d), lambda i: (0, 0))]
    if out is None:
        return pl.pallas_call(_fn_body, in_specs=specs, **common)(x, g)
    return pl.pallas_call(_fn_body_into, in_specs=specs + [pl.BlockSpec(memory_space=pl.ANY)],
                          input_output_aliases={2: 0}, **common)(x, g, out)


def _pack_body(u_ref, v_ref, o_ref):
    o_ref[:, 0:D_MODEL // 2] = pack_pairs(u_ref[...])
    o_ref[:, D_MODEL // 2:D_MODEL] = pack_pairs(v_ref[...])


def pack_tables(u, v, tm=512):
    e, d = u.shape
    return pl.pallas_call(
        _pack_body, grid=(e // tm,),
        in_specs=[pl.BlockSpec((tm, d), lambda i: (i, 0)), pl.BlockSpec((tm, d), lambda i: (i, 0))],
        out_specs=pl.BlockSpec((tm, d), lambda i: (i, 0)),
        out_shape=jax.ShapeDtypeStruct((e, d), I32),
        compiler_params=_cp(("parallel",)), name="pack_tables",
    )(u, v)


def _prep_layer(w_in, b_forget, conv_dw_w, conv_dw_b, conv_ln_g, conv_ln_b, rg_conv_w, rg_conv_b,
                rg_w_r, rg_b_r, rg_w_i, rg_b_i, rg_lambda, w_out, peer_wq, peer_k1, peer_k2):
    f0 = 3 * D_ATT
    wf = jnp.zeros((D_MODEL, FPAD), BF).at[:, 0:ATT_HEADS].set(w_in[:, f0:f0 + ATT_HEADS].astype(BF))
    bfg = jnp.zeros((1, FPAD), F32).at[0, 0:ATT_HEADS].set(b_forget)
    cw = jnp.zeros((32, D_CONV), F32).at[0:CONV_K].set(conv_dw_w)
    rw = jnp.zeros((8, D_RNN), F32).at[0:RNN_CONV_K].set(rg_conv_w)
    bd = lambda w: jax.scipy.linalg.block_diag(*[w[i] for i in range(RNN_BLOCKS)]).astype(BF)
    row = lambda v: v.reshape(1, -1).astype(F32)
    keys = jnp.stack([peer_k1, peer_k2], axis=1).reshape(2 * PEER_HEADS, N_KEYS, D_HALF).astype(BF)
    return dict(wqk=w_in[:, 0:2 * D_ATT].astype(BF), wvt=w_in[:, 2 * D_ATT:f0].T.astype(BF), wf=wf,
                wrest=w_in[:, f0 + ATT_HEADS:].astype(BF), bfg=bfg,
                cw=cw, cb=row(conv_dw_b), lg=row(conv_ln_g), lb=row(conv_ln_b),
                rw=rw, rb=row(rg_conv_b), wr=bd(rg_w_r), br=row(rg_b_r), wi=bd(rg_w_i), bi=row(rg_b_i),
                lam=row(rg_lambda), woa=w_out[0:D_ATT].astype(BF), wob=w_out[D_ATT:].astype(BF),
                wq=peer_wq.astype(BF), keys=keys)


def kernel(x, norm1_g, w_in, b_forget, conv_dw_w, conv_dw_b, conv_ln_g, conv_ln_b,
           rg_conv_w, rg_conv_b, rg_w_r, rg_b_r, rg_w_i, rg_b_i, rg_lambda, w_out,
           norm2_g, peer_wq, peer_k1, peer_k2, peer_u, peer_v, final_g):
    b, s, d = x.shape
    raw = (w_in, b_forget, conv_dw_w, conv_dw_b, conv_ln_g, conv_ln_b, rg_conv_w, rg_conv_b,
           rg_w_r, rg_b_r, rg_w_i, rg_b_i, rg_lambda, w_out, peer_wq, peer_k1, peer_k2)
    params = {0: _prep_layer(*[a[0] for a in raw])}
    tabs = {0: pack_tables(peer_u[0], peer_v[0])}
    bs = b // N_SLICES
    T = bs * s
    xf = x.reshape(b * s, d)
    xs = [None] * N_SLICES
    prev = None
    for l in range(DEPTH):
        for i in range(N_SLICES):
            src, row0 = (xf, i * T) if l == 0 else (xs[i], 0)
            first = l == 0 and i == 0
            late = l == 0 and i == LATE_PREP_SLICE
            deps = ([prev] if prev is not None else []) + ([tabs[0]] if first else [])
            deps += [*raw, peer_u, peer_v] if late else []
            if deps:
                tied = list(lax.optimization_barrier((src, *deps)))
                src = tied.pop(0)
                if prev is not None:
                    prev = tied.pop(0)
                if first:
                    tabs[0] = tied.pop(0)
                if late:
                    for ll in range(1, DEPTH):
                        params[ll] = _prep_layer(*[a[ll] for a in tied[:len(raw)]])
                        tabs[ll] = pack_tables(tied[-2][ll], tied[-1][ll])
            p = params[l]
            qkb, vt, rest = in_proj(src, norm1_g[l].reshape(1, d), p, s, T, row0)
            y_att = attention(qkb, vt, bs, s)
            y_cr = mixers(rest, p, bs, s)
            npiece = FIRST_PIECES if first else 1
            tp = T // npiece
            pieces = []
            for j in range(npiece):
                if j > 0:
                    src, experts = lax.optimization_barrier((src, experts))
                x1, h2p, st = out_proj(src, y_att, y_cr, p, norm2_g[l].reshape(1, d), row0, j * tp, tp)
                experts, gates = route(st)
                pieces.append(peer_sc(h2p, x1, experts, gates, tabs[l]))
            prev = experts
            xs[i] = pieces[0] if npiece == 1 else jnp.concatenate(pieces, axis=0)
    out = None
    for i in range(N_SLICES):
        out = final_norm(xs[i], final_g.reshape(1, d), out, i * T, b * s)
    return out.reshape(b, s, d)
```

```python
import functools
import math

import jax
import jax.numpy as jnp
from jax import lax
from jax.experimental import pallas as pl
from jax.experimental.pallas import tpu as pltpu
from jax.experimental.pallas import tpu_sc as plsc

BF = jnp.bfloat16
F32 = jnp.float32
I32 = jnp.int32

D_MODEL = 1024
DEPTH = 2
ATT_HEADS = 8
ATT_HD = 64
D_ATT = ATT_HEADS * ATT_HD
D_CONV = 256
CONV_K = 31
D_RNN = 256
RNN_BLOCKS = 4
RNN_CONV_K = 4
RG_C = 8.0
EPS = 1e-6
N_REST = 2 * D_CONV + 2 * D_RNN
PEER_HEADS = 8
N_KEYS = 128
D_HALF = 128
TOPK = 16
KSEL = PEER_HEADS * TOPK
GC = 0.7978845608028654
NEG = float("-inf")

PIECES = {(0, 0): 4, (0, 1): 2}
LATE_PREP_SLICE = 2
N_SLICES = 4
LANES = 128
VMEM_LIMIT = 48 * 1024 * 1024


def _cp(sem):
    return pltpu.CompilerParams(dimension_semantics=sem, vmem_limit_bytes=VMEM_LIMIT)


def _split3(x):
    hi = x.astype(BF)
    r = x - hi.astype(F32)
    mid = r.astype(BF)
    lo = (r - mid.astype(F32)).astype(BF)
    return hi, mid, lo


def _nt(a, b):
    return lax.dot_general(a, b, (((1,), (1,)), ((), ())), preferred_element_type=F32)


def _dot(a, b):
    return jnp.dot(a, b, preferred_element_type=F32)


def _sigmoid(x):
    return 1.0 / (1.0 + jnp.exp(-x))


def _gelu(x):
    return 0.5 * x * (1.0 + jnp.tanh(GC * (x + 0.044715 * x * x * x)))


LOG2E = 1.4426950408889634
NSPLIT = 3
FPAD = 16


def _inproj_body(x_ref, g_ref, wqk_ref, wvt_ref, wf_ref, wrest_ref, bf_ref, tri_ref, place_ref,
                 qkb_ref, vt_ref, rest_ref, carry_ref, *, blocks_per_seq, tm):
    i = pl.program_id(0)
    x = x_ref[...]
    h = x * lax.rsqrt(jnp.mean(x * x, axis=-1, keepdims=True) + EPS) * g_ref[...]
    hb = h.astype(BF)
    qk = _dot(hb, wqk_ref[...])
    col = lax.broadcasted_iota(I32, (1, 2 * D_ATT), 1)
    qk = jnp.where(col < D_ATT, qk * (LOG2E / math.sqrt(ATT_HD)), qk)
    qkb_ref[:, 0:2 * D_ATT] = qk.astype(BF)
    vt_ref[...] = _nt(wvt_ref[...], hb).astype(BF)
    rest_ref[...] = _dot(hb, wrest_ref[...])
    ft = _dot(hb, wf_ref[...]) + bf_ref[...]
    lf = jnp.minimum(ft, 0.0) - jnp.log(1.0 + jnp.exp(-jnp.abs(ft)))
    hi, mid, lo = _split3(lf)
    tri = tri_ref[...]
    cs = _dot(tri, hi) + _dot(tri, mid) + _dot(tri, lo)

    @pl.when(i % blocks_per_seq == 0)
    def _():
        carry_ref[...] = jnp.zeros_like(carry_ref)

    cum = cs + carry_ref[...]
    carry_ref[...] = cum[tm - 1:tm, :]
    pieces = _split3(cum * (-LOG2E))
    kb = _dot(pieces[0], place_ref[0]) + _dot(pieces[1], place_ref[1]) + _dot(pieces[2], place_ref[2])
    qkb_ref[:, 2 * D_ATT:3 * D_ATT] = kb.astype(BF)


def bias_lane(hh, j):
    return (ATT_HD if hh == 0 else 0) + j


def in_proj(x, g, p, seq, T, row0=0, tm=512):
    blk0 = row0 // tm
    tri =(lax.broadcasted_iota(I32, (tm, tm), 0) >= lax.broadcasted_iota(I32, (tm, tm), 1)).astype(BF)
    shp = (NSPLIT, FPAD, D_ATT)
    hd = lax.broadcasted_iota(I32, shp, 1)
    target = (hd // 2) * LANES + jnp.where(hd % 2 == 0, ATT_HD, 0) + lax.broadcasted_iota(I32, shp, 0)
    place = ((lax.broadcasted_iota(I32, shp, 2) == target) & (hd < ATT_HEADS)).astype(BF)
    body = functools.partial(_inproj_body, blocks_per_seq=seq // tm, tm=tm)
    return pl.pallas_call(
        body, grid=(T // tm,),
        in_specs=[pl.BlockSpec((tm, D_MODEL), lambda i: (i + blk0, 0)),
                  pl.BlockSpec((1, D_MODEL), lambda i: (0, 0)),
                  pl.BlockSpec((D_MODEL, 2 * D_ATT), lambda i: (0, 0)),
                  pl.BlockSpec((D_ATT, D_MODEL), lambda i: (0, 0)),
                  pl.BlockSpec((D_MODEL, FPAD), lambda i: (0, 0)),
                  pl.BlockSpec((D_MODEL, N_REST), lambda i: (0, 0)),
                  pl.BlockSpec((1, FPAD), lambda i: (0, 0)),
                  pl.BlockSpec((tm, tm), lambda i: (0, 0)),
                  pl.BlockSpec(shp, lambda i: (0, 0, 0))],
        out_specs=[pl.BlockSpec((tm, 3 * D_ATT), lambda i: (i, 0)),
                   pl.BlockSpec((D_ATT, tm), lambda i: (0, i)),
                   pl.BlockSpec((tm, N_REST), lambda i: (i, 0))],
        out_shape=[jax.ShapeDtypeStruct((T, 3 * D_ATT), BF),
                   jax.ShapeDtypeStruct((D_ATT, T), BF),
                   jax.ShapeDtypeStruct((T, N_REST), F32)],
        scratch_shapes=[pltpu.VMEM((1, FPAD), F32)],
        compiler_params=_cp(("arbitrary",)), name="in_proj",
    )(x, g, p["wqk"], p["wvt"], p["wf"], p["wrest"], p["bfg"], tri, place)


def _attn_body(q_ref, k_ref, kb_ref, vt_ref, o_ref, m_ref, acc_ref, *, tq, tk):
    qi = pl.program_id(2)
    ki = pl.program_id(3)

    @pl.when(ki == 0)
    def _():
        m_ref[...] = jnp.full_like(m_ref, NEG)
        acc_ref[...] = jnp.zeros_like(acc_ref)

    lane = lax.broadcasted_iota(I32, (1, LANES), 1)
    first = lane < ATT_HD
    vrow = lax.broadcasted_iota(I32, (LANES, 1), 0) < ATT_HD

    def step(masked):
        q = q_ref[...]
        k = k_ref[...]
        kb = kb_ref[...]
        vt = vt_ref[...]
        if masked:
            keep = (lax.broadcasted_iota(I32, (tk, tq), 0) <= lax.broadcasted_iota(I32, (tk, tq), 1))
        for hh in range(2):
            own = first if hh == 0 else jnp.logical_not(first)
            ones = (lane >= bias_lane(hh, 0)) & (lane < bias_lane(hh, NSPLIT))
            qa = jnp.where(own, q, jnp.where(ones, 1.0, 0.0).astype(BF))
            ka = jnp.where(own, k, kb)
            st = _nt(ka, qa)
            if masked:
                st = jnp.where(keep, st, NEG)
            m_prev = m_ref[hh]
            m_new = jnp.maximum(m_prev, jnp.max(st, axis=0, keepdims=True))
            alpha = jnp.exp2(m_prev - m_new)
            p = jnp.exp2(st - m_new).astype(BF)
            m_ref[hh] = m_new
            vown = vrow if hh == 0 else jnp.logical_not(vrow)
            va = jnp.where(vown, vt, jnp.ones_like(vt))
            acc_ref[hh] = alpha * acc_ref[hh] + _dot(va, p)

    @pl.when(ki < qi)
    def _():
        step(False)

    @pl.when(ki == qi)
    def _():
        step(True)
        a0 = acc_ref[0]
        a1 = acc_ref[1]
        ot = jnp.where(vrow, a0 / a0[ATT_HD:ATT_HD + 1, :], a1 / a1[0:1, :])
        o_ref[...] = ot.T.astype(o_ref.dtype)


def attention(qkb, vt, batch, seq, tq=512):
    T = qkb.shape[0]
    tk = tq
    nq = seq // tq
    npair = ATT_HEADS // 2
    body = functools.partial(_attn_body, tq=tq, tk=tk)
    kblk = lambda b, qi, ki: b * nq + jnp.minimum(ki, qi)
    return pl.pallas_call(
        body, grid=(batch, npair, nq, nq),
        in_specs=[pl.BlockSpec((tq, LANES), lambda b, p, qi, ki: (b * nq + qi, p)),
                  pl.BlockSpec((tk, LANES), lambda b, p, qi, ki: (kblk(b, qi, ki), npair + p)),
                  pl.BlockSpec((tk, LANES), lambda b, p, qi, ki: (kblk(b, qi, ki), 2 * npair + p)),
                  pl.BlockSpec((LANES, tk), lambda b, p, qi, ki: (p, kblk(b, qi, ki)))],
        out_specs=pl.BlockSpec((tq, LANES), lambda b, p, qi, ki: (b * nq + qi, p)),
        out_shape=jax.ShapeDtypeStruct((T, D_ATT), BF),
        scratch_shapes=[pltpu.VMEM((2, 1, tq), F32), pltpu.VMEM((2, LANES, tq), F32)],
        compiler_params=_cp(("parallel", "parallel", "parallel", "arbitrary")), name="fox_attention",
    )(qkb, qkb, qkb, vt)


CONV_HALO = 32
RG_HALO = 8


def _mix_body(rest_ref, cw_ref, cb_ref, lg_ref, lb_ref, rw_ref, rb_ref, wr_ref, br_ref, wi_ref, bi_ref, lam_ref,
              o_ref, ybuf, xbuf, hc, *, ts):
    si = pl.program_id(1)

    @pl.when(si == 0)
    def _():
        ybuf[0:CONV_HALO, :] = jnp.zeros((CONV_HALO, D_CONV), F32)
        xbuf[0:RG_HALO, :] = jnp.zeros((RG_HALO, D_RNN), F32)
        hc[...] = jnp.zeros_like(hc)

    y = rest_ref[:, 0:D_CONV] * _sigmoid(rest_ref[:, D_CONV:2 * D_CONV])
    ybuf[CONV_HALO:CONV_HALO + ts, :] = y
    acc = jnp.zeros((ts, D_CONV), F32)
    for k in range(CONV_K):
        acc = acc + cw_ref[k:k + 1, :] * ybuf[pl.ds(CONV_HALO - (CONV_K - 1) + k, ts), :]
    yc = acc + cb_ref[...]
    mu = jnp.mean(yc, axis=-1, keepdims=True)
    var = jnp.mean(jnp.square(yc - mu), axis=-1, keepdims=True)
    yn = (yc - mu) * lax.rsqrt(var + EPS) * lg_ref[...] + lb_ref[...]
    o_ref[:, 0:D_CONV] = (yn * _sigmoid(yn)).astype(o_ref.dtype)
    ybuf[0:CONV_HALO, :] = ybuf[ts:ts + CONV_HALO, :]

    xbuf[RG_HALO:RG_HALO + ts, :] = rest_ref[:, 2 * D_CONV:2 * D_CONV + D_RNN]
    xc = jnp.zeros((ts, D_RNN), F32)
    for k in range(RNN_CONV_K):
        xc = xc + rw_ref[k:k + 1, :] * xbuf[pl.ds(RG_HALO - (RNN_CONV_K - 1) + k, ts), :]
    xc = xc + rb_ref[...]
    xbuf[0:RG_HALO, :] = xbuf[ts:ts + RG_HALO, :]
    xcb = xc.astype(BF)
    r = _sigmoid(_dot(xcb, wr_ref[...]) + br_ref[...])
    gi = _sigmoid(_dot(xcb, wi_ref[...]) + bi_ref[...])
    nl = -lam_ref[...]
    sp = jnp.maximum(nl, 0.0) + jnp.log(1.0 + jnp.exp(-jnp.abs(nl)))
    log_a = -RG_C * r * sp
    a = jnp.exp(log_a)
    bt = jnp.sqrt(1.0 - jnp.exp(2.0 * log_a)) * (gi * xc)
    row = lax.broadcasted_iota(I32, (ts, 1), 0)
    sh = 1
    while sh < ts:
        live = row >= sh
        a_s = jnp.where(live, pltpu.roll(a, sh, 0), 1.0)
        b_s = jnp.where(live, pltpu.roll(bt, sh, 0), 0.0)
        bt = bt + a * b_s
        a = a * a_s
        sh *= 2
    h = bt + a * hc[...]
    hc[...] = h[ts - 1:ts, :]
    gate_in = rest_ref[:, 2 * D_CONV + D_RNN:2 * D_CONV + 2 * D_RNN]
    o_ref[:, D_CONV:D_CONV + D_RNN] = (h * _gelu(gate_in)).astype(o_ref.dtype)


def mixers(rest, p, batch, seq, ts=512):
    T = rest.shape[0]
    ns = seq // ts
    body = functools.partial(_mix_body, ts=ts)
    vec = lambda: pl.BlockSpec((1, D_CONV), lambda b, s: (0, 0))
    return pl.pallas_call(
        body, grid=(batch, ns),
        in_specs=[pl.BlockSpec((ts, N_REST), lambda b, s: (b * ns + s, 0)),
                  pl.BlockSpec((32, D_CONV), lambda b, s: (0, 0)), vec(), vec(), vec(),
                  pl.BlockSpec((8, D_RNN), lambda b, s: (0, 0)), vec(),
                  pl.BlockSpec((D_RNN, D_RNN), lambda b, s: (0, 0)), vec(),
                  pl.BlockSpec((D_RNN, D_RNN), lambda b, s: (0, 0)), vec(), vec()],
        out_specs=pl.BlockSpec((ts, D_CONV + D_RNN), lambda b, s: (b * ns + s, 0)),
        out_shape=jax.ShapeDtypeStruct((T, D_CONV + D_RNN), BF),
        scratch_shapes=[pltpu.VMEM((ts + CONV_HALO, D_CONV), F32), pltpu.VMEM((ts + RG_HALO, D_RNN), F32),
                        pltpu.VMEM((1, D_RNN), F32)],
        compiler_params=_cp(("arbitrary", "arbitrary")), name="conv_rglru",
    )(rest, p["cw"], p["cb"], p["lg"], p["lb"], p["rw"], p["rb"], p["wr"], p["br"], p["wi"], p["bi"], p["lam"])


def pack_pairs(a):
    half = a.shape[1] // 2
    r = lax.bitcast_convert_type(a, I32)
    r = r + jnp.int32(0x7FFF) + jnp.bitwise_and(lax.shift_right_logical(r, 16), 1)
    lo = lax.shift_right_logical(r[:, :half], 16)
    hi = jnp.bitwise_and(r[:, half:], jnp.int32(-65536))
    return jnp.bitwise_or(hi, lo)


def _outproj_body(x_ref, ya_ref, yc_ref, woa_ref, wob_ref, g2_ref, wq_ref, keys_ref, x1_ref, h2p_ref, st_ref):
    x1 = x_ref[...] + _dot(ya_ref[...], woa_ref[...]) + _dot(yc_ref[...], wob_ref[...])
    x1_ref[...] = x1
    h2 = x1 * lax.rsqrt(jnp.mean(x1 * x1, axis=-1, keepdims=True) + EPS) * g2_ref[...]
    h2p_ref[...] = pack_pairs(h2)
    q = _dot(h2.astype(BF), wq_ref[...]).astype(BF)
    for g in range(2 * PEER_HEADS):
        st_ref[g * N_KEYS:(g + 1) * N_KEYS, :] = _nt(keys_ref[g], q[:, g * D_HALF:(g + 1) * D_HALF])


def out_proj(x, ya, yc, p, g2, row0=0, sub0=0, T=None, tm=256):
    T = ya.shape[0] if T is None else T
    ng = 2 * PEER_HEADS
    blk0 = (row0 + sub0) // tm
    sblk = sub0 // tm
    return pl.pallas_call(
        _outproj_body, grid=(T // tm,),
        in_specs=[pl.BlockSpec((tm, D_MODEL), lambda i: (i + blk0, 0)),
                  pl.BlockSpec((tm, D_ATT), lambda i: (i + sblk, 0)),
                  pl.BlockSpec((tm, D_CONV + D_RNN), lambda i: (i + sblk, 0)),
                  pl.BlockSpec((D_ATT, D_MODEL), lambda i: (0, 0)),
                  pl.BlockSpec((D_CONV + D_RNN, D_MODEL), lambda i: (0, 0)),
                  pl.BlockSpec((1, D_MODEL), lambda i: (0, 0)),
                  pl.BlockSpec((D_MODEL, ng * D_HALF), lambda i: (0, 0)),
                  pl.BlockSpec((ng, N_KEYS, D_HALF), lambda i: (0, 0, 0))],
        out_specs=[pl.BlockSpec((tm, D_MODEL), lambda i: (i, 0)),
                   pl.BlockSpec((tm, D_MODEL // 2), lambda i: (i, 0)),
                   pl.BlockSpec((ng * N_KEYS, tm), lambda i: (0, i))],
        out_shape=[jax.ShapeDtypeStruct((T, D_MODEL), F32),
                   jax.ShapeDtypeStruct((T, D_MODEL // 2), I32),
                   jax.ShapeDtypeStruct((ng * N_KEYS, T), F32)],
        compiler_params=_cp(("parallel",)), name="out_proj_peer_scores",
    )(x, ya, yc, p["woa"], p["wob"], g2, p["wq"], p["keys"])


BIG_ID = 1 << 20
SUBL = 8
NPAR = 4
NPAR1 = 8
SEL_CHAIN = 4


def _take_rounds(problems, nrounds):
    state = [list(slabs) for slabs, _ in problems]
    res = [([], []) for _ in problems]
    for _ in range(nrounds):
        for pi, (_, ids) in enumerate(problems):
            slabs = state[pi]
            m8 = slabs[0]
            for sl in slabs[1:]:
                m8 = jnp.maximum(m8, sl)
            m = jnp.max(m8, axis=0, keepdims=True)
            chains = []
            for c0 in range(0, len(slabs), SEL_CHAIN):
                v = jnp.full((SUBL, LANES), BIG_ID, I32)
                for sl, idc in zip(reversed(slabs[c0:c0 + SEL_CHAIN]), reversed(ids[c0:c0 + SEL_CHAIN])):
                    v = jnp.where(sl == m, idc, v)
                chains.append(v)
            while len(chains) > 1:
                chains = [jnp.minimum(chains[i], chains[i + 1]) if i + 1 < len(chains) else chains[i]
                          for i in range(0, len(chains), 2)]
            pick = jnp.min(chains[0], axis=0, keepdims=True)
            state[pi] = [jnp.where(idc == pick, NEG, sl) for sl, idc in zip(slabs, ids)]
            res[pi][0].append(m)
            res[pi][1].append(pick)
    return res


def _route_body(st_ref, e_ref, g_ref, v_scr, i_scr, sv_scr, ci_scr, et_scr, gt_scr):
    ng = 2 * PEER_HEADS
    sub = lax.broadcasted_iota(I32, (SUBL, LANES), 0)
    key_ids = [sub + SUBL * i for i in range(N_KEYS // SUBL)]

    def stage1(gg, carry):
        probs = []
        for q in range(NPAR1):
            base = pl.multiple_of((gg * NPAR1 + q) * N_KEYS, N_KEYS)
            probs.append(([st_ref[pl.ds(base + SUBL * i, SUBL), :] for i in range(N_KEYS // SUBL)], key_ids))
        for q, (vals, picks) in enumerate(_take_rounds(probs, TOPK)):
            for r in range(TOPK):
                v_scr[gg * NPAR1 + q, r:r + 1, :] = vals[r]
                i_scr[gg * NPAR1 + q, r:r + 1, :] = picks[r]
        return carry

    lax.fori_loop(0, ng // NPAR1, stage1, 0)

    def stage2(hh, carry):
        probs = []
        for q in range(NPAR):
            h = hh * NPAR + q
            v1 = v_scr[2 * h]
            v2 = v_scr[2 * h + 1]
            slabs = [v1[0:1, :] + v2[0:SUBL, :], v1[0:1, :] + v2[SUBL:TOPK, :]]
            ids = [sub, sub + SUBL]
            for i in range(1, TOPK):
                nj = TOPK // (i + 1)
                slabs.append(jnp.where(sub < nj, v1[i:i + 1, :] + v2[0:SUBL, :], NEG))
                ids.append(sub + i * TOPK)
            probs.append((slabs, ids))
        for q, (vals, picks) in enumerate(_take_rounds(probs, TOPK)):
            h = hh * NPAR + q
            i1 = i_scr[2 * h]
            i2 = i_scr[2 * h + 1]
            for r in range(TOPK):
                sv_scr[q, r:r + 1, :] = vals[r]
                ci_scr[q, r:r + 1, :] = picks[r]
            sv = sv_scr[q]
            ci = ci_scr[q]
            ci_hi = lax.shift_right_logical(ci, 4)
            ci_lo = jnp.bitwise_and(ci, TOPK - 1)
            e1 = jnp.zeros((TOPK, LANES), I32)
            e2 = jnp.zeros((TOPK, LANES), I32)
            for i in range(TOPK):
                e1 = jnp.where(ci_hi == i, i1[i:i + 1, :], e1)
                e2 = jnp.where(ci_lo == i, i2[i:i + 1, :], e2)
            p = jnp.exp(sv - sv[0:1, :])
            gates = p / jnp.sum(p, axis=0, keepdims=True)
            et_scr[pl.ds(pl.multiple_of(h * TOPK, TOPK), TOPK), :] = e1 * N_KEYS + e2
            gt_scr[pl.ds(pl.multiple_of(h * TOPK, TOPK), TOPK), :] = gates
        return carry

    lax.fori_loop(0, PEER_HEADS // NPAR, stage2, 0)
    e_ref[...] = et_scr[...].T
    g_ref[...] = gt_scr[...].T


def route(st):
    T = st.shape[1]
    ng = 2 * PEER_HEADS
    return pl.pallas_call(
        _route_body, grid=(T // LANES,),
        in_specs=[pl.BlockSpec((ng * N_KEYS, LANES), lambda i: (0, i))],
        out_specs=[pl.BlockSpec((LANES, KSEL), lambda i: (i, 0)),
                   pl.BlockSpec((LANES, KSEL), lambda i: (i, 0))],
        out_shape=[jax.ShapeDtypeStruct((T, KSEL), I32),
                   jax.ShapeDtypeStruct((T, KSEL), F32)],
        scratch_shapes=[pltpu.VMEM((ng, TOPK, LANES), F32), pltpu.VMEM((ng, TOPK, LANES), I32),
                        pltpu.VMEM((NPAR, TOPK, LANES), F32), pltpu.VMEM((NPAR, TOPK, LANES), I32),
                        pltpu.VMEM((KSEL, LANES), I32), pltpu.VMEM((KSEL, LANES), F32)],
        compiler_params=_cp(("parallel",)), name="peer_route",
    )(st)


NC, NS, L = 2, 16, 16
NW = NC * NS
NJ = D_MODEL // L
R = TOPK
NCH = KSEL // R
NB = 4
USTEP = 2
G = 8
NSLOT = 2
DW = D_MODEL // 2


def _perm(x, idx):
    return jnp.take_along_axis(x, idx, axis=0, mode="promise_in_bounds")


def _halves(w):
    lo = lax.bitcast_convert_type(lax.shift_left(w, 16), F32)
    hi = lax.bitcast_convert_type(jnp.bitwise_and(w, jnp.int32(-65536)), F32)
    return lo, hi


def _bf(w):
    return plsc.bitcast(w, BF)


def peer_sc(x, resid, idx, gates, uv_tab):
    T = x.shape[0]
    tpw = T // NW
    ngroups = tpw // G
    nchunks = G * NCH
    idx3 = idx.reshape(T * NCH, R)
    g3 = gates.reshape(T * NCH, R)
    mesh = plsc.VectorSubcoreMesh(core_axis_name="c", subcore_axis_name="s")

    @functools.partial(
        pl.kernel, mesh=mesh,
        out_type=jax.ShapeDtypeStruct((T, D_MODEL), F32),
        scratch_types=[
            pltpu.VMEM((NSLOT, G, DW), I32),
            pltpu.VMEM((NSLOT, G, D_MODEL), F32),
            pltpu.VMEM((NSLOT, nchunks, R), I32),
            pltpu.VMEM((NSLOT, nchunks, R), F32),
            pltpu.SemaphoreType.DMA((NSLOT,)),
            pltpu.SemaphoreType.DMA((NSLOT,)),
            pltpu.SemaphoreType.DMA((NSLOT,)),
        ] + [pltpu.VMEM((R, 2 * DW), I32) for _ in range(NB)]
          + [pltpu.SemaphoreType.DMA for _ in range(NB)],
        compiler_params=pltpu.CompilerParams(needs_layout_passes=False),
        name="peer_experts_sc",
    )
    def k(x_hbm, r_hbm, idx_hbm, g_hbm, uv_hbm, out_hbm, x_v, out_v, idx_v, g_v, st_sem, ix_sem, wb_sem, *ring):
        wid = lax.axis_index("s") * NC + lax.axis_index("c")
        bufs, sems = ring[:NB], ring[NB:]
        iota = lax.iota(I32, L)

        def tok0_of(g):
            return wid * tpw + g * G

        def stage_copies(g, slot):
            t0 = tok0_of(g)
            return (pltpu.make_async_copy(x_hbm.at[pl.ds(t0, G)], x_v.at[slot], st_sem.at[slot]),
                    pltpu.make_async_copy(r_hbm.at[pl.ds(t0, G)], out_v.at[slot], st_sem.at[slot]),
                    pltpu.make_async_copy(g_hbm.at[pl.ds(t0 * NCH, nchunks)], g_v.at[slot], st_sem.at[slot]))

        def idx_copy(g, slot):
            return pltpu.make_async_copy(idx_hbm.at[pl.ds(tok0_of(g) * NCH, nchunks)], idx_v.at[slot], ix_sem.at[slot])

        def wb_copy(g, slot):
            return pltpu.make_async_copy(out_v.at[slot], out_hbm.at[pl.ds(tok0_of(g), G)], wb_sem.at[slot])

        def gather_copy(slot, c, b):
            return pltpu.make_async_copy(uv_hbm.at[idx_v.at[slot, c]], bufs[b], sems[b])

        def compute(slot, c, b):
            ub = vb = bufs[b]
            t = c // NCH

            def ubody(mm, accs):
                xs_ = [_bf(x_v[slot, t, pl.ds((mm * USTEP + q) * L, L)]) for q in range(USTEP)]
                out = []
                for kk in range(R):
                    pr = [xs_[q] * _bf(ub[kk, pl.ds((mm * USTEP + q) * L, L)]) for q in range(USTEP)]
                    while len(pr) > 1:
                        pr = [pr[i] + pr[i + 1] for i in range(0, len(pr), 2)]
                    lo, hi = _halves(plsc.bitcast(pr[0], I32))
                    out.append(accs[kk] + (lo + hi))
                return tuple(out)

            accs = lax.fori_loop(0, NJ // (2 * USTEP), ubody, tuple(jnp.zeros((L,), F32) for _ in range(R)))
            vecs = list(accs)
            dist = L // 2
            while dist >= 1:
                pidx = jnp.bitwise_xor(iota, dist)
                low = jnp.bitwise_and(iota, dist) == 0
                nxt = []
                for kk in range(dist):
                    a = vecs[kk]
                    bvec = vecs[kk + dist]
                    a = a + _perm(a, pidx)
                    bvec = bvec + _perm(bvec, pidx)
                    nxt.append(jnp.where(low, a, bvec))
                vecs = nxt
                dist //= 2
            hid = vecs[0]
            z = GC * (hid + 0.044715 * hid * hid * hid)
            gel = hid / (1.0 + jnp.exp(-2.0 * z))
            w = g_v[slot, c, :] * gel
            wbs = []
            for kk in range(R):
                wb = _perm(w, jnp.full((L,), kk, I32))
                wbs.append(plsc.pack(wb, wb, format=plsc.PackFormat.INTERLEAVED))

            @plsc.parallel_loop(0, NJ // 2)
            def _(m):
                pr = [wbs[kk] * _bf(vb[kk, pl.ds(DW + m * L, L)]) for kk in range(R)]
                for _lvl in range(2):
                    pr = [pr[i] + pr[i + 1] for i in range(0, len(pr), 2)]
                los, his = [], []
                for q in pr:
                    lo, hi = _halves(plsc.bitcast(q, I32))
                    los.append(lo)
                    his.append(hi)
                while len(los) > 1:
                    los = [los[i] + los[i + 1] for i in range(0, len(los), 2)]
                    his = [his[i] + his[i + 1] for i in range(0, len(his), 2)]
                out_v[slot, t, pl.ds(m * L, L)] = out_v[slot, t, pl.ds(m * L, L)] + los[0]
                out_v[slot, t, pl.ds(DW + m * L, L)] = out_v[slot, t, pl.ds(DW + m * L, L)] + his[0]

        idx_copy(0, 0).start()
        for cp in stage_copies(0, 0):
            cp.start()
        idx_copy(0, 0).wait()
        for b in range(NB - 1):
            gather_copy(0, b, b).start()

        def group(g, carry):
            slot = g % NSLOT
            nslot = 1 - slot
            has_next = g + 1 < ngroups

            @pl.when(g >= 1)
            def _():
                wb_copy(g - 1, nslot).wait()

            @pl.when(has_next)
            def _():
                idx_copy(g + 1, nslot).start()
                for cp in stage_copies(g + 1, nslot):
                    cp.start()

            for cp in stage_copies(g, slot):
                cp.wait()

            def cbody(cc, c2):
                for b in range(NB):
                    c = cc * NB + b
                    cn = c + NB - 1
                    nb = (b + NB - 1) % NB

                    @pl.when(cn < nchunks)
                    def _():
                        gather_copy(slot, cn, nb).start()

                    @pl.when(jnp.logical_and(cn >= nchunks, has_next))
                    def _():
                        @pl.when(cn == nchunks)
                        def _():
                            idx_copy(g + 1, nslot).wait()

                        gather_copy(nslot, cn - nchunks, nb).start()

                    gather_copy(slot, c, b).wait()
                    compute(slot, c, b)
                return c2

            lax.fori_loop(0, nchunks // NB, cbody, 0)
            wb_copy(g, slot).start()
            return carry

        lax.fori_loop(0, ngroups, group, 0)
        wb_copy(ngroups - 1, (ngroups - 1) % NSLOT).wait()

    return k(x, resid, idx3, g3, uv_tab)


def _fn_body(x_ref, g_ref, o_ref):
    xf = x_ref[...]
    o_ref[...] = xf * lax.rsqrt(jnp.mean(xf * xf, axis=-1, keepdims=True) + EPS) * g_ref[...]


def _fn_body_into(x_ref, g_ref, prev_ref, o_ref):
    del prev_ref
    _fn_body(x_ref, g_ref, o_ref)


def final_norm(x, g, out, row0, t_total, tm=1024):
    T, d = x.shape
    blk0 = row0 // tm
    common = dict(grid=(T // tm,), out_specs=pl.BlockSpec((tm, d), lambda i: (i + blk0, 0)),
                  out_shape=jax.ShapeDtypeStruct((t_total, d), F32),
                  compiler_params=_cp(("parallel",)), name="final_norm")
    specs = [pl.BlockSpec((tm, d), lambda i: (i, 0)), pl.BlockSpec((1, d), lambda i: (0, 0))]
    if out is None:
        return pl.pallas_call(_fn_body, in_specs=specs, **common)(x, g)
    return pl.pallas_call(_fn_body_into, in_specs=specs + [pl.BlockSpec(memory_space=pl.ANY)],
                          input_output_aliases={2: 0}, **common)(x, g, out)


def _pack_body(u_ref, v_ref, o_ref):
    o_ref[:, 0:D_MODEL // 2] = pack_pairs(u_ref[...])
    o_ref[:, D_MODEL // 2:D_MODEL] = pack_pairs(v_ref[...])


def pack_tables(u, v, tm=512):
    e, d = u.shape
    return pl.pallas_call(
        _pack_body, grid=(e // tm,),
        in_specs=[pl.BlockSpec((tm, d), lambda i: (i, 0)), pl.BlockSpec((tm, d), lambda i: (i, 0))],
        out_specs=pl.BlockSpec((tm, d), lambda i: (i, 0)),
        out_shape=jax.ShapeDtypeStruct((e, d), I32),
        compiler_params=_cp(("parallel",)), name="pack_tables",
    )(u, v)


def _prep_layer(w_in, b_forget, conv_dw_w, conv_dw_b, conv_ln_g, conv_ln_b, rg_conv_w, rg_conv_b,
                rg_w_r, rg_b_r, rg_w_i, rg_b_i, rg_lambda, w_out, peer_wq, peer_k1, peer_k2):
    f0 = 3 * D_ATT
    wf = jnp.zeros((D_MODEL, FPAD), BF).at[:, 0:ATT_HEADS].set(w_in[:, f0:f0 + ATT_HEADS].astype(BF))
    bfg = jnp.zeros((1, FPAD), F32).at[0, 0:ATT_HEADS].set(b_forget)
    cw = jnp.zeros((32, D_CONV), F32).at[0:CONV_K].set(conv_dw_w)
    rw = jnp.zeros((8, D_RNN), F32).at[0:RNN_CONV_K].set(rg_conv_w)
    bd = lambda w: jax.scipy.linalg.block_diag(*[w[i] for i in range(RNN_BLOCKS)]).astype(BF)
    row = lambda v: v.reshape(1, -1).astype(F32)
    keys = jnp.stack([peer_k1, peer_k2], axis=1).reshape(2 * PEER_HEADS, N_KEYS, D_HALF).astype(BF)
    return dict(wqk=w_in[:, 0:2 * D_ATT].astype(BF), wvt=w_in[:, 2 * D_ATT:f0].T.astype(BF), wf=wf,
                wrest=w_in[:, f0 + ATT_HEADS:].astype(BF), bfg=bfg,
                cw=cw, cb=row(conv_dw_b), lg=row(conv_ln_g), lb=row(conv_ln_b),
                rw=rw, rb=row(rg_conv_b), wr=bd(rg_w_r), br=row(rg_b_r), wi=bd(rg_w_i), bi=row(rg_b_i),
                lam=row(rg_lambda), woa=w_out[0:D_ATT].astype(BF), wob=w_out[D_ATT:].astype(BF),
                wq=peer_wq.astype(BF), keys=keys)


def kernel(x, norm1_g, w_in, b_forget, conv_dw_w, conv_dw_b, conv_ln_g, conv_ln_b,
           rg_conv_w, rg_conv_b, rg_w_r, rg_b_r, rg_w_i, rg_b_i, rg_lambda, w_out,
           norm2_g, peer_wq, peer_k1, peer_k2, peer_u, peer_v, final_g):
    b, s, d = x.shape
    raw = (w_in, b_forget, conv_dw_w, conv_dw_b, conv_ln_g, conv_ln_b, rg_conv_w, rg_conv_b,
           rg_w_r, rg_b_r, rg_w_i, rg_b_i, rg_lambda, w_out, peer_wq, peer_k1, peer_k2)
    params = {0: _prep_layer(*[a[0] for a in raw])}
    tabs = {0: pack_tables(peer_u[0], peer_v[0])}
    bs = b // N_SLICES
    T = bs * s
    xf = x.reshape(b * s, d)
    xs = [None] * N_SLICES
    prev = None
    for l in range(DEPTH):
        for i in range(N_SLICES):
            src, row0 = (xf, i * T) if l == 0 else (xs[i], 0)
            first = l == 0 and i == 0
            late = l == 0 and i == LATE_PREP_SLICE
            deps = ([prev] if prev is not None else []) + ([tabs[0]] if first else [])
            deps += [*raw, peer_u, peer_v] if late else []
            if deps:
                tied = list(lax.optimization_barrier((src, *deps)))
                src = tied.pop(0)
                if prev is not None:
                    prev = tied.pop(0)
                if first:
                    tabs[0] = tied.pop(0)
                if late:
                    for ll in range(1, DEPTH):
                        params[ll] = _prep_layer(*[a[ll] for a in tied[:len(raw)]])
                        tabs[ll] = pack_tables(tied[-2][ll], tied[-1][ll])
            p = params[l]
            qkb, vt, rest = in_proj(src, norm1_g[l].reshape(1, d), p, s, T, row0)
            y_att = attention(qkb, vt, bs, s)
            y_cr = mixers(rest, p, bs, s)
            npiece = PIECES.get((l, i), 1)
            tp = T // npiece
            pieces = []
            for j in range(npiece):
                if j > 0:
                    src, experts = lax.optimization_barrier((src, experts))
                x1, h2p, st = out_proj(src, y_att, y_cr, p, norm2_g[l].reshape(1, d), row0, j * tp, tp)
                experts, gates = route(st)
                pieces.append(peer_sc(h2p, x1, experts, gates, tabs[l]))
            prev = experts
            xs[i] = pieces[0] if npiece == 1 else jnp.concatenate(pieces, axis=0)
    out = None
    for i in range(N_SLICES):
        out = final_norm(xs[i], final_g.reshape(1, d), out, i * T, b * s)
    return out.reshape(b, s, d)
```

```python
import functools
import math

import jax
import jax.numpy as jnp
from jax import lax
from jax.experimental import pallas as pl
from jax.experimental.pallas import tpu as pltpu
from jax.experimental.pallas import tpu_sc as plsc

BF = jnp.bfloat16
F32 = jnp.float32
I32 = jnp.int32

D_MODEL = 1024
DEPTH = 2
ATT_HEADS = 8
ATT_HD = 64
D_ATT = ATT_HEADS * ATT_HD
D_CONV = 256
CONV_K = 31
D_RNN = 256
RNN_BLOCKS = 4
RNN_CONV_K = 4
RG_C = 8.0
EPS = 1e-6
N_REST = 2 * D_CONV + 2 * D_RNN
PEER_HEADS = 8
N_KEYS = 128
D_HALF = 128
TOPK = 16
KSEL = PEER_HEADS * TOPK
GC = 0.7978845608028654
NEG = float("-inf")

PIECES = {(0, 0): 4, (0, 1): 2, (0, 2): 2}
LATE_PREP_SLICE = 2
N_SLICES = 4
LANES = 128
VMEM_LIMIT = 48 * 1024 * 1024


def _cp(sem):
    return pltpu.CompilerParams(dimension_semantics=sem, vmem_limit_bytes=VMEM_LIMIT)


def _split3(x):
    hi = x.astype(BF)
    r = x - hi.astype(F32)
    mid = r.astype(BF)
    lo = (r - mid.astype(F32)).astype(BF)
    return hi, mid, lo


def _nt(a, b):
    return lax.dot_general(a, b, (((1,), (1,)), ((), ())), preferred_element_type=F32)


def _dot(a, b):
    return jnp.dot(a, b, preferred_element_type=F32)


def _sigmoid(x):
    return 1.0 / (1.0 + jnp.exp(-x))


def _gelu(x):
    return 0.5 * x * (1.0 + jnp.tanh(GC * (x + 0.044715 * x * x * x)))


LOG2E = 1.4426950408889634
NSPLIT = 3
FPAD = 16


def _inproj_body(x_ref, g_ref, wqk_ref, wvt_ref, wf_ref, wrest_ref, bf_ref, tri_ref, place_ref,
                 qkb_ref, vt_ref, rest_ref, carry_ref, *, blocks_per_seq, tm):
    i = pl.program_id(0)
    x = x_ref[...]
    h = x * lax.rsqrt(jnp.mean(x * x, axis=-1, keepdims=True) + EPS) * g_ref[...]
    hb = h.astype(BF)
    qk = _dot(hb, wqk_ref[...])
    col = lax.broadcasted_iota(I32, (1, 2 * D_ATT), 1)
    qk = jnp.where(col < D_ATT, qk * (LOG2E / math.sqrt(ATT_HD)), qk)
    qkb_ref[:, 0:2 * D_ATT] = qk.astype(BF)
    vt_ref[...] = _nt(wvt_ref[...], hb).astype(BF)
    rest_ref[...] = _dot(hb, wrest_ref[...])
    ft = _dot(hb, wf_ref[...]) + bf_ref[...]
    lf = jnp.minimum(ft, 0.0) - jnp.log(1.0 + jnp.exp(-jnp.abs(ft)))
    hi, mid, lo = _split3(lf)
    tri = tri_ref[...]
    cs = _dot(tri, hi) + _dot(tri, mid) + _dot(tri, lo)

    @pl.when(i % blocks_per_seq == 0)
    def _():
        carry_ref[...] = jnp.zeros_like(carry_ref)

    cum = cs + carry_ref[...]
    carry_ref[...] = cum[tm - 1:tm, :]
    pieces = _split3(cum * (-LOG2E))
    kb = _dot(pieces[0], place_ref[0]) + _dot(pieces[1], place_ref[1]) + _dot(pieces[2], place_ref[2])
    qkb_ref[:, 2 * D_ATT:3 * D_ATT] = kb.astype(BF)


def bias_lane(hh, j):
    return (ATT_HD if hh == 0 else 0) + j


def in_proj(x, g, p, seq, T, row0=0, tm=512):
    blk0 = row0 // tm
    tri =(lax.broadcasted_iota(I32, (tm, tm), 0) >= lax.broadcasted_iota(I32, (tm, tm), 1)).astype(BF)
    shp = (NSPLIT, FPAD, D_ATT)
    hd = lax.broadcasted_iota(I32, shp, 1)
    target = (hd // 2) * LANES + jnp.where(hd % 2 == 0, ATT_HD, 0) + lax.broadcasted_iota(I32, shp, 0)
    place = ((lax.broadcasted_iota(I32, shp, 2) == target) & (hd < ATT_HEADS)).astype(BF)
    body = functools.partial(_inproj_body, blocks_per_seq=seq // tm, tm=tm)
    return pl.pallas_call(
        body, grid=(T // tm,),
        in_specs=[pl.BlockSpec((tm, D_MODEL), lambda i: (i + blk0, 0)),
                  pl.BlockSpec((1, D_MODEL), lambda i: (0, 0)),
                  pl.BlockSpec((D_MODEL, 2 * D_ATT), lambda i: (0, 0)),
                  pl.BlockSpec((D_ATT, D_MODEL), lambda i: (0, 0)),
                  pl.BlockSpec((D_MODEL, FPAD), lambda i: (0, 0)),
                  pl.BlockSpec((D_MODEL, N_REST), lambda i: (0, 0)),
                  pl.BlockSpec((1, FPAD), lambda i: (0, 0)),
                  pl.BlockSpec((tm, tm), lambda i: (0, 0)),
                  pl.BlockSpec(shp, lambda i: (0, 0, 0))],
        out_specs=[pl.BlockSpec((tm, 3 * D_ATT), lambda i: (i, 0)),
                   pl.BlockSpec((D_ATT, tm), lambda i: (0, i)),
                   pl.BlockSpec((tm, N_REST), lambda i: (i, 0))],
        out_shape=[jax.ShapeDtypeStruct((T, 3 * D_ATT), BF),
                   jax.ShapeDtypeStruct((D_ATT, T), BF),
                   jax.ShapeDtypeStruct((T, N_REST), F32)],
        scratch_shapes=[pltpu.VMEM((1, FPAD), F32)],
        compiler_params=_cp(("arbitrary",)), name="in_proj",
    )(x, g, p["wqk"], p["wvt"], p["wf"], p["wrest"], p["bfg"], tri, place)


def _attn_body(q_ref, k_ref, kb_ref, vt_ref, o_ref, m_ref, acc_ref, *, tq, tk):
    qi = pl.program_id(2)
    ki = pl.program_id(3)

    @pl.when(ki == 0)
    def _():
        m_ref[...] = jnp.full_like(m_ref, NEG)
        acc_ref[...] = jnp.zeros_like(acc_ref)

    lane = lax.broadcasted_iota(I32, (1, LANES), 1)
    first = lane < ATT_HD
    vrow = lax.broadcasted_iota(I32, (LANES, 1), 0) < ATT_HD

    def step(masked):
        q = q_ref[...]
        k = k_ref[...]
        kb = kb_ref[...]
        vt = vt_ref[...]
        if masked:
            keep = (lax.broadcasted_iota(I32, (tk, tq), 0) <= lax.broadcasted_iota(I32, (tk, tq), 1))
        for hh in range(2):
            own = first if hh == 0 else jnp.logical_not(first)
            ones = (lane >= bias_lane(hh, 0)) & (lane < bias_lane(hh, NSPLIT))
            qa = jnp.where(own, q, jnp.where(ones, 1.0, 0.0).astype(BF))
            ka = jnp.where(own, k, kb)
            st = _nt(ka, qa)
            if masked:
                st = jnp.where(keep, st, NEG)
            m_prev = m_ref[hh]
            m_new = jnp.maximum(m_prev, jnp.max(st, axis=0, keepdims=True))
            alpha = jnp.exp2(m_prev - m_new)
            p = jnp.exp2(st - m_new).astype(BF)
            m_ref[hh] = m_new
            vown = vrow if hh == 0 else jnp.logical_not(vrow)
            va = jnp.where(vown, vt, jnp.ones_like(vt))
            acc_ref[hh] = alpha * acc_ref[hh] + _dot(va, p)

    @pl.when(ki < qi)
    def _():
        step(False)

    @pl.when(ki == qi)
    def _():
        step(True)
        a0 = acc_ref[0]
        a1 = acc_ref[1]
        ot = jnp.where(vrow, a0 / a0[ATT_HD:ATT_HD + 1, :], a1 / a1[0:1, :])
        o_ref[...] = ot.T.astype(o_ref.dtype)


def attention(qkb, vt, batch, seq, tq=512):
    T = qkb.shape[0]
    tk = tq
    nq = seq // tq
    npair = ATT_HEADS // 2
    body = functools.partial(_attn_body, tq=tq, tk=tk)
    kblk = lambda b, qi, ki: b * nq + jnp.minimum(ki, qi)
    return pl.pallas_call(
        body, grid=(batch, npair, nq, nq),
        in_specs=[pl.BlockSpec((tq, LANES), lambda b, p, qi, ki: (b * nq + qi, p)),
                  pl.BlockSpec((tk, LANES), lambda b, p, qi, ki: (kblk(b, qi, ki), npair + p)),
                  pl.BlockSpec((tk, LANES), lambda b, p, qi, ki: (kblk(b, qi, ki), 2 * npair + p)),
                  pl.BlockSpec((LANES, tk), lambda b, p, qi, ki: (p, kblk(b, qi, ki)))],
        out_specs=pl.BlockSpec((tq, LANES), lambda b, p, qi, ki: (b * nq + qi, p)),
        out_shape=jax.ShapeDtypeStruct((T, D_ATT), BF),
        scratch_shapes=[pltpu.VMEM((2, 1, tq), F32), pltpu.VMEM((2, LANES, tq), F32)],
        compiler_params=_cp(("parallel", "parallel", "parallel", "arbitrary")), name="fox_attention",
    )(qkb, qkb, qkb, vt)


CONV_HALO = 32
RG_HALO = 8


def _mix_body(rest_ref, cw_ref, cb_ref, lg_ref, lb_ref, rw_ref, rb_ref, wr_ref, br_ref, wi_ref, bi_ref, lam_ref,
              o_ref, ybuf, xbuf, hc, *, ts):
    si = pl.program_id(1)

    @pl.when(si == 0)
    def _():
        ybuf[0:CONV_HALO, :] = jnp.zeros((CONV_HALO, D_CONV), F32)
        xbuf[0:RG_HALO, :] = jnp.zeros((RG_HALO, D_RNN), F32)
        hc[...] = jnp.zeros_like(hc)

    y = rest_ref[:, 0:D_CONV] * _sigmoid(rest_ref[:, D_CONV:2 * D_CONV])
    ybuf[CONV_HALO:CONV_HALO + ts, :] = y
    acc = jnp.zeros((ts, D_CONV), F32)
    for k in range(CONV_K):
        acc = acc + cw_ref[k:k + 1, :] * ybuf[pl.ds(CONV_HALO - (CONV_K - 1) + k, ts), :]
    yc = acc + cb_ref[...]
    mu = jnp.mean(yc, axis=-1, keepdims=True)
    var = jnp.mean(jnp.square(yc - mu), axis=-1, keepdims=True)
    yn = (yc - mu) * lax.rsqrt(var + EPS) * lg_ref[...] + lb_ref[...]
    o_ref[:, 0:D_CONV] = (yn * _sigmoid(yn)).astype(o_ref.dtype)
    ybuf[0:CONV_HALO, :] = ybuf[ts:ts + CONV_HALO, :]

    xbuf[RG_HALO:RG_HALO + ts, :] = rest_ref[:, 2 * D_CONV:2 * D_CONV + D_RNN]
    xc = jnp.zeros((ts, D_RNN), F32)
    for k in range(RNN_CONV_K):
        xc = xc + rw_ref[k:k + 1, :] * xbuf[pl.ds(RG_HALO - (RNN_CONV_K - 1) + k, ts), :]
    xc = xc + rb_ref[...]
    xbuf[0:RG_HALO, :] = xbuf[ts:ts + RG_HALO, :]
    xcb = xc.astype(BF)
    r = _sigmoid(_dot(xcb, wr_ref[...]) + br_ref[...])
    gi = _sigmoid(_dot(xcb, wi_ref[...]) + bi_ref[...])
    nl = -lam_ref[...]
    sp = jnp.maximum(nl, 0.0) + jnp.log(1.0 + jnp.exp(-jnp.abs(nl)))
    log_a = -RG_C * r * sp
    a = jnp.exp(log_a)
    bt = jnp.sqrt(1.0 - jnp.exp(2.0 * log_a)) * (gi * xc)
    row = lax.broadcasted_iota(I32, (ts, 1), 0)
    sh = 1
    while sh < ts:
        live = row >= sh
        a_s = jnp.where(live, pltpu.roll(a, sh, 0), 1.0)
        b_s = jnp.where(live, pltpu.roll(bt, sh, 0), 0.0)
        bt = bt + a * b_s
        a = a * a_s
        sh *= 2
    h = bt + a * hc[...]
    hc[...] = h[ts - 1:ts, :]
    gate_in = rest_ref[:, 2 * D_CONV + D_RNN:2 * D_CONV + 2 * D_RNN]
    o_ref[:, D_CONV:D_CONV + D_RNN] = (h * _gelu(gate_in)).astype(o_ref.dtype)


def mixers(rest, p, batch, seq, ts=512):
    T = rest.shape[0]
    ns = seq // ts
    body = functools.partial(_mix_body, ts=ts)
    vec = lambda: pl.BlockSpec((1, D_CONV), lambda b, s: (0, 0))
    return pl.pallas_call(
        body, grid=(batch, ns),
        in_specs=[pl.BlockSpec((ts, N_REST), lambda b, s: (b * ns + s, 0)),
                  pl.BlockSpec((32, D_CONV), lambda b, s: (0, 0)), vec(), vec(), vec(),
                  pl.BlockSpec((8, D_RNN), lambda b, s: (0, 0)), vec(),
                  pl.BlockSpec((D_RNN, D_RNN), lambda b, s: (0, 0)), vec(),
                  pl.BlockSpec((D_RNN, D_RNN), lambda b, s: (0, 0)), vec(), vec()],
        out_specs=pl.BlockSpec((ts, D_CONV + D_RNN), lambda b, s: (b * ns + s, 0)),
        out_shape=jax.ShapeDtypeStruct((T, D_CONV + D_RNN), BF),
        scratch_shapes=[pltpu.VMEM((ts + CONV_HALO, D_CONV), F32), pltpu.VMEM((ts + RG_HALO, D_RNN), F32),
                        pltpu.VMEM((1, D_RNN), F32)],
        compiler_params=_cp(("arbitrary", "arbitrary")), name="conv_rglru",
    )(rest, p["cw"], p["cb"], p["lg"], p["lb"], p["rw"], p["rb"], p["wr"], p["br"], p["wi"], p["bi"], p["lam"])


def pack_pairs(a):
    half = a.shape[1] // 2
    r = lax.bitcast_convert_type(a, I32)
    r = r + jnp.int32(0x7FFF) + jnp.bitwise_and(lax.shift_right_logical(r, 16), 1)
    lo = lax.shift_right_logical(r[:, :half], 16)
    hi = jnp.bitwise_and(r[:, half:], jnp.int32(-65536))
    return jnp.bitwise_or(hi, lo)


def _outproj_body(x_ref, ya_ref, yc_ref, woa_ref, wob_ref, g2_ref, wq_ref, keys_ref, x1_ref, h2p_ref, st_ref):
    x1 = x_ref[...] + _dot(ya_ref[...], woa_ref[...]) + _dot(yc_ref[...], wob_ref[...])
    x1_ref[...] = x1
    h2 = x1 * lax.rsqrt(jnp.mean(x1 * x1, axis=-1, keepdims=True) + EPS) * g2_ref[...]
    h2p_ref[...] = pack_pairs(h2)
    q = _dot(h2.astype(BF), wq_ref[...]).astype(BF)
    for g in range(2 * PEER_HEADS):
        st_ref[g * N_KEYS:(g + 1) * N_KEYS, :] = _nt(keys_ref[g], q[:, g * D_HALF:(g + 1) * D_HALF])


def out_proj(x, ya, yc, p, g2, row0=0, sub0=0, T=None, tm=256):
    T = ya.shape[0] if T is None else T
    ng = 2 * PEER_HEADS
    blk0 = (row0 + sub0) // tm
    sblk = sub0 // tm
    return pl.pallas_call(
        _outproj_body, grid=(T // tm,),
        in_specs=[pl.BlockSpec((tm, D_MODEL), lambda i: (i + blk0, 0)),
                  pl.BlockSpec((tm, D_ATT), lambda i: (i + sblk, 0)),
                  pl.BlockSpec((tm, D_CONV + D_RNN), lambda i: (i + sblk, 0)),
                  pl.BlockSpec((D_ATT, D_MODEL), lambda i: (0, 0)),
                  pl.BlockSpec((D_CONV + D_RNN, D_MODEL), lambda i: (0, 0)),
                  pl.BlockSpec((1, D_MODEL), lambda i: (0, 0)),
                  pl.BlockSpec((D_MODEL, ng * D_HALF), lambda i: (0, 0)),
                  pl.BlockSpec((ng, N_KEYS, D_HALF), lambda i: (0, 0, 0))],
        out_specs=[pl.BlockSpec((tm, D_MODEL), lambda i: (i, 0)),
                   pl.BlockSpec((tm, D_MODEL // 2), lambda i: (i, 0)),
                   pl.BlockSpec((ng * N_KEYS, tm), lambda i: (0, i))],
        out_shape=[jax.ShapeDtypeStruct((T, D_MODEL), F32),
                   jax.ShapeDtypeStruct((T, D_MODEL // 2), I32),
                   jax.ShapeDtypeStruct((ng * N_KEYS, T), F32)],
        compiler_params=_cp(("parallel",)), name="out_proj_peer_scores",
    )(x, ya, yc, p["woa"], p["wob"], g2, p["wq"], p["keys"])


BIG_ID = 1 << 20
SUBL = 8
NPAR = 4
NPAR1 = 8
SEL_CHAIN = 4


def _take_rounds(problems, nrounds):
    state = [list(slabs) for slabs, _ in problems]
    res = [([], []) for _ in problems]
    for _ in range(nrounds):
        for pi, (_, ids) in enumerate(problems):
            slabs = state[pi]
            m8 = slabs[0]
            for sl in slabs[1:]:
                m8 = jnp.maximum(m8, sl)
            m = jnp.max(m8, axis=0, keepdims=True)
            chains = []
            for c0 in range(0, len(slabs), SEL_CHAIN):
                v = jnp.full((SUBL, LANES), BIG_ID, I32)
                for sl, idc in zip(reversed(slabs[c0:c0 + SEL_CHAIN]), reversed(ids[c0:c0 + SEL_CHAIN])):
                    v = jnp.where(sl == m, idc, v)
                chains.append(v)
            while len(chains) > 1:
                chains = [jnp.minimum(chains[i], chains[i + 1]) if i + 1 < len(chains) else chains[i]
                          for i in range(0, len(chains), 2)]
            pick = jnp.min(chains[0], axis=0, keepdims=True)
            state[pi] = [jnp.where(idc == pick, NEG, sl) for sl, idc in zip(slabs, ids)]
            res[pi][0].append(m)
            res[pi][1].append(pick)
    return res


def _route_body(st_ref, e_ref, g_ref, v_scr, i_scr, sv_scr, ci_scr, et_scr, gt_scr):
    ng = 2 * PEER_HEADS
    sub = lax.broadcasted_iota(I32, (SUBL, LANES), 0)
    key_ids = [sub + SUBL * i for i in range(N_KEYS // SUBL)]

    def stage1(gg, carry):
        probs = []
        for q in range(NPAR1):
            base = pl.multiple_of((gg * NPAR1 + q) * N_KEYS, N_KEYS)
            probs.append(([st_ref[pl.ds(base + SUBL * i, SUBL), :] for i in range(N_KEYS // SUBL)], key_ids))
        for q, (vals, picks) in enumerate(_take_rounds(probs, TOPK)):
            for r in range(TOPK):
                v_scr[gg * NPAR1 + q, r:r + 1, :] = vals[r]
                i_scr[gg * NPAR1 + q, r:r + 1, :] = picks[r]
        return carry

    lax.fori_loop(0, ng // NPAR1, stage1, 0)

    def stage2(hh, carry):
        probs = []
        for q in range(NPAR):
            h = hh * NPAR + q
            v1 = v_scr[2 * h]
            v2 = v_scr[2 * h + 1]
            slabs = [v1[0:1, :] + v2[0:SUBL, :], v1[0:1, :] + v2[SUBL:TOPK, :]]
            ids = [sub, sub + SUBL]
            for i in range(1, TOPK):
                nj = TOPK // (i + 1)
                slabs.append(jnp.where(sub < nj, v1[i:i + 1, :] + v2[0:SUBL, :], NEG))
                ids.append(sub + i * TOPK)
            probs.append((slabs, ids))
        for q, (vals, picks) in enumerate(_take_rounds(probs, TOPK)):
            h = hh * NPAR + q
            i1 = i_scr[2 * h]
            i2 = i_scr[2 * h + 1]
            for r in range(TOPK):
                sv_scr[q, r:r + 1, :] = vals[r]
                ci_scr[q, r:r + 1, :] = picks[r]
            sv = sv_scr[q]
            ci = ci_scr[q]
            ci_hi = lax.shift_right_logical(ci, 4)
            ci_lo = jnp.bitwise_and(ci, TOPK - 1)
            e1 = jnp.zeros((TOPK, LANES), I32)
            e2 = jnp.zeros((TOPK, LANES), I32)
            for i in range(TOPK):
                e1 = jnp.where(ci_hi == i, i1[i:i + 1, :], e1)
                e2 = jnp.where(ci_lo == i, i2[i:i + 1, :], e2)
            p = jnp.exp(sv - sv[0:1, :])
            gates = p / jnp.sum(p, axis=0, keepdims=True)
            et_scr[pl.ds(pl.multiple_of(h * TOPK, TOPK), TOPK), :] = e1 * N_KEYS + e2
            gt_scr[pl.ds(pl.multiple_of(h * TOPK, TOPK), TOPK), :] = gates
        return carry

    lax.fori_loop(0, PEER_HEADS // NPAR, stage2, 0)
    e_ref[...] = et_scr[...].T
    g_ref[...] = gt_scr[...].T


def route(st):
    T = st.shape[1]
    ng = 2 * PEER_HEADS
    return pl.pallas_call(
        _route_body, grid=(T // LANES,),
        in_specs=[pl.BlockSpec((ng * N_KEYS, LANES), lambda i: (0, i))],
        out_specs=[pl.BlockSpec((LANES, KSEL), lambda i: (i, 0)),
                   pl.BlockSpec((LANES, KSEL), lambda i: (i, 0))],
        out_shape=[jax.ShapeDtypeStruct((T, KSEL), I32),
                   jax.ShapeDtypeStruct((T, KSEL), F32)],
        scratch_shapes=[pltpu.VMEM((ng, TOPK, LANES), F32), pltpu.VMEM((ng, TOPK, LANES), I32),
                        pltpu.VMEM((NPAR, TOPK, LANES), F32), pltpu.VMEM((NPAR, TOPK, LANES), I32),
                        pltpu.VMEM((KSEL, LANES), I32), pltpu.VMEM((KSEL, LANES), F32)],
        compiler_params=_cp(("parallel",)), name="peer_route",
    )(st)


NC, NS, L = 2, 16, 16
NW = NC * NS
NJ = D_MODEL // L
R = TOPK
NCH = KSEL // R
NB = 4
USTEP = 2
G = 8
NSLOT = 2
DW = D_MODEL // 2


def _perm(x, idx):
    return jnp.take_along_axis(x, idx, axis=0, mode="promise_in_bounds")


def _halves(w):
    lo = lax.bitcast_convert_type(lax.shift_left(w, 16), F32)
    hi = lax.bitcast_convert_type(jnp.bitwise_and(w, jnp.int32(-65536)), F32)
    return lo, hi


def _bf(w):
    return plsc.bitcast(w, BF)


def peer_sc(x, resid, idx, gates, uv_tab):
    T = x.shape[0]
    tpw = T // NW
    ngroups = tpw // G
    nchunks = G * NCH
    idx3 = idx.reshape(T * NCH, R)
    g3 = gates.reshape(T * NCH, R)
    mesh = plsc.VectorSubcoreMesh(core_axis_name="c", subcore_axis_name="s")

    @functools.partial(
        pl.kernel, mesh=mesh,
        out_type=jax.ShapeDtypeStruct((T, D_MODEL), F32),
        scratch_types=[
            pltpu.VMEM((NSLOT, G, DW), I32),
            pltpu.VMEM((NSLOT, G, D_MODEL), F32),
            pltpu.VMEM((NSLOT, nchunks, R), I32),
            pltpu.VMEM((NSLOT, nchunks, R), F32),
            pltpu.SemaphoreType.DMA((NSLOT,)),
            pltpu.SemaphoreType.DMA((NSLOT,)),
            pltpu.SemaphoreType.DMA((NSLOT,)),
        ] + [pltpu.VMEM((R, 2 * DW), I32) for _ in range(NB)]
          + [pltpu.SemaphoreType.DMA for _ in range(NB)],
        compiler_params=pltpu.CompilerParams(needs_layout_passes=False),
        name="peer_experts_sc",
    )
    def k(x_hbm, r_hbm, idx_hbm, g_hbm, uv_hbm, out_hbm, x_v, out_v, idx_v, g_v, st_sem, ix_sem, wb_sem, *ring):
        wid = lax.axis_index("s") * NC + lax.axis_index("c")
        bufs, sems = ring[:NB], ring[NB:]
        iota = lax.iota(I32, L)

        def tok0_of(g):
            return wid * tpw + g * G

        def stage_copies(g, slot):
            t0 = tok0_of(g)
            return (pltpu.make_async_copy(x_hbm.at[pl.ds(t0, G)], x_v.at[slot], st_sem.at[slot]),
                    pltpu.make_async_copy(r_hbm.at[pl.ds(t0, G)], out_v.at[slot], st_sem.at[slot]),
                    pltpu.make_async_copy(g_hbm.at[pl.ds(t0 * NCH, nchunks)], g_v.at[slot], st_sem.at[slot]))

        def idx_copy(g, slot):
            return pltpu.make_async_copy(idx_hbm.at[pl.ds(tok0_of(g) * NCH, nchunks)], idx_v.at[slot], ix_sem.at[slot])

        def wb_copy(g, slot):
            return pltpu.make_async_copy(out_v.at[slot], out_hbm.at[pl.ds(tok0_of(g), G)], wb_sem.at[slot])

        def gather_copy(slot, c, b):
            return pltpu.make_async_copy(uv_hbm.at[idx_v.at[slot, c]], bufs[b], sems[b])

        def compute(slot, c, b):
            ub = vb = bufs[b]
            t = c // NCH

            def ubody(mm, accs):
                xs_ = [_bf(x_v[slot, t, pl.ds((mm * USTEP + q) * L, L)]) for q in range(USTEP)]
                out = []
                for kk in range(R):
                    pr = [xs_[q] * _bf(ub[kk, pl.ds((mm * USTEP + q) * L, L)]) for q in range(USTEP)]
                    while len(pr) > 1:
                        pr = [pr[i] + pr[i + 1] for i in range(0, len(pr), 2)]
                    lo, hi = _halves(plsc.bitcast(pr[0], I32))
                    out.append(accs[kk] + (lo + hi))
                return tuple(out)

            accs = lax.fori_loop(0, NJ // (2 * USTEP), ubody, tuple(jnp.zeros((L,), F32) for _ in range(R)))
            vecs = list(accs)
            dist = L // 2
            while dist >= 1:
                pidx = jnp.bitwise_xor(iota, dist)
                low = jnp.bitwise_and(iota, dist) == 0
                nxt = []
                for kk in range(dist):
                    a = vecs[kk]
                    bvec = vecs[kk + dist]
                    a = a + _perm(a, pidx)
                    bvec = bvec + _perm(bvec, pidx)
                    nxt.append(jnp.where(low, a, bvec))
                vecs = nxt
                dist //= 2
            hid = vecs[0]
            z = GC * (hid + 0.044715 * hid * hid * hid)
            gel = hid / (1.0 + jnp.exp(-2.0 * z))
            w = g_v[slot, c, :] * gel
            wbs = []
            for kk in range(R):
                wb = _perm(w, jnp.full((L,), kk, I32))
                wbs.append(plsc.pack(wb, wb, format=plsc.PackFormat.INTERLEAVED))

            @plsc.parallel_loop(0, NJ // 2)
            def _(m):
                pr = [wbs[kk] * _bf(vb[kk, pl.ds(DW + m * L, L)]) for kk in range(R)]
                for _lvl in range(2):
                    pr = [pr[i] + pr[i + 1] for i in range(0, len(pr), 2)]
                los, his = [], []
                for q in pr:
                    lo, hi = _halves(plsc.bitcast(q, I32))
                    los.append(lo)
                    his.append(hi)
                while len(los) > 1:
                    los = [los[i] + los[i + 1] for i in range(0, len(los), 2)]
                    his = [his[i] + his[i + 1] for i in range(0, len(his), 2)]
                out_v[slot, t, pl.ds(m * L, L)] = out_v[slot, t, pl.ds(m * L, L)] + los[0]
                out_v[slot, t, pl.ds(DW + m * L, L)] = out_v[slot, t, pl.ds(DW + m * L, L)] + his[0]

        idx_copy(0, 0).start()
        for cp in stage_copies(0, 0):
            cp.start()
        idx_copy(0, 0).wait()
        for b in range(NB - 1):
            gather_copy(0, b, b).start()

        def group(g, carry):
            slot = g % NSLOT
            nslot = 1 - slot
            has_next = g + 1 < ngroups

            @pl.when(g >= 1)
            def _():
                wb_copy(g - 1, nslot).wait()

            @pl.when(has_next)
            def _():
                idx_copy(g + 1, nslot).start()
                for cp in stage_copies(g + 1, nslot):
                    cp.start()

            for cp in stage_copies(g, slot):
                cp.wait()

            def cbody(cc, c2):
                for b in range(NB):
                    c = cc * NB + b
                    cn = c + NB - 1
                    nb = (b + NB - 1) % NB

                    @pl.when(cn < nchunks)
                    def _():
                        gather_copy(slot, cn, nb).start()

                    @pl.when(jnp.logical_and(cn >= nchunks, has_next))
                    def _():
                        @pl.when(cn == nchunks)
                        def _():
                            idx_copy(g + 1, nslot).wait()

                        gather_copy(nslot, cn - nchunks, nb).start()

                    gather_copy(slot, c, b).wait()
                    compute(slot, c, b)
                return c2

            lax.fori_loop(0, nchunks // NB, cbody, 0)
            wb_copy(g, slot).start()
            return carry

        lax.fori_loop(0, ngroups, group, 0)
        wb_copy(ngroups - 1, (ngroups - 1) % NSLOT).wait()

    return k(x, resid, idx3, g3, uv_tab)


def _fn_body(x_ref, g_ref, o_ref):
    xf = x_ref[...]
    o_ref[...] = xf * lax.rsqrt(jnp.mean(xf * xf, axis=-1, keepdims=True) + EPS) * g_ref[...]


def _fn_body_into(x_ref, g_ref, prev_ref, o_ref):
    del prev_ref
    _fn_body(x_ref, g_ref, o_ref)


def final_norm(x, g, out, row0, t_total, tm=1024):
    T, d = x.shape
    blk0 = row0 // tm
    common = dict(grid=(T // tm,), out_specs=pl.BlockSpec((tm, d), lambda i: (i + blk0, 0)),
                  out_shape=jax.ShapeDtypeStruct((t_total, d), F32),
                  compiler_params=_cp(("parallel",)), name="final_norm")
    specs = [pl.BlockSpec((tm, d), lambda i: (i, 0)), pl.BlockSpec((1, d), lambda i: (0, 0))]
    if out is None:
        return pl.pallas_call(_fn_body, in_specs=specs, **common)(x, g)
    return pl.pallas_call(_fn_body_into, in_specs=specs + [pl.BlockSpec(memory_space=pl.ANY)],
                          input_output_aliases={2: 0}, **common)(x, g, out)


def _pack_body(u_ref, v_ref, o_ref):
    o_ref[:, 0:D_MODEL // 2] = pack_pairs(u_ref[...])
    o_ref[:, D_MODEL // 2:D_MODEL] = pack_pairs(v_ref[...])


def pack_tables(u, v, tm=512):
    e, d = u.shape
    return pl.pallas_call(
        _pack_body, grid=(e // tm,),
        in_specs=[pl.BlockSpec((tm, d), lambda i: (i, 0)), pl.BlockSpec((tm, d), lambda i: (i, 0))],
        out_specs=pl.BlockSpec((tm, d), lambda i: (i, 0)),
        out_shape=jax.ShapeDtypeStruct((e, d), I32),
        compiler_params=_cp(("parallel",)), name="pack_tables",
    )(u, v)


def _prep_layer(w_in, b_forget, conv_dw_w, conv_dw_b, conv_ln_g, conv_ln_b, rg_conv_w, rg_conv_b,
                rg_w_r, rg_b_r, rg_w_i, rg_b_i, rg_lambda, w_out, peer_wq, peer_k1, peer_k2):
    f0 = 3 * D_ATT
    wf = jnp.zeros((D_MODEL, FPAD), BF).at[:, 0:ATT_HEADS].set(w_in[:, f0:f0 + ATT_HEADS].astype(BF))
    bfg = jnp.zeros((1, FPAD), F32).at[0, 0:ATT_HEADS].set(b_forget)
    cw = jnp.zeros((32, D_CONV), F32).at[0:CONV_K].set(conv_dw_w)
    rw = jnp.zeros((8, D_RNN), F32).at[0:RNN_CONV_K].set(rg_conv_w)
    bd = lambda w: jax.scipy.linalg.block_diag(*[w[i] for i in range(RNN_BLOCKS)]).astype(BF)
    row = lambda v: v.reshape(1, -1).astype(F32)
    keys = jnp.stack([peer_k1, peer_k2], axis=1).reshape(2 * PEER_HEADS, N_KEYS, D_HALF).astype(BF)
    return dict(wqk=w_in[:, 0:2 * D_ATT].astype(BF), wvt=w_in[:, 2 * D_ATT:f0].T.astype(BF), wf=wf,
                wrest=w_in[:, f0 + ATT_HEADS:].astype(BF), bfg=bfg,
                cw=cw, cb=row(conv_dw_b), lg=row(conv_ln_g), lb=row(conv_ln_b),
                rw=rw, rb=row(rg_conv_b), wr=bd(rg_w_r), br=row(rg_b_r), wi=bd(rg_w_i), bi=row(rg_b_i),
                lam=row(rg_lambda), woa=w_out[0:D_ATT].astype(BF), wob=w_out[D_ATT:].astype(BF),
                wq=peer_wq.astype(BF), keys=keys)


def kernel(x, norm1_g, w_in, b_forget, conv_dw_w, conv_dw_b, conv_ln_g, conv_ln_b,
           rg_conv_w, rg_conv_b, rg_w_r, rg_b_r, rg_w_i, rg_b_i, rg_lambda, w_out,
           norm2_g, peer_wq, peer_k1, peer_k2, peer_u, peer_v, final_g):
    b, s, d = x.shape
    raw = (w_in, b_forget, conv_dw_w, conv_dw_b, conv_ln_g, conv_ln_b, rg_conv_w, rg_conv_b,
           rg_w_r, rg_b_r, rg_w_i, rg_b_i, rg_lambda, w_out, peer_wq, peer_k1, peer_k2)
    params = {0: _prep_layer(*[a[0] for a in raw])}
    tabs = {0: pack_tables(peer_u[0], peer_v[0])}
    bs = b // N_SLICES
    T = bs * s
    xf = x.reshape(b * s, d)
    xs = [None] * N_SLICES
    prev = None
    for l in range(DEPTH):
        for i in range(N_SLICES):
            src, row0 = (xf, i * T) if l == 0 else (xs[i], 0)
            first = l == 0 and i == 0
            late = l == 0 and i == LATE_PREP_SLICE
            deps = ([prev] if prev is not None else []) + ([tabs[0]] if first else [])
            deps += [*raw, peer_u, peer_v] if late else []
            if deps:
                tied = list(lax.optimization_barrier((src, *deps)))
                src = tied.pop(0)
                if prev is not None:
                    prev = tied.pop(0)
                if first:
                    tabs[0] = tied.pop(0)
                if late:
                    for ll in range(1, DEPTH):
                        params[ll] = _prep_layer(*[a[ll] for a in tied[:len(raw)]])
                        tabs[ll] = pack_tables(tied[-2][ll], tied[-1][ll])
            p = params[l]
            qkb, vt, rest = in_proj(src, norm1_g[l].reshape(1, d), p, s, T, row0)
            y_att = attention(qkb, vt, bs, s)
            y_cr = mixers(rest, p, bs, s)
            npiece = PIECES.get((l, i), 1)
            tp = T // npiece
            pieces = []
            for j in range(npiece):
                if j > 0:
                    src, experts = lax.optimization_barrier((src, experts))
                x1, h2p, st = out_proj(src, y_att, y_cr, p, norm2_g[l].reshape(1, d), row0, j * tp, tp)
                experts, gates = route(st)
                pieces.append(peer_sc(h2p, x1, experts, gates, tabs[l]))
            prev = experts
            xs[i] = pieces[0] if npiece == 1 else jnp.concatenate(pieces, axis=0)
    out = None
    for i in range(N_SLICES):
        out = final_norm(xs[i], final_g.reshape(1, d), out, i * T, b * s)
    return out.reshape(b, s, d)
```

```python
import functools
import math

import jax
import jax.numpy as jnp
from jax import lax
from jax.experimental import pallas as pl
from jax.experimental.pallas import tpu as pltpu
from jax.experimental.pallas import tpu_sc as plsc

BF = jnp.bfloat16
F32 = jnp.float32
I32 = jnp.int32

D_MODEL = 1024
DEPTH = 2
ATT_HEADS = 8
ATT_HD = 64
D_ATT = ATT_HEADS * ATT_HD
D_CONV = 256
CONV_K = 31
D_RNN = 256
RNN_BLOCKS = 4
RNN_CONV_K = 4
RG_C = 8.0
EPS = 1e-6
N_REST = 2 * D_CONV + 2 * D_RNN
PEER_HEADS = 8
N_KEYS = 128
D_HALF = 128
TOPK = 16
KSEL = PEER_HEADS * TOPK
GC = 0.7978845608028654
NEG = float("-inf")

PIECES = {(0, 0): 4, (0, 1): 2}
LATE_PREP_SLICE = 2
N_SLICES = 4
LANES = 128
VMEM_LIMIT = 48 * 1024 * 1024


def _cp(sem):
    return pltpu.CompilerParams(dimension_semantics=sem, vmem_limit_bytes=VMEM_LIMIT)


def _split3(x):
    hi = x.astype(BF)
    r = x - hi.astype(F32)
    mid = r.astype(BF)
    lo = (r - mid.astype(F32)).astype(BF)
    return hi, mid, lo


def _nt(a, b):
    return lax.dot_general(a, b, (((1,), (1,)), ((), ())), preferred_element_type=F32)


def _dot(a, b):
    return jnp.dot(a, b, preferred_element_type=F32)


def _sigmoid(x):
    return 1.0 / (1.0 + jnp.exp(-x))


def _gelu(x):
    return 0.5 * x * (1.0 + jnp.tanh(GC * (x + 0.044715 * x * x * x)))


LOG2E = 1.4426950408889634
NSPLIT = 3
FPAD = 16


def _inproj_body(x_ref, g_ref, wqk_ref, wvt_ref, wf_ref, wrest_ref, bf_ref, tri_ref, place_ref,
                 qkb_ref, vt_ref, rest_ref, carry_ref, *, blocks_per_seq, tm):
    i = pl.program_id(0)
    x = x_ref[...]
    h = x * lax.rsqrt(jnp.mean(x * x, axis=-1, keepdims=True) + EPS) * g_ref[...]
    hb = h.astype(BF)
    qk = _dot(hb, wqk_ref[...])
    col = lax.broadcasted_iota(I32, (1, 2 * D_ATT), 1)
    qk = jnp.where(col < D_ATT, qk * (LOG2E / math.sqrt(ATT_HD)), qk)
    qkb_ref[:, 0:2 * D_ATT] = qk.astype(BF)
    vt_ref[...] = _nt(wvt_ref[...], hb).astype(BF)
    rest_ref[...] = _dot(hb, wrest_ref[...])
    ft = _dot(hb, wf_ref[...]) + bf_ref[...]
    lf = jnp.minimum(ft, 0.0) - jnp.log(1.0 + jnp.exp(-jnp.abs(ft)))
    hi, mid, lo = _split3(lf)
    tri = tri_ref[...]
    cs = _dot(tri, hi) + _dot(tri, mid) + _dot(tri, lo)

    @pl.when(i % blocks_per_seq == 0)
    def _():
        carry_ref[...] = jnp.zeros_like(carry_ref)

    cum = cs + carry_ref[...]
    carry_ref[...] = cum[tm - 1:tm, :]
    pieces = _split3(cum * (-LOG2E))
    kb = _dot(pieces[0], place_ref[0]) + _dot(pieces[1], place_ref[1]) + _dot(pieces[2], place_ref[2])
    qkb_ref[:, 2 * D_ATT:3 * D_ATT] = kb.astype(BF)


def bias_lane(hh, j):
    return (ATT_HD if hh == 0 else 0) + j


def in_proj(x, g, p, seq, T, row0=0, tm=512):
    blk0 = row0 // tm
    tri =(lax.broadcasted_iota(I32, (tm, tm), 0) >= lax.broadcasted_iota(I32, (tm, tm), 1)).astype(BF)
    shp = (NSPLIT, FPAD, D_ATT)
    hd = lax.broadcasted_iota(I32, shp, 1)
    target = (hd // 2) * LANES + jnp.where(hd % 2 == 0, ATT_HD, 0) + lax.broadcasted_iota(I32, shp, 0)
    place = ((lax.broadcasted_iota(I32, shp, 2) == target) & (hd < ATT_HEADS)).astype(BF)
    body = functools.partial(_inproj_body, blocks_per_seq=seq // tm, tm=tm)
    return pl.pallas_call(
        body, grid=(T // tm,),
        in_specs=[pl.BlockSpec((tm, D_MODEL), lambda i: (i + blk0, 0)),
                  pl.BlockSpec((1, D_MODEL), lambda i: (0, 0)),
                  pl.BlockSpec((D_MODEL, 2 * D_ATT), lambda i: (0, 0)),
                  pl.BlockSpec((D_ATT, D_MODEL), lambda i: (0, 0)),
                  pl.BlockSpec((D_MODEL, FPAD), lambda i: (0, 0)),
                  pl.BlockSpec((D_MODEL, N_REST), lambda i: (0, 0)),
                  pl.BlockSpec((1, FPAD), lambda i: (0, 0)),
                  pl.BlockSpec((tm, tm), lambda i: (0, 0)),
                  pl.BlockSpec(shp, lambda i: (0, 0, 0))],
        out_specs=[pl.BlockSpec((tm, 3 * D_ATT), lambda i: (i, 0)),
                   pl.BlockSpec((D_ATT, tm), lambda i: (0, i)),
                   pl.BlockSpec((tm, N_REST), lambda i: (i, 0))],
        out_shape=[jax.ShapeDtypeStruct((T, 3 * D_ATT), BF),
                   jax.ShapeDtypeStruct((D_ATT, T), BF),
                   jax.ShapeDtypeStruct((T, N_REST), F32)],
        scratch_shapes=[pltpu.VMEM((1, FPAD), F32)],
        compiler_params=_cp(("arbitrary",)), name="in_proj",
    )(x, g, p["wqk"], p["wvt"], p["wf"], p["wrest"], p["bfg"], tri, place)


def _attn_body(q_ref, k_ref, kb_ref, vt_ref, o_ref, m_ref, acc_ref, *, tq, tk):
    qi = pl.program_id(2)
    ki = pl.program_id(3)

    @pl.when(ki == 0)
    def _():
        m_ref[...] = jnp.full_like(m_ref, NEG)
        acc_ref[...] = jnp.zeros_like(acc_ref)

    lane = lax.broadcasted_iota(I32, (1, LANES), 1)
    first = lane < ATT_HD
    vrow = lax.broadcasted_iota(I32, (LANES, 1), 0) < ATT_HD

    def step(masked):
        q = q_ref[...]
        k = k_ref[...]
        kb = kb_ref[...]
        vt = vt_ref[...]
        if masked:
            keep = (lax.broadcasted_iota(I32, (tk, tq), 0) <= lax.broadcasted_iota(I32, (tk, tq), 1))
        for hh in range(2):
            own = first if hh == 0 else jnp.logical_not(first)
            ones = (lane >= bias_lane(hh, 0)) & (lane < bias_lane(hh, NSPLIT))
            qa = jnp.where(own, q, jnp.where(ones, 1.0, 0.0).astype(BF))
            ka = jnp.where(own, k, kb)
            st = _nt(ka, qa)
            if masked:
                st = jnp.where(keep, st, NEG)
            m_prev = m_ref[hh]
            m_new = jnp.maximum(m_prev, jnp.max(st, axis=0, keepdims=True))
            alpha = jnp.exp2(m_prev - m_new)
            p = jnp.exp2(st - m_new).astype(BF)
            m_ref[hh] = m_new
            vown = vrow if hh == 0 else jnp.logical_not(vrow)
            va = jnp.where(vown, vt, jnp.ones_like(vt))
            acc_ref[hh] = alpha * acc_ref[hh] + _dot(va, p)

    @pl.when(ki < qi)
    def _():
        step(False)

    @pl.when(ki == qi)
    def _():
        step(True)
        a0 = acc_ref[0]
        a1 = acc_ref[1]
        ot = jnp.where(vrow, a0 / a0[ATT_HD:ATT_HD + 1, :], a1 / a1[0:1, :])
        o_ref[...] = ot.T.astype(o_ref.dtype)


def attention(qkb, vt, batch, seq, tq=512):
    T = qkb.shape[0]
    tk = tq
    nq = seq // tq
    npair = ATT_HEADS // 2
    body = functools.partial(_attn_body, tq=tq, tk=tk)
    kblk = lambda b, qi, ki: b * nq + jnp.minimum(ki, qi)
    return pl.pallas_call(
        body, grid=(batch, npair, nq, nq),
        in_specs=[pl.BlockSpec((tq, LANES), lambda b, p, qi, ki: (b * nq + qi, p)),
                  pl.BlockSpec((tk, LANES), lambda b, p, qi, ki: (kblk(b, qi, ki), npair + p)),
                  pl.BlockSpec((tk, LANES), lambda b, p, qi, ki: (kblk(b, qi, ki), 2 * npair + p)),
                  pl.BlockSpec((LANES, tk), lambda b, p, qi, ki: (p, kblk(b, qi, ki)))],
        out_specs=pl.BlockSpec((tq, LANES), lambda b, p, qi, ki: (b * nq + qi, p)),
        out_shape=jax.ShapeDtypeStruct((T, D_ATT), BF),
        scratch_shapes=[pltpu.VMEM((2, 1, tq), F32), pltpu.VMEM((2, LANES, tq), F32)],
        compiler_params=_cp(("parallel", "parallel", "parallel", "arbitrary")), name="fox_attention",
    )(qkb, qkb, qkb, vt)


CONV_HALO = 32
RG_HALO = 8


def _mix_body(rest_ref, cw_ref, cb_ref, lg_ref, lb_ref, rw_ref, rb_ref, wr_ref, br_ref, wi_ref, bi_ref, lam_ref,
              o_ref, ybuf, xbuf, hc, *, ts):
    si = pl.program_id(1)

    @pl.when(si == 0)
    def _():
        ybuf[0:CONV_HALO, :] = jnp.zeros((CONV_HALO, D_CONV), F32)
        xbuf[0:RG_HALO, :] = jnp.zeros((RG_HALO, D_RNN), F32)
        hc[...] = jnp.zeros_like(hc)

    y = rest_ref[:, 0:D_CONV] * _sigmoid(rest_ref[:, D_CONV:2 * D_CONV])
    ybuf[CONV_HALO:CONV_HALO + ts, :] = y
    acc = jnp.zeros((ts, D_CONV), F32)
    for k in range(CONV_K):
        acc = acc + cw_ref[k:k + 1, :] * ybuf[pl.ds(CONV_HALO - (CONV_K - 1) + k, ts), :]
    yc = acc + cb_ref[...]
    mu = jnp.mean(yc, axis=-1, keepdims=True)
    var = jnp.mean(jnp.square(yc - mu), axis=-1, keepdims=True)
    yn = (yc - mu) * lax.rsqrt(var + EPS) * lg_ref[...] + lb_ref[...]
    o_ref[:, 0:D_CONV] = (yn * _sigmoid(yn)).astype(o_ref.dtype)
    ybuf[0:CONV_HALO, :] = ybuf[ts:ts + CONV_HALO, :]

    xbuf[RG_HALO:RG_HALO + ts, :] = rest_ref[:, 2 * D_CONV:2 * D_CONV + D_RNN]
    xc = jnp.zeros((ts, D_RNN), F32)
    for k in range(RNN_CONV_K):
        xc = xc + rw_ref[k:k + 1, :] * xbuf[pl.ds(RG_HALO - (RNN_CONV_K - 1) + k, ts), :]
    xc = xc + rb_ref[...]
    xbuf[0:RG_HALO, :] = xbuf[ts:ts + RG_HALO, :]
    xcb = xc.astype(BF)
    r = _sigmoid(_dot(xcb, wr_ref[...]) + br_ref[...])
    gi = _sigmoid(_dot(xcb, wi_ref[...]) + bi_ref[...])
    nl = -lam_ref[...]
    sp = jnp.maximum(nl, 0.0) + jnp.log(1.0 + jnp.exp(-jnp.abs(nl)))
    log_a = -RG_C * r * sp
    a = jnp.exp(log_a)
    bt = jnp.sqrt(1.0 - jnp.exp(2.0 * log_a)) * (gi * xc)
    row = lax.broadcasted_iota(I32, (ts, 1), 0)
    sh = 1
    while sh < ts:
        live = row >= sh
        a_s = jnp.where(live, pltpu.roll(a, sh, 0), 1.0)
        b_s = jnp.where(live, pltpu.roll(bt, sh, 0), 0.0)
        bt = bt + a * b_s
        a = a * a_s
        sh *= 2
    h = bt + a * hc[...]
    hc[...] = h[ts - 1:ts, :]
    gate_in = rest_ref[:, 2 * D_CONV + D_RNN:2 * D_CONV + 2 * D_RNN]
    o_ref[:, D_CONV:D_CONV + D_RNN] = (h * _gelu(gate_in)).astype(o_ref.dtype)


def mixers(rest, p, batch, seq, ts=512):
    T = rest.shape[0]
    ns = seq // ts
    body = functools.partial(_mix_body, ts=ts)
    vec = lambda: pl.BlockSpec((1, D_CONV), lambda b, s: (0, 0))
    return pl.pallas_call(
        body, grid=(batch, ns),
        in_specs=[pl.BlockSpec((ts, N_REST), lambda b, s: (b * ns + s, 0)),
                  pl.BlockSpec((32, D_CONV), lambda b, s: (0, 0)), vec(), vec(), vec(),
                  pl.BlockSpec((8, D_RNN), lambda b, s: (0, 0)), vec(),
                  pl.BlockSpec((D_RNN, D_RNN), lambda b, s: (0, 0)), vec(),
                  pl.BlockSpec((D_RNN, D_RNN), lambda b, s: (0, 0)), vec(), vec()],
        out_specs=pl.BlockSpec((ts, D_CONV + D_RNN), lambda b, s: (b * ns + s, 0)),
        out_shape=jax.ShapeDtypeStruct((T, D_CONV + D_RNN), BF),
        scratch_shapes=[pltpu.VMEM((ts + CONV_HALO, D_CONV), F32), pltpu.VMEM((ts + RG_HALO, D_RNN), F32),
                        pltpu.VMEM((1, D_RNN), F32)],
        compiler_params=_cp(("arbitrary", "arbitrary")), name="conv_rglru",
    )(rest, p["cw"], p["cb"], p["lg"], p["lb"], p["rw"], p["rb"], p["wr"], p["br"], p["wi"], p["bi"], p["lam"])


def pack_pairs(a):
    half = a.shape[1] // 2
    r = lax.bitcast_convert_type(a, I32)
    r = r + jnp.int32(0x7FFF) + jnp.bitwise_and(lax.shift_right_logical(r, 16), 1)
    lo = lax.shift_right_logical(r[:, :half], 16)
    hi = jnp.bitwise_and(r[:, half:], jnp.int32(-65536))
    return jnp.bitwise_or(hi, lo)


def _outproj_body(x_ref, ya_ref, yc_ref, woa_ref, wob_ref, g2_ref, wq_ref, keys_ref, x1_ref, h2p_ref, st_ref):
    x1 = x_ref[...] + _dot(ya_ref[...], woa_ref[...]) + _dot(yc_ref[...], wob_ref[...])
    x1_ref[...] = x1
    h2 = x1 * lax.rsqrt(jnp.mean(x1 * x1, axis=-1, keepdims=True) + EPS) * g2_ref[...]
    h2p_ref[...] = pack_pairs(h2)
    q = _dot(h2.astype(BF), wq_ref[...]).astype(BF)
    for g in range(2 * PEER_HEADS):
        st_ref[g * N_KEYS:(g + 1) * N_KEYS, :] = _nt(keys_ref[g], q[:, g * D_HALF:(g + 1) * D_HALF])


def out_proj(x, ya, yc, p, g2, row0=0, sub0=0, T=None, tm=256):
    T = ya.shape[0] if T is None else T
    ng = 2 * PEER_HEADS
    blk0 = (row0 + sub0) // tm
    sblk = sub0 // tm
    return pl.pallas_call(
        _outproj_body, grid=(T // tm,),
        in_specs=[pl.BlockSpec((tm, D_MODEL), lambda i: (i + blk0, 0)),
                  pl.BlockSpec((tm, D_ATT), lambda i: (i + sblk, 0)),
                  pl.BlockSpec((tm, D_CONV + D_RNN), lambda i: (i + sblk, 0)),
                  pl.BlockSpec((D_ATT, D_MODEL), lambda i: (0, 0)),
                  pl.BlockSpec((D_CONV + D_RNN, D_MODEL), lambda i: (0, 0)),
                  pl.BlockSpec((1, D_MODEL), lambda i: (0, 0)),
                  pl.BlockSpec((D_MODEL, ng * D_HALF), lambda i: (0, 0)),
                  pl.BlockSpec((ng, N_KEYS, D_HALF), lambda i: (0, 0, 0))],
        out_specs=[pl.BlockSpec((tm, D_MODEL), lambda i: (i, 0)),
                   pl.BlockSpec((tm, D_MODEL // 2), lambda i: (i, 0)),
                   pl.BlockSpec((ng * N_KEYS, tm), lambda i: (0, i))],
        out_shape=[jax.ShapeDtypeStruct((T, D_MODEL), F32),
                   jax.ShapeDtypeStruct((T, D_MODEL // 2), I32),
                   jax.ShapeDtypeStruct((ng * N_KEYS, T), F32)],
        compiler_params=_cp(("parallel",)), name="out_proj_peer_scores",
    )(x, ya, yc, p["woa"], p["wob"], g2, p["wq"], p["keys"])


BIG_ID = 1 << 20
SUBL = 8
NPAR = 4
NPAR1 = 8
SEL_CHAIN = 4


def _take_rounds(problems, nrounds):
    state = [list(slabs) for slabs, _ in problems]
    res = [([], []) for _ in problems]
    for _ in range(nrounds):
        for pi, (_, ids) in enumerate(problems):
            slabs = state[pi]
            m8 = slabs[0]
            for sl in slabs[1:]:
                m8 = jnp.maximum(m8, sl)
            m = jnp.max(m8, axis=0, keepdims=True)
            chains = []
            for c0 in range(0, len(slabs), SEL_CHAIN):
                v = jnp.full((SUBL, LANES), BIG_ID, I32)
                for sl, idc in zip(reversed(slabs[c0:c0 + SEL_CHAIN]), reversed(ids[c0:c0 + SEL_CHAIN])):
                    v = jnp.where(sl == m, idc, v)
                chains.append(v)
            while len(chains) > 1:
                chains = [jnp.minimum(chains[i], chains[i + 1]) if i + 1 < len(chains) else chains[i]
                          for i in range(0, len(chains), 2)]
            pick = jnp.min(chains[0], axis=0, keepdims=True)
            state[pi] = [jnp.where(idc == pick, NEG, sl) for sl, idc in zip(slabs, ids)]
            res[pi][0].append(m)
            res[pi][1].append(pick)
    return res


def _route_body(st_ref, e_ref, g_ref, v_scr, i_scr, sv_scr, ci_scr, et_scr, gt_scr):
    ng = 2 * PEER_HEADS
    sub = lax.broadcasted_iota(I32, (SUBL, LANES), 0)
    key_ids = [sub + SUBL * i for i in range(N_KEYS // SUBL)]

    def stage1(gg, carry):
        probs = []
        for q in range(NPAR1):
            base = pl.multiple_of((gg * NPAR1 + q) * N_KEYS, N_KEYS)
            probs.append(([st_ref[pl.ds(base + SUBL * i, SUBL), :] for i in range(N_KEYS // SUBL)], key_ids))
        for q, (vals, picks) in enumerate(_take_rounds(probs, TOPK)):
            for r in range(TOPK):
                v_scr[gg * NPAR1 + q, r:r + 1, :] = vals[r]
                i_scr[gg * NPAR1 + q, r:r + 1, :] = picks[r]
        return carry

    lax.fori_loop(0, ng // NPAR1, stage1, 0)

    def stage2(hh, carry):
        probs = []
        for q in range(NPAR):
            h = hh * NPAR + q
            v1 = v_scr[2 * h]
            v2 = v_scr[2 * h + 1]
            slabs = [v1[0:1, :] + v2[0:SUBL, :], v1[0:1, :] + v2[SUBL:TOPK, :]]
            ids = [sub, sub + SUBL]
            for i in range(1, TOPK):
                nj = TOPK // (i + 1)
                slabs.append(jnp.where(sub < nj, v1[i:i + 1, :] + v2[0:SUBL, :], NEG))
                ids.append(sub + i * TOPK)
            probs.append((slabs, ids))
        for q, (vals, picks) in enumerate(_take_rounds(probs, TOPK)):
            h = hh * NPAR + q
            i1 = i_scr[2 * h]
            i2 = i_scr[2 * h + 1]
            for r in range(TOPK):
                sv_scr[q, r:r + 1, :] = vals[r]
                ci_scr[q, r:r + 1, :] = picks[r]
            sv = sv_scr[q]
            ci = ci_scr[q]
            ci_hi = lax.shift_right_logical(ci, 4)
            ci_lo = jnp.bitwise_and(ci, TOPK - 1)
            e1 = jnp.zeros((TOPK, LANES), I32)
            e2 = jnp.zeros((TOPK, LANES), I32)
            for i in range(TOPK):
                e1 = jnp.where(ci_hi == i, i1[i:i + 1, :], e1)
                e2 = jnp.where(ci_lo == i, i2[i:i + 1, :], e2)
            p = jnp.exp(sv - sv[0:1, :])
            gates = p / jnp.sum(p, axis=0, keepdims=True)
            et_scr[pl.ds(pl.multiple_of(h * TOPK, TOPK), TOPK), :] = e1 * N_KEYS + e2
            gt_scr[pl.ds(pl.multiple_of(h * TOPK, TOPK), TOPK), :] = gates
        return carry

    lax.fori_loop(0, PEER_HEADS // NPAR, stage2, 0)
    e_ref[...] = et_scr[...].T
    g_ref[...] = gt_scr[...].T


def route(st):
    T = st.shape[1]
    ng = 2 * PEER_HEADS
    return pl.pallas_call(
        _route_body, grid=(T // LANES,),
        in_specs=[pl.BlockSpec((ng * N_KEYS, LANES), lambda i: (0, i))],
        out_specs=[pl.BlockSpec((LANES, KSEL), lambda i: (i, 0)),
                   pl.BlockSpec((LANES, KSEL), lambda i: (i, 0))],
        out_shape=[jax.ShapeDtypeStruct((T, KSEL), I32),
                   jax.ShapeDtypeStruct((T, KSEL), F32)],
        scratch_shapes=[pltpu.VMEM((ng, TOPK, LANES), F32), pltpu.VMEM((ng, TOPK, LANES), I32),
                        pltpu.VMEM((NPAR, TOPK, LANES), F32), pltpu.VMEM((NPAR, TOPK, LANES), I32),
                        pltpu.VMEM((KSEL, LANES), I32), pltpu.VMEM((KSEL, LANES), F32)],
        compiler_params=_cp(("parallel",)), name="peer_route",
    )(st)


NC, NS, L = 2, 16, 16
NW = NC * NS
NJ = D_MODEL // L
R = TOPK
NCH = KSEL // R
NB = 4
USTEP = 2
VTREE_BF16 = 3
G = 8
NSLOT = 2
DW = D_MODEL // 2


def _perm(x, idx):
    return jnp.take_along_axis(x, idx, axis=0, mode="promise_in_bounds")


def _halves(w):
    lo = lax.bitcast_convert_type(lax.shift_left(w, 16), F32)
    hi = lax.bitcast_convert_type(jnp.bitwise_and(w, jnp.int32(-65536)), F32)
    return lo, hi


def _bf(w):
    return plsc.bitcast(w, BF)


def peer_sc(x, resid, idx, gates, uv_tab):
    T = x.shape[0]
    tpw = T // NW
    ngroups = tpw // G
    nchunks = G * NCH
    idx3 = idx.reshape(T * NCH, R)
    g3 = gates.reshape(T * NCH, R)
    mesh = plsc.VectorSubcoreMesh(core_axis_name="c", subcore_axis_name="s")

    @functools.partial(
        pl.kernel, mesh=mesh,
        out_type=jax.ShapeDtypeStruct((T, D_MODEL), F32),
        scratch_types=[
            pltpu.VMEM((NSLOT, G, DW), I32),
            pltpu.VMEM((NSLOT, G, D_MODEL), F32),
            pltpu.VMEM((NSLOT, nchunks, R), I32),
            pltpu.VMEM((NSLOT, nchunks, R), F32),
            pltpu.SemaphoreType.DMA((NSLOT,)),
            pltpu.SemaphoreType.DMA((NSLOT,)),
            pltpu.SemaphoreType.DMA((NSLOT,)),
        ] + [pltpu.VMEM((R, 2 * DW), I32) for _ in range(NB)]
          + [pltpu.SemaphoreType.DMA for _ in range(NB)],
        compiler_params=pltpu.CompilerParams(needs_layout_passes=False),
        name="peer_experts_sc",
    )
    def k(x_hbm, r_hbm, idx_hbm, g_hbm, uv_hbm, out_hbm, x_v, out_v, idx_v, g_v, st_sem, ix_sem, wb_sem, *ring):
        wid = lax.axis_index("s") * NC + lax.axis_index("c")
        bufs, sems = ring[:NB], ring[NB:]
        iota = lax.iota(I32, L)

        def tok0_of(g):
            return wid * tpw + g * G

        def stage_copies(g, slot):
            t0 = tok0_of(g)
            return (pltpu.make_async_copy(x_hbm.at[pl.ds(t0, G)], x_v.at[slot], st_sem.at[slot]),
                    pltpu.make_async_copy(r_hbm.at[pl.ds(t0, G)], out_v.at[slot], st_sem.at[slot]),
                    pltpu.make_async_copy(g_hbm.at[pl.ds(t0 * NCH, nchunks)], g_v.at[slot], st_sem.at[slot]))

        def idx_copy(g, slot):
            return pltpu.make_async_copy(idx_hbm.at[pl.ds(tok0_of(g) * NCH, nchunks)], idx_v.at[slot], ix_sem.at[slot])

        def wb_copy(g, slot):
            return pltpu.make_async_copy(out_v.at[slot], out_hbm.at[pl.ds(tok0_of(g), G)], wb_sem.at[slot])

        def gather_copy(slot, c, b):
            return pltpu.make_async_copy(uv_hbm.at[idx_v.at[slot, c]], bufs[b], sems[b])

        def compute(slot, c, b):
            ub = vb = bufs[b]
            t = c // NCH

            def ubody(mm, accs):
                xs_ = [_bf(x_v[slot, t, pl.ds((mm * USTEP + q) * L, L)]) for q in range(USTEP)]
                out = []
                for kk in range(R):
                    pr = [xs_[q] * _bf(ub[kk, pl.ds((mm * USTEP + q) * L, L)]) for q in range(USTEP)]
                    while len(pr) > 1:
                        pr = [pr[i] + pr[i + 1] for i in range(0, len(pr), 2)]
                    lo, hi = _halves(plsc.bitcast(pr[0], I32))
                    out.append(accs[kk] + (lo + hi))
                return tuple(out)

            accs = lax.fori_loop(0, NJ // (2 * USTEP), ubody, tuple(jnp.zeros((L,), F32) for _ in range(R)))
            vecs = list(accs)
            dist = L // 2
            while dist >= 1:
                pidx = jnp.bitwise_xor(iota, dist)
                low = jnp.bitwise_and(iota, dist) == 0
                nxt = []
                for kk in range(dist):
                    a = vecs[kk]
                    bvec = vecs[kk + dist]
                    a = a + _perm(a, pidx)
                    bvec = bvec + _perm(bvec, pidx)
                    nxt.append(jnp.where(low, a, bvec))
                vecs = nxt
                dist //= 2
            hid = vecs[0]
            z = GC * (hid + 0.044715 * hid * hid * hid)
            gel = hid / (1.0 + jnp.exp(-2.0 * z))
            w = g_v[slot, c, :] * gel
            wbs = []
            for kk in range(R):
                wb = _perm(w, jnp.full((L,), kk, I32))
                wbs.append(plsc.pack(wb, wb, format=plsc.PackFormat.INTERLEAVED))

            @plsc.parallel_loop(0, NJ // 2)
            def _(m):
                pr = [wbs[kk] * _bf(vb[kk, pl.ds(DW + m * L, L)]) for kk in range(R)]
                for _lvl in range(VTREE_BF16):
                    pr = [pr[i] + pr[i + 1] for i in range(0, len(pr), 2)]
                los, his = [], []
                for q in pr:
                    lo, hi = _halves(plsc.bitcast(q, I32))
                    los.append(lo)
                    his.append(hi)
                while len(los) > 1:
                    los = [los[i] + los[i + 1] for i in range(0, len(los), 2)]
                    his = [his[i] + his[i + 1] for i in range(0, len(his), 2)]
                out_v[slot, t, pl.ds(m * L, L)] = out_v[slot, t, pl.ds(m * L, L)] + los[0]
                out_v[slot, t, pl.ds(DW + m * L, L)] = out_v[slot, t, pl.ds(DW + m * L, L)] + his[0]

        idx_copy(0, 0).start()
        for cp in stage_copies(0, 0):
            cp.start()
        idx_copy(0, 0).wait()
        for b in range(NB - 1):
            gather_copy(0, b, b).start()

        def group(g, carry):
            slot = g % NSLOT
            nslot = 1 - slot
            has_next = g + 1 < ngroups

            @pl.when(g >= 1)
            def _():
                wb_copy(g - 1, nslot).wait()

            @pl.when(has_next)
            def _():
                idx_copy(g + 1, nslot).start()
                for cp in stage_copies(g + 1, nslot):
                    cp.start()

            for cp in stage_copies(g, slot):
                cp.wait()

            def cbody(cc, c2):
                for b in range(NB):
                    c = cc * NB + b
                    cn = c + NB - 1
                    nb = (b + NB - 1) % NB

                    @pl.when(cn < nchunks)
                    def _():
                        gather_copy(slot, cn, nb).start()

                    @pl.when(jnp.logical_and(cn >= nchunks, has_next))
                    def _():
                        @pl.when(cn == nchunks)
                        def _():
                            idx_copy(g + 1, nslot).wait()

                        gather_copy(nslot, cn - nchunks, nb).start()

                    gather_copy(slot, c, b).wait()
                    compute(slot, c, b)
                return c2

            lax.fori_loop(0, nchunks // NB, cbody, 0)
            wb_copy(g, slot).start()
            return carry

        lax.fori_loop(0, ngroups, group, 0)
        wb_copy(ngroups - 1, (ngroups - 1) % NSLOT).wait()

    return k(x, resid, idx3, g3, uv_tab)


def _fn_body(x_ref, g_ref, o_ref):
    xf = x_ref[...]
    o_ref[...] = xf * lax.rsqrt(jnp.mean(xf * xf, axis=-1, keepdims=True) + EPS) * g_ref[...]


def _fn_body_into(x_ref, g_ref, prev_ref, o_ref):
    del prev_ref
    _fn_body(x_ref, g_ref, o_ref)


def final_norm(x, g, out, row0, t_total, tm=1024):
    T, d = x.shape
    blk0 = row0 // tm
    common = dict(grid=(T // tm,), out_specs=pl.BlockSpec((tm, d), lambda i: (i + blk0, 0)),
                  out_shape=jax.ShapeDtypeStruct((t_total, d), F32),
                  compiler_params=_cp(("parallel",)), name="final_norm")
    specs = [pl.BlockSpec((tm, d), lambda i: (i, 0)), pl.BlockSpec((1, d), lambda i: (0, 0))]
    if out is None:
        return pl.pallas_call(_fn_body, in_specs=specs, **common)(x, g)
    return pl.pallas_call(_fn_body_into, in_specs=specs + [pl.BlockSpec(memory_space=pl.ANY)],
                          input_output_aliases={2: 0}, **common)(x, g, out)


def _pack_body(u_ref, v_ref, o_ref):
    o_ref[:, 0:D_MODEL // 2] = pack_pairs(u_ref[...])
    o_ref[:, D_MODEL // 2:D_MODEL] = pack_pairs(v_ref[...])


def pack_tables(u, v, tm=512):
    e, d = u.shape
    return pl.pallas_call(
        _pack_body, grid=(e // tm,),
        in_specs=[pl.BlockSpec((tm, d), lambda i: (i, 0)), pl.BlockSpec((tm, d), lambda i: (i, 0))],
        out_specs=pl.BlockSpec((tm, d), lambda i: (i, 0)),
        out_shape=jax.ShapeDtypeStruct((e, d), I32),
        compiler_params=_cp(("parallel",)), name="pack_tables",
    )(u, v)


def _prep_layer(w_in, b_forget, conv_dw_w, conv_dw_b, conv_ln_g, conv_ln_b, rg_conv_w, rg_conv_b,
                rg_w_r, rg_b_r, rg_w_i, rg_b_i, rg_lambda, w_out, peer_wq, peer_k1, peer_k2):
    f0 = 3 * D_ATT
    wf = jnp.zeros((D_MODEL, FPAD), BF).at[:, 0:ATT_HEADS].set(w_in[:, f0:f0 + ATT_HEADS].astype(BF))
    bfg = jnp.zeros((1, FPAD), F32).at[0, 0:ATT_HEADS].set(b_forget)
    cw = jnp.zeros((32, D_CONV), F32).at[0:CONV_K].set(conv_dw_w)
    rw = jnp.zeros((8, D_RNN), F32).at[0:RNN_CONV_K].set(rg_conv_w)
    bd = lambda w: jax.scipy.linalg.block_diag(*[w[i] for i in range(RNN_BLOCKS)]).astype(BF)
    row = lambda v: v.reshape(1, -1).astype(F32)
    keys = jnp.stack([peer_k1, peer_k2], axis=1).reshape(2 * PEER_HEADS, N_KEYS, D_HALF).astype(BF)
    return dict(wqk=w_in[:, 0:2 * D_ATT].astype(BF), wvt=w_in[:, 2 * D_ATT:f0].T.astype(BF), wf=wf,
                wrest=w_in[:, f0 + ATT_HEADS:].astype(BF), bfg=bfg,
                cw=cw, cb=row(conv_dw_b), lg=row(conv_ln_g), lb=row(conv_ln_b),
                rw=rw, rb=row(rg_conv_b), wr=bd(rg_w_r), br=row(rg_b_r), wi=bd(rg_w_i), bi=row(rg_b_i),
                lam=row(rg_lambda), woa=w_out[0:D_ATT].astype(BF), wob=w_out[D_ATT:].astype(BF),
                wq=peer_wq.astype(BF), keys=keys)


def kernel(x, norm1_g, w_in, b_forget, conv_dw_w, conv_dw_b, conv_ln_g, conv_ln_b,
           rg_conv_w, rg_conv_b, rg_w_r, rg_b_r, rg_w_i, rg_b_i, rg_lambda, w_out,
           norm2_g, peer_wq, peer_k1, peer_k2, peer_u, peer_v, final_g):
    b, s, d = x.shape
    raw = (w_in, b_forget, conv_dw_w, conv_dw_b, conv_ln_g, conv_ln_b, rg_conv_w, rg_conv_b,
           rg_w_r, rg_b_r, rg_w_i, rg_b_i, rg_lambda, w_out, peer_wq, peer_k1, peer_k2)
    params = {0: _prep_layer(*[a[0] for a in raw])}
    tabs = {0: pack_tables(peer_u[0], peer_v[0])}
    bs = b // N_SLICES
    T = bs * s
    xf = x.reshape(b * s, d)
    xs = [None] * N_SLICES
    prev = None
    for l in range(DEPTH):
        for i in range(N_SLICES):
            src, row0 = (xf, i * T) if l == 0 else (xs[i], 0)
            first = l == 0 and i == 0
            late = l == 0 and i == LATE_PREP_SLICE
            deps = ([prev] if prev is not None else []) + ([tabs[0]] if first else [])
            deps += [*raw, peer_u, peer_v] if late else []
            if deps:
                tied = list(lax.optimization_barrier((src, *deps)))
                src = tied.pop(0)
                if prev is not None:
                    prev = tied.pop(0)
                if first:
                    tabs[0] = tied.pop(0)
                if late:
                    for ll in range(1, DEPTH):
                        params[ll] = _prep_layer(*[a[ll] for a in tied[:len(raw)]])
                        tabs[ll] = pack_tables(tied[-2][ll], tied[-1][ll])
            p = params[l]
            qkb, vt, rest = in_proj(src, norm1_g[l].reshape(1, d), p, s, T, row0)
            y_att = attention(qkb, vt, bs, s)
            y_cr = mixers(rest, p, bs, s)
            npiece = PIECES.get((l, i), 1)
            tp = T // npiece
            pieces = []
            for j in range(npiece):
                if j > 0:
                    src, experts = lax.optimization_barrier((src, experts))
                x1, h2p, st = out_proj(src, y_att, y_cr, p, norm2_g[l].reshape(1, d), row0, j * tp, tp)
                experts, gates = route(st)
                pieces.append(peer_sc(h2p, x1, experts, gates, tabs[l]))
            prev = experts
            xs[i] = pieces[0] if npiece == 1 else jnp.concatenate(pieces, axis=0)
    out = None
    for i in range(N_SLICES):
        out = final_norm(xs[i], final_g.reshape(1, d), out, i * T, b * s)
    return out.reshape(b, s, d)
```

```python
import functools
import math

import jax
import jax.numpy as jnp
from jax import lax
from jax.experimental import pallas as pl
from jax.experimental.pallas import tpu as pltpu
from jax.experimental.pallas import tpu_sc as plsc

BF = jnp.bfloat16
F32 = jnp.float32
I32 = jnp.int32

D_MODEL = 1024
DEPTH = 2
ATT_HEADS = 8
ATT_HD = 64
D_ATT = ATT_HEADS * ATT_HD
D_CONV = 256
CONV_K = 31
D_RNN = 256
RNN_BLOCKS = 4
RNN_CONV_K = 4
RG_C = 8.0
EPS = 1e-6
N_REST = 2 * D_CONV + 2 * D_RNN
PEER_HEADS = 8
N_KEYS = 128
D_HALF = 128
TOPK = 16
KSEL = PEER_HEADS * TOPK
GC = 0.7978845608028654
NEG = float("-inf")

PIECES = {(0, 0): 4, (0, 1): 2}
LATE_PREP_SLICE = 2
N_SLICES = 4
LANES = 128
VMEM_LIMIT = 48 * 1024 * 1024


def _cp(sem):
    return pltpu.CompilerParams(dimension_semantics=sem, vmem_limit_bytes=VMEM_LIMIT)


def _split3(x):
    hi = x.astype(BF)
    r = x - hi.astype(F32)
    mid = r.astype(BF)
    lo = (r - mid.astype(F32)).astype(BF)
    return hi, mid, lo


def _nt(a, b):
    return lax.dot_general(a, b, (((1,), (1,)), ((), ())), preferred_element_type=F32)


def _dot(a, b):
    return jnp.dot(a, b, preferred_element_type=F32)


def _sigmoid(x):
    return 1.0 / (1.0 + jnp.exp(-x))


def _gelu(x):
    return 0.5 * x * (1.0 + jnp.tanh(GC * (x + 0.044715 * x * x * x)))


LOG2E = 1.4426950408889634
NSPLIT = 3
FPAD = 16


def _inproj_body(x_ref, g_ref, wqk_ref, wvt_ref, wf_ref, wrest_ref, bf_ref, tri_ref, place_ref,
                 qkb_ref, vt_ref, rest_ref, carry_ref, *, blocks_per_seq, tm):
    i = pl.program_id(0)
    x = x_ref[...]
    h = x * lax.rsqrt(jnp.mean(x * x, axis=-1, keepdims=True) + EPS) * g_ref[...]
    hb = h.astype(BF)
    qk = _dot(hb, wqk_ref[...])
    col = lax.broadcasted_iota(I32, (1, 2 * D_ATT), 1)
    qk = jnp.where(col < D_ATT, qk * (LOG2E / math.sqrt(ATT_HD)), qk)
    qkb_ref[:, 0:2 * D_ATT] = qk.astype(BF)
    vt_ref[...] = _nt(wvt_ref[...], hb).astype(BF)
    rest_ref[...] = _dot(hb, wrest_ref[...])
    ft = _dot(hb, wf_ref[...]) + bf_ref[...]
    lf = jnp.minimum(ft, 0.0) - jnp.log(1.0 + jnp.exp(-jnp.abs(ft)))
    hi, mid, lo = _split3(lf)
    tri = tri_ref[...]
    cs = _dot(tri, hi) + _dot(tri, mid) + _dot(tri, lo)

    @pl.when(i % blocks_per_seq == 0)
    def _():
        carry_ref[...] = jnp.zeros_like(carry_ref)

    cum = cs + carry_ref[...]
    carry_ref[...] = cum[tm - 1:tm, :]
    pieces = _split3(cum * (-LOG2E))
    kb = _dot(pieces[0], place_ref[0]) + _dot(pieces[1], place_ref[1]) + _dot(pieces[2], place_ref[2])
    qkb_ref[:, 2 * D_ATT:3 * D_ATT] = kb.astype(BF)


def bias_lane(hh, j):
    return (ATT_HD if hh == 0 else 0) + j


def in_proj(x, g, p, seq, T, row0=0, tm=512):
    blk0 = row0 // tm
    tri =(lax.broadcasted_iota(I32, (tm, tm), 0) >= lax.broadcasted_iota(I32, (tm, tm), 1)).astype(BF)
    shp = (NSPLIT, FPAD, D_ATT)
    hd = lax.broadcasted_iota(I32, shp, 1)
    target = (hd // 2) * LANES + jnp.where(hd % 2 == 0, ATT_HD, 0) + lax.broadcasted_iota(I32, shp, 0)
    place = ((lax.broadcasted_iota(I32, shp, 2) == target) & (hd < ATT_HEADS)).astype(BF)
    body = functools.partial(_inproj_body, blocks_per_seq=seq // tm, tm=tm)
    return pl.pallas_call(
        body, grid=(T // tm,),
        in_specs=[pl.BlockSpec((tm, D_MODEL), lambda i: (i + blk0, 0)),
                  pl.BlockSpec((1, D_MODEL), lambda i: (0, 0)),
                  pl.BlockSpec((D_MODEL, 2 * D_ATT), lambda i: (0, 0)),
                  pl.BlockSpec((D_ATT, D_MODEL), lambda i: (0, 0)),
                  pl.BlockSpec((D_MODEL, FPAD), lambda i: (0, 0)),
                  pl.BlockSpec((D_MODEL, N_REST), lambda i: (0, 0)),
                  pl.BlockSpec((1, FPAD), lambda i: (0, 0)),
                  pl.BlockSpec((tm, tm), lambda i: (0, 0)),
                  pl.BlockSpec(shp, lambda i: (0, 0, 0))],
        out_specs=[pl.BlockSpec((tm, 3 * D_ATT), lambda i: (i, 0)),
                   pl.BlockSpec((D_ATT, tm), lambda i: (0, i)),
                   pl.BlockSpec((tm, N_REST), lambda i: (i, 0))],
        out_shape=[jax.ShapeDtypeStruct((T, 3 * D_ATT), BF),
                   jax.ShapeDtypeStruct((D_ATT, T), BF),
                   jax.ShapeDtypeStruct((T, N_REST), F32)],
        scratch_shapes=[pltpu.VMEM((1, FPAD), F32)],
        compiler_params=_cp(("arbitrary",)), name="in_proj",
    )(x, g, p["wqk"], p["wvt"], p["wf"], p["wrest"], p["bfg"], tri, place)


def _attn_body(q_ref, k_ref, kb_ref, vt_ref, o_ref, m_ref, acc_ref, *, tq, tk):
    qi = pl.program_id(2)
    ki = pl.program_id(3)

    @pl.when(ki == 0)
    def _():
        m_ref[...] = jnp.full_like(m_ref, NEG)
        acc_ref[...] = jnp.zeros_like(acc_ref)

    lane = lax.broadcasted_iota(I32, (1, LANES), 1)
    first = lane < ATT_HD
    vrow = lax.broadcasted_iota(I32, (LANES, 1), 0) < ATT_HD

    def step(masked):
        q = q_ref[...]
        k = k_ref[...]
        kb = kb_ref[...]
        vt = vt_ref[...]
        if masked:
            keep = (lax.broadcasted_iota(I32, (tk, tq), 0) <= lax.broadcasted_iota(I32, (tk, tq), 1))
        for hh in range(2):
            own = first if hh == 0 else jnp.logical_not(first)
            ones = (lane >= bias_lane(hh, 0)) & (lane < bias_lane(hh, NSPLIT))
            qa = jnp.where(own, q, jnp.where(ones, 1.0, 0.0).astype(BF))
            ka = jnp.where(own, k, kb)
            st = _nt(ka, qa)
            if masked:
                st = jnp.where(keep, st, NEG)
            m_prev = m_ref[hh]
            m_new = jnp.maximum(m_prev, jnp.max(st, axis=0, keepdims=True))
            alpha = jnp.exp2(m_prev - m_new)
            p = jnp.exp2(st - m_new).astype(BF)
            m_ref[hh] = m_new
            vown = vrow if hh == 0 else jnp.logical_not(vrow)
            va = jnp.where(vown, vt, jnp.ones_like(vt))
            acc_ref[hh] = alpha * acc_ref[hh] + _dot(va, p)

    @pl.when(ki < qi)
    def _():
        step(False)

    @pl.when(ki == qi)
    def _():
        step(True)
        a0 = acc_ref[0]
        a1 = acc_ref[1]
        ot = jnp.where(vrow, a0 / a0[ATT_HD:ATT_HD + 1, :], a1 / a1[0:1, :])
        o_ref[...] = ot.T.astype(o_ref.dtype)


def attention(qkb, vt, batch, seq, tq=512):
    T = qkb.shape[0]
    tk = tq
    nq = seq // tq
    npair = ATT_HEADS // 2
    body = functools.partial(_attn_body, tq=tq, tk=tk)
    kblk = lambda b, qi, ki: b * nq + jnp.minimum(ki, qi)
    return pl.pallas_call(
        body, grid=(batch, npair, nq, nq),
        in_specs=[pl.BlockSpec((tq, LANES), lambda b, p, qi, ki: (b * nq + qi, p)),
                  pl.BlockSpec((tk, LANES), lambda b, p, qi, ki: (kblk(b, qi, ki), npair + p)),
                  pl.BlockSpec((tk, LANES), lambda b, p, qi, ki: (kblk(b, qi, ki), 2 * npair + p)),
                  pl.BlockSpec((LANES, tk), lambda b, p, qi, ki: (p, kblk(b, qi, ki)))],
        out_specs=pl.BlockSpec((tq, LANES), lambda b, p, qi, ki: (b * nq + qi, p)),
        out_shape=jax.ShapeDtypeStruct((T, D_ATT), BF),
        scratch_shapes=[pltpu.VMEM((2, 1, tq), F32), pltpu.VMEM((2, LANES, tq), F32)],
        compiler_params=_cp(("parallel", "parallel", "parallel", "arbitrary")), name="fox_attention",
    )(qkb, qkb, qkb, vt)


CONV_HALO = 32
RG_HALO = 8


def _mix_body(rest_ref, cw_ref, cb_ref, lg_ref, lb_ref, rw_ref, rb_ref, wr_ref, br_ref, wi_ref, bi_ref, lam_ref,
              o_ref, ybuf, xbuf, hc, *, ts):
    si = pl.program_id(1)

    @pl.when(si == 0)
    def _():
        ybuf[0:CONV_HALO, :] = jnp.zeros((CONV_HALO, D_CONV), F32)
        xbuf[0:RG_HALO, :] = jnp.zeros((RG_HALO, D_RNN), F32)
        hc[...] = jnp.zeros_like(hc)

    y = rest_ref[:, 0:D_CONV] * _sigmoid(rest_ref[:, D_CONV:2 * D_CONV])
    ybuf[CONV_HALO:CONV_HALO + ts, :] = y
    acc = jnp.zeros((ts, D_CONV), F32)
    for k in range(CONV_K):
        acc = acc + cw_ref[k:k + 1, :] * ybuf[pl.ds(CONV_HALO - (CONV_K - 1) + k, ts), :]
    yc = acc + cb_ref[...]
    mu = jnp.mean(yc, axis=-1, keepdims=True)
    var = jnp.mean(jnp.square(yc - mu), axis=-1, keepdims=True)
    yn = (yc - mu) * lax.rsqrt(var + EPS) * lg_ref[...] + lb_ref[...]
    o_ref[:, 0:D_CONV] = (yn * _sigmoid(yn)).astype(o_ref.dtype)
    ybuf[0:CONV_HALO, :] = ybuf[ts:ts + CONV_HALO, :]

    xbuf[RG_HALO:RG_HALO + ts, :] = rest_ref[:, 2 * D_CONV:2 * D_CONV + D_RNN]
    xc = jnp.zeros((ts, D_RNN), F32)
    for k in range(RNN_CONV_K):
        xc = xc + rw_ref[k:k + 1, :] * xbuf[pl.ds(RG_HALO - (RNN_CONV_K - 1) + k, ts), :]
    xc = xc + rb_ref[...]
    xbuf[0:RG_HALO, :] = xbuf[ts:ts + RG_HALO, :]
    xcb = xc.astype(BF)
    r = _sigmoid(_dot(xcb, wr_ref[...]) + br_ref[...])
    gi = _sigmoid(_dot(xcb, wi_ref[...]) + bi_ref[...])
    nl = -lam_ref[...]
    sp = jnp.maximum(nl, 0.0) + jnp.log(1.0 + jnp.exp(-jnp.abs(nl)))
    log_a = -RG_C * r * sp
    a = jnp.exp(log_a)
    bt = jnp.sqrt(1.0 - jnp.exp(2.0 * log_a)) * (gi * xc)
    row = lax.broadcasted_iota(I32, (ts, 1), 0)
    sh = 1
    while sh < ts:
        live = row >= sh
        a_s = jnp.where(live, pltpu.roll(a, sh, 0), 1.0)
        b_s = jnp.where(live, pltpu.roll(bt, sh, 0), 0.0)
        bt = bt + a * b_s
        a = a * a_s
        sh *= 2
    h = bt + a * hc[...]
    hc[...] = h[ts - 1:ts, :]
    gate_in = rest_ref[:, 2 * D_CONV + D_RNN:2 * D_CONV + 2 * D_RNN]
    o_ref[:, D_CONV:D_CONV + D_RNN] = (h * _gelu(gate_in)).astype(o_ref.dtype)


def mixers(rest, p, batch, seq, ts=512):
    T = rest.shape[0]
    ns = seq // ts
    body = functools.partial(_mix_body, ts=ts)
    vec = lambda: pl.BlockSpec((1, D_CONV), lambda b, s: (0, 0))
    return pl.pallas_call(
        body, grid=(batch, ns),
        in_specs=[pl.BlockSpec((ts, N_REST), lambda b, s: (b * ns + s, 0)),
                  pl.BlockSpec((32, D_CONV), lambda b, s: (0, 0)), vec(), vec(), vec(),
                  pl.BlockSpec((8, D_RNN), lambda b, s: (0, 0)), vec(),
                  pl.BlockSpec((D_RNN, D_RNN), lambda b, s: (0, 0)), vec(),
                  pl.BlockSpec((D_RNN, D_RNN), lambda b, s: (0, 0)), vec(), vec()],
        out_specs=pl.BlockSpec((ts, D_CONV + D_RNN), lambda b, s: (b * ns + s, 0)),
        out_shape=jax.ShapeDtypeStruct((T, D_CONV + D_RNN), BF),
        scratch_shapes=[pltpu.VMEM((ts + CONV_HALO, D_CONV), F32), pltpu.VMEM((ts + RG_HALO, D_RNN), F32),
                        pltpu.VMEM((1, D_RNN), F32)],
        compiler_params=_cp(("arbitrary", "arbitrary")), name="conv_rglru",
    )(rest, p["cw"], p["cb"], p["lg"], p["lb"], p["rw"], p["rb"], p["wr"], p["br"], p["wi"], p["bi"], p["lam"])


def pack_pairs(a):
    half = a.shape[1] // 2
    r = lax.bitcast_convert_type(a, I32)
    r = r + jnp.int32(0x7FFF) + jnp.bitwise_and(lax.shift_right_logical(r, 16), 1)
    lo = lax.shift_right_logical(r[:, :half], 16)
    hi = jnp.bitwise_and(r[:, half:], jnp.int32(-65536))
    return jnp.bitwise_or(hi, lo)


def _outproj_body(x_ref, ya_ref, yc_ref, woa_ref, wob_ref, g2_ref, wq_ref, keys_ref, x1_ref, h2p_ref, st_ref):
    x1 = x_ref[...] + _dot(ya_ref[...], woa_ref[...]) + _dot(yc_ref[...], wob_ref[...])
    x1_ref[...] = x1
    h2 = x1 * lax.rsqrt(jnp.mean(x1 * x1, axis=-1, keepdims=True) + EPS) * g2_ref[...]
    h2p_ref[...] = pack_pairs(h2)
    q = _dot(h2.astype(BF), wq_ref[...]).astype(BF)
    for g in range(2 * PEER_HEADS):
        st_ref[g * N_KEYS:(g + 1) * N_KEYS, :] = _nt(keys_ref[g], q[:, g * D_HALF:(g + 1) * D_HALF])


def out_proj(x, ya, yc, p, g2, row0=0, sub0=0, T=None, tm=256):
    T = ya.shape[0] if T is None else T
    ng = 2 * PEER_HEADS
    blk0 = (row0 + sub0) // tm
    sblk = sub0 // tm
    return pl.pallas_call(
        _outproj_body, grid=(T // tm,),
        in_specs=[pl.BlockSpec((tm, D_MODEL), lambda i: (i + blk0, 0)),
                  pl.BlockSpec((tm, D_ATT), lambda i: (i + sblk, 0)),
                  pl.BlockSpec((tm, D_CONV + D_RNN), lambda i: (i + sblk, 0)),
                  pl.BlockSpec((D_ATT, D_MODEL), lambda i: (0, 0)),
                  pl.BlockSpec((D_CONV + D_RNN, D_MODEL), lambda i: (0, 0)),
                  pl.BlockSpec((1, D_MODEL), lambda i: (0, 0)),
                  pl.BlockSpec((D_MODEL, ng * D_HALF), lambda i: (0, 0)),
                  pl.BlockSpec((ng, N_KEYS, D_HALF), lambda i: (0, 0, 0))],
        out_specs=[pl.BlockSpec((tm, D_MODEL), lambda i: (i, 0)),
                   pl.BlockSpec((tm, D_MODEL // 2), lambda i: (i, 0)),
                   pl.BlockSpec((ng * N_KEYS, tm), lambda i: (0, i))],
        out_shape=[jax.ShapeDtypeStruct((T, D_MODEL), F32),
                   jax.ShapeDtypeStruct((T, D_MODEL // 2), I32),
                   jax.ShapeDtypeStruct((ng * N_KEYS, T), F32)],
        compiler_params=_cp(("parallel",)), name="out_proj_peer_scores",
    )(x, ya, yc, p["woa"], p["wob"], g2, p["wq"], p["keys"])


BIG_ID = 1 << 20
SUBL = 8
NPAR = 4
NPAR1 = 8
SEL_CHAIN = 4


def _take_rounds(problems, nrounds):
    state = [list(slabs) for slabs, _ in problems]
    res = [([], []) for _ in problems]
    for _ in range(nrounds):
        for pi, (_, ids) in enumerate(problems):
            slabs = state[pi]
            m8 = slabs[0]
            for sl in slabs[1:]:
                m8 = jnp.maximum(m8, sl)
            m = jnp.max(m8, axis=0, keepdims=True)
            chains = []
            for c0 in range(0, len(slabs), SEL_CHAIN):
                v = jnp.full((SUBL, LANES), BIG_ID, I32)
                for sl, idc in zip(reversed(slabs[c0:c0 + SEL_CHAIN]), reversed(ids[c0:c0 + SEL_CHAIN])):
                    v = jnp.where(sl == m, idc, v)
                chains.append(v)
            while len(chains) > 1:
                chains = [jnp.minimum(chains[i], chains[i + 1]) if i + 1 < len(chains) else chains[i]
                          for i in range(0, len(chains), 2)]
            pick = jnp.min(chains[0], axis=0, keepdims=True)
            state[pi] = [jnp.where(idc == pick, NEG, sl) for sl, idc in zip(slabs, ids)]
            res[pi][0].append(m)
            res[pi][1].append(pick)
    return res


def _route_body(st_ref, e_ref, g_ref, v_scr, i_scr, sv_scr, ci_scr, et_scr, gt_scr):
    ng = 2 * PEER_HEADS
    sub = lax.broadcasted_iota(I32, (SUBL, LANES), 0)
    key_ids = [sub + SUBL * i for i in range(N_KEYS // SUBL)]

    def stage1(gg, carry):
        probs = []
        for q in range(NPAR1):
            base = pl.multiple_of((gg * NPAR1 + q) * N_KEYS, N_KEYS)
            probs.append(([st_ref[pl.ds(base + SUBL * i, SUBL), :] for i in range(N_KEYS // SUBL)], key_ids))
        for q, (vals, picks) in enumerate(_take_rounds(probs, TOPK)):
            for r in range(TOPK):
                v_scr[gg * NPAR1 + q, r:r + 1, :] = vals[r]
                i_scr[gg * NPAR1 + q, r:r + 1, :] = picks[r]
        return carry

    lax.fori_loop(0, ng // NPAR1, stage1, 0)

    def stage2(hh, carry):
        probs = []
        for q in range(NPAR):
            h = hh * NPAR + q
            v1 = v_scr[2 * h]
            v2 = v_scr[2 * h + 1]
            slabs = [v1[0:1, :] + v2[0:SUBL, :], v1[0:1, :] + v2[SUBL:TOPK, :]]
            ids = [sub, sub + SUBL]
            for i in range(1, TOPK):
                nj = TOPK // (i + 1)
                slabs.append(jnp.where(sub < nj, v1[i:i + 1, :] + v2[0:SUBL, :], NEG))
                ids.append(sub + i * TOPK)
            probs.append((slabs, ids))
        for q, (vals, picks) in enumerate(_take_rounds(probs, TOPK)):
            h = hh * NPAR + q
            i1 = i_scr[2 * h]
            i2 = i_scr[2 * h + 1]
            for r in range(TOPK):
                sv_scr[q, r:r + 1, :] = vals[r]
                ci_scr[q, r:r + 1, :] = picks[r]
            sv = sv_scr[q]
            ci = ci_scr[q]
            ci_hi = lax.shift_right_logical(ci, 4)
            ci_lo = jnp.bitwise_and(ci, TOPK - 1)
            e1 = jnp.zeros((TOPK, LANES), I32)
            e2 = jnp.zeros((TOPK, LANES), I32)
            for i in range(TOPK):
                e1 = jnp.where(ci_hi == i, i1[i:i + 1, :], e1)
                e2 = jnp.where(ci_lo == i, i2[i:i + 1, :], e2)
            p = jnp.exp(sv - sv[0:1, :])
            gates = p / jnp.sum(p, axis=0, keepdims=True)
            et_scr[pl.ds(pl.multiple_of(h * TOPK, TOPK), TOPK), :] = e1 * N_KEYS + e2
            gt_scr[pl.ds(pl.multiple_of(h * TOPK, TOPK), TOPK), :] = gates
        return carry

    lax.fori_loop(0, PEER_HEADS // NPAR, stage2, 0)
    e_ref[...] = et_scr[...].T
    g_ref[...] = gt_scr[...].T


def route(st):
    T = st.shape[1]
    ng = 2 * PEER_HEADS
    return pl.pallas_call(
        _route_body, grid=(T // LANES,),
        in_specs=[pl.BlockSpec((ng * N_KEYS, LANES), lambda i: (0, i))],
        out_specs=[pl.BlockSpec((LANES, KSEL), lambda i: (i, 0)),
                   pl.BlockSpec((LANES, KSEL), lambda i: (i, 0))],
        out_shape=[jax.ShapeDtypeStruct((T, KSEL), I32),
                   jax.ShapeDtypeStruct((T, KSEL), F32)],
        scratch_shapes=[pltpu.VMEM((ng, TOPK, LANES), F32), pltpu.VMEM((ng, TOPK, LANES), I32),
                        pltpu.VMEM((NPAR, TOPK, LANES), F32), pltpu.VMEM((NPAR, TOPK, LANES), I32),
                        pltpu.VMEM((KSEL, LANES), I32), pltpu.VMEM((KSEL, LANES), F32)],
        compiler_params=_cp(("parallel",)), name="peer_route",
    )(st)


NC, NS, L = 2, 16, 16
NW = NC * NS
NJ = D_MODEL // L
R = TOPK
NCH = KSEL // R
NB = 6
USTEP = 2
VTREE_BF16 = 2
G = 4
NSLOT = 2
DW = D_MODEL // 2


def _perm(x, idx):
    return jnp.take_along_axis(x, idx, axis=0, mode="promise_in_bounds")


def _halves(w):
    lo = lax.bitcast_convert_type(lax.shift_left(w, 16), F32)
    hi = lax.bitcast_convert_type(jnp.bitwise_and(w, jnp.int32(-65536)), F32)
    return lo, hi


def _bf(w):
    return plsc.bitcast(w, BF)


def peer_sc(x, resid, idx, gates, uv_tab):
    T = x.shape[0]
    tpw = T // NW
    ngroups = tpw // G
    nchunks = G * NCH
    idx3 = idx.reshape(T * NCH, R)
    g3 = gates.reshape(T * NCH, R)
    mesh = plsc.VectorSubcoreMesh(core_axis_name="c", subcore_axis_name="s")

    @functools.partial(
        pl.kernel, mesh=mesh,
        out_type=jax.ShapeDtypeStruct((T, D_MODEL), F32),
        scratch_types=[
            pltpu.VMEM((NSLOT, G, DW), I32),
            pltpu.VMEM((NSLOT, G, D_MODEL), F32),
            pltpu.VMEM((NSLOT, nchunks, R), I32),
            pltpu.VMEM((NSLOT, nchunks, R), F32),
            pltpu.SemaphoreType.DMA((NSLOT,)),
            pltpu.SemaphoreType.DMA((NSLOT,)),
            pltpu.SemaphoreType.DMA((NSLOT,)),
            pltpu.VMEM((NB, R, 2 * DW), I32),
            pltpu.SemaphoreType.DMA((NB,)),
        ],
        compiler_params=pltpu.CompilerParams(needs_layout_passes=False),
        name="peer_experts_sc",
    )
    def k(x_hbm, r_hbm, idx_hbm, g_hbm, uv_hbm, out_hbm, x_v, out_v, idx_v, g_v, st_sem, ix_sem, wb_sem, ring, ring_sem):
        wid = lax.axis_index("s") * NC + lax.axis_index("c")
        iota = lax.iota(I32, L)

        def tok0_of(g):
            return wid * tpw + g * G

        def stage_copies(g, slot):
            t0 = tok0_of(g)
            return (pltpu.make_async_copy(x_hbm.at[pl.ds(t0, G)], x_v.at[slot], st_sem.at[slot]),
                    pltpu.make_async_copy(r_hbm.at[pl.ds(t0, G)], out_v.at[slot], st_sem.at[slot]),
                    pltpu.make_async_copy(g_hbm.at[pl.ds(t0 * NCH, nchunks)], g_v.at[slot], st_sem.at[slot]))

        def idx_copy(g, slot):
            return pltpu.make_async_copy(idx_hbm.at[pl.ds(tok0_of(g) * NCH, nchunks)], idx_v.at[slot], ix_sem.at[slot])

        def wb_copy(g, slot):
            return pltpu.make_async_copy(out_v.at[slot], out_hbm.at[pl.ds(tok0_of(g), G)], wb_sem.at[slot])

        def gather_copy(slot, c, b):
            return pltpu.make_async_copy(uv_hbm.at[idx_v.at[slot, c]], ring.at[b], ring_sem.at[b])

        def compute(slot, c, b):
            t = c // NCH

            def ubody(mm, accs):
                xs_ = [_bf(x_v[slot, t, pl.ds((mm * USTEP + q) * L, L)]) for q in range(USTEP)]
                out = []
                for kk in range(R):
                    pr = [xs_[q] * _bf(ring[b, kk, pl.ds((mm * USTEP + q) * L, L)]) for q in range(USTEP)]
                    while len(pr) > 1:
                        pr = [pr[i] + pr[i + 1] for i in range(0, len(pr), 2)]
                    lo, hi = _halves(plsc.bitcast(pr[0], I32))
                    out.append(accs[kk] + (lo + hi))
                return tuple(out)

            accs = lax.fori_loop(0, NJ // (2 * USTEP), ubody, tuple(jnp.zeros((L,), F32) for _ in range(R)))
            vecs = list(accs)
            dist = L // 2
            while dist >= 1:
                pidx = jnp.bitwise_xor(iota, dist)
                low = jnp.bitwise_and(iota, dist) == 0
                nxt = []
                for kk in range(dist):
                    a = vecs[kk]
                    bvec = vecs[kk + dist]
                    a = a + _perm(a, pidx)
                    bvec = bvec + _perm(bvec, pidx)
                    nxt.append(jnp.where(low, a, bvec))
                vecs = nxt
                dist //= 2
            hid = vecs[0]
            z = GC * (hid + 0.044715 * hid * hid * hid)
            gel = hid / (1.0 + jnp.exp(-2.0 * z))
            w = g_v[slot, c, :] * gel
            wbs = []
            for kk in range(R):
                wb = _perm(w, jnp.full((L,), kk, I32))
                wbs.append(plsc.pack(wb, wb, format=plsc.PackFormat.INTERLEAVED))

            @plsc.parallel_loop(0, NJ // 2)
            def _(m):
                pr = [wbs[kk] * _bf(ring[b, kk, pl.ds(DW + m * L, L)]) for kk in range(R)]
                for _lvl in range(VTREE_BF16):
                    pr = [pr[i] + pr[i + 1] for i in range(0, len(pr), 2)]
                los, his = [], []
                for q in pr:
                    lo, hi = _halves(plsc.bitcast(q, I32))
                    los.append(lo)
                    his.append(hi)
                while len(los) > 1:
                    los = [los[i] + los[i + 1] for i in range(0, len(los), 2)]
                    his = [his[i] + his[i + 1] for i in range(0, len(his), 2)]
                out_v[slot, t, pl.ds(m * L, L)] = out_v[slot, t, pl.ds(m * L, L)] + los[0]
                out_v[slot, t, pl.ds(DW + m * L, L)] = out_v[slot, t, pl.ds(DW + m * L, L)] + his[0]

        idx_copy(0, 0).start()
        for cp in stage_copies(0, 0):
            cp.start()
        idx_copy(0, 0).wait()
        for b in range(NB - 1):
            gather_copy(0, b, b).start()

        def group(g, carry):
            slot = g % NSLOT
            nslot = 1 - slot
            has_next = g + 1 < ngroups

            @pl.when(g >= 1)
            def _():
                wb_copy(g - 1, nslot).wait()

            @pl.when(has_next)
            def _():
                idx_copy(g + 1, nslot).start()
                for cp in stage_copies(g + 1, nslot):
                    cp.start()

            for cp in stage_copies(g, slot):
                cp.wait()

            def cbody(c, c2):
                gc = g * nchunks + c
                b = gc % NB
                cn = c + NB - 1
                nb = (gc + NB - 1) % NB

                @pl.when(cn < nchunks)
                def _():
                    gather_copy(slot, cn, nb).start()

                @pl.when(jnp.logical_and(cn >= nchunks, has_next))
                def _():
                    @pl.when(cn == nchunks)
                    def _():
                        idx_copy(g + 1, nslot).wait()

                    gather_copy(nslot, cn - nchunks, nb).start()

                gather_copy(slot, c, b).wait()
                compute(slot, c, b)
                return c2

            lax.fori_loop(0, nchunks, cbody, 0)
            wb_copy(g, slot).start()
            return carry

        lax.fori_loop(0, ngroups, group, 0)
        wb_copy(ngroups - 1, (ngroups - 1) % NSLOT).wait()

    return k(x, resid, idx3, g3, uv_tab)


def _fn_body(x_ref, g_ref, o_ref):
    xf = x_ref[...]
    o_ref[...] = xf * lax.rsqrt(jnp.mean(xf * xf, axis=-1, keepdims=True) + EPS) * g_ref[...]


def _fn_body_into(x_ref, g_ref, prev_ref, o_ref):
    del prev_ref
    _fn_body(x_ref, g_ref, o_ref)


def final_norm(x, g, out, row0, t_total, tm=1024):
    T, d = x.shape
    blk0 = row0 // tm
    common = dict(grid=(T // tm,), out_specs=pl.BlockSpec((tm, d), lambda i: (i + blk0, 0)),
                  out_shape=jax.ShapeDtypeStruct((t_total, d), F32),
                  compiler_params=_cp(("parallel",)), name="final_norm")
    specs = [pl.BlockSpec((tm, d), lambda i: (i, 0)), pl.BlockSpec((1, d), lambda i: (0, 0))]
    if out is None:
        return pl.pallas_call(_fn_body, in_specs=specs, **common)(x, g)
    return pl.pallas_call(_fn_body_into, in_specs=specs + [pl.BlockSpec(memory_space=pl.ANY)],
                          input_output_aliases={2: 0}, **common)(x, g, out)


def _pack_body(u_ref, v_ref, o_ref):
    o_ref[:, 0:D_MODEL // 2] = pack_pairs(u_ref[...])
    o_ref[:, D_MODEL // 2:D_MODEL] = pack_pairs(v_ref[...])


def pack_tables(u, v, tm=512):
    e, d = u.shape
    return pl.pallas_call(
        _pack_body, grid=(e // tm,),
        in_specs=[pl.BlockSpec((tm, d), lambda i: (i, 0)), pl.BlockSpec((tm, d), lambda i: (i, 0))],
        out_specs=pl.BlockSpec((tm, d), lambda i: (i, 0)),
        out_shape=jax.ShapeDtypeStruct((e, d), I32),
        compiler_params=_cp(("parallel",)), name="pack_tables",
    )(u, v)


def _prep_layer(w_in, b_forget, conv_dw_w, conv_dw_b, conv_ln_g, conv_ln_b, rg_conv_w, rg_conv_b,
                rg_w_r, rg_b_r, rg_w_i, rg_b_i, rg_lambda, w_out, peer_wq, peer_k1, peer_k2):
    f0 = 3 * D_ATT
    wf = jnp.zeros((D_MODEL, FPAD), BF).at[:, 0:ATT_HEADS].set(w_in[:, f0:f0 + ATT_HEADS].astype(BF))
    bfg = jnp.zeros((1, FPAD), F32).at[0, 0:ATT_HEADS].set(b_forget)
    cw = jnp.zeros((32, D_CONV), F32).at[0:CONV_K].set(conv_dw_w)
    rw = jnp.zeros((8, D_RNN), F32).at[0:RNN_CONV_K].set(rg_conv_w)
    bd = lambda w: jax.scipy.linalg.block_diag(*[w[i] for i in range(RNN_BLOCKS)]).astype(BF)
    row = lambda v: v.reshape(1, -1).astype(F32)
    keys = jnp.stack([peer_k1, peer_k2], axis=1).reshape(2 * PEER_HEADS, N_KEYS, D_HALF).astype(BF)
    return dict(wqk=w_in[:, 0:2 * D_ATT].astype(BF), wvt=w_in[:, 2 * D_ATT:f0].T.astype(BF), wf=wf,
                wrest=w_in[:, f0 + ATT_HEADS:].astype(BF), bfg=bfg,
                cw=cw, cb=row(conv_dw_b), lg=row(conv_ln_g), lb=row(conv_ln_b),
                rw=rw, rb=row(rg_conv_b), wr=bd(rg_w_r), br=row(rg_b_r), wi=bd(rg_w_i), bi=row(rg_b_i),
                lam=row(rg_lambda), woa=w_out[0:D_ATT].astype(BF), wob=w_out[D_ATT:].astype(BF),
                wq=peer_wq.astype(BF), keys=keys)


def kernel(x, norm1_g, w_in, b_forget, conv_dw_w, conv_dw_b, conv_ln_g, conv_ln_b,
           rg_conv_w, rg_conv_b, rg_w_r, rg_b_r, rg_w_i, rg_b_i, rg_lambda, w_out,
           norm2_g, peer_wq, peer_k1, peer_k2, peer_u, peer_v, final_g):
    b, s, d = x.shape
    raw = (w_in, b_forget, conv_dw_w, conv_dw_b, conv_ln_g, conv_ln_b, rg_conv_w, rg_conv_b,
           rg_w_r, rg_b_r, rg_w_i, rg_b_i, rg_lambda, w_out, peer_wq, peer_k1, peer_k2)
    params = {0: _prep_layer(*[a[0] for a in raw])}
    tabs = {0: pack_tables(peer_u[0], peer_v[0])}
    bs = b // N_SLICES
    T = bs * s
    xf = x.reshape(b * s, d)
    xs = [None] * N_SLICES
    prev = None
    for l in range(DEPTH):
        for i in range(N_SLICES):
            src, row0 = (xf, i * T) if l == 0 else (xs[i], 0)
            first = l == 0 and i == 0
            late = l == 0 and i == LATE_PREP_SLICE
            deps = ([prev] if prev is not None else []) + ([tabs[0]] if first else [])
            deps += [*raw, peer_u, peer_v] if late else []
            if deps:
                tied = list(lax.optimization_barrier((src, *deps)))
                src = tied.pop(0)
                if prev is not None:
                    prev = tied.pop(0)
                if first:
                    tabs[0] = tied.pop(0)
                if late:
                    for ll in range(1, DEPTH):
                        params[ll] = _prep_layer(*[a[ll] for a in tied[:len(raw)]])
                        tabs[ll] = pack_tables(tied[-2][ll], tied[-1][ll])
            p = params[l]
            qkb, vt, rest = in_proj(src, norm1_g[l].reshape(1, d), p, s, T, row0)
            y_att = attention(qkb, vt, bs, s)
            y_cr = mixers(rest, p, bs, s)
            npiece = PIECES.get((l, i), 1)
            tp = T // npiece
            pieces = []
            for j in range(npiece):
                if j > 0:
                    src, experts = lax.optimization_barrier((src, experts))
                x1, h2p, st = out_proj(src, y_att, y_cr, p, norm2_g[l].reshape(1, d), row0, j * tp, tp)
                experts, gates = route(st)
                pieces.append(peer_sc(h2p, x1, experts, gates, tabs[l]))
            prev = experts
            xs[i] = pieces[0] if npiece == 1 else jnp.concatenate(pieces, axis=0)
    out = None
    for i in range(N_SLICES):
        out = final_norm(xs[i], final_g.reshape(1, d), out, i * T, b * s)
    return out.reshape(b, s, d)
```

```python
import functools
import math

import jax
import jax.numpy as jnp
from jax import lax
from jax.experimental import pallas as pl
from jax.experimental.pallas import tpu as pltpu
from jax.experimental.pallas import tpu_sc as plsc

BF = jnp.bfloat16
F32 = jnp.float32
I32 = jnp.int32

D_MODEL = 1024
DEPTH = 2
ATT_HEADS = 8
ATT_HD = 64
D_ATT = ATT_HEADS * ATT_HD
D_CONV = 256
CONV_K = 31
D_RNN = 256
RNN_BLOCKS = 4
RNN_CONV_K = 4
RG_C = 8.0
EPS = 1e-6
N_REST = 2 * D_CONV + 2 * D_RNN
PEER_HEADS = 8
N_KEYS = 128
D_HALF = 128
TOPK = 16
KSEL = PEER_HEADS * TOPK
GC = 0.7978845608028654
NEG = float("-inf")

PIECES = {(0, 0): 4, (0, 1): 2}
LATE_PREP_SLICE = 2
N_SLICES = 4
LANES = 128
VMEM_LIMIT = 48 * 1024 * 1024


def _cp(sem):
    return pltpu.CompilerParams(dimension_semantics=sem, vmem_limit_bytes=VMEM_LIMIT)


def _split3(x):
    hi = x.astype(BF)
    r = x - hi.astype(F32)
    mid = r.astype(BF)
    lo = (r - mid.astype(F32)).astype(BF)
    return hi, mid, lo


def _nt(a, b):
    return lax.dot_general(a, b, (((1,), (1,)), ((), ())), preferred_element_type=F32)


def _dot(a, b):
    return jnp.dot(a, b, preferred_element_type=F32)


def _sigmoid(x):
    return 1.0 / (1.0 + jnp.exp(-x))


def _gelu(x):
    return 0.5 * x * (1.0 + jnp.tanh(GC * (x + 0.044715 * x * x * x)))


LOG2E = 1.4426950408889634
NSPLIT = 3
FPAD = 16


def _inproj_body(x_ref, g_ref, wqk_ref, wvt_ref, wf_ref, wrest_ref, bf_ref, tri_ref, place_ref,
                 qkb_ref, vt_ref, rest_ref, carry_ref, *, blocks_per_seq, tm):
    i = pl.program_id(0)
    x = x_ref[...]
    h = x * lax.rsqrt(jnp.mean(x * x, axis=-1, keepdims=True) + EPS) * g_ref[...]
    hb = h.astype(BF)
    qk = _dot(hb, wqk_ref[...])
    col = lax.broadcasted_iota(I32, (1, 2 * D_ATT), 1)
    qk = jnp.where(col < D_ATT, qk * (LOG2E / math.sqrt(ATT_HD)), qk)
    qkb_ref[:, 0:2 * D_ATT] = qk.astype(BF)
    vt_ref[...] = _nt(wvt_ref[...], hb).astype(BF)
    rest_ref[...] = _dot(hb, wrest_ref[...])
    ft = _dot(hb, wf_ref[...]) + bf_ref[...]
    lf = jnp.minimum(ft, 0.0) - jnp.log(1.0 + jnp.exp(-jnp.abs(ft)))
    hi, mid, lo = _split3(lf)
    tri = tri_ref[...]
    cs = _dot(tri, hi) + _dot(tri, mid) + _dot(tri, lo)

    @pl.when(i % blocks_per_seq == 0)
    def _():
        carry_ref[...] = jnp.zeros_like(carry_ref)

    cum = cs + carry_ref[...]
    carry_ref[...] = cum[tm - 1:tm, :]
    pieces = _split3(cum * (-LOG2E))
    kb = _dot(pieces[0], place_ref[0]) + _dot(pieces[1], place_ref[1]) + _dot(pieces[2], place_ref[2])
    qkb_ref[:, 2 * D_ATT:3 * D_ATT] = kb.astype(BF)


def bias_lane(hh, j):
    return (ATT_HD if hh == 0 else 0) + j


def in_proj(x, g, p, seq, T, row0=0, tm=512):
    blk0 = row0 // tm
    tri =(lax.broadcasted_iota(I32, (tm, tm), 0) >= lax.broadcasted_iota(I32, (tm, tm), 1)).astype(BF)
    shp = (NSPLIT, FPAD, D_ATT)
    hd = lax.broadcasted_iota(I32, shp, 1)
    target = (hd // 2) * LANES + jnp.where(hd % 2 == 0, ATT_HD, 0) + lax.broadcasted_iota(I32, shp, 0)
    place = ((lax.broadcasted_iota(I32, shp, 2) == target) & (hd < ATT_HEADS)).astype(BF)
    body = functools.partial(_inproj_body, blocks_per_seq=seq // tm, tm=tm)
    return pl.pallas_call(
        body, grid=(T // tm,),
        in_specs=[pl.BlockSpec((tm, D_MODEL), lambda i: (i + blk0, 0)),
                  pl.BlockSpec((1, D_MODEL), lambda i: (0, 0)),
                  pl.BlockSpec((D_MODEL, 2 * D_ATT), lambda i: (0, 0)),
                  pl.BlockSpec((D_ATT, D_MODEL), lambda i: (0, 0)),
                  pl.BlockSpec((D_MODEL, FPAD), lambda i: (0, 0)),
                  pl.BlockSpec((D_MODEL, N_REST), lambda i: (0, 0)),
                  pl.BlockSpec((1, FPAD), lambda i: (0, 0)),
                  pl.BlockSpec((tm, tm), lambda i: (0, 0)),
                  pl.BlockSpec(shp, lambda i: (0, 0, 0))],
        out_specs=[pl.BlockSpec((tm, 3 * D_ATT), lambda i: (i, 0)),
                   pl.BlockSpec((D_ATT, tm), lambda i: (0, i)),
                   pl.BlockSpec((tm, N_REST), lambda i: (i, 0))],
        out_shape=[jax.ShapeDtypeStruct((T, 3 * D_ATT), BF),
                   jax.ShapeDtypeStruct((D_ATT, T), BF),
                   jax.ShapeDtypeStruct((T, N_REST), F32)],
        scratch_shapes=[pltpu.VMEM((1, FPAD), F32)],
        compiler_params=_cp(("arbitrary",)), name="in_proj",
    )(x, g, p["wqk"], p["wvt"], p["wf"], p["wrest"], p["bfg"], tri, place)


def _attn_body(q_ref, k_ref, kb_ref, vt_ref, o_ref, m_ref, acc_ref, *, tq, tk):
    qi = pl.program_id(2)
    ki = pl.program_id(3)

    @pl.when(ki == 0)
    def _():
        m_ref[...] = jnp.full_like(m_ref, NEG)
        acc_ref[...] = jnp.zeros_like(acc_ref)

    lane = lax.broadcasted_iota(I32, (1, LANES), 1)
    first = lane < ATT_HD
    vrow = lax.broadcasted_iota(I32, (LANES, 1), 0) < ATT_HD

    def step(masked):
        q = q_ref[...]
        k = k_ref[...]
        kb = kb_ref[...]
        vt = vt_ref[...]
        if masked:
            keep = (lax.broadcasted_iota(I32, (tk, tq), 0) <= lax.broadcasted_iota(I32, (tk, tq), 1))
        for hh in range(2):
            own = first if hh == 0 else jnp.logical_not(first)
            ones = (lane >= bias_lane(hh, 0)) & (lane < bias_lane(hh, NSPLIT))
            qa = jnp.where(own, q, jnp.where(ones, 1.0, 0.0).astype(BF))
            ka = jnp.where(own, k, kb)
            st = _nt(ka, qa)
            if masked:
                st = jnp.where(keep, st, NEG)
            m_prev = m_ref[hh]
            m_new = jnp.maximum(m_prev, jnp.max(st, axis=0, keepdims=True))
            alpha = jnp.exp2(m_prev - m_new)
            p = jnp.exp2(st - m_new).astype(BF)
            m_ref[hh] = m_new
            vown = vrow if hh == 0 else jnp.logical_not(vrow)
            va = jnp.where(vown, vt, jnp.ones_like(vt))
            acc_ref[hh] = alpha * acc_ref[hh] + _dot(va, p)

    @pl.when(ki < qi)
    def _():
        step(False)

    @pl.when(ki == qi)
    def _():
        step(True)
        a0 = acc_ref[0]
        a1 = acc_ref[1]
        ot = jnp.where(vrow, a0 / a0[ATT_HD:ATT_HD + 1, :], a1 / a1[0:1, :])
        o_ref[...] = ot.T.astype(o_ref.dtype)


def attention(qkb, vt, batch, seq, tq=1024):
    T = qkb.shape[0]
    tk = tq
    nq = seq // tq
    npair = ATT_HEADS // 2
    body = functools.partial(_attn_body, tq=tq, tk=tk)
    kblk = lambda b, qi, ki: b * nq + jnp.minimum(ki, qi)
    return pl.pallas_call(
        body, grid=(batch, npair, nq, nq),
        in_specs=[pl.BlockSpec((tq, LANES), lambda b, p, qi, ki: (b * nq + qi, p)),
                  pl.BlockSpec((tk, LANES), lambda b, p, qi, ki: (kblk(b, qi, ki), npair + p)),
                  pl.BlockSpec((tk, LANES), lambda b, p, qi, ki: (kblk(b, qi, ki), 2 * npair + p)),
                  pl.BlockSpec((LANES, tk), lambda b, p, qi, ki: (p, kblk(b, qi, ki)))],
        out_specs=pl.BlockSpec((tq, LANES), lambda b, p, qi, ki: (b * nq + qi, p)),
        out_shape=jax.ShapeDtypeStruct((T, D_ATT), BF),
        scratch_shapes=[pltpu.VMEM((2, 1, tq), F32), pltpu.VMEM((2, LANES, tq), F32)],
        compiler_params=_cp(("parallel", "parallel", "parallel", "arbitrary")), name="fox_attention",
    )(qkb, qkb, qkb, vt)


CONV_HALO = 32
RG_HALO = 8


def _mix_body(rest_ref, cw_ref, cb_ref, lg_ref, lb_ref, rw_ref, rb_ref, wr_ref, br_ref, wi_ref, bi_ref, lam_ref,
              o_ref, ybuf, xbuf, hc, *, ts):
    si = pl.program_id(1)

    @pl.when(si == 0)
    def _():
        ybuf[0:CONV_HALO, :] = jnp.zeros((CONV_HALO, D_CONV), F32)
        xbuf[0:RG_HALO, :] = jnp.zeros((RG_HALO, D_RNN), F32)
        hc[...] = jnp.zeros_like(hc)

    y = rest_ref[:, 0:D_CONV] * _sigmoid(rest_ref[:, D_CONV:2 * D_CONV])
    ybuf[CONV_HALO:CONV_HALO + ts, :] = y
    acc = jnp.zeros((ts, D_CONV), F32)
    for k in range(CONV_K):
        acc = acc + cw_ref[k:k + 1, :] * ybuf[pl.ds(CONV_HALO - (CONV_K - 1) + k, ts), :]
    yc = acc + cb_ref[...]
    mu = jnp.mean(yc, axis=-1, keepdims=True)
    var = jnp.mean(jnp.square(yc - mu), axis=-1, keepdims=True)
    yn = (yc - mu) * lax.rsqrt(var + EPS) * lg_ref[...] + lb_ref[...]
    o_ref[:, 0:D_CONV] = (yn * _sigmoid(yn)).astype(o_ref.dtype)
    ybuf[0:CONV_HALO, :] = ybuf[ts:ts + CONV_HALO, :]

    xbuf[RG_HALO:RG_HALO + ts, :] = rest_ref[:, 2 * D_CONV:2 * D_CONV + D_RNN]
    xc = jnp.zeros((ts, D_RNN), F32)
    for k in range(RNN_CONV_K):
        xc = xc + rw_ref[k:k + 1, :] * xbuf[pl.ds(RG_HALO - (RNN_CONV_K - 1) + k, ts), :]
    xc = xc + rb_ref[...]
    xbuf[0:RG_HALO, :] = xbuf[ts:ts + RG_HALO, :]
    xcb = xc.astype(BF)
    r = _sigmoid(_dot(xcb, wr_ref[...]) + br_ref[...])
    gi = _sigmoid(_dot(xcb, wi_ref[...]) + bi_ref[...])
    nl = -lam_ref[...]
    sp = jnp.maximum(nl, 0.0) + jnp.log(1.0 + jnp.exp(-jnp.abs(nl)))
    log_a = -RG_C * r * sp
    a = jnp.exp(log_a)
    bt = jnp.sqrt(1.0 - jnp.exp(2.0 * log_a)) * (gi * xc)
    row = lax.broadcasted_iota(I32, (ts, 1), 0)
    sh = 1
    while sh < ts:
        live = row >= sh
        a_s = jnp.where(live, pltpu.roll(a, sh, 0), 1.0)
        b_s = jnp.where(live, pltpu.roll(bt, sh, 0), 0.0)
        bt = bt + a * b_s
        a = a * a_s
        sh *= 2
    h = bt + a * hc[...]
    hc[...] = h[ts - 1:ts, :]
    gate_in = rest_ref[:, 2 * D_CONV + D_RNN:2 * D_CONV + 2 * D_RNN]
    o_ref[:, D_CONV:D_CONV + D_RNN] = (h * _gelu(gate_in)).astype(o_ref.dtype)


def mixers(rest, p, batch, seq, ts=512):
    T = rest.shape[0]
    ns = seq // ts
    body = functools.partial(_mix_body, ts=ts)
    vec = lambda: pl.BlockSpec((1, D_CONV), lambda b, s: (0, 0))
    return pl.pallas_call(
        body, grid=(batch, ns),
        in_specs=[pl.BlockSpec((ts, N_REST), lambda b, s: (b * ns + s, 0)),
                  pl.BlockSpec((32, D_CONV), lambda b, s: (0, 0)), vec(), vec(), vec(),
                  pl.BlockSpec((8, D_RNN), lambda b, s: (0, 0)), vec(),
                  pl.BlockSpec((D_RNN, D_RNN), lambda b, s: (0, 0)), vec(),
                  pl.BlockSpec((D_RNN, D_RNN), lambda b, s: (0, 0)), vec(), vec()],
        out_specs=pl.BlockSpec((ts, D_CONV + D_RNN), lambda b, s: (b * ns + s, 0)),
        out_shape=jax.ShapeDtypeStruct((T, D_CONV + D_RNN), BF),
        scratch_shapes=[pltpu.VMEM((ts + CONV_HALO, D_CONV), F32), pltpu.VMEM((ts + RG_HALO, D_RNN), F32),
                        pltpu.VMEM((1, D_RNN), F32)],
        compiler_params=_cp(("arbitrary", "arbitrary")), name="conv_rglru",
    )(rest, p["cw"], p["cb"], p["lg"], p["lb"], p["rw"], p["rb"], p["wr"], p["br"], p["wi"], p["bi"], p["lam"])


def pack_pairs(a):
    half = a.shape[1] // 2
    r = lax.bitcast_convert_type(a, I32)
    r = r + jnp.int32(0x7FFF) + jnp.bitwise_and(lax.shift_right_logical(r, 16), 1)
    lo = lax.shift_right_logical(r[:, :half], 16)
    hi = jnp.bitwise_and(r[:, half:], jnp.int32(-65536))
    return jnp.bitwise_or(hi, lo)


def _outproj_body(x_ref, ya_ref, yc_ref, woa_ref, wob_ref, g2_ref, wq_ref, keys_ref, x1_ref, h2p_ref, st_ref):
    x1 = x_ref[...] + _dot(ya_ref[...], woa_ref[...]) + _dot(yc_ref[...], wob_ref[...])
    x1_ref[...] = x1
    h2 = x1 * lax.rsqrt(jnp.mean(x1 * x1, axis=-1, keepdims=True) + EPS) * g2_ref[...]
    h2p_ref[...] = pack_pairs(h2)
    q = _dot(h2.astype(BF), wq_ref[...]).astype(BF)
    for g in range(2 * PEER_HEADS):
        st_ref[g * N_KEYS:(g + 1) * N_KEYS, :] = _nt(keys_ref[g], q[:, g * D_HALF:(g + 1) * D_HALF])


def out_proj(x, ya, yc, p, g2, row0=0, sub0=0, T=None, tm=256):
    T = ya.shape[0] if T is None else T
    ng = 2 * PEER_HEADS
    blk0 = (row0 + sub0) // tm
    sblk = sub0 // tm
    return pl.pallas_call(
        _outproj_body, grid=(T // tm,),
        in_specs=[pl.BlockSpec((tm, D_MODEL), lambda i: (i + blk0, 0)),
                  pl.BlockSpec((tm, D_ATT), lambda i: (i + sblk, 0)),
                  pl.BlockSpec((tm, D_CONV + D_RNN), lambda i: (i + sblk, 0)),
                  pl.BlockSpec((D_ATT, D_MODEL), lambda i: (0, 0)),
                  pl.BlockSpec((D_CONV + D_RNN, D_MODEL), lambda i: (0, 0)),
                  pl.BlockSpec((1, D_MODEL), lambda i: (0, 0)),
                  pl.BlockSpec((D_MODEL, ng * D_HALF), lambda i: (0, 0)),
                  pl.BlockSpec((ng, N_KEYS, D_HALF), lambda i: (0, 0, 0))],
        out_specs=[pl.BlockSpec((tm, D_MODEL), lambda i: (i, 0)),
                   pl.BlockSpec((tm, D_MODEL // 2), lambda i: (i, 0)),
                   pl.BlockSpec((ng * N_KEYS, tm), lambda i: (0, i))],
        out_shape=[jax.ShapeDtypeStruct((T, D_MODEL), F32),
                   jax.ShapeDtypeStruct((T, D_MODEL // 2), I32),
                   jax.ShapeDtypeStruct((ng * N_KEYS, T), F32)],
        compiler_params=_cp(("parallel",)), name="out_proj_peer_scores",
    )(x, ya, yc, p["woa"], p["wob"], g2, p["wq"], p["keys"])


BIG_ID = 1 << 20
SUBL = 8
NPAR = 4
NPAR1 = 8
SEL_CHAIN = 4


def _take_rounds(problems, nrounds):
    state = [list(slabs) for slabs, _ in problems]
    res = [([], []) for _ in problems]
    for _ in range(nrounds):
        for pi, (_, ids) in enumerate(problems):
            slabs = state[pi]
            m8 = slabs[0]
            for sl in slabs[1:]:
                m8 = jnp.maximum(m8, sl)
            m = jnp.max(m8, axis=0, keepdims=True)
            chains = []
            for c0 in range(0, len(slabs), SEL_CHAIN):
                v = jnp.full((SUBL, LANES), BIG_ID, I32)
                for sl, idc in zip(reversed(slabs[c0:c0 + SEL_CHAIN]), reversed(ids[c0:c0 + SEL_CHAIN])):
                    v = jnp.where(sl == m, idc, v)
                chains.append(v)
            while len(chains) > 1:
                chains = [jnp.minimum(chains[i], chains[i + 1]) if i + 1 < len(chains) else chains[i]
                          for i in range(0, len(chains), 2)]
            pick = jnp.min(chains[0], axis=0, keepdims=True)
            state[pi] = [jnp.where(idc == pick, NEG, sl) for sl, idc in zip(slabs, ids)]
            res[pi][0].append(m)
            res[pi][1].append(pick)
    return res


def _route_body(st_ref, e_ref, g_ref, v_scr, i_scr, sv_scr, ci_scr, et_scr, gt_scr):
    ng = 2 * PEER_HEADS
    sub = lax.broadcasted_iota(I32, (SUBL, LANES), 0)
    key_ids = [sub + SUBL * i for i in range(N_KEYS // SUBL)]

    def stage1(gg, carry):
        probs = []
        for q in range(NPAR1):
            base = pl.multiple_of((gg * NPAR1 + q) * N_KEYS, N_KEYS)
            probs.append(([st_ref[pl.ds(base + SUBL * i, SUBL), :] for i in range(N_KEYS // SUBL)], key_ids))
        for q, (vals, picks) in enumerate(_take_rounds(probs, TOPK)):
            for r in range(TOPK):
                v_scr[gg * NPAR1 + q, r:r + 1, :] = vals[r]
                i_scr[gg * NPAR1 + q, r:r + 1, :] = picks[r]
        return carry

    lax.fori_loop(0, ng // NPAR1, stage1, 0)

    def stage2(hh, carry):
        probs = []
        for q in range(NPAR):
            h = hh * NPAR + q
            v1 = v_scr[2 * h]
            v2 = v_scr[2 * h + 1]
            slabs = [v1[0:1, :] + v2[0:SUBL, :], v1[0:1, :] + v2[SUBL:TOPK, :]]
            ids = [sub, sub + SUBL]
            for i in range(1, TOPK):
                nj = TOPK // (i + 1)
                slabs.append(jnp.where(sub < nj, v1[i:i + 1, :] + v2[0:SUBL, :], NEG))
                ids.append(sub + i * TOPK)
            probs.append((slabs, ids))
        for q, (vals, picks) in enumerate(_take_rounds(probs, TOPK)):
            h = hh * NPAR + q
            i1 = i_scr[2 * h]
            i2 = i_scr[2 * h + 1]
            for r in range(TOPK):
                sv_scr[q, r:r + 1, :] = vals[r]
                ci_scr[q, r:r + 1, :] = picks[r]
            sv = sv_scr[q]
            ci = ci_scr[q]
            ci_hi = lax.shift_right_logical(ci, 4)
            ci_lo = jnp.bitwise_and(ci, TOPK - 1)
            e1 = jnp.zeros((TOPK, LANES), I32)
            e2 = jnp.zeros((TOPK, LANES), I32)
            for i in range(TOPK):
                e1 = jnp.where(ci_hi == i, i1[i:i + 1, :], e1)
                e2 = jnp.where(ci_lo == i, i2[i:i + 1, :], e2)
            p = jnp.exp(sv - sv[0:1, :])
            gates = p / jnp.sum(p, axis=0, keepdims=True)
            et_scr[pl.ds(pl.multiple_of(h * TOPK, TOPK), TOPK), :] = e1 * N_KEYS + e2
            gt_scr[pl.ds(pl.multiple_of(h * TOPK, TOPK), TOPK), :] = gates
        return carry

    lax.fori_loop(0, PEER_HEADS // NPAR, stage2, 0)
    e_ref[...] = et_scr[...].T
    g_ref[...] = gt_scr[...].T


def route(st):
    T = st.shape[1]
    ng = 2 * PEER_HEADS
    return pl.pallas_call(
        _route_body, grid=(T // LANES,),
        in_specs=[pl.BlockSpec((ng * N_KEYS, LANES), lambda i: (0, i))],
        out_specs=[pl.BlockSpec((LANES, KSEL), lambda i: (i, 0)),
                   pl.BlockSpec((LANES, KSEL), lambda i: (i, 0))],
        out_shape=[jax.ShapeDtypeStruct((T, KSEL), I32),
                   jax.ShapeDtypeStruct((T, KSEL), F32)],
        scratch_shapes=[pltpu.VMEM((ng, TOPK, LANES), F32), pltpu.VMEM((ng, TOPK, LANES), I32),
                        pltpu.VMEM((NPAR, TOPK, LANES), F32), pltpu.VMEM((NPAR, TOPK, LANES), I32),
                        pltpu.VMEM((KSEL, LANES), I32), pltpu.VMEM((KSEL, LANES), F32)],
        compiler_params=_cp(("parallel",)), name="peer_route",
    )(st)


NC, NS, L = 2, 16, 16
NW = NC * NS
NJ = D_MODEL // L
R = TOPK
NCH = KSEL // R
NB = 4
USTEP = 2
G = 8
NSLOT = 2
DW = D_MODEL // 2


def _perm(x, idx):
    return jnp.take_along_axis(x, idx, axis=0, mode="promise_in_bounds")


def _halves(w):
    lo = lax.bitcast_convert_type(lax.shift_left(w, 16), F32)
    hi = lax.bitcast_convert_type(jnp.bitwise_and(w, jnp.int32(-65536)), F32)
    return lo, hi


def _bf(w):
    return plsc.bitcast(w, BF)


def peer_sc(x, resid, idx, gates, uv_tab):
    T = x.shape[0]
    tpw = T // NW
    ngroups = tpw // G
    nchunks = G * NCH
    idx3 = idx.reshape(T * NCH, R)
    g3 = gates.reshape(T * NCH, R)
    mesh = plsc.VectorSubcoreMesh(core_axis_name="c", subcore_axis_name="s")

    @functools.partial(
        pl.kernel, mesh=mesh,
        out_type=jax.ShapeDtypeStruct((T, D_MODEL), F32),
        scratch_types=[
            pltpu.VMEM((NSLOT, G, DW), I32),
            pltpu.VMEM((NSLOT, G, D_MODEL), F32),
            pltpu.VMEM((NSLOT, nchunks, R), I32),
            pltpu.VMEM((NSLOT, nchunks, R), F32),
            pltpu.SemaphoreType.DMA((NSLOT,)),
            pltpu.SemaphoreType.DMA((NSLOT,)),
            pltpu.SemaphoreType.DMA((NSLOT,)),
        ] + [pltpu.VMEM((R, 2 * DW), I32) for _ in range(NB)]
          + [pltpu.SemaphoreType.DMA for _ in range(NB)],
        compiler_params=pltpu.CompilerParams(needs_layout_passes=False),
        name="peer_experts_sc",
    )
    def k(x_hbm, r_hbm, idx_hbm, g_hbm, uv_hbm, out_hbm, x_v, out_v, idx_v, g_v, st_sem, ix_sem, wb_sem, *ring):
        wid = lax.axis_index("s") * NC + lax.axis_index("c")
        bufs, sems = ring[:NB], ring[NB:]
        iota = lax.iota(I32, L)

        def tok0_of(g):
            return wid * tpw + g * G

        def stage_copies(g, slot):
            t0 = tok0_of(g)
            return (pltpu.make_async_copy(x_hbm.at[pl.ds(t0, G)], x_v.at[slot], st_sem.at[slot]),
                    pltpu.make_async_copy(r_hbm.at[pl.ds(t0, G)], out_v.at[slot], st_sem.at[slot]),
                    pltpu.make_async_copy(g_hbm.at[pl.ds(t0 * NCH, nchunks)], g_v.at[slot], st_sem.at[slot]))

        def idx_copy(g, slot):
            return pltpu.make_async_copy(idx_hbm.at[pl.ds(tok0_of(g) * NCH, nchunks)], idx_v.at[slot], ix_sem.at[slot])

        def wb_copy(g, slot):
            return pltpu.make_async_copy(out_v.at[slot], out_hbm.at[pl.ds(tok0_of(g), G)], wb_sem.at[slot])

        def gather_copy(slot, c, b):
            return pltpu.make_async_copy(uv_hbm.at[idx_v.at[slot, c]], bufs[b], sems[b])

        def compute(slot, c, b):
            ub = vb = bufs[b]
            t = c // NCH

            def ubody(mm, accs):
                xs_ = [_bf(x_v[slot, t, pl.ds((mm * USTEP + q) * L, L)]) for q in range(USTEP)]
                out = []
                for kk in range(R):
                    pr = [xs_[q] * _bf(ub[kk, pl.ds((mm * USTEP + q) * L, L)]) for q in range(USTEP)]
                    while len(pr) > 1:
                        pr = [pr[i] + pr[i + 1] for i in range(0, len(pr), 2)]
                    lo, hi = _halves(plsc.bitcast(pr[0], I32))
                    out.append(accs[kk] + (lo + hi))
                return tuple(out)

            accs = lax.fori_loop(0, NJ // (2 * USTEP), ubody, tuple(jnp.zeros((L,), F32) for _ in range(R)))
            vecs = list(accs)
            dist = L // 2
            while dist >= 1:
                pidx = jnp.bitwise_xor(iota, dist)
                low = jnp.bitwise_and(iota, dist) == 0
                nxt = []
                for kk in range(dist):
                    a = vecs[kk]
                    bvec = vecs[kk + dist]
                    a = a + _perm(a, pidx)
                    bvec = bvec + _perm(bvec, pidx)
                    nxt.append(jnp.where(low, a, bvec))
                vecs = nxt
                dist //= 2
            hid = vecs[0]
            z = GC * (hid + 0.044715 * hid * hid * hid)
            gel = hid / (1.0 + jnp.exp(-2.0 * z))
            w = g_v[slot, c, :] * gel
            wbs = []
            for kk in range(R):
                wb = _perm(w, jnp.full((L,), kk, I32))
                wbs.append(plsc.pack(wb, wb, format=plsc.PackFormat.INTERLEAVED))

            @plsc.parallel_loop(0, NJ // 2)
            def _(m):
                pr = [wbs[kk] * _bf(vb[kk, pl.ds(DW + m * L, L)]) for kk in range(R)]
                for _lvl in range(2):
                    pr = [pr[i] + pr[i + 1] for i in range(0, len(pr), 2)]
                los, his = [], []
                for q in pr:
                    lo, hi = _halves(plsc.bitcast(q, I32))
                    los.append(lo)
                    his.append(hi)
                while len(los) > 1:
                    los = [los[i] + los[i + 1] for i in range(0, len(los), 2)]
                    his = [his[i] + his[i + 1] for i in range(0, len(his), 2)]
                out_v[slot, t, pl.ds(m * L, L)] = out_v[slot, t, pl.ds(m * L, L)] + los[0]
                out_v[slot, t, pl.ds(DW + m * L, L)] = out_v[slot, t, pl.ds(DW + m * L, L)] + his[0]

        idx_copy(0, 0).start()
        for cp in stage_copies(0, 0):
            cp.start()
        idx_copy(0, 0).wait()
        for b in range(NB - 1):
            gather_copy(0, b, b).start()

        def group(g, carry):
            slot = g % NSLOT
            nslot = 1 - slot
            has_next = g + 1 < ngroups

            @pl.when(g >= 1)
            def _():
                wb_copy(g - 1, nslot).wait()

            @pl.when(has_next)
            def _():
                idx_copy(g + 1, nslot).start()
                for cp in stage_copies(g + 1, nslot):
                    cp.start()

            for cp in stage_copies(g, slot):
                cp.wait()

            def cbody(cc, c2):
                for b in range(NB):
                    c = cc * NB + b
                    cn = c + NB - 1
                    nb = (b + NB - 1) % NB

                    @pl.when(cn < nchunks)
                    def _():
                        gather_copy(slot, cn, nb).start()

                    @pl.when(jnp.logical_and(cn >= nchunks, has_next))
                    def _():
                        @pl.when(cn == nchunks)
                        def _():
                            idx_copy(g + 1, nslot).wait()

                        gather_copy(nslot, cn - nchunks, nb).start()

                    gather_copy(slot, c, b).wait()
                    compute(slot, c, b)
                return c2

            lax.fori_loop(0, nchunks // NB, cbody, 0)
            wb_copy(g, slot).start()
            return carry

        lax.fori_loop(0, ngroups, group, 0)
        wb_copy(ngroups - 1, (ngroups - 1) % NSLOT).wait()

    return k(x, resid, idx3, g3, uv_tab)


def _fn_body(x_ref, g_ref, o_ref):
    xf = x_ref[...]
    o_ref[...] = xf * lax.rsqrt(jnp.mean(xf * xf, axis=-1, keepdims=True) + EPS) * g_ref[...]


def _fn_body_into(x_ref, g_ref, prev_ref, o_ref):
    del prev_ref
    _fn_body(x_ref, g_ref, o_ref)


def final_norm(x, g, out, row0, t_total, tm=1024):
    T, d = x.shape
    blk0 = row0 // tm
    common = dict(grid=(T // tm,), out_specs=pl.BlockSpec((tm, d), lambda i: (i + blk0, 0)),
                  out_shape=jax.ShapeDtypeStruct((t_total, d), F32),
                  compiler_params=_cp(("parallel",)), name="final_norm")
    specs = [pl.BlockSpec((tm, d), lambda i: (i, 0)), pl.BlockSpec((1, d), lambda i: (0, 0))]
    if out is None:
        return pl.pallas_call(_fn_body, in_specs=specs, **common)(x, g)
    return pl.pallas_call(_fn_body_into, in_specs=specs + [pl.BlockSpec(memory_space=pl.ANY)],
                          input_output_aliases={2: 0}, **common)(x, g, out)


def _pack_body(u_ref, v_ref, o_ref):
    o_ref[:, 0:D_MODEL // 2] = pack_pairs(u_ref[...])
    o_ref[:, D_MODEL // 2:D_MODEL] = pack_pairs(v_ref[...])


def pack_tables(u, v, tm=512):
    e, d = u.shape
    return pl.pallas_call(
        _pack_body, grid=(e // tm,),
        in_specs=[pl.BlockSpec((tm, d), lambda i: (i, 0)), pl.BlockSpec((tm, d), lambda i: (i, 0))],
        out_specs=pl.BlockSpec((tm, d), lambda i: (i, 0)),
        out_shape=jax.ShapeDtypeStruct((e, d), I32),
        compiler_params=_cp(("parallel",)), name="pack_tables",
    )(u, v)


def _prep_layer(w_in, b_forget, conv_dw_w, conv_dw_b, conv_ln_g, conv_ln_b, rg_conv_w, rg_conv_b,
                rg_w_r, rg_b_r, rg_w_i, rg_b_i, rg_lambda, w_out, peer_wq, peer_k1, peer_k2):
    f0 = 3 * D_ATT
    wf = jnp.zeros((D_MODEL, FPAD), BF).at[:, 0:ATT_HEADS].set(w_in[:, f0:f0 + ATT_HEADS].astype(BF))
    bfg = jnp.zeros((1, FPAD), F32).at[0, 0:ATT_HEADS].set(b_forget)
    cw = jnp.zeros((32, D_CONV), F32).at[0:CONV_K].set(conv_dw_w)
    rw = jnp.zeros((8, D_RNN), F32).at[0:RNN_CONV_K].set(rg_conv_w)
    bd = lambda w: jax.scipy.linalg.block_diag(*[w[i] for i in range(RNN_BLOCKS)]).astype(BF)
    row = lambda v: v.reshape(1, -1).astype(F32)
    keys = jnp.stack([peer_k1, peer_k2], axis=1).reshape(2 * PEER_HEADS, N_KEYS, D_HALF).astype(BF)
    return dict(wqk=w_in[:, 0:2 * D_ATT].astype(BF), wvt=w_in[:, 2 * D_ATT:f0].T.astype(BF), wf=wf,
                wrest=w_in[:, f0 + ATT_HEADS:].astype(BF), bfg=bfg,
                cw=cw, cb=row(conv_dw_b), lg=row(conv_ln_g), lb=row(conv_ln_b),
                rw=rw, rb=row(rg_conv_b), wr=bd(rg_w_r), br=row(rg_b_r), wi=bd(rg_w_i), bi=row(rg_b_i),
                lam=row(rg_lambda), woa=w_out[0:D_ATT].astype(BF), wob=w_out[D_ATT:].astype(BF),
                wq=peer_wq.astype(BF), keys=keys)


def kernel(x, norm1_g, w_in, b_forget, conv_dw_w, conv_dw_b, conv_ln_g, conv_ln_b,
           rg_conv_w, rg_conv_b, rg_w_r, rg_b_r, rg_w_i, rg_b_i, rg_lambda, w_out,
           norm2_g, peer_wq, peer_k1, peer_k2, peer_u, peer_v, final_g):
    b, s, d = x.shape
    raw = (w_in, b_forget, conv_dw_w, conv_dw_b, conv_ln_g, conv_ln_b, rg_conv_w, rg_conv_b,
           rg_w_r, rg_b_r, rg_w_i, rg_b_i, rg_lambda, w_out, peer_wq, peer_k1, peer_k2)
    params = {0: _prep_layer(*[a[0] for a in raw])}
    tabs = {0: pack_tables(peer_u[0], peer_v[0])}
    bs = b // N_SLICES
    T = bs * s
    xf = x.reshape(b * s, d)
    xs = [None] * N_SLICES
    prev = None
    for l in range(DEPTH):
        for i in range(N_SLICES):
            src, row0 = (xf, i * T) if l == 0 else (xs[i], 0)
            first = l == 0 and i == 0
            late = l == 0 and i == LATE_PREP_SLICE
            deps = ([prev] if prev is not None else []) + ([tabs[0]] if first else [])
            deps += [*raw, peer_u, peer_v] if late else []
            if deps:
                tied = list(lax.optimization_barrier((src, *deps)))
                src = tied.pop(0)
                if prev is not None:
                    prev = tied.pop(0)
                if first:
                    tabs[0] = tied.pop(0)
                if late:
                    for ll in range(1, DEPTH):
                        params[ll] = _prep_layer(*[a[ll] for a in tied[:len(raw)]])
                        tabs[ll] = pack_tables(tied[-2][ll], tied[-1][ll])
            p = params[l]
            qkb, vt, rest = in_proj(src, norm1_g[l].reshape(1, d), p, s, T, row0)
            y_att = attention(qkb, vt, bs, s)
            y_cr = mixers(rest, p, bs, s)
            npiece = PIECES.get((l, i), 1)
            tp = T // npiece
            pieces = []
            for j in range(npiece):
                if j > 0:
                    src, experts = lax.optimization_barrier((src, experts))
                x1, h2p, st = out_proj(src, y_att, y_cr, p, norm2_g[l].reshape(1, d), row0, j * tp, tp)
                experts, gates = route(st)
                pieces.append(peer_sc(h2p, x1, experts, gates, tabs[l]))
            prev = experts
            xs[i] = pieces[0] if npiece == 1 else jnp.concatenate(pieces, axis=0)
    out = None
    for i in range(N_SLICES):
        out = final_norm(xs[i], final_g.reshape(1, d), out, i * T, b * s)
    return out.reshape(b, s, d)
```

```python
import functools
import math

import jax
import jax.numpy as jnp
from jax import lax
from jax.experimental import pallas as pl
from jax.experimental.pallas import tpu as pltpu
from jax.experimental.pallas import tpu_sc as plsc

BF = jnp.bfloat16
F32 = jnp.float32
I32 = jnp.int32

D_MODEL = 1024
DEPTH = 2
ATT_HEADS = 8
ATT_HD = 64
D_ATT = ATT_HEADS * ATT_HD
D_CONV = 256
CONV_K = 31
D_RNN = 256
RNN_BLOCKS = 4
RNN_CONV_K = 4
RG_C = 8.0
EPS = 1e-6
N_REST = 2 * D_CONV + 2 * D_RNN
PEER_HEADS = 8
N_KEYS = 128
D_HALF = 128
TOPK = 16
KSEL = PEER_HEADS * TOPK
GC = 0.7978845608028654
NEG = float("-inf")

PIECES = {(0, 0): 4, (0, 1): 2}
LATE_PREP_SLICE = 2
N_SLICES = 4
LANES = 128
VMEM_LIMIT = 48 * 1024 * 1024


def _cp(sem):
    return pltpu.CompilerParams(dimension_semantics=sem, vmem_limit_bytes=VMEM_LIMIT)


def _split3(x):
    hi = x.astype(BF)
    r = x - hi.astype(F32)
    mid = r.astype(BF)
    lo = (r - mid.astype(F32)).astype(BF)
    return hi, mid, lo


def _nt(a, b):
    return lax.dot_general(a, b, (((1,), (1,)), ((), ())), preferred_element_type=F32)


def _dot(a, b):
    return jnp.dot(a, b, preferred_element_type=F32)


def _sigmoid(x):
    return 1.0 / (1.0 + jnp.exp(-x))


def _gelu(x):
    return 0.5 * x * (1.0 + jnp.tanh(GC * (x + 0.044715 * x * x * x)))


LOG2E = 1.4426950408889634
NSPLIT = 3
FPAD = 16


def _inproj_body(x_ref, g_ref, wqk_ref, wvt_ref, wf_ref, wrest_ref, bf_ref, tri_ref, place_ref,
                 qkb_ref, vt_ref, rest_ref, carry_ref, *, blocks_per_seq, tm):
    i = pl.program_id(0)
    x = x_ref[...]
    h = x * lax.rsqrt(jnp.mean(x * x, axis=-1, keepdims=True) + EPS) * g_ref[...]
    hb = h.astype(BF)
    qk = _dot(hb, wqk_ref[...])
    col = lax.broadcasted_iota(I32, (1, 2 * D_ATT), 1)
    qk = jnp.where(col < D_ATT, qk * (LOG2E / math.sqrt(ATT_HD)), qk)
    qkb_ref[:, 0:2 * D_ATT] = qk.astype(BF)
    vt_ref[...] = _nt(wvt_ref[...], hb).astype(BF)
    rest_ref[...] = _dot(hb, wrest_ref[...])
    ft = _dot(hb, wf_ref[...]) + bf_ref[...]
    lf = jnp.minimum(ft, 0.0) - jnp.log(1.0 + jnp.exp(-jnp.abs(ft)))
    hi, mid, lo = _split3(lf)
    tri = tri_ref[...]
    cs = _dot(tri, hi) + _dot(tri, mid) + _dot(tri, lo)

    @pl.when(i % blocks_per_seq == 0)
    def _():
        carry_ref[...] = jnp.zeros_like(carry_ref)

    cum = cs + carry_ref[...]
    carry_ref[...] = cum[tm - 1:tm, :]
    pieces = _split3(cum * (-LOG2E))
    kb = _dot(pieces[0], place_ref[0]) + _dot(pieces[1], place_ref[1]) + _dot(pieces[2], place_ref[2])
    qkb_ref[:, 2 * D_ATT:3 * D_ATT] = kb.astype(BF)


def bias_lane(hh, j):
    return (ATT_HD if hh == 0 else 0) + j


def in_proj(x, g, p, seq, T, row0=0, tm=512):
    blk0 = row0 // tm
    tri =(lax.broadcasted_iota(I32, (tm, tm), 0) >= lax.broadcasted_iota(I32, (tm, tm), 1)).astype(BF)
    shp = (NSPLIT, FPAD, D_ATT)
    hd = lax.broadcasted_iota(I32, shp, 1)
    target = (hd // 2) * LANES + jnp.where(hd % 2 == 0, ATT_HD, 0) + lax.broadcasted_iota(I32, shp, 0)
    place = ((lax.broadcasted_iota(I32, shp, 2) == target) & (hd < ATT_HEADS)).astype(BF)
    body = functools.partial(_inproj_body, blocks_per_seq=seq // tm, tm=tm)
    return pl.pallas_call(
        body, grid=(T // tm,),
        in_specs=[pl.BlockSpec((tm, D_MODEL), lambda i: (i + blk0, 0)),
                  pl.BlockSpec((1, D_MODEL), lambda i: (0, 0)),
                  pl.BlockSpec((D_MODEL, 2 * D_ATT), lambda i: (0, 0)),
                  pl.BlockSpec((D_ATT, D_MODEL), lambda i: (0, 0)),
                  pl.BlockSpec((D_MODEL, FPAD), lambda i: (0, 0)),
                  pl.BlockSpec((D_MODEL, N_REST), lambda i: (0, 0)),
                  pl.BlockSpec((1, FPAD), lambda i: (0, 0)),
                  pl.BlockSpec((tm, tm), lambda i: (0, 0)),
                  pl.BlockSpec(shp, lambda i: (0, 0, 0))],
        out_specs=[pl.BlockSpec((tm, 3 * D_ATT), lambda i: (i, 0)),
                   pl.BlockSpec((D_ATT, tm), lambda i: (0, i)),
                   pl.BlockSpec((tm, N_REST), lambda i: (i, 0))],
        out_shape=[jax.ShapeDtypeStruct((T, 3 * D_ATT), BF),
                   jax.ShapeDtypeStruct((D_ATT, T), BF),
                   jax.ShapeDtypeStruct((T, N_REST), F32)],
        scratch_shapes=[pltpu.VMEM((1, FPAD), F32)],
        compiler_params=_cp(("arbitrary",)), name="in_proj",
    )(x, g, p["wqk"], p["wvt"], p["wf"], p["wrest"], p["bfg"], tri, place)


def _attn_body(q_ref, k_ref, kb_ref, vt_ref, o_ref, m_ref, acc_ref, *, tq, tk):
    qi = pl.program_id(2)
    ki = pl.program_id(3)

    @pl.when(ki == 0)
    def _():
        m_ref[...] = jnp.full_like(m_ref, NEG)
        acc_ref[...] = jnp.zeros_like(acc_ref)

    lane = lax.broadcasted_iota(I32, (1, LANES), 1)
    first = lane < ATT_HD
    vrow = lax.broadcasted_iota(I32, (LANES, 1), 0) < ATT_HD

    def step(masked):
        q = q_ref[...]
        k = k_ref[...]
        kb = kb_ref[...]
        vt = vt_ref[...]
        if masked:
            keep = (lax.broadcasted_iota(I32, (tk, tq), 0) <= lax.broadcasted_iota(I32, (tk, tq), 1))
        for hh in range(2):
            own = first if hh == 0 else jnp.logical_not(first)
            ones = (lane >= bias_lane(hh, 0)) & (lane < bias_lane(hh, NSPLIT))
            qa = jnp.where(own, q, jnp.where(ones, 1.0, 0.0).astype(BF))
            ka = jnp.where(own, k, kb)
            st = _nt(ka, qa)
            if masked:
                st = jnp.where(keep, st, NEG)
            m_prev = m_ref[hh]
            m_new = jnp.maximum(m_prev, jnp.max(st, axis=0, keepdims=True))
            alpha = jnp.exp2(m_prev - m_new)
            p = jnp.exp2(st - m_new).astype(BF)
            m_ref[hh] = m_new
            vown = vrow if hh == 0 else jnp.logical_not(vrow)
            va = jnp.where(vown, vt, jnp.ones_like(vt))
            acc_ref[hh] = alpha * acc_ref[hh] + _dot(va, p)

    @pl.when(ki < qi)
    def _():
        step(False)

    @pl.when(ki == qi)
    def _():
        step(True)
        a0 = acc_ref[0]
        a1 = acc_ref[1]
        ot = jnp.where(vrow, a0 / a0[ATT_HD:ATT_HD + 1, :], a1 / a1[0:1, :])
        o_ref[...] = ot.T.astype(o_ref.dtype)


def attention(qkb, vt, batch, seq, tq=1024):
    T = qkb.shape[0]
    tk = tq
    nq = seq // tq
    npair = ATT_HEADS // 2
    body = functools.partial(_attn_body, tq=tq, tk=tk)
    kblk = lambda b, qi, ki: b * nq + jnp.minimum(ki, qi)
    return pl.pallas_call(
        body, grid=(batch, npair, nq, nq),
        in_specs=[pl.BlockSpec((tq, LANES), lambda b, p, qi, ki: (b * nq + qi, p)),
                  pl.BlockSpec((tk, LANES), lambda b, p, qi, ki: (kblk(b, qi, ki), npair + p)),
                  pl.BlockSpec((tk, LANES), lambda b, p, qi, ki: (kblk(b, qi, ki), 2 * npair + p)),
                  pl.BlockSpec((LANES, tk), lambda b, p, qi, ki: (p, kblk(b, qi, ki)))],
        out_specs=pl.BlockSpec((tq, LANES), lambda b, p, qi, ki: (b * nq + qi, p)),
        out_shape=jax.ShapeDtypeStruct((T, D_ATT), BF),
        scratch_shapes=[pltpu.VMEM((2, 1, tq), F32), pltpu.VMEM((2, LANES, tq), F32)],
        compiler_params=_cp(("parallel", "parallel", "parallel", "arbitrary")), name="fox_attention",
    )(qkb, qkb, qkb, vt)


CONV_HALO = 32
RG_HALO = 8


def _mix_body(rest_ref, cw_ref, cb_ref, lg_ref, lb_ref, rw_ref, rb_ref, wr_ref, br_ref, wi_ref, bi_ref, lam_ref,
              o_ref, ybuf, xbuf, hc, *, ts):
    si = pl.program_id(1)

    @pl.when(si == 0)
    def _():
        ybuf[0:CONV_HALO, :] = jnp.zeros((CONV_HALO, D_CONV), F32)
        xbuf[0:RG_HALO, :] = jnp.zeros((RG_HALO, D_RNN), F32)
        hc[...] = jnp.zeros_like(hc)

    y = rest_ref[:, 0:D_CONV] * _sigmoid(rest_ref[:, D_CONV:2 * D_CONV])
    ybuf[CONV_HALO:CONV_HALO + ts, :] = y
    acc = jnp.zeros((ts, D_CONV), F32)
    for k in range(CONV_K):
        acc = acc + cw_ref[k:k + 1, :] * ybuf[pl.ds(CONV_HALO - (CONV_K - 1) + k, ts), :]
    yc = acc + cb_ref[...]
    mu = jnp.mean(yc, axis=-1, keepdims=True)
    var = jnp.mean(jnp.square(yc - mu), axis=-1, keepdims=True)
    yn = (yc - mu) * lax.rsqrt(var + EPS) * lg_ref[...] + lb_ref[...]
    o_ref[:, 0:D_CONV] = (yn * _sigmoid(yn)).astype(o_ref.dtype)
    ybuf[0:CONV_HALO, :] = ybuf[ts:ts + CONV_HALO, :]

    xbuf[RG_HALO:RG_HALO + ts, :] = rest_ref[:, 2 * D_CONV:2 * D_CONV + D_RNN]
    xc = jnp.zeros((ts, D_RNN), F32)
    for k in range(RNN_CONV_K):
        xc = xc + rw_ref[k:k + 1, :] * xbuf[pl.ds(RG_HALO - (RNN_CONV_K - 1) + k, ts), :]
    xc = xc + rb_ref[...]
    xbuf[0:RG_HALO, :] = xbuf[ts:ts + RG_HALO, :]
    xcb = xc.astype(BF)
    r = _sigmoid(_dot(xcb, wr_ref[...]) + br_ref[...])
    gi = _sigmoid(_dot(xcb, wi_ref[...]) + bi_ref[...])
    nl = -lam_ref[...]
    sp = jnp.maximum(nl, 0.0) + jnp.log(1.0 + jnp.exp(-jnp.abs(nl)))
    log_a = -RG_C * r * sp
    a = jnp.exp(log_a)
    bt = jnp.sqrt(1.0 - jnp.exp(2.0 * log_a)) * (gi * xc)
    row = lax.broadcasted_iota(I32, (ts, 1), 0)
    sh = 1
    while sh < ts:
        live = row >= sh
        a_s = jnp.where(live, pltpu.roll(a, sh, 0), 1.0)
        b_s = jnp.where(live, pltpu.roll(bt, sh, 0), 0.0)
        bt = bt + a * b_s
        a = a * a_s
        sh *= 2
    h = bt + a * hc[...]
    hc[...] = h[ts - 1:ts, :]
    gate_in = rest_ref[:, 2 * D_CONV + D_RNN:2 * D_CONV + 2 * D_RNN]
    o_ref[:, D_CONV:D_CONV + D_RNN] = (h * _gelu(gate_in)).astype(o_ref.dtype)


def mixers(rest, p, batch, seq, ts=512):
    T = rest.shape[0]
    ns = seq // ts
    body = functools.partial(_mix_body, ts=ts)
    vec = lambda: pl.BlockSpec((1, D_CONV), lambda b, s: (0, 0))
    return pl.pallas_call(
        body, grid=(batch, ns),
        in_specs=[pl.BlockSpec((ts, N_REST), lambda b, s: (b * ns + s, 0)),
                  pl.BlockSpec((32, D_CONV), lambda b, s: (0, 0)), vec(), vec(), vec(),
                  pl.BlockSpec((8, D_RNN), lambda b, s: (0, 0)), vec(),
                  pl.BlockSpec((D_RNN, D_RNN), lambda b, s: (0, 0)), vec(),
                  pl.BlockSpec((D_RNN, D_RNN), lambda b, s: (0, 0)), vec(), vec()],
        out_specs=pl.BlockSpec((ts, D_CONV + D_RNN), lambda b, s: (b * ns + s, 0)),
        out_shape=jax.ShapeDtypeStruct((T, D_CONV + D_RNN), BF),
        scratch_shapes=[pltpu.VMEM((ts + CONV_HALO, D_CONV), F32), pltpu.VMEM((ts + RG_HALO, D_RNN), F32),
                        pltpu.VMEM((1, D_RNN), F32)],
        compiler_params=_cp(("arbitrary", "arbitrary")), name="conv_rglru",
    )(rest, p["cw"], p["cb"], p["lg"], p["lb"], p["rw"], p["rb"], p["wr"], p["br"], p["wi"], p["bi"], p["lam"])


def pack_pairs(a):
    half = a.shape[1] // 2
    r = lax.bitcast_convert_type(a, I32)
    r = r + jnp.int32(0x7FFF) + jnp.bitwise_and(lax.shift_right_logical(r, 16), 1)
    lo = lax.shift_right_logical(r[:, :half], 16)
    hi = jnp.bitwise_and(r[:, half:], jnp.int32(-65536))
    return jnp.bitwise_or(hi, lo)


def _outproj_body(x_ref, ya_ref, yc_ref, woa_ref, wob_ref, g2_ref, wq_ref, keys_ref, x1_ref, h2p_ref, e_ref, gt_ref,
                  st_ref, *route_scr, tm):
    x1 = x_ref[...] + _dot(ya_ref[...], woa_ref[...]) + _dot(yc_ref[...], wob_ref[...])
    x1_ref[...] = x1
    h2 = x1 * lax.rsqrt(jnp.mean(x1 * x1, axis=-1, keepdims=True) + EPS) * g2_ref[...]
    h2p_ref[...] = pack_pairs(h2)
    q = _dot(h2.astype(BF), wq_ref[...]).astype(BF)
    for g in range(2 * PEER_HEADS):
        st_ref[g * N_KEYS:(g + 1) * N_KEYS, :] = _nt(keys_ref[g], q[:, g * D_HALF:(g + 1) * D_HALF])

    def cols(hc, carry):
        _route_body(st_ref, pl.multiple_of(hc * LANES, LANES), e_ref, gt_ref, *route_scr)
        return carry

    lax.fori_loop(0, tm // LANES, cols, 0)


def out_proj(x, ya, yc, p, g2, row0=0, sub0=0, T=None, tm=256):
    T = ya.shape[0] if T is None else T
    ng = 2 * PEER_HEADS
    blk0 = (row0 + sub0) // tm
    sblk = sub0 // tm
    return pl.pallas_call(
        functools.partial(_outproj_body, tm=tm), grid=(T // tm,),
        in_specs=[pl.BlockSpec((tm, D_MODEL), lambda i: (i + blk0, 0)),
                  pl.BlockSpec((tm, D_ATT), lambda i: (i + sblk, 0)),
                  pl.BlockSpec((tm, D_CONV + D_RNN), lambda i: (i + sblk, 0)),
                  pl.BlockSpec((D_ATT, D_MODEL), lambda i: (0, 0)),
                  pl.BlockSpec((D_CONV + D_RNN, D_MODEL), lambda i: (0, 0)),
                  pl.BlockSpec((1, D_MODEL), lambda i: (0, 0)),
                  pl.BlockSpec((D_MODEL, ng * D_HALF), lambda i: (0, 0)),
                  pl.BlockSpec((ng, N_KEYS, D_HALF), lambda i: (0, 0, 0))],
        out_specs=[pl.BlockSpec((tm, D_MODEL), lambda i: (i, 0)),
                   pl.BlockSpec((tm, D_MODEL // 2), lambda i: (i, 0)),
                   pl.BlockSpec((tm, KSEL), lambda i: (i, 0)),
                   pl.BlockSpec((tm, KSEL), lambda i: (i, 0))],
        out_shape=[jax.ShapeDtypeStruct((T, D_MODEL), F32),
                   jax.ShapeDtypeStruct((T, D_MODEL // 2), I32),
                   jax.ShapeDtypeStruct((T, KSEL), I32),
                   jax.ShapeDtypeStruct((T, KSEL), F32)],
        scratch_shapes=[pltpu.VMEM((ng * N_KEYS, tm), F32),
                        pltpu.VMEM((ng, TOPK, LANES), F32), pltpu.VMEM((ng, TOPK, LANES), I32),
                        pltpu.VMEM((NPAR, TOPK, LANES), F32), pltpu.VMEM((NPAR, TOPK, LANES), I32),
                        pltpu.VMEM((KSEL, LANES), I32), pltpu.VMEM((KSEL, LANES), F32)],
        compiler_params=_cp(("parallel",)), name="out_proj_peer_route",
    )(x, ya, yc, p["woa"], p["wob"], g2, p["wq"], p["keys"])


BIG_ID = 1 << 20
SUBL = 8
NPAR = 4
NPAR1 = 8
SEL_CHAIN = 4


def _take_rounds(problems, nrounds):
    state = [list(slabs) for slabs, _ in problems]
    res = [([], []) for _ in problems]
    for _ in range(nrounds):
        for pi, (_, ids) in enumerate(problems):
            slabs = state[pi]
            m8 = slabs[0]
            for sl in slabs[1:]:
                m8 = jnp.maximum(m8, sl)
            m = jnp.max(m8, axis=0, keepdims=True)
            chains = []
            for c0 in range(0, len(slabs), SEL_CHAIN):
                v = jnp.full((SUBL, LANES), BIG_ID, I32)
                for sl, idc in zip(reversed(slabs[c0:c0 + SEL_CHAIN]), reversed(ids[c0:c0 + SEL_CHAIN])):
                    v = jnp.where(sl == m, idc, v)
                chains.append(v)
            while len(chains) > 1:
                chains = [jnp.minimum(chains[i], chains[i + 1]) if i + 1 < len(chains) else chains[i]
                          for i in range(0, len(chains), 2)]
            pick = jnp.min(chains[0], axis=0, keepdims=True)
            state[pi] = [jnp.where(idc == pick, NEG, sl) for sl, idc in zip(slabs, ids)]
            res[pi][0].append(m)
            res[pi][1].append(pick)
    return res


def _route_body(st_ref, col0, e_ref, g_ref, v_scr, i_scr, sv_scr, ci_scr, et_scr, gt_scr):
    ng = 2 * PEER_HEADS
    sub = lax.broadcasted_iota(I32, (SUBL, LANES), 0)
    key_ids = [sub + SUBL * i for i in range(N_KEYS // SUBL)]

    def stage1(gg, carry):
        probs = []
        for q in range(NPAR1):
            base = pl.multiple_of((gg * NPAR1 + q) * N_KEYS, N_KEYS)
            probs.append(([st_ref[pl.ds(base + SUBL * i, SUBL), pl.ds(col0, LANES)]
                           for i in range(N_KEYS // SUBL)], key_ids))
        for q, (vals, picks) in enumerate(_take_rounds(probs, TOPK)):
            for r in range(TOPK):
                v_scr[gg * NPAR1 + q, r:r + 1, :] = vals[r]
                i_scr[gg * NPAR1 + q, r:r + 1, :] = picks[r]
        return carry

    lax.fori_loop(0, ng // NPAR1, stage1, 0)

    def stage2(hh, carry):
        probs = []
        for q in range(NPAR):
            h = hh * NPAR + q
            v1 = v_scr[2 * h]
            v2 = v_scr[2 * h + 1]
            slabs = [v1[0:1, :] + v2[0:SUBL, :], v1[0:1, :] + v2[SUBL:TOPK, :]]
            ids = [sub, sub + SUBL]
            for i in range(1, TOPK):
                nj = TOPK // (i + 1)
                slabs.append(jnp.where(sub < nj, v1[i:i + 1, :] + v2[0:SUBL, :], NEG))
                ids.append(sub + i * TOPK)
            probs.append((slabs, ids))
        for q, (vals, picks) in enumerate(_take_rounds(probs, TOPK)):
            h = hh * NPAR + q
            i1 = i_scr[2 * h]
            i2 = i_scr[2 * h + 1]
            for r in range(TOPK):
                sv_scr[q, r:r + 1, :] = vals[r]
                ci_scr[q, r:r + 1, :] = picks[r]
            sv = sv_scr[q]
            ci = ci_scr[q]
            ci_hi = lax.shift_right_logical(ci, 4)
            ci_lo = jnp.bitwise_and(ci, TOPK - 1)
            e1 = jnp.zeros((TOPK, LANES), I32)
            e2 = jnp.zeros((TOPK, LANES), I32)
            for i in range(TOPK):
                e1 = jnp.where(ci_hi == i, i1[i:i + 1, :], e1)
                e2 = jnp.where(ci_lo == i, i2[i:i + 1, :], e2)
            p = jnp.exp(sv - sv[0:1, :])
            gates = p / jnp.sum(p, axis=0, keepdims=True)
            et_scr[pl.ds(pl.multiple_of(h * TOPK, TOPK), TOPK), :] = e1 * N_KEYS + e2
            gt_scr[pl.ds(pl.multiple_of(h * TOPK, TOPK), TOPK), :] = gates
        return carry

    lax.fori_loop(0, PEER_HEADS // NPAR, stage2, 0)
    e_ref[pl.ds(col0, LANES), :] = et_scr[...].T
    g_ref[pl.ds(col0, LANES), :] = gt_scr[...].T


NC, NS, L = 2, 16, 16
NW = NC * NS
NJ = D_MODEL // L
R = TOPK
NCH = KSEL // R
NB = 4
USTEP = 2
G = 8
NSLOT = 2
DW = D_MODEL // 2


def _perm(x, idx):
    return jnp.take_along_axis(x, idx, axis=0, mode="promise_in_bounds")


def _halves(w):
    lo = lax.bitcast_convert_type(lax.shift_left(w, 16), F32)
    hi = lax.bitcast_convert_type(jnp.bitwise_and(w, jnp.int32(-65536)), F32)
    return lo, hi


def _bf(w):
    return plsc.bitcast(w, BF)


def peer_sc(x, resid, idx, gates, uv_tab):
    T = x.shape[0]
    tpw = T // NW
    ngroups = tpw // G
    nchunks = G * NCH
    idx3 = idx.reshape(T * NCH, R)
    g3 = gates.reshape(T * NCH, R)
    mesh = plsc.VectorSubcoreMesh(core_axis_name="c", subcore_axis_name="s")

    @functools.partial(
        pl.kernel, mesh=mesh,
        out_type=jax.ShapeDtypeStruct((T, D_MODEL), F32),
        scratch_types=[
            pltpu.VMEM((NSLOT, G, DW), I32),
            pltpu.VMEM((NSLOT, G, D_MODEL), F32),
            pltpu.VMEM((NSLOT, nchunks, R), I32),
            pltpu.VMEM((NSLOT, nchunks, R), F32),
            pltpu.SemaphoreType.DMA((NSLOT,)),
            pltpu.SemaphoreType.DMA((NSLOT,)),
            pltpu.SemaphoreType.DMA((NSLOT,)),
        ] + [pltpu.VMEM((R, 2 * DW), I32) for _ in range(NB)]
          + [pltpu.SemaphoreType.DMA for _ in range(NB)],
        compiler_params=pltpu.CompilerParams(needs_layout_passes=False),
        name="peer_experts_sc",
    )
    def k(x_hbm, r_hbm, idx_hbm, g_hbm, uv_hbm, out_hbm, x_v, out_v, idx_v, g_v, st_sem, ix_sem, wb_sem, *ring):
        wid = lax.axis_index("s") * NC + lax.axis_index("c")
        bufs, sems = ring[:NB], ring[NB:]
        iota = lax.iota(I32, L)

        def tok0_of(g):
            return wid * tpw + g * G

        def stage_copies(g, slot):
            t0 = tok0_of(g)
            return (pltpu.make_async_copy(x_hbm.at[pl.ds(t0, G)], x_v.at[slot], st_sem.at[slot]),
                    pltpu.make_async_copy(r_hbm.at[pl.ds(t0, G)], out_v.at[slot], st_sem.at[slot]),
                    pltpu.make_async_copy(g_hbm.at[pl.ds(t0 * NCH, nchunks)], g_v.at[slot], st_sem.at[slot]))

        def idx_copy(g, slot):
            return pltpu.make_async_copy(idx_hbm.at[pl.ds(tok0_of(g) * NCH, nchunks)], idx_v.at[slot], ix_sem.at[slot])

        def wb_copy(g, slot):
            return pltpu.make_async_copy(out_v.at[slot], out_hbm.at[pl.ds(tok0_of(g), G)], wb_sem.at[slot])

        def gather_copy(slot, c, b):
            return pltpu.make_async_copy(uv_hbm.at[idx_v.at[slot, c]], bufs[b], sems[b])

        def compute(slot, c, b):
            ub = vb = bufs[b]
            t = c // NCH

            def ubody(mm, accs):
                xs_ = [_bf(x_v[slot, t, pl.ds((mm * USTEP + q) * L, L)]) for q in range(USTEP)]
                out = []
                for kk in range(R):
                    pr = [xs_[q] * _bf(ub[kk, pl.ds((mm * USTEP + q) * L, L)]) for q in range(USTEP)]
                    while len(pr) > 1:
                        pr = [pr[i] + pr[i + 1] for i in range(0, len(pr), 2)]
                    lo, hi = _halves(plsc.bitcast(pr[0], I32))
                    out.append(accs[kk] + (lo + hi))
                return tuple(out)

            accs = lax.fori_loop(0, NJ // (2 * USTEP), ubody, tuple(jnp.zeros((L,), F32) for _ in range(R)))
            vecs = list(accs)
            dist = L // 2
            while dist >= 1:
                pidx = jnp.bitwise_xor(iota, dist)
                low = jnp.bitwise_and(iota, dist) == 0
                nxt = []
                for kk in range(dist):
                    a = vecs[kk]
                    bvec = vecs[kk + dist]
                    a = a + _perm(a, pidx)
                    bvec = bvec + _perm(bvec, pidx)
                    nxt.append(jnp.where(low, a, bvec))
                vecs = nxt
                dist //= 2
            hid = vecs[0]
            z = GC * (hid + 0.044715 * hid * hid * hid)
            gel = hid / (1.0 + jnp.exp(-2.0 * z))
            w = g_v[slot, c, :] * gel
            wbs = []
            for kk in range(R):
                wb = _perm(w, jnp.full((L,), kk, I32))
                wbs.append(plsc.pack(wb, wb, format=plsc.PackFormat.INTERLEAVED))

            @plsc.parallel_loop(0, NJ // 2)
            def _(m):
                pr = [wbs[kk] * _bf(vb[kk, pl.ds(DW + m * L, L)]) for kk in range(R)]
                for _lvl in range(2):
                    pr = [pr[i] + pr[i + 1] for i in range(0, len(pr), 2)]
                los, his = [], []
                for q in pr:
                    lo, hi = _halves(plsc.bitcast(q, I32))
                    los.append(lo)
                    his.append(hi)
                while len(los) > 1:
                    los = [los[i] + los[i + 1] for i in range(0, len(los), 2)]
                    his = [his[i] + his[i + 1] for i in range(0, len(his), 2)]
                out_v[slot, t, pl.ds(m * L, L)] = out_v[slot, t, pl.ds(m * L, L)] + los[0]
                out_v[slot, t, pl.ds(DW + m * L, L)] = out_v[slot, t, pl.ds(DW + m * L, L)] + his[0]

        idx_copy(0, 0).start()
        for cp in stage_copies(0, 0):
            cp.start()
        idx_copy(0, 0).wait()
        for b in range(NB - 1):
            gather_copy(0, b, b).start()

        def group(g, carry):
            slot = g % NSLOT
            nslot = 1 - slot
            has_next = g + 1 < ngroups

            @pl.when(g >= 1)
            def _():
                wb_copy(g - 1, nslot).wait()

            @pl.when(has_next)
            def _():
                idx_copy(g + 1, nslot).start()
                for cp in stage_copies(g + 1, nslot):
                    cp.start()

            for cp in stage_copies(g, slot):
                cp.wait()

            def cbody(cc, c2):
                for b in range(NB):
                    c = cc * NB + b
                    cn = c + NB - 1
                    nb = (b + NB - 1) % NB

                    @pl.when(cn < nchunks)
                    def _():
                        gather_copy(slot, cn, nb).start()

                    @pl.when(jnp.logical_and(cn >= nchunks, has_next))
                    def _():
                        @pl.when(cn == nchunks)
                        def _():
                            idx_copy(g + 1, nslot).wait()

                        gather_copy(nslot, cn - nchunks, nb).start()

                    gather_copy(slot, c, b).wait()
                    compute(slot, c, b)
                return c2

            lax.fori_loop(0, nchunks // NB, cbody, 0)
            wb_copy(g, slot).start()
            return carry

        lax.fori_loop(0, ngroups, group, 0)
        wb_copy(ngroups - 1, (ngroups - 1) % NSLOT).wait()

    return k(x, resid, idx3, g3, uv_tab)


def _fn_body(x_ref, g_ref, o_ref):
    xf = x_ref[...]
    o_ref[...] = xf * lax.rsqrt(jnp.mean(xf * xf, axis=-1, keepdims=True) + EPS) * g_ref[...]


def _fn_body_into(x_ref, g_ref, prev_ref, o_ref):
    del prev_ref
    _fn_body(x_ref, g_ref, o_ref)


def final_norm(x, g, out, row0, t_total, tm=1024):
    T, d = x.shape
    blk0 = row0 // tm
    common = dict(grid=(T // tm,), out_specs=pl.BlockSpec((tm, d), lambda i: (i + blk0, 0)),
                  out_shape=jax.ShapeDtypeStruct((t_total, d), F32),
                  compiler_params=_cp(("parallel",)), name="final_norm")
    specs = [pl.BlockSpec((tm, d), lambda i: (i, 0)), pl.BlockSpec((1, d), lambda i: (0, 0))]
    if out is None:
        return pl.pallas_call(_fn_body, in_specs=specs, **common)(x, g)
    return pl.pallas_call(_fn_body_into, in_specs=specs + [pl.BlockSpec(memory_space=pl.ANY)],
                          input_output_aliases={2: 0}, **common)(x, g, out)


def _pack_body(u_ref, v_ref, o_ref):
    o_ref[:, 0:D_MODEL // 2] = pack_pairs(u_ref[...])
    o_ref[:, D_MODEL // 2:D_MODEL] = pack_pairs(v_ref[...])


def pack_tables(u, v, tm=512):
    e, d = u.shape
    return pl.pallas_call(
        _pack_body, grid=(e // tm,),
        in_specs=[pl.BlockSpec((tm, d), lambda i: (i, 0)), pl.BlockSpec((tm, d), lambda i: (i, 0))],
        out_specs=pl.BlockSpec((tm, d), lambda i: (i, 0)),
        out_shape=jax.ShapeDtypeStruct((e, d), I32),
        compiler_params=_cp(("parallel",)), name="pack_tables",
    )(u, v)


def _prep_layer(w_in, b_forget, conv_dw_w, conv_dw_b, conv_ln_g, conv_ln_b, rg_conv_w, rg_conv_b,
                rg_w_r, rg_b_r, rg_w_i, rg_b_i, rg_lambda, w_out, peer_wq, peer_k1, peer_k2):
    f0 = 3 * D_ATT
    wf = jnp.zeros((D_MODEL, FPAD), BF).at[:, 0:ATT_HEADS].set(w_in[:, f0:f0 + ATT_HEADS].astype(BF))
    bfg = jnp.zeros((1, FPAD), F32).at[0, 0:ATT_HEADS].set(b_forget)
    cw = jnp.zeros((32, D_CONV), F32).at[0:CONV_K].set(conv_dw_w)
    rw = jnp.zeros((8, D_RNN), F32).at[0:RNN_CONV_K].set(rg_conv_w)
    bd = lambda w: jax.scipy.linalg.block_diag(*[w[i] for i in range(RNN_BLOCKS)]).astype(BF)
    row = lambda v: v.reshape(1, -1).astype(F32)
    keys = jnp.stack([peer_k1, peer_k2], axis=1).reshape(2 * PEER_HEADS, N_KEYS, D_HALF).astype(BF)
    return dict(wqk=w_in[:, 0:2 * D_ATT].astype(BF), wvt=w_in[:, 2 * D_ATT:f0].T.astype(BF), wf=wf,
                wrest=w_in[:, f0 + ATT_HEADS:].astype(BF), bfg=bfg,
                cw=cw, cb=row(conv_dw_b), lg=row(conv_ln_g), lb=row(conv_ln_b),
                rw=rw, rb=row(rg_conv_b), wr=bd(rg_w_r), br=row(rg_b_r), wi=bd(rg_w_i), bi=row(rg_b_i),
                lam=row(rg_lambda), woa=w_out[0:D_ATT].astype(BF), wob=w_out[D_ATT:].astype(BF),
                wq=peer_wq.astype(BF), keys=keys)


def kernel(x, norm1_g, w_in, b_forget, conv_dw_w, conv_dw_b, conv_ln_g, conv_ln_b,
           rg_conv_w, rg_conv_b, rg_w_r, rg_b_r, rg_w_i, rg_b_i, rg_lambda, w_out,
           norm2_g, peer_wq, peer_k1, peer_k2, peer_u, peer_v, final_g):
    b, s, d = x.shape
    raw = (w_in, b_forget, conv_dw_w, conv_dw_b, conv_ln_g, conv_ln_b, rg_conv_w, rg_conv_b,
           rg_w_r, rg_b_r, rg_w_i, rg_b_i, rg_lambda, w_out, peer_wq, peer_k1, peer_k2)
    params = {0: _prep_layer(*[a[0] for a in raw])}
    tabs = {0: pack_tables(peer_u[0], peer_v[0])}
    bs = b // N_SLICES
    T = bs * s
    xf = x.reshape(b * s, d)
    xs = [None] * N_SLICES
    prev = None
    for l in range(DEPTH):
        for i in range(N_SLICES):
            src, row0 = (xf, i * T) if l == 0 else (xs[i], 0)
            first = l == 0 and i == 0
            late = l == 0 and i == LATE_PREP_SLICE
            deps = ([prev] if prev is not None else []) + ([tabs[0]] if first else [])
            deps += [*raw, peer_u, peer_v] if late else []
            if deps:
                tied = list(lax.optimization_barrier((src, *deps)))
                src = tied.pop(0)
                if prev is not None:
                    prev = tied.pop(0)
                if first:
                    tabs[0] = tied.pop(0)
                if late:
                    for ll in range(1, DEPTH):
                        params[ll] = _prep_layer(*[a[ll] for a in tied[:len(raw)]])
                        tabs[ll] = pack_tables(tied[-2][ll], tied[-1][ll])
            p = params[l]
            qkb, vt, rest = in_proj(src, norm1_g[l].reshape(1, d), p, s, T, row0)
            y_att = attention(qkb, vt, bs, s)
            y_cr = mixers(rest, p, bs, s)
            npiece = PIECES.get((l, i), 1)
            tp = T // npiece
            pieces = []
            for j in range(npiece):
                if j > 0:
                    src, experts = lax.optimization_barrier((src, experts))
                x1, h2p, experts, gates = out_proj(src, y_att, y_cr, p, norm2_g[l].reshape(1, d), row0, j * tp, tp)
                pieces.append(peer_sc(h2p, x1, experts, gates, tabs[l]))
            prev = experts
            xs[i] = pieces[0] if npiece == 1 else jnp.concatenate(pieces, axis=0)
    out = None
    for i in range(N_SLICES):
        out = final_norm(xs[i], final_g.reshape(1, d), out, i * T, b * s)
    return out.reshape(b, s, d)
```

```python
import functools
import math

import jax
import jax.numpy as jnp
from jax import lax
from jax.experimental import pallas as pl
from jax.experimental.pallas import tpu as pltpu
from jax.experimental.pallas import tpu_sc as plsc

BF = jnp.bfloat16
F32 = jnp.float32
I32 = jnp.int32

D_MODEL = 1024
DEPTH = 2
ATT_HEADS = 8
ATT_HD = 64
D_ATT = ATT_HEADS * ATT_HD
D_CONV = 256
CONV_K = 31
D_RNN = 256
RNN_BLOCKS = 4
RNN_CONV_K = 4
RG_C = 8.0
EPS = 1e-6
N_REST = 2 * D_CONV + 2 * D_RNN
PEER_HEADS = 8
N_KEYS = 128
D_HALF = 128
TOPK = 16
KSEL = PEER_HEADS * TOPK
GC = 0.7978845608028654
NEG = float("-inf")

PIECES = {(0, 0): 4, (0, 1): 2}
LATE_PREP_SLICE = 2
N_SLICES = 4
LANES = 128
VMEM_LIMIT = 48 * 1024 * 1024


def _cp(sem):
    return pltpu.CompilerParams(dimension_semantics=sem, vmem_limit_bytes=VMEM_LIMIT)


def _split3(x):
    hi = x.astype(BF)
    r = x - hi.astype(F32)
    mid = r.astype(BF)
    lo = (r - mid.astype(F32)).astype(BF)
    return hi, mid, lo


def _nt(a, b):
    return lax.dot_general(a, b, (((1,), (1,)), ((), ())), preferred_element_type=F32)


def _dot(a, b):
    return jnp.dot(a, b, preferred_element_type=F32)


def _sigmoid(x):
    return 1.0 / (1.0 + jnp.exp(-x))


def _gelu(x):
    return 0.5 * x * (1.0 + jnp.tanh(GC * (x + 0.044715 * x * x * x)))


LOG2E = 1.4426950408889634
NSPLIT = 3
FPAD = 16


def _inproj_body(x_ref, g_ref, wqk_ref, wvt_ref, wf_ref, wrest_ref, bf_ref, tri_ref, place_ref, *rest,
                 blocks_per_seq, tm):
    mix_params, (qkb_ref, vt_ref, ycr_ref, carry_ref, rest_ref, ybuf, xbuf, hc) = rest[:11], rest[11:]
    i = pl.program_id(0)
    x = x_ref[...]
    h = x * lax.rsqrt(jnp.mean(x * x, axis=-1, keepdims=True) + EPS) * g_ref[...]
    hb = h.astype(BF)
    qk = _dot(hb, wqk_ref[...])
    col = lax.broadcasted_iota(I32, (1, 2 * D_ATT), 1)
    qk = jnp.where(col < D_ATT, qk * (LOG2E / math.sqrt(ATT_HD)), qk)
    qkb_ref[:, 0:2 * D_ATT] = qk.astype(BF)
    vt_ref[...] = _nt(wvt_ref[...], hb).astype(BF)
    rest_ref[...] = _dot(hb, wrest_ref[...])
    _mix_body(rest_ref, *mix_params, ycr_ref, ybuf, xbuf, hc, ts=tm, first=i % blocks_per_seq == 0)
    ft = _dot(hb, wf_ref[...]) + bf_ref[...]
    lf = jnp.minimum(ft, 0.0) - jnp.log(1.0 + jnp.exp(-jnp.abs(ft)))
    hi, mid, lo = _split3(lf)
    tri = tri_ref[...]
    cs = _dot(tri, hi) + _dot(tri, mid) + _dot(tri, lo)

    @pl.when(i % blocks_per_seq == 0)
    def _():
        carry_ref[...] = jnp.zeros_like(carry_ref)

    cum = cs + carry_ref[...]
    carry_ref[...] = cum[tm - 1:tm, :]
    pieces = _split3(cum * (-LOG2E))
    kb = _dot(pieces[0], place_ref[0]) + _dot(pieces[1], place_ref[1]) + _dot(pieces[2], place_ref[2])
    qkb_ref[:, 2 * D_ATT:3 * D_ATT] = kb.astype(BF)


def bias_lane(hh, j):
    return (ATT_HD if hh == 0 else 0) + j


def in_proj(x, g, p, seq, T, row0=0, tm=512):
    blk0 = row0 // tm
    tri =(lax.broadcasted_iota(I32, (tm, tm), 0) >= lax.broadcasted_iota(I32, (tm, tm), 1)).astype(BF)
    shp = (NSPLIT, FPAD, D_ATT)
    hd = lax.broadcasted_iota(I32, shp, 1)
    target = (hd // 2) * LANES + jnp.where(hd % 2 == 0, ATT_HD, 0) + lax.broadcasted_iota(I32, shp, 0)
    place = ((lax.broadcasted_iota(I32, shp, 2) == target) & (hd < ATT_HEADS)).astype(BF)
    body = functools.partial(_inproj_body, blocks_per_seq=seq // tm, tm=tm)
    vec = lambda: pl.BlockSpec((1, D_CONV), lambda i: (0, 0))
    return pl.pallas_call(
        body, grid=(T // tm,),
        in_specs=[pl.BlockSpec((tm, D_MODEL), lambda i: (i + blk0, 0)),
                  pl.BlockSpec((1, D_MODEL), lambda i: (0, 0)),
                  pl.BlockSpec((D_MODEL, 2 * D_ATT), lambda i: (0, 0)),
                  pl.BlockSpec((D_ATT, D_MODEL), lambda i: (0, 0)),
                  pl.BlockSpec((D_MODEL, FPAD), lambda i: (0, 0)),
                  pl.BlockSpec((D_MODEL, N_REST), lambda i: (0, 0)),
                  pl.BlockSpec((1, FPAD), lambda i: (0, 0)),
                  pl.BlockSpec((tm, tm), lambda i: (0, 0)),
                  pl.BlockSpec(shp, lambda i: (0, 0, 0)),
                  pl.BlockSpec((32, D_CONV), lambda i: (0, 0)), vec(), vec(), vec(),
                  pl.BlockSpec((8, D_RNN), lambda i: (0, 0)), vec(),
                  pl.BlockSpec((D_RNN, D_RNN), lambda i: (0, 0)), vec(),
                  pl.BlockSpec((D_RNN, D_RNN), lambda i: (0, 0)), vec(), vec()],
        out_specs=[pl.BlockSpec((tm, 3 * D_ATT), lambda i: (i, 0)),
                   pl.BlockSpec((D_ATT, tm), lambda i: (0, i)),
                   pl.BlockSpec((tm, D_CONV + D_RNN), lambda i: (i, 0))],
        out_shape=[jax.ShapeDtypeStruct((T, 3 * D_ATT), BF),
                   jax.ShapeDtypeStruct((D_ATT, T), BF),
                   jax.ShapeDtypeStruct((T, D_CONV + D_RNN), BF)],
        scratch_shapes=[pltpu.VMEM((1, FPAD), F32), pltpu.VMEM((tm, N_REST), F32),
                        pltpu.VMEM((tm + CONV_HALO, D_CONV), F32), pltpu.VMEM((tm + RG_HALO, D_RNN), F32),
                        pltpu.VMEM((1, D_RNN), F32)],
        compiler_params=_cp(("arbitrary",)), name="in_proj_conv_rglru",
    )(x, g, p["wqk"], p["wvt"], p["wf"], p["wrest"], p["bfg"], tri, place,
      p["cw"], p["cb"], p["lg"], p["lb"], p["rw"], p["rb"], p["wr"], p["br"], p["wi"], p["bi"], p["lam"])


def _attn_body(q_ref, k_ref, kb_ref, vt_ref, o_ref, m_ref, acc_ref, *, tq, tk):
    qi = pl.program_id(2)
    ki = pl.program_id(3)

    @pl.when(ki == 0)
    def _():
        m_ref[...] = jnp.full_like(m_ref, NEG)
        acc_ref[...] = jnp.zeros_like(acc_ref)

    lane = lax.broadcasted_iota(I32, (1, LANES), 1)
    first = lane < ATT_HD
    vrow = lax.broadcasted_iota(I32, (LANES, 1), 0) < ATT_HD

    def step(masked):
        q = q_ref[...]
        k = k_ref[...]
        kb = kb_ref[...]
        vt = vt_ref[...]
        if masked:
            keep = (lax.broadcasted_iota(I32, (tk, tq), 0) <= lax.broadcasted_iota(I32, (tk, tq), 1))
        for hh in range(2):
            own = first if hh == 0 else jnp.logical_not(first)
            ones = (lane >= bias_lane(hh, 0)) & (lane < bias_lane(hh, NSPLIT))
            qa = jnp.where(own, q, jnp.where(ones, 1.0, 0.0).astype(BF))
            ka = jnp.where(own, k, kb)
            st = _nt(ka, qa)
            if masked:
                st = jnp.where(keep, st, NEG)
            m_prev = m_ref[hh]
            m_new = jnp.maximum(m_prev, jnp.max(st, axis=0, keepdims=True))
            alpha = jnp.exp2(m_prev - m_new)
            p = jnp.exp2(st - m_new).astype(BF)
            m_ref[hh] = m_new
            vown = vrow if hh == 0 else jnp.logical_not(vrow)
            va = jnp.where(vown, vt, jnp.ones_like(vt))
            acc_ref[hh] = alpha * acc_ref[hh] + _dot(va, p)

    @pl.when(ki < qi)
    def _():
        step(False)

    @pl.when(ki == qi)
    def _():
        step(True)
        a0 = acc_ref[0]
        a1 = acc_ref[1]
        ot = jnp.where(vrow, a0 / a0[ATT_HD:ATT_HD + 1, :], a1 / a1[0:1, :])
        o_ref[...] = ot.T.astype(o_ref.dtype)


def attention(qkb, vt, batch, seq, tq=1024):
    T = qkb.shape[0]
    tk = tq
    nq = seq // tq
    npair = ATT_HEADS // 2
    body = functools.partial(_attn_body, tq=tq, tk=tk)
    kblk = lambda b, qi, ki: b * nq + jnp.minimum(ki, qi)
    return pl.pallas_call(
        body, grid=(batch, npair, nq, nq),
        in_specs=[pl.BlockSpec((tq, LANES), lambda b, p, qi, ki: (b * nq + qi, p)),
                  pl.BlockSpec((tk, LANES), lambda b, p, qi, ki: (kblk(b, qi, ki), npair + p)),
                  pl.BlockSpec((tk, LANES), lambda b, p, qi, ki: (kblk(b, qi, ki), 2 * npair + p)),
                  pl.BlockSpec((LANES, tk), lambda b, p, qi, ki: (p, kblk(b, qi, ki)))],
        out_specs=pl.BlockSpec((tq, LANES), lambda b, p, qi, ki: (b * nq + qi, p)),
        out_shape=jax.ShapeDtypeStruct((T, D_ATT), BF),
        scratch_shapes=[pltpu.VMEM((2, 1, tq), F32), pltpu.VMEM((2, LANES, tq), F32)],
        compiler_params=_cp(("parallel", "parallel", "parallel", "arbitrary")), name="fox_attention",
    )(qkb, qkb, qkb, vt)


CONV_HALO = 32
RG_HALO = 8


def _mix_body(rest_ref, cw_ref, cb_ref, lg_ref, lb_ref, rw_ref, rb_ref, wr_ref, br_ref, wi_ref, bi_ref, lam_ref,
              o_ref, ybuf, xbuf, hc, *, ts, first):
    @pl.when(first)
    def _():
        ybuf[0:CONV_HALO, :] = jnp.zeros((CONV_HALO, D_CONV), F32)
        xbuf[0:RG_HALO, :] = jnp.zeros((RG_HALO, D_RNN), F32)
        hc[...] = jnp.zeros_like(hc)

    y = rest_ref[:, 0:D_CONV] * _sigmoid(rest_ref[:, D_CONV:2 * D_CONV])
    ybuf[CONV_HALO:CONV_HALO + ts, :] = y
    acc = jnp.zeros((ts, D_CONV), F32)
    for k in range(CONV_K):
        acc = acc + cw_ref[k:k + 1, :] * ybuf[pl.ds(CONV_HALO - (CONV_K - 1) + k, ts), :]
    yc = acc + cb_ref[...]
    mu = jnp.mean(yc, axis=-1, keepdims=True)
    var = jnp.mean(jnp.square(yc - mu), axis=-1, keepdims=True)
    yn = (yc - mu) * lax.rsqrt(var + EPS) * lg_ref[...] + lb_ref[...]
    o_ref[:, 0:D_CONV] = (yn * _sigmoid(yn)).astype(o_ref.dtype)
    ybuf[0:CONV_HALO, :] = ybuf[ts:ts + CONV_HALO, :]

    xbuf[RG_HALO:RG_HALO + ts, :] = rest_ref[:, 2 * D_CONV:2 * D_CONV + D_RNN]
    xc = jnp.zeros((ts, D_RNN), F32)
    for k in range(RNN_CONV_K):
        xc = xc + rw_ref[k:k + 1, :] * xbuf[pl.ds(RG_HALO - (RNN_CONV_K - 1) + k, ts), :]
    xc = xc + rb_ref[...]
    xbuf[0:RG_HALO, :] = xbuf[ts:ts + RG_HALO, :]
    xcb = xc.astype(BF)
    r = _sigmoid(_dot(xcb, wr_ref[...]) + br_ref[...])
    gi = _sigmoid(_dot(xcb, wi_ref[...]) + bi_ref[...])
    nl = -lam_ref[...]
    sp = jnp.maximum(nl, 0.0) + jnp.log(1.0 + jnp.exp(-jnp.abs(nl)))
    log_a = -RG_C * r * sp
    a = jnp.exp(log_a)
    bt = jnp.sqrt(1.0 - jnp.exp(2.0 * log_a)) * (gi * xc)
    row = lax.broadcasted_iota(I32, (ts, 1), 0)
    sh = 1
    while sh < ts:
        live = row >= sh
        a_s = jnp.where(live, pltpu.roll(a, sh, 0), 1.0)
        b_s = jnp.where(live, pltpu.roll(bt, sh, 0), 0.0)
        bt = bt + a * b_s
        a = a * a_s
        sh *= 2
    h = bt + a * hc[...]
    hc[...] = h[ts - 1:ts, :]
    gate_in = rest_ref[:, 2 * D_CONV + D_RNN:2 * D_CONV + 2 * D_RNN]
    o_ref[:, D_CONV:D_CONV + D_RNN] = (h * _gelu(gate_in)).astype(o_ref.dtype)


def pack_pairs(a):
    half = a.shape[1] // 2
    r = lax.bitcast_convert_type(a, I32)
    r = r + jnp.int32(0x7FFF) + jnp.bitwise_and(lax.shift_right_logical(r, 16), 1)
    lo = lax.shift_right_logical(r[:, :half], 16)
    hi = jnp.bitwise_and(r[:, half:], jnp.int32(-65536))
    return jnp.bitwise_or(hi, lo)


def _outproj_body(x_ref, ya_ref, yc_ref, woa_ref, wob_ref, g2_ref, wq_ref, keys_ref, x1_ref, h2p_ref, e_ref, gt_ref,
                  st_ref, *route_scr, tm):
    x1 = x_ref[...] + _dot(ya_ref[...], woa_ref[...]) + _dot(yc_ref[...], wob_ref[...])
    x1_ref[...] = x1
    h2 = x1 * lax.rsqrt(jnp.mean(x1 * x1, axis=-1, keepdims=True) + EPS) * g2_ref[...]
    h2p_ref[...] = pack_pairs(h2)
    q = _dot(h2.astype(BF), wq_ref[...]).astype(BF)
    for g in range(2 * PEER_HEADS):
        st_ref[g * N_KEYS:(g + 1) * N_KEYS, :] = _nt(keys_ref[g], q[:, g * D_HALF:(g + 1) * D_HALF])

    def cols(hc, carry):
        _route_body(st_ref, pl.multiple_of(hc * LANES, LANES), e_ref, gt_ref, *route_scr)
        return carry

    lax.fori_loop(0, tm // LANES, cols, 0)


def out_proj(x, ya, yc, p, g2, row0=0, sub0=0, T=None, tm=256):
    T = ya.shape[0] if T is None else T
    ng = 2 * PEER_HEADS
    blk0 = (row0 + sub0) // tm
    sblk = sub0 // tm
    return pl.pallas_call(
        functools.partial(_outproj_body, tm=tm), grid=(T // tm,),
        in_specs=[pl.BlockSpec((tm, D_MODEL), lambda i: (i + blk0, 0)),
                  pl.BlockSpec((tm, D_ATT), lambda i: (i + sblk, 0)),
                  pl.BlockSpec((tm, D_CONV + D_RNN), lambda i: (i + sblk, 0)),
                  pl.BlockSpec((D_ATT, D_MODEL), lambda i: (0, 0)),
                  pl.BlockSpec((D_CONV + D_RNN, D_MODEL), lambda i: (0, 0)),
                  pl.BlockSpec((1, D_MODEL), lambda i: (0, 0)),
                  pl.BlockSpec((D_MODEL, ng * D_HALF), lambda i: (0, 0)),
                  pl.BlockSpec((ng, N_KEYS, D_HALF), lambda i: (0, 0, 0))],
        out_specs=[pl.BlockSpec((tm, D_MODEL), lambda i: (i, 0)),
                   pl.BlockSpec((tm, D_MODEL // 2), lambda i: (i, 0)),
                   pl.BlockSpec((tm, KSEL), lambda i: (i, 0)),
                   pl.BlockSpec((tm, KSEL), lambda i: (i, 0))],
        out_shape=[jax.ShapeDtypeStruct((T, D_MODEL), F32),
                   jax.ShapeDtypeStruct((T, D_MODEL // 2), I32),
                   jax.ShapeDtypeStruct((T, KSEL), I32),
                   jax.ShapeDtypeStruct((T, KSEL), F32)],
        scratch_shapes=[pltpu.VMEM((ng * N_KEYS, tm), F32),
                        pltpu.VMEM((ng, TOPK, LANES), F32), pltpu.VMEM((ng, TOPK, LANES), I32),
                        pltpu.VMEM((NPAR, TOPK, LANES), F32), pltpu.VMEM((NPAR, TOPK, LANES), I32),
                        pltpu.VMEM((KSEL, LANES), I32), pltpu.VMEM((KSEL, LANES), F32)],
        compiler_params=_cp(("parallel",)), name="out_proj_peer_route",
    )(x, ya, yc, p["woa"], p["wob"], g2, p["wq"], p["keys"])


BIG_ID = 1 << 20
SUBL = 8
NPAR = 4
NPAR1 = 8
SEL_CHAIN = 4


def _take_rounds(problems, nrounds):
    state = [list(slabs) for slabs, _ in problems]
    res = [([], []) for _ in problems]
    for _ in range(nrounds):
        for pi, (_, ids) in enumerate(problems):
            slabs = state[pi]
            m8 = slabs[0]
            for sl in slabs[1:]:
                m8 = jnp.maximum(m8, sl)
            m = jnp.max(m8, axis=0, keepdims=True)
            chains = []
            for c0 in range(0, len(slabs), SEL_CHAIN):
                v = jnp.full((SUBL, LANES), BIG_ID, I32)
                for sl, idc in zip(reversed(slabs[c0:c0 + SEL_CHAIN]), reversed(ids[c0:c0 + SEL_CHAIN])):
                    v = jnp.where(sl == m, idc, v)
                chains.append(v)
            while len(chains) > 1:
                chains = [jnp.minimum(chains[i], chains[i + 1]) if i + 1 < len(chains) else chains[i]
                          for i in range(0, len(chains), 2)]
            pick = jnp.min(chains[0], axis=0, keepdims=True)
            state[pi] = [jnp.where(idc == pick, NEG, sl) for sl, idc in zip(slabs, ids)]
            res[pi][0].append(m)
            res[pi][1].append(pick)
    return res


def _route_body(st_ref, col0, e_ref, g_ref, v_scr, i_scr, sv_scr, ci_scr, et_scr, gt_scr):
    ng = 2 * PEER_HEADS
    sub = lax.broadcasted_iota(I32, (SUBL, LANES), 0)
    key_ids = [sub + SUBL * i for i in range(N_KEYS // SUBL)]

    def stage1(gg, carry):
        probs = []
        for q in range(NPAR1):
            base = pl.multiple_of((gg * NPAR1 + q) * N_KEYS, N_KEYS)
            probs.append(([st_ref[pl.ds(base + SUBL * i, SUBL), pl.ds(col0, LANES)]
                           for i in range(N_KEYS // SUBL)], key_ids))
        for q, (vals, picks) in enumerate(_take_rounds(probs, TOPK)):
            for r in range(TOPK):
                v_scr[gg * NPAR1 + q, r:r + 1, :] = vals[r]
                i_scr[gg * NPAR1 + q, r:r + 1, :] = picks[r]
        return carry

    lax.fori_loop(0, ng // NPAR1, stage1, 0)

    def stage2(hh, carry):
        probs = []
        for q in range(NPAR):
            h = hh * NPAR + q
            v1 = v_scr[2 * h]
            v2 = v_scr[2 * h + 1]
            slabs = [v1[0:1, :] + v2[0:SUBL, :], v1[0:1, :] + v2[SUBL:TOPK, :]]
            ids = [sub, sub + SUBL]
            for i in range(1, TOPK):
                nj = TOPK // (i + 1)
                slabs.append(jnp.where(sub < nj, v1[i:i + 1, :] + v2[0:SUBL, :], NEG))
                ids.append(sub + i * TOPK)
            probs.append((slabs, ids))
        for q, (vals, picks) in enumerate(_take_rounds(probs, TOPK)):
            h = hh * NPAR + q
            i1 = i_scr[2 * h]
            i2 = i_scr[2 * h + 1]
            for r in range(TOPK):
                sv_scr[q, r:r + 1, :] = vals[r]
                ci_scr[q, r:r + 1, :] = picks[r]
            sv = sv_scr[q]
            ci = ci_scr[q]
            ci_hi = lax.shift_right_logical(ci, 4)
            ci_lo = jnp.bitwise_and(ci, TOPK - 1)
            e1 = jnp.zeros((TOPK, LANES), I32)
            e2 = jnp.zeros((TOPK, LANES), I32)
            for i in range(TOPK):
                e1 = jnp.where(ci_hi == i, i1[i:i + 1, :], e1)
                e2 = jnp.where(ci_lo == i, i2[i:i + 1, :], e2)
            p = jnp.exp(sv - sv[0:1, :])
            gates = p / jnp.sum(p, axis=0, keepdims=True)
            et_scr[pl.ds(pl.multiple_of(h * TOPK, TOPK), TOPK), :] = e1 * N_KEYS + e2
            gt_scr[pl.ds(pl.multiple_of(h * TOPK, TOPK), TOPK), :] = gates
        return carry

    lax.fori_loop(0, PEER_HEADS // NPAR, stage2, 0)
    e_ref[pl.ds(col0, LANES), :] = et_scr[...].T
    g_ref[pl.ds(col0, LANES), :] = gt_scr[...].T


NC, NS, L = 2, 16, 16
NW = NC * NS
NJ = D_MODEL // L
R = TOPK
NCH = KSEL // R
NB = 4
USTEP = 2
G = 8
NSLOT = 2
DW = D_MODEL // 2


def _perm(x, idx):
    return jnp.take_along_axis(x, idx, axis=0, mode="promise_in_bounds")


def _halves(w):
    lo = lax.bitcast_convert_type(lax.shift_left(w, 16), F32)
    hi = lax.bitcast_convert_type(jnp.bitwise_and(w, jnp.int32(-65536)), F32)
    return lo, hi


def _bf(w):
    return plsc.bitcast(w, BF)


def peer_sc(x, resid, idx, gates, uv_tab):
    T = x.shape[0]
    tpw = T // NW
    ngroups = tpw // G
    nchunks = G * NCH
    idx3 = idx.reshape(T * NCH, R)
    g3 = gates.reshape(T * NCH, R)
    mesh = plsc.VectorSubcoreMesh(core_axis_name="c", subcore_axis_name="s")

    @functools.partial(
        pl.kernel, mesh=mesh,
        out_type=jax.ShapeDtypeStruct((T, D_MODEL), F32),
        scratch_types=[
            pltpu.VMEM((NSLOT, G, DW), I32),
            pltpu.VMEM((NSLOT, G, D_MODEL), F32),
            pltpu.VMEM((NSLOT, nchunks, R), I32),
            pltpu.VMEM((NSLOT, nchunks, R), F32),
            pltpu.SemaphoreType.DMA((NSLOT,)),
            pltpu.SemaphoreType.DMA((NSLOT,)),
            pltpu.SemaphoreType.DMA((NSLOT,)),
        ] + [pltpu.VMEM((R, 2 * DW), I32) for _ in range(NB)]
          + [pltpu.SemaphoreType.DMA for _ in range(NB)],
        compiler_params=pltpu.CompilerParams(needs_layout_passes=False),
        name="peer_experts_sc",
    )
    def k(x_hbm, r_hbm, idx_hbm, g_hbm, uv_hbm, out_hbm, x_v, out_v, idx_v, g_v, st_sem, ix_sem, wb_sem, *ring):
        wid = lax.axis_index("s") * NC + lax.axis_index("c")
        bufs, sems = ring[:NB], ring[NB:]
        iota = lax.iota(I32, L)

        def tok0_of(g):
            return wid * tpw + g * G

        def stage_copies(g, slot):
            t0 = tok0_of(g)
            return (pltpu.make_async_copy(x_hbm.at[pl.ds(t0, G)], x_v.at[slot], st_sem.at[slot]),
                    pltpu.make_async_copy(r_hbm.at[pl.ds(t0, G)], out_v.at[slot], st_sem.at[slot]),
                    pltpu.make_async_copy(g_hbm.at[pl.ds(t0 * NCH, nchunks)], g_v.at[slot], st_sem.at[slot]))

        def idx_copy(g, slot):
            return pltpu.make_async_copy(idx_hbm.at[pl.ds(tok0_of(g) * NCH, nchunks)], idx_v.at[slot], ix_sem.at[slot])

        def wb_copy(g, slot):
            return pltpu.make_async_copy(out_v.at[slot], out_hbm.at[pl.ds(tok0_of(g), G)], wb_sem.at[slot])

        def gather_copy(slot, c, b):
            return pltpu.make_async_copy(uv_hbm.at[idx_v.at[slot, c]], bufs[b], sems[b])

        def compute(slot, c, b):
            ub = vb = bufs[b]
            t = c // NCH

            def ubody(mm, accs):
                xs_ = [_bf(x_v[slot, t, pl.ds((mm * USTEP + q) * L, L)]) for q in range(USTEP)]
                out = []
                for kk in range(R):
                    pr = [xs_[q] * _bf(ub[kk, pl.ds((mm * USTEP + q) * L, L)]) for q in range(USTEP)]
                    while len(pr) > 1:
                        pr = [pr[i] + pr[i + 1] for i in range(0, len(pr), 2)]
                    lo, hi = _halves(plsc.bitcast(pr[0], I32))
                    out.append(accs[kk] + (lo + hi))
                return tuple(out)

            accs = lax.fori_loop(0, NJ // (2 * USTEP), ubody, tuple(jnp.zeros((L,), F32) for _ in range(R)))
            vecs = list(accs)
            dist = L // 2
            while dist >= 1:
                pidx = jnp.bitwise_xor(iota, dist)
                low = jnp.bitwise_and(iota, dist) == 0
                nxt = []
                for kk in range(dist):
                    a = vecs[kk]
                    bvec = vecs[kk + dist]
                    a = a + _perm(a, pidx)
                    bvec = bvec + _perm(bvec, pidx)
                    nxt.append(jnp.where(low, a, bvec))
                vecs = nxt
                dist //= 2
            hid = vecs[0]
            z = GC * (hid + 0.044715 * hid * hid * hid)
            gel = hid / (1.0 + jnp.exp(-2.0 * z))
            w = g_v[slot, c, :] * gel
            wbs = []
            for kk in range(R):
                wb = _perm(w, jnp.full((L,), kk, I32))
                wbs.append(plsc.pack(wb, wb, format=plsc.PackFormat.INTERLEAVED))

            @plsc.parallel_loop(0, NJ // 2)
            def _(m):
                pr = [wbs[kk] * _bf(vb[kk, pl.ds(DW + m * L, L)]) for kk in range(R)]
                for _lvl in range(2):
                    pr = [pr[i] + pr[i + 1] for i in range(0, len(pr), 2)]
                los, his = [], []
                for q in pr:
                    lo, hi = _halves(plsc.bitcast(q, I32))
                    los.append(lo)
                    his.append(hi)
                while len(los) > 1:
                    los = [los[i] + los[i + 1] for i in range(0, len(los), 2)]
                    his = [his[i] + his[i + 1] for i in range(0, len(his), 2)]
                out_v[slot, t, pl.ds(m * L, L)] = out_v[slot, t, pl.ds(m * L, L)] + los[0]
                out_v[slot, t, pl.ds(DW + m * L, L)] = out_v[slot, t, pl.ds(DW + m * L, L)] + his[0]

        idx_copy(0, 0).start()
        for cp in stage_copies(0, 0):
            cp.start()
        idx_copy(0, 0).wait()
        for b in range(NB - 1):
            gather_copy(0, b, b).start()

        def group(g, carry):
            slot = g % NSLOT
            nslot = 1 - slot
            has_next = g + 1 < ngroups

            @pl.when(g >= 1)
            def _():
                wb_copy(g - 1, nslot).wait()

            @pl.when(has_next)
            def _():
                idx_copy(g + 1, nslot).start()
                for cp in stage_copies(g + 1, nslot):
                    cp.start()

            for cp in stage_copies(g, slot):
                cp.wait()

            def cbody(cc, c2):
                for b in range(NB):
                    c = cc * NB + b
                    cn = c + NB - 1
                    nb = (b + NB - 1) % NB

                    @pl.when(cn < nchunks)
                    def _():
                        gather_copy(slot, cn, nb).start()

                    @pl.when(jnp.logical_and(cn >= nchunks, has_next))
                    def _():
                        @pl.when(cn == nchunks)
                        def _():
                            idx_copy(g + 1, nslot).wait()

                        gather_copy(nslot, cn - nchunks, nb).start()

                    gather_copy(slot, c, b).wait()
                    compute(slot, c, b)
                return c2

            lax.fori_loop(0, nchunks // NB, cbody, 0)
            wb_copy(g, slot).start()
            return carry

        lax.fori_loop(0, ngroups, group, 0)
        wb_copy(ngroups - 1, (ngroups - 1) % NSLOT).wait()

    return k(x, resid, idx3, g3, uv_tab)


def _fn_body(x_ref, g_ref, o_ref):
    xf = x_ref[...]
    o_ref[...] = xf * lax.rsqrt(jnp.mean(xf * xf, axis=-1, keepdims=True) + EPS) * g_ref[...]


def _fn_body_into(x_ref, g_ref, prev_ref, o_ref):
    del prev_ref
    _fn_body(x_ref, g_ref, o_ref)


def final_norm(x, g, out, row0, t_total, tm=1024):
    T, d = x.shape
    blk0 = row0 // tm
    common = dict(grid=(T // tm,), out_specs=pl.BlockSpec((tm, d), lambda i: (i + blk0, 0)),
                  out_shape=jax.ShapeDtypeStruct((t_total, d), F32),
                  compiler_params=_cp(("parallel",)), name="final_norm")
    specs = [pl.BlockSpec((tm, d), lambda i: (i, 0)), pl.BlockSpec((1, d), lambda i: (0, 0))]
    if out is None:
        return pl.pallas_call(_fn_body, in_specs=specs, **common)(x, g)
    return pl.pallas_call(_fn_body_into, in_specs=specs + [pl.BlockSpec(memory_space=pl.ANY)],
                          input_output_aliases={2: 0}, **common)(x, g, out)


def _pack_body(u_ref, v_ref, o_ref):
    o_ref[:, 0:D_MODEL // 2] = pack_pairs(u_ref[...])
    o_ref[:, D_MODEL // 2:D_MODEL] = pack_pairs(v_ref[...])


def pack_tables(u, v, tm=512):
    e, d = u.shape
    return pl.pallas_call(
        _pack_body, grid=(e // tm,),
        in_specs=[pl.BlockSpec((tm, d), lambda i: (i, 0)), pl.BlockSpec((tm, d), lambda i: (i, 0))],
        out_specs=pl.BlockSpec((tm, d), lambda i: (i, 0)),
        out_shape=jax.ShapeDtypeStruct((e, d), I32),
        compiler_params=_cp(("parallel",)), name="pack_tables",
    )(u, v)


def _prep_layer(w_in, b_forget, conv_dw_w, conv_dw_b, conv_ln_g, conv_ln_b, rg_conv_w, rg_conv_b,
                rg_w_r, rg_b_r, rg_w_i, rg_b_i, rg_lambda, w_out, peer_wq, peer_k1, peer_k2):
    f0 = 3 * D_ATT
    wf = jnp.zeros((D_MODEL, FPAD), BF).at[:, 0:ATT_HEADS].set(w_in[:, f0:f0 + ATT_HEADS].astype(BF))
    bfg = jnp.zeros((1, FPAD), F32).at[0, 0:ATT_HEADS].set(b_forget)
    cw = jnp.zeros((32, D_CONV), F32).at[0:CONV_K].set(conv_dw_w)
    rw = jnp.zeros((8, D_RNN), F32).at[0:RNN_CONV_K].set(rg_conv_w)
    bd = lambda w: jax.scipy.linalg.block_diag(*[w[i] for i in range(RNN_BLOCKS)]).astype(BF)
    row = lambda v: v.reshape(1, -1).astype(F32)
    keys = jnp.stack([peer_k1, peer_k2], axis=1).reshape(2 * PEER_HEADS, N_KEYS, D_HALF).astype(BF)
    return dict(wqk=w_in[:, 0:2 * D_ATT].astype(BF), wvt=w_in[:, 2 * D_ATT:f0].T.astype(BF), wf=wf,
                wrest=w_in[:, f0 + ATT_HEADS:].astype(BF), bfg=bfg,
                cw=cw, cb=row(conv_dw_b), lg=row(conv_ln_g), lb=row(conv_ln_b),
                rw=rw, rb=row(rg_conv_b), wr=bd(rg_w_r), br=row(rg_b_r), wi=bd(rg_w_i), bi=row(rg_b_i),
                lam=row(rg_lambda), woa=w_out[0:D_ATT].astype(BF), wob=w_out[D_ATT:].astype(BF),
                wq=peer_wq.astype(BF), keys=keys)


def kernel(x, norm1_g, w_in, b_forget, conv_dw_w, conv_dw_b, conv_ln_g, conv_ln_b,
           rg_conv_w, rg_conv_b, rg_w_r, rg_b_r, rg_w_i, rg_b_i, rg_lambda, w_out,
           norm2_g, peer_wq, peer_k1, peer_k2, peer_u, peer_v, final_g):
    b, s, d = x.shape
    raw = (w_in, b_forget, conv_dw_w, conv_dw_b, conv_ln_g, conv_ln_b, rg_conv_w, rg_conv_b,
           rg_w_r, rg_b_r, rg_w_i, rg_b_i, rg_lambda, w_out, peer_wq, peer_k1, peer_k2)
    params = {0: _prep_layer(*[a[0] for a in raw])}
    tabs = {0: pack_tables(peer_u[0], peer_v[0])}
    bs = b // N_SLICES
    T = bs * s
    xf = x.reshape(b * s, d)
    xs = [None] * N_SLICES
    prev = None
    for l in range(DEPTH):
        for i in range(N_SLICES):
            src, row0 = (xf, i * T) if l == 0 else (xs[i], 0)
            first = l == 0 and i == 0
            late = l == 0 and i == LATE_PREP_SLICE
            deps = ([prev] if prev is not None else []) + ([tabs[0]] if first else [])
            deps += [*raw, peer_u, peer_v] if late else []
            if deps:
                tied = list(lax.optimization_barrier((src, *deps)))
                src = tied.pop(0)
                if prev is not None:
                    prev = tied.pop(0)
                if first:
                    tabs[0] = tied.pop(0)
                if late:
                    for ll in range(1, DEPTH):
                        params[ll] = _prep_layer(*[a[ll] for a in tied[:len(raw)]])
                        tabs[ll] = pack_tables(tied[-2][ll], tied[-1][ll])
            p = params[l]
            qkb, vt, y_cr = in_proj(src, norm1_g[l].reshape(1, d), p, s, T, row0)
            y_att = attention(qkb, vt, bs, s)
            npiece = PIECES.get((l, i), 1)
            tp = T // npiece
            pieces = []
            for j in range(npiece):
                if j > 0:
                    src, experts = lax.optimization_barrier((src, experts))
                x1, h2p, experts, gates = out_proj(src, y_att, y_cr, p, norm2_g[l].reshape(1, d), row0, j * tp, tp)
                pieces.append(peer_sc(h2p, x1, experts, gates, tabs[l]))
            prev = experts
            xs[i] = pieces[0] if npiece == 1 else jnp.concatenate(pieces, axis=0)
    out = None
    for i in range(N_SLICES):
        out = final_norm(xs[i], final_g.reshape(1, d), out, i * T, b * s)
    return out.reshape(b, s, d)
```
